```python
import math, functools
import jax, jax.numpy as jnp
from jax import lax
import numpy as np

D_MODEL = 2048
BATCH = 8
SEQ = 4096
DEPTH = 2

DN_HEADS = 8
DN_HEAD_DIM = 128
DN_WIDTH = DN_HEADS * DN_HEAD_DIM
DN_CONV = 4
DN_CHUNK = 64
SG_GROUPS = 8
SG_GROUP_DIM = 128
SG_WIDTH = SG_GROUPS * SG_GROUP_DIM
SG_CHUNK = 128
D_FF = 5632
FFN_CONV = 3
EPS = 1e-6

SPLIT_SIZES = (3 * DN_WIDTH, DN_WIDTH, DN_HEADS, DN_HEADS, SG_WIDTH, SG_WIDTH, D_MODEL, D_MODEL)
SPLIT_POINTS = tuple(int(s) for s in np.cumsum(SPLIT_SIZES)[:-1])
IN_COLS = int(sum(SPLIT_SIZES))

kernel_name = "hybrid_deltanet_gmlp_convffn_gated_merge"


def rmsnorm(x, g):
    xf = x.astype(jnp.float32)
    y = xf * lax.rsqrt(jnp.mean(xf * xf, axis=-1, keepdims=True) + EPS)
    return (y * g.astype(jnp.float32)).astype(x.dtype)


def layernorm(x, g, b):
    xf = x.astype(jnp.float32)
    mu = jnp.mean(xf, axis=-1, keepdims=True)
    xc = xf - mu
    y = xc * lax.rsqrt(jnp.mean(xc * xc, axis=-1, keepdims=True) + EPS)
    return (y * g.astype(jnp.float32) + b.astype(jnp.float32)).astype(x.dtype)


def l2norm(x):
    xf = x.astype(jnp.float32)
    return xf * lax.rsqrt(jnp.sum(xf * xf, axis=-1, keepdims=True) + EPS)


def causal_dwconv(x, w):
    K = w.shape[0]
    T = x.shape[1]
    xp = jnp.pad(x, ((0, 0), (K - 1, 0), (0, 0)))
    out = xp[:, 0:T] * w[0]
    for j in range(1, K):
        out = out + xp[:, j:j + T] * w[j]
    return out


def gated_delta_rule_chunked(q, k, v, g, beta):
    B, T, H, Dk = q.shape
    Dv = v.shape[-1]
    C = DN_CHUNK
    N = T // C

    def to_chunks(t):
        t = t.astype(jnp.float32).reshape((B, N, C, H) + t.shape[3:])
        return jnp.moveaxis(t, 3, 1)

    q = to_chunks(q) * (Dk ** -0.5)
    k = to_chunks(k)
    v = to_chunks(v)
    g = jnp.cumsum(to_chunks(g), axis=-1)
    beta = to_chunks(beta)
    k_beta = k * beta[..., None]
    v_beta = v * beta[..., None]

    causal = jnp.tril(jnp.ones((C, C), dtype=bool))
    strict = jnp.tril(jnp.ones((C, C), dtype=bool), -1)
    decay = jnp.exp(jnp.where(causal, g[..., :, None] - g[..., None, :], -jnp.inf))

    L = jnp.where(strict, jnp.einsum('bhnid,bhnjd->bhnij', k_beta, k) * decay, 0.0)
    eye = jnp.eye(C, dtype=jnp.float32)
    Tinv = lax.linalg.triangular_solve(L + eye, jnp.broadcast_to(eye, L.shape),
                                       left_side=True, lower=True, unit_diagonal=True)
    u = jnp.einsum('bhnij,bhnjv->bhniv', Tinv, v_beta)
    w = jnp.einsum('bhnij,bhnjk->bhnik', Tinv, k_beta * jnp.exp(g)[..., None])

    attn = jnp.where(causal, jnp.einsum('bhnid,bhnjd->bhnij', q, k) * decay, 0.0)
    q_dec = q * jnp.exp(g)[..., None]
    g_last = g[..., -1]
    k_dec = k * jnp.exp(g_last[..., None] - g)[..., None]

    xs = tuple(jnp.moveaxis(t, 2, 0) for t in (u, w, attn, q_dec, k_dec, g_last))

    def step(S, inp):
        u_n, w_n, a_n, qd_n, kd_n, gl_n = inp
        v_new = u_n - jnp.einsum('bhck,bhkv->bhcv', w_n, S)
        o_n = (jnp.einsum('bhck,bhkv->bhcv', qd_n, S)
               + jnp.einsum('bhij,bhjv->bhiv', a_n, v_new))
        S = S * jnp.exp(gl_n)[..., None, None] + jnp.einsum('bhck,bhcv->bhkv', kd_n, v_new)
        return S, o_n

    S0 = jnp.zeros((B, H, Dk, Dv), jnp.float32)
    _, o = lax.scan(step, S0, xs)
    return jnp.transpose(o, (1, 0, 3, 2, 4)).reshape(B, T, H, Dv)


def _fwd_setup_inputs(seed: int = 0) -> dict:
    key = jax.random.key(seed)
    ks = jax.random.split(key, 24)
    f32 = jnp.float32

    def nrm(k, shape, scale):
        return jax.random.normal(k, shape, f32) * scale

    x = jax.random.normal(ks[0], (BATCH, SEQ, D_MODEL), f32)
    norm1_g = 1.0 + nrm(ks[1], (DEPTH, D_MODEL), 0.02)
    w_in = nrm(ks[2], (DEPTH, D_MODEL, IN_COLS), D_MODEL ** -0.5)
    dn_conv_w = nrm(ks[3], (DEPTH, DN_CONV, 3 * DN_WIDTH), DN_CONV ** -0.5)
    dn_a_log = jnp.log(jax.random.uniform(ks[4], (DEPTH, DN_HEADS), f32, 1.0, 16.0))
    dt = jnp.exp(jax.random.uniform(ks[5], (DEPTH, DN_HEADS), f32,
                                    math.log(0.001), math.log(0.1)))
    dn_dt_bias = dt + jnp.log(-jnp.expm1(-dt))
    dn_onorm_g = 1.0 + nrm(ks[6], (DEPTH, DN_HEAD_DIM), 0.02)
    sg_ln_g = 1.0 + nrm(ks[7], (DEPTH, SG_WIDTH), 0.02)
    sg_ln_b = nrm(ks[8], (DEPTH, SG_WIDTH), 0.02)
    sg_w = nrm(ks[9], (DEPTH, SG_GROUPS, SG_CHUNK, SG_CHUNK), 0.5 * SG_CHUNK ** -0.5)
    sg_b = 1.0 + nrm(ks[10], (DEPTH, SG_GROUPS, SG_CHUNK), 0.02)
    w_branch_a = nrm(ks[11], (DEPTH, DN_WIDTH, D_MODEL), DN_WIDTH ** -0.5)
    w_branch_b = nrm(ks[12], (DEPTH, SG_WIDTH, D_MODEL), SG_WIDTH ** -0.5)
    w_out = nrm(ks[13], (DEPTH, D_MODEL, D_MODEL), D_MODEL ** -0.5)
    norm2_g = 1.0 + nrm(ks[14], (DEPTH, D_MODEL), 0.02)
    ffn_w_gate = nrm(ks[15], (DEPTH, D_MODEL, D_FF), D_MODEL ** -0.5)
    ffn_w_up = nrm(ks[16], (DEPTH, D_MODEL, D_FF), D_MODEL ** -0.5)
    ffn_conv_w = nrm(ks[17], (DEPTH, FFN_CONV, D_FF), FFN_CONV ** -0.5)
    ffn_conv_b = nrm(ks[18], (DEPTH, D_FF), 0.02)
    ffn_w_down = nrm(ks[19], (DEPTH, D_FF, D_MODEL), D_FF ** -0.5)
    final_norm_g = 1.0 + nrm(ks[20], (D_MODEL,), 0.02)
    return {"x": x, "norm1_g": norm1_g, "w_in": w_in, "dn_conv_w": dn_conv_w,
            "dn_a_log": dn_a_log, "dn_dt_bias": dn_dt_bias, "dn_onorm_g": dn_onorm_g,
            "sg_ln_g": sg_ln_g, "sg_ln_b": sg_ln_b, "sg_w": sg_w, "sg_b": sg_b,
            "w_branch_a": w_branch_a, "w_branch_b": w_branch_b, "w_out": w_out,
            "norm2_g": norm2_g, "ffn_w_gate": ffn_w_gate, "ffn_w_up": ffn_w_up,
            "ffn_conv_w": ffn_conv_w, "ffn_conv_b": ffn_conv_b, "ffn_w_down": ffn_w_down,
            "final_norm_g": final_norm_g}


def _fwd_reference(x, norm1_g, w_in, dn_conv_w, dn_a_log, dn_dt_bias, dn_onorm_g,
              sg_ln_g, sg_ln_b, sg_w, sg_b, w_branch_a, w_branch_b, w_out,
              norm2_g, ffn_w_gate, ffn_w_up, ffn_conv_w, ffn_conv_b, ffn_w_down,
              final_norm_g):
    B, T, _ = x.shape
    sg_mask = jnp.tril(jnp.ones((SG_CHUNK, SG_CHUNK), dtype=bool))
    for l in range(DEPTH):
        h = rmsnorm(x, norm1_g[l])
        proj = h @ w_in[l]
        qkv, z, b_raw, a_raw, u_raw, v_raw, ga_raw, gb_raw = jnp.split(proj, SPLIT_POINTS, axis=-1)

        qkv = jax.nn.silu(causal_dwconv(qkv, dn_conv_w[l]))
        q, k, v = jnp.split(qkv, 3, axis=-1)
        q = l2norm(q.reshape(B, T, DN_HEADS, DN_HEAD_DIM))
        k = l2norm(k.reshape(B, T, DN_HEADS, DN_HEAD_DIM))
        v = v.reshape(B, T, DN_HEADS, DN_HEAD_DIM)
        beta = jax.nn.sigmoid(b_raw.astype(jnp.float32))
        g = -jnp.exp(dn_a_log[l].astype(jnp.float32)) * jax.nn.softplus(
            a_raw.astype(jnp.float32) + dn_dt_bias[l].astype(jnp.float32))
        o = gated_delta_rule_chunked(q, k, v, g, beta)
        o = rmsnorm(o, dn_onorm_g[l]) * jax.nn.silu(
            z.reshape(B, T, DN_HEADS, DN_HEAD_DIM).astype(jnp.float32))
        y_a = o.reshape(B, T, DN_WIDTH).astype(x.dtype)

        u = jax.nn.gelu(u_raw, approximate=False)
        vg = layernorm(jax.nn.gelu(v_raw, approximate=False), sg_ln_g[l], sg_ln_b[l])
        vg = vg.reshape(B, T // SG_CHUNK, SG_CHUNK, SG_GROUPS, SG_GROUP_DIM)
        ws = jnp.where(sg_mask, sg_w[l], 0.0)
        mixed = (jnp.einsum('gij,bnjgc->bnigc', ws, vg)
                 + jnp.transpose(sg_b[l])[None, None, :, :, None])
        y_b = u * mixed.reshape(B, T, SG_WIDTH)

        merged = (jax.nn.sigmoid(ga_raw) * (y_a @ w_branch_a[l])
                  + jax.nn.sigmoid(gb_raw) * (y_b @ w_branch_b[l]))
        x = x + merged @ w_out[l]

        h2 = rmsnorm(x, norm2_g[l])
        gate = causal_dwconv(h2 @ ffn_w_gate[l], ffn_conv_w[l]) + ffn_conv_b[l]
        x = x + (jax.nn.silu(gate) * (h2 @ ffn_w_up[l])) @ ffn_w_down[l]
    return rmsnorm(x, final_norm_g)


import jax as _jax
import jax.numpy as _jnp

TWIN_FORMAT = 'train_step'
FWD_PARAMS = ['x', 'norm1_g', 'w_in', 'dn_conv_w', 'dn_a_log', 'dn_dt_bias', 'dn_onorm_g', 'sg_ln_g', 'sg_ln_b', 'sg_w', 'sg_b', 'w_branch_a', 'w_branch_b', 'w_out', 'norm2_g', 'ffn_w_gate', 'ffn_w_up', 'ffn_conv_w', 'ffn_conv_b', 'ffn_w_down', 'final_norm_g']
TWIN_WEIGHTS = ['norm1_g', 'w_in', 'dn_conv_w', 'dn_a_log', 'dn_dt_bias', 'dn_onorm_g', 'sg_ln_g', 'sg_ln_b', 'sg_w', 'sg_b', 'w_branch_a', 'w_branch_b', 'w_out', 'norm2_g', 'ffn_w_gate', 'ffn_w_up', 'ffn_conv_w', 'ffn_conv_b', 'ffn_w_down', 'final_norm_g']
TWIN_DIFF_INPUT = 'x'
TWIN_INPUTS = ['x', 'norm1_g', 'w_in', 'dn_conv_w', 'dn_a_log', 'dn_dt_bias', 'dn_onorm_g', 'sg_ln_g', 'sg_ln_b', 'sg_w', 'sg_b', 'w_branch_a', 'w_branch_b', 'w_out', 'norm2_g', 'ffn_w_gate', 'ffn_w_up', 'ffn_conv_w', 'ffn_conv_b', 'ffn_w_down', 'final_norm_g', 'loss_target', 'm_norm1_g', 'm_w_in', 'm_dn_conv_w', 'm_dn_a_log', 'm_dn_dt_bias', 'm_dn_onorm_g', 'm_sg_ln_g', 'm_sg_ln_b', 'm_sg_w', 'm_sg_b', 'm_w_branch_a', 'm_w_branch_b', 'm_w_out', 'm_norm2_g', 'm_ffn_w_gate', 'm_ffn_w_up', 'm_ffn_conv_w', 'm_ffn_conv_b', 'm_ffn_w_down', 'm_final_norm_g', 'v_norm1_g', 'v_w_in', 'v_dn_conv_w', 'v_dn_a_log', 'v_dn_dt_bias', 'v_dn_onorm_g', 'v_sg_ln_g', 'v_sg_ln_b', 'v_sg_w', 'v_sg_b', 'v_w_branch_a', 'v_w_branch_b', 'v_w_out', 'v_norm2_g', 'v_ffn_w_gate', 'v_ffn_w_up', 'v_ffn_conv_w', 'v_ffn_conv_b', 'v_ffn_w_down', 'v_final_norm_g']
TWIN_OUTPUTS = ['loss', 'grad_x', 'grad_norm1_g', 'grad_w_in', 'grad_dn_conv_w', 'grad_dn_a_log', 'grad_dn_dt_bias', 'grad_dn_onorm_g', 'grad_sg_ln_g', 'grad_sg_ln_b', 'grad_sg_w', 'grad_sg_b', 'grad_w_branch_a', 'grad_w_branch_b', 'grad_w_out', 'grad_norm2_g', 'grad_ffn_w_gate', 'grad_ffn_w_up', 'grad_ffn_conv_w', 'grad_ffn_conv_b', 'grad_ffn_w_down', 'grad_final_norm_g', 'delta_norm1_g', 'delta_w_in', 'delta_dn_conv_w', 'delta_dn_a_log', 'delta_dn_dt_bias', 'delta_dn_onorm_g', 'delta_sg_ln_g', 'delta_sg_ln_b', 'delta_sg_w', 'delta_sg_b', 'delta_w_branch_a', 'delta_w_branch_b', 'delta_w_out', 'delta_norm2_g', 'delta_ffn_w_gate', 'delta_ffn_w_up', 'delta_ffn_conv_w', 'delta_ffn_conv_b', 'delta_ffn_w_down', 'delta_final_norm_g', 'new_m_norm1_g', 'new_m_w_in', 'new_m_dn_conv_w', 'new_m_dn_a_log', 'new_m_dn_dt_bias', 'new_m_dn_onorm_g', 'new_m_sg_ln_g', 'new_m_sg_ln_b', 'new_m_sg_w', 'new_m_sg_b', 'new_m_w_branch_a', 'new_m_w_branch_b', 'new_m_w_out', 'new_m_norm2_g', 'new_m_ffn_w_gate', 'new_m_ffn_w_up', 'new_m_ffn_conv_w', 'new_m_ffn_conv_b', 'new_m_ffn_w_down', 'new_m_final_norm_g', 'new_v_norm1_g', 'new_v_w_in', 'new_v_dn_conv_w', 'new_v_dn_a_log', 'new_v_dn_dt_bias', 'new_v_dn_onorm_g', 'new_v_sg_ln_g', 'new_v_sg_ln_b', 'new_v_sg_w', 'new_v_sg_b', 'new_v_w_branch_a', 'new_v_w_branch_b', 'new_v_w_out', 'new_v_norm2_g', 'new_v_ffn_w_gate', 'new_v_ffn_w_up', 'new_v_ffn_conv_w', 'new_v_ffn_conv_b', 'new_v_ffn_w_down', 'new_v_final_norm_g']
TWIN_LEAF_KINDS = {'loss': 'loss', 'grad_x': 'grad_x', 'grad_norm1_g': 'grad_w', 'grad_w_in': 'grad_w', 'grad_dn_conv_w': 'grad_w', 'grad_dn_a_log': 'grad_w', 'grad_dn_dt_bias': 'grad_w', 'grad_dn_onorm_g': 'grad_w', 'grad_sg_ln_g': 'grad_w', 'grad_sg_ln_b': 'grad_w', 'grad_sg_w': 'grad_w', 'grad_sg_b': 'grad_w', 'grad_w_branch_a': 'grad_w', 'grad_w_branch_b': 'grad_w', 'grad_w_out': 'grad_w', 'grad_norm2_g': 'grad_w', 'grad_ffn_w_gate': 'grad_w', 'grad_ffn_w_up': 'grad_w', 'grad_ffn_conv_w': 'grad_w', 'grad_ffn_conv_b': 'grad_w', 'grad_ffn_w_down': 'grad_w', 'grad_final_norm_g': 'grad_w', 'delta_norm1_g': 'delta_w', 'delta_w_in': 'delta_w', 'delta_dn_conv_w': 'delta_w', 'delta_dn_a_log': 'delta_w', 'delta_dn_dt_bias': 'delta_w', 'delta_dn_onorm_g': 'delta_w', 'delta_sg_ln_g': 'delta_w', 'delta_sg_ln_b': 'delta_w', 'delta_sg_w': 'delta_w', 'delta_sg_b': 'delta_w', 'delta_w_branch_a': 'delta_w', 'delta_w_branch_b': 'delta_w', 'delta_w_out': 'delta_w', 'delta_norm2_g': 'delta_w', 'delta_ffn_w_gate': 'delta_w', 'delta_ffn_w_up': 'delta_w', 'delta_ffn_conv_w': 'delta_w', 'delta_ffn_conv_b': 'delta_w', 'delta_ffn_w_down': 'delta_w', 'delta_final_norm_g': 'delta_w', 'new_m_norm1_g': 'new_m', 'new_m_w_in': 'new_m', 'new_m_dn_conv_w': 'new_m', 'new_m_dn_a_log': 'new_m', 'new_m_dn_dt_bias': 'new_m', 'new_m_dn_onorm_g': 'new_m', 'new_m_sg_ln_g': 'new_m', 'new_m_sg_ln_b': 'new_m', 'new_m_sg_w': 'new_m', 'new_m_sg_b': 'new_m', 'new_m_w_branch_a': 'new_m', 'new_m_w_branch_b': 'new_m', 'new_m_w_out': 'new_m', 'new_m_norm2_g': 'new_m', 'new_m_ffn_w_gate': 'new_m', 'new_m_ffn_w_up': 'new_m', 'new_m_ffn_conv_w': 'new_m', 'new_m_ffn_conv_b': 'new_m', 'new_m_ffn_w_down': 'new_m', 'new_m_final_norm_g': 'new_m', 'new_v_norm1_g': 'new_v', 'new_v_w_in': 'new_v', 'new_v_dn_conv_w': 'new_v', 'new_v_dn_a_log': 'new_v', 'new_v_dn_dt_bias': 'new_v', 'new_v_dn_onorm_g': 'new_v', 'new_v_sg_ln_g': 'new_v', 'new_v_sg_ln_b': 'new_v', 'new_v_sg_w': 'new_v', 'new_v_sg_b': 'new_v', 'new_v_w_branch_a': 'new_v', 'new_v_w_branch_b': 'new_v', 'new_v_w_out': 'new_v', 'new_v_norm2_g': 'new_v', 'new_v_ffn_w_gate': 'new_v', 'new_v_ffn_w_up': 'new_v', 'new_v_ffn_conv_w': 'new_v', 'new_v_ffn_conv_b': 'new_v', 'new_v_ffn_w_down': 'new_v', 'new_v_final_norm_g': 'new_v'}


def _forward(args):
    return _fwd_reference(*[args[k] for k in FWD_PARAMS])


def _output_shape():
    def fwd():
        inp = _fwd_setup_inputs(0)
        return _fwd_reference(*[inp[k] for k in FWD_PARAMS])
    out = _jax.eval_shape(fwd)
    return out.shape, out.dtype

N_MICROBATCH = 1
ADAM_LR = 0.001
ADAM_B1 = 0.9
ADAM_B2 = 0.999
ADAM_EPS = 1e-08
ADAM_WD = 0.01
ADAM_STEP = 10
PER_EXAMPLE_BATCH_AXIS = {'x': 0, 'loss_target': 0}
SHARED_INPUTS = []
_WEIGHT_DTYPES = {'norm1_g': _jnp.float32, 'w_in': _jnp.float32, 'dn_conv_w': _jnp.float32, 'dn_a_log': _jnp.float32, 'dn_dt_bias': _jnp.float32, 'dn_onorm_g': _jnp.float32, 'sg_ln_g': _jnp.float32, 'sg_ln_b': _jnp.float32, 'sg_w': _jnp.float32, 'sg_b': _jnp.float32, 'w_branch_a': _jnp.float32, 'w_branch_b': _jnp.float32, 'w_out': _jnp.float32, 'norm2_g': _jnp.float32, 'ffn_w_gate': _jnp.float32, 'ffn_w_up': _jnp.float32, 'ffn_conv_w': _jnp.float32, 'ffn_conv_b': _jnp.float32, 'ffn_w_down': _jnp.float32, 'final_norm_g': _jnp.float32}
MOMENT_SCALE = {'norm1_g': 6.266646e-02, 'w_in': 2.791791e-02, 'dn_conv_w': 3.131049e-02, 'dn_a_log': 1.478695e-01, 'dn_dt_bias': 1.393570e-01, 'dn_onorm_g': 1.442272e-01, 'sg_ln_g': 1.563002e-02, 'sg_ln_b': 1.573899e-02, 'sg_w': 3.131238e-02, 'sg_b': 4.230681e-02, 'w_branch_a': 2.813854e-02, 'w_branch_b': 3.403150e-02, 'w_out': 4.411956e-02, 'norm2_g': 6.291538e-02, 'ffn_w_gate': 2.694530e-02, 'ffn_w_up': 2.620282e-02, 'ffn_conv_w': 2.714569e-02, 'ffn_conv_b': 2.583307e-02, 'ffn_w_down': 4.345900e-02, 'final_norm_g': 1.602529e+01}


def _to_microbatches(a, axis):
    t = _jnp.moveaxis(a, axis, 0)
    t = t.reshape((N_MICROBATCH, t.shape[0] // N_MICROBATCH) + t.shape[1:])
    return _jnp.moveaxis(t, 1, axis + 1)


def setup_inputs(seed: int = 0) -> dict:
    inp = _fwd_setup_inputs(seed)
    key = _jax.random.fold_in(_jax.random.key(seed), 7919)
    shape, _ = _output_shape()
    out = dict(inp)
    out["loss_target"] = _jax.random.normal(_jax.random.fold_in(key, 0), shape, _jnp.float32)
    for i, name in enumerate(TWIN_WEIGHTS):
        w = inp[name].astype(_jnp.float32)
        if MOMENT_SCALE is None:
            s = _jnp.sqrt(_jnp.mean(_jnp.square(w)) + 1e-30)
        else:
            s = MOMENT_SCALE[name]
        km, kv = _jax.random.split(_jax.random.fold_in(key, i + 1))
        out[name] = w
        out["m_" + name] = s * _jax.random.normal(km, w.shape, _jnp.float32)
        out["v_" + name] = (s * s) * _jax.random.uniform(kv, w.shape, _jnp.float32, 0.5, 1.5)
    if N_MICROBATCH > 1:
        for name, axis in PER_EXAMPLE_BATCH_AXIS.items():
            out[name] = _to_microbatches(out[name], axis)
    return {'x': out['x'], 'norm1_g': out['norm1_g'], 'w_in': out['w_in'], 'dn_conv_w': out['dn_conv_w'], 'dn_a_log': out['dn_a_log'], 'dn_dt_bias': out['dn_dt_bias'], 'dn_onorm_g': out['dn_onorm_g'], 'sg_ln_g': out['sg_ln_g'], 'sg_ln_b': out['sg_ln_b'], 'sg_w': out['sg_w'], 'sg_b': out['sg_b'], 'w_branch_a': out['w_branch_a'], 'w_branch_b': out['w_branch_b'], 'w_out': out['w_out'], 'norm2_g': out['norm2_g'], 'ffn_w_gate': out['ffn_w_gate'], 'ffn_w_up': out['ffn_w_up'], 'ffn_conv_w': out['ffn_conv_w'], 'ffn_conv_b': out['ffn_conv_b'], 'ffn_w_down': out['ffn_w_down'], 'final_norm_g': out['final_norm_g'], 'loss_target': out['loss_target'], 'm_norm1_g': out['m_norm1_g'], 'm_w_in': out['m_w_in'], 'm_dn_conv_w': out['m_dn_conv_w'], 'm_dn_a_log': out['m_dn_a_log'], 'm_dn_dt_bias': out['m_dn_dt_bias'], 'm_dn_onorm_g': out['m_dn_onorm_g'], 'm_sg_ln_g': out['m_sg_ln_g'], 'm_sg_ln_b': out['m_sg_ln_b'], 'm_sg_w': out['m_sg_w'], 'm_sg_b': out['m_sg_b'], 'm_w_branch_a': out['m_w_branch_a'], 'm_w_branch_b': out['m_w_branch_b'], 'm_w_out': out['m_w_out'], 'm_norm2_g': out['m_norm2_g'], 'm_ffn_w_gate': out['m_ffn_w_gate'], 'm_ffn_w_up': out['m_ffn_w_up'], 'm_ffn_conv_w': out['m_ffn_conv_w'], 'm_ffn_conv_b': out['m_ffn_conv_b'], 'm_ffn_w_down': out['m_ffn_w_down'], 'm_final_norm_g': out['m_final_norm_g'], 'v_norm1_g': out['v_norm1_g'], 'v_w_in': out['v_w_in'], 'v_dn_conv_w': out['v_dn_conv_w'], 'v_dn_a_log': out['v_dn_a_log'], 'v_dn_dt_bias': out['v_dn_dt_bias'], 'v_dn_onorm_g': out['v_dn_onorm_g'], 'v_sg_ln_g': out['v_sg_ln_g'], 'v_sg_ln_b': out['v_sg_ln_b'], 'v_sg_w': out['v_sg_w'], 'v_sg_b': out['v_sg_b'], 'v_w_branch_a': out['v_w_branch_a'], 'v_w_branch_b': out['v_w_branch_b'], 'v_w_out': out['v_w_out'], 'v_norm2_g': out['v_norm2_g'], 'v_ffn_w_gate': out['v_ffn_w_gate'], 'v_ffn_w_up': out['v_ffn_w_up'], 'v_ffn_conv_w': out['v_ffn_conv_w'], 'v_ffn_conv_b': out['v_ffn_conv_b'], 'v_ffn_w_down': out['v_ffn_w_down'], 'v_final_norm_g': out['v_final_norm_g']}


def _loss(weights, diff, rest, loss_target):
    with _jax.named_scope("forward"):
        args = {**rest, TWIN_DIFF_INPUT: diff, **{k: w.astype(_WEIGHT_DTYPES[k]) for k, w in weights.items()}}
        y = _forward(args)
    with _jax.named_scope("loss_head"):
        err = _jnp.square(y.astype(_jnp.float32) - loss_target)
        return 0.5 * _jnp.sum(_jnp.mean(err, axis=-1)) if err.ndim else 0.5 * err


def _adamw(w, g, m, v):
    m = ADAM_B1 * m + (1.0 - ADAM_B1) * g
    v = ADAM_B2 * v + (1.0 - ADAM_B2) * _jnp.square(g)
    m_hat = m / (1.0 - ADAM_B1 ** ADAM_STEP)
    v_hat = v / (1.0 - ADAM_B2 ** ADAM_STEP)
    delta = -ADAM_LR * (m_hat / (_jnp.sqrt(v_hat) + ADAM_EPS) + ADAM_WD * w)
    return delta, m, v


def reference(x, norm1_g, w_in, dn_conv_w, dn_a_log, dn_dt_bias, dn_onorm_g, sg_ln_g, sg_ln_b, sg_w, sg_b, w_branch_a, w_branch_b, w_out, norm2_g, ffn_w_gate, ffn_w_up, ffn_conv_w, ffn_conv_b, ffn_w_down, final_norm_g, loss_target, m_norm1_g, m_w_in, m_dn_conv_w, m_dn_a_log, m_dn_dt_bias, m_dn_onorm_g, m_sg_ln_g, m_sg_ln_b, m_sg_w, m_sg_b, m_w_branch_a, m_w_branch_b, m_w_out, m_norm2_g, m_ffn_w_gate, m_ffn_w_up, m_ffn_conv_w, m_ffn_conv_b, m_ffn_w_down, m_final_norm_g, v_norm1_g, v_w_in, v_dn_conv_w, v_dn_a_log, v_dn_dt_bias, v_dn_onorm_g, v_sg_ln_g, v_sg_ln_b, v_sg_w, v_sg_b, v_w_branch_a, v_w_branch_b, v_w_out, v_norm2_g, v_ffn_w_gate, v_ffn_w_up, v_ffn_conv_w, v_ffn_conv_b, v_ffn_w_down, v_final_norm_g):
    given = dict(x=x, norm1_g=norm1_g, w_in=w_in, dn_conv_w=dn_conv_w, dn_a_log=dn_a_log, dn_dt_bias=dn_dt_bias, dn_onorm_g=dn_onorm_g, sg_ln_g=sg_ln_g, sg_ln_b=sg_ln_b, sg_w=sg_w, sg_b=sg_b, w_branch_a=w_branch_a, w_branch_b=w_branch_b, w_out=w_out, norm2_g=norm2_g, ffn_w_gate=ffn_w_gate, ffn_w_up=ffn_w_up, ffn_conv_w=ffn_conv_w, ffn_conv_b=ffn_conv_b, ffn_w_down=ffn_w_down, final_norm_g=final_norm_g, loss_target=loss_target, m_norm1_g=m_norm1_g, m_w_in=m_w_in, m_dn_conv_w=m_dn_conv_w, m_dn_a_log=m_dn_a_log, m_dn_dt_bias=m_dn_dt_bias, m_dn_onorm_g=m_dn_onorm_g, m_sg_ln_g=m_sg_ln_g, m_sg_ln_b=m_sg_ln_b, m_sg_w=m_sg_w, m_sg_b=m_sg_b, m_w_branch_a=m_w_branch_a, m_w_branch_b=m_w_branch_b, m_w_out=m_w_out, m_norm2_g=m_norm2_g, m_ffn_w_gate=m_ffn_w_gate, m_ffn_w_up=m_ffn_w_up, m_ffn_conv_w=m_ffn_conv_w, m_ffn_conv_b=m_ffn_conv_b, m_ffn_w_down=m_ffn_w_down, m_final_norm_g=m_final_norm_g, v_norm1_g=v_norm1_g, v_w_in=v_w_in, v_dn_conv_w=v_dn_conv_w, v_dn_a_log=v_dn_a_log, v_dn_dt_bias=v_dn_dt_bias, v_dn_onorm_g=v_dn_onorm_g, v_sg_ln_g=v_sg_ln_g, v_sg_ln_b=v_sg_ln_b, v_sg_w=v_sg_w, v_sg_b=v_sg_b, v_w_branch_a=v_w_branch_a, v_w_branch_b=v_w_branch_b, v_w_out=v_w_out, v_norm2_g=v_norm2_g, v_ffn_w_gate=v_ffn_w_gate, v_ffn_w_up=v_ffn_w_up, v_ffn_conv_w=v_ffn_conv_w, v_ffn_conv_b=v_ffn_conv_b, v_ffn_w_down=v_ffn_w_down, v_final_norm_g=v_final_norm_g)
    weights = {n: given[n] for n in TWIN_WEIGHTS}
    shared = {n: given[n] for n in SHARED_INPUTS}
    per_example = {n: given[n] for n in ['x']}
    grad_fn = _jax.value_and_grad(_loss, argnums=(0, 1))

    def one_microbatch(ex, loss_target):
        ex = dict(ex)
        diff = ex.pop(TWIN_DIFF_INPUT)
        return grad_fn(weights, diff, {**shared, **ex}, loss_target)

    if N_MICROBATCH == 1:
        loss, (grad_w, grad_x) = one_microbatch(per_example, given["loss_target"])
    else:
        def body(carry, xs):
            loss_sum, grad_sum = carry
            l_k, (gw_k, gx_k) = one_microbatch(xs[0], xs[1])
            with _jax.named_scope("update"):
                return (loss_sum + l_k, _jax.tree.map(_jnp.add, grad_sum, gw_k)), gx_k

        init = (_jnp.zeros((), _jnp.float32), _jax.tree.map(_jnp.zeros_like, weights))
        (loss, grad_w), grad_x = _jax.lax.scan(body, init, (per_example, given["loss_target"]))
    with _jax.named_scope("update"):
        delta_w, new_m, new_v = {}, {}, {}
        for n in TWIN_WEIGHTS:
            delta_w[n], new_m[n], new_v[n] = _adamw(weights[n], grad_w[n], given["m_" + n], given["v_" + n])
    return (loss, grad_x, *[grad_w[n] for n in TWIN_WEIGHTS], *[delta_w[n] for n in TWIN_WEIGHTS],
            *[new_m[n] for n in TWIN_WEIGHTS], *[new_v[n] for n in TWIN_WEIGHTS])
```

```python
import functools
import math

import jax
import jax.numpy as jnp
from jax import lax
from jax.experimental import pallas as pl
from jax.experimental.pallas import tpu as pltpu

F32 = jnp.float32
BF16 = jnp.bfloat16
EPS = 1e-6
N_DEV = 8
LANES = 128
SUBLANES = 8
HEAD_DIM = 128
DN_CHUNK = 64
SG_CHUNK = 128
VMEM_LIMIT = 56 * 1024 * 1024
MESH = pl.DeviceIdType.MESH
HIGHEST = lax.Precision.HIGHEST

ADAM_LR = 0.001
ADAM_B1 = 0.9
ADAM_B2 = 0.999
ADAM_EPS = 1e-08
ADAM_WD = 0.01
ADAM_STEP = 10


def _pick(n, target, mult=LANES):
    best = None
    d = mult
    while d <= min(n, target):
        if n % d == 0:
            best = d
        d += mult
    return n if best is None else best


def _params(sem):
    return pltpu.CompilerParams(dimension_semantics=sem, vmem_limit_bytes=VMEM_LIMIT)


_NN = (((1,), (0,)), ((), ()))
_NT = (((1,), (1,)), ((), ()))
_TN = (((0,), (0,)), ((), ()))


def _dg(a, b, dims, hi):
    if hi:
        return lax.dot_general(a.astype(F32), b.astype(F32), dims, precision=HIGHEST, preferred_element_type=F32)
    return lax.dot_general(a.astype(BF16), b.astype(BF16), dims, preferred_element_type=F32)


@functools.partial(jax.custom_vjp, nondiff_argnums=(2,))
def mm_nn(a, b, hi=False):
    return _dg(a, b, _NN, hi)


def _mm_nn_f(a, b, hi):
    return _dg(a, b, _NN, hi), (a, b)


def _mm_nn_b(hi, res, g):
    a, b = res
    return mm_nt(g, b, hi), mm_tn(a, g, hi)


@functools.partial(jax.custom_vjp, nondiff_argnums=(2,))
def mm_nt(a, b, hi=False):
    return _dg(a, b, _NT, hi)


def _mm_nt_f(a, b, hi):
    return _dg(a, b, _NT, hi), (a, b)


def _mm_nt_b(hi, res, g):
    a, b = res
    return mm_nn(g, b, hi), mm_tn(g, a, hi)


@functools.partial(jax.custom_vjp, nondiff_argnums=(2,))
def mm_tn(a, b, hi=False):
    return _dg(a, b, _TN, hi)


def _mm_tn_f(a, b, hi):
    return _dg(a, b, _TN, hi), (a, b)


def _mm_tn_b(hi, res, g):
    a, b = res
    return mm_nt(b, g, hi), mm_nn(a, g, hi)


mm_nn.defvjp(_mm_nn_f, _mm_nn_b)
mm_nt.defvjp(_mm_nt_f, _mm_nt_b)
mm_tn.defvjp(_mm_tn_f, _mm_tn_b)


def matmul(a, b, mode, name, c=None, out_dtype=F32, tm=1024, tn=1024, tk=1024):
    if mode == "nn":
        (M, K), N = a.shape, b.shape[1]
    elif mode == "nt":
        (M, K), N = a.shape, b.shape[0]
    else:
        (K, M), N = a.shape, b.shape[1]
    tm, tn, tk = _pick(M, tm), _pick(N, tn), _pick(K, tk)
    nk = K // tk
    dims = {"nn": _NN, "nt": _NT, "tn": _TN}[mode]
    a_spec = pl.BlockSpec((tk, tm), lambda i, j, k: (k, i)) if mode == "tn" else pl.BlockSpec((tm, tk), lambda i, j, k: (i, k))
    b_spec = pl.BlockSpec((tn, tk), lambda i, j, k: (j, k)) if mode == "nt" else pl.BlockSpec((tk, tn), lambda i, j, k: (k, j))
    o_spec = pl.BlockSpec((tm, tn), lambda i, j, k: (i, j))
    has_c = c is not None

    def body(*refs):
        if has_c:
            a_ref, b_ref, c_ref, o_ref, acc_ref = refs
        else:
            a_ref, b_ref, o_ref, acc_ref = refs
        k = pl.program_id(2)
        part = lax.dot_general(a_ref[...].astype(BF16), b_ref[...].astype(BF16), dims, preferred_element_type=F32)

        @pl.when(k == 0)
        def _():
            acc_ref[...] = part

        @pl.when(k > 0)
        def _():
            acc_ref[...] += part

        @pl.when(k == nk - 1)
        def _():
            r = acc_ref[...]
            if has_c:
                r = r + c_ref[...].astype(F32)
            o_ref[...] = r.astype(o_ref.dtype)

    ins = [a, b] + ([c] if has_c else [])
    specs = [a_spec, b_spec] + ([o_spec] if has_c else [])
    return pl.pallas_call(
        body, name=name, grid=(M // tm, N // tn, nk), in_specs=specs, out_specs=o_spec,
        out_shape=jax.ShapeDtypeStruct((M, N), out_dtype), scratch_shapes=[pltpu.VMEM((tm, tn), F32)],
        compiler_params=_params(("parallel", "parallel", "arbitrary")),
    )(*ins)


def rowcall(name, fn, ins, in_specs, outs, out_specs, acc, nrow, ncol=1):
    n_in = len(ins)

    def body(*refs):
        i = pl.program_id(1)
        res = fn(i, *[r[...] for r in refs[:n_in]])
        for r, v, is_acc in zip(refs[n_in:], res, acc):
            if is_acc:
                @pl.when(i == 0)
                def _(r=r, v=v):
                    r[...] = v.astype(r.dtype)

                @pl.when(i > 0)
                def _(r=r, v=v):
                    r[...] += v.astype(r.dtype)
            else:
                r[...] = v.astype(r.dtype)

    return pl.pallas_call(
        body, name=name, grid=(ncol, nrow), in_specs=list(in_specs), out_specs=list(out_specs), out_shape=list(outs),
        compiler_params=_params(("parallel", "arbitrary")),
    )(*ins)


class Tiles:
    def __init__(self, T, tm):
        self.T, self.tm, self.n = T, tm, T // tm
        self.r8 = tm // SUBLANES

    def row(self, w, cb=0):
        return pl.BlockSpec((self.tm, w), lambda j, i: (i, cb))

    def rowj(self, tc):
        return pl.BlockSpec((self.tm, tc), lambda j, i: (i, j))

    def prev(self, w, cb=0):
        return pl.BlockSpec((SUBLANES, w), lambda j, i: (jnp.maximum(i * self.r8 - 1, 0), cb))

    def prevj(self, tc):
        return pl.BlockSpec((SUBLANES, tc), lambda j, i: (jnp.maximum(i * self.r8 - 1, 0), j))

    def nxt(self, w, cb=0):
        last = self.T // SUBLANES - 1
        return pl.BlockSpec((SUBLANES, w), lambda j, i: (jnp.minimum((i + 1) * self.r8, last), cb))

    def nxtj(self, tc):
        last = self.T // SUBLANES - 1
        return pl.BlockSpec((SUBLANES, tc), lambda j, i: (jnp.minimum((i + 1) * self.r8, last), j))

    def heads(self, H):
        return pl.BlockSpec((H, self.tm, HEAD_DIM), lambda j, i: (0, i, 0))

    def heads_nxt(self, H):
        last = self.T // SUBLANES - 1
        return pl.BlockSpec((H, SUBLANES, HEAD_DIM), lambda j, i: (0, jnp.minimum((i + 1) * self.r8, last), 0))


def full(shape):
    return pl.BlockSpec(tuple(shape), lambda j, i: (0,) * len(shape))


def constj(r, tc):
    return pl.BlockSpec((r, tc), lambda j, i: (0, j))


def sds(shape, dtype=F32):
    return jax.ShapeDtypeStruct(tuple(shape), dtype)


def rms(x, g):
    return x * lax.rsqrt(jnp.mean(x * x, axis=-1, keepdims=True) + EPS) * g


def silu(x):
    return x * jax.nn.sigmoid(x)


def gelu(x):
    return 0.5 * x * (1.0 + lax.erf(x * (2.0 ** -0.5)))


def causal_conv(xwin, w, K, R):
    base = SUBLANES - (K - 1)
    out = w[0:1, :] * xwin[base:base + R, :]
    for j in range(1, K):
        out = out + w[j:j + 1, :] * xwin[base + j:base + j + R, :]
    return out


def rows_to8(rows, C):
    rid = lax.broadcasted_iota(jnp.int32, (SUBLANES, C), 0)
    out = jnp.zeros((SUBLANES, C), F32)
    for k, r in enumerate(rows):
        out = out + jnp.where(rid == k, jnp.broadcast_to(r, (SUBLANES, C)), 0.0)
    return out


def dn_qkv(pre, H):
    a = silu(pre)
    W = H * HEAD_DIM

    def l2(t):
        return t * lax.rsqrt(jnp.sum(t * t, axis=-1, keepdims=True) + EPS)

    q = [l2(a[:, h * HEAD_DIM:(h + 1) * HEAD_DIM]) for h in range(H)]
    k = [l2(a[:, W + h * HEAD_DIM:W + (h + 1) * HEAD_DIM]) for h in range(H)]
    v = [a[:, 2 * W + h * HEAD_DIM:2 * W + (h + 1) * HEAD_DIM] for h in range(H)]
    return q, k, v


def dn_gates(ba, alog, dtb, H, R):
    lane = lax.broadcasted_iota(jnp.int32, (R, LANES), 1)
    beta = jax.nn.sigmoid(ba)
    g = -jnp.exp(alog) * jax.nn.softplus(ba + dtb)
    g = jnp.where((lane >= H) & (lane < 2 * H), g, 0.0)
    ri = lax.broadcasted_iota(jnp.int32, (R, R), 0)
    ci = lax.broadcasted_iota(jnp.int32, (R, R), 1)
    cum = jnp.where((ri // DN_CHUNK == ci // DN_CHUNK) & (ci <= ri), 1.0, 0.0).astype(F32)
    gc = mm_nn(cum, g, True)
    return jnp.where(lane < H, beta, gc)


def neumann_inverse(L):
    C = L.shape[0]
    ri = lax.broadcasted_iota(jnp.int32, (C, C), 0)
    ci = lax.broadcasted_iota(jnp.int32, (C, C), 1)
    P = -L
    R = jnp.where(ri == ci, 1.0, 0.0).astype(F32) + P
    for _ in range(int(math.log2(C)) - 1):
        P = mm_nn(P, P, True)
        R = R + mm_nn(R, P, True)
    return R


def dn_chunk(q, k, v, beta, gc, gr):
    C = q.shape[0]
    ri = lax.broadcasted_iota(jnp.int32, (C, C), 0)
    ci = lax.broadcasted_iota(jnp.int32, (C, C), 1)
    qs = q * (HEAD_DIM ** -0.5)
    kb = k * beta
    vb = v * beta
    decay = jnp.exp(jnp.where(ri >= ci, gc - gr, -jnp.inf))
    L = jnp.where(ri > ci, mm_nt(kb, k) * decay, 0.0)
    Tinv = neumann_inverse(L)
    eg = jnp.exp(gc)
    u = mm_nn(Tinv, vb)
    w = mm_nn(Tinv, kb * eg)
    attn = jnp.where(ri >= ci, mm_nt(qs, k) * decay, 0.0)
    qd = qs * eg
    gl = gc[C - 1:C, :]
    kd = k * jnp.exp(gl - gc)
    return u, w, attn, qd, kd, gl


def dn_step(u, w, a, qd, kd, gl, S):
    v_new = u - mm_nn(w, S)
    o = mm_nn(qd, S) + mm_nn(a, v_new)
    S_new = S * jnp.exp(gl) + mm_tn(kd, v_new)
    return o, S_new


def dn_post(o, z, g):
    H = o.shape[0]
    return jnp.concatenate([rms(o[h], g) * silu(z[:, h * HEAD_DIM:(h + 1) * HEAD_DIM]) for h in range(H)], axis=1)


def gmlp(u_raw, v_raw, ln_g, ln_b, sgw, sgbT):
    R = u_raw.shape[0]
    G = sgw.shape[0]
    nc = R // SG_CHUNK
    u = gelu(u_raw)
    vv = gelu(v_raw)
    xc = vv - jnp.mean(vv, axis=-1, keepdims=True)
    vg = xc * lax.rsqrt(jnp.mean(xc * xc, axis=-1, keepdims=True) + EPS) * ln_g + ln_b
    ri = lax.broadcasted_iota(jnp.int32, (SG_CHUNK, SG_CHUNK), 0)
    ci = lax.broadcasted_iota(jnp.int32, (SG_CHUNK, SG_CHUNK), 1)
    cols = []
    for g in range(G):
        ws = jnp.where(ri >= ci, sgw[g], 0.0)
        rhs = jnp.concatenate([vg[c * SG_CHUNK:(c + 1) * SG_CHUNK, g * HEAD_DIM:(g + 1) * HEAD_DIM] for c in range(nc)], axis=1)
        mixed = mm_nn(ws, rhs) + sgbT[:, g:g + 1]
        cols.append(jnp.concatenate([mixed[:, c * HEAD_DIM:(c + 1) * HEAD_DIM] for c in range(nc)], axis=0))
    return u * jnp.concatenate(cols, axis=1)


def merge(ga, gb, ap, bp):
    return jax.nn.sigmoid(ga) * ap + jax.nn.sigmoid(gb) * bp


def norm_fwd(x, g, name, tm=256):
    T, D = x.shape
    tl = Tiles(T, _pick(T, tm))
    (h,) = rowcall(name, lambda i, x, g: (rms(x, g),), [x, g], [tl.row(D), full((1, D))],
                   [sds((T, D), BF16)], [tl.row(D)], [False], tl.n)
    return h


def norm_bwd(x, g, dh, dres, name, tm=256):
    T, D = x.shape
    tl = Tiles(T, _pick(T, tm))

    def fn(i, x, g, dh, dres):
        _, vj = jax.vjp(rms, x, g)
        dx, dg = vj(dh.astype(F32))
        return dx + dres, dg

    return rowcall(name, fn, [x, g, dh, dres], [tl.row(D), full((1, D)), tl.row(D), tl.row(D)],
                   [sds((T, D)), sds((1, D))], [tl.row(D), full((1, D))], [False, True], tl.n)


def head_fwd_bwd(x, g, tgt, name, tm=256):
    T, D = x.shape
    tl = Tiles(T, _pick(T, tm))

    def fn(i, x, g, tgt):
        y, vj = jax.vjp(rms, x, g)
        e = y - tgt
        loss = 0.5 * jnp.sum(jnp.mean(e * e, axis=-1, keepdims=True), axis=0, keepdims=True)
        dx, dg = vj(e * (1.0 / D))
        return loss, dx, dg

    return rowcall(name, fn, [x, g, tgt], [tl.row(D), full((1, D)), tl.row(D)],
                   [sds((1, 1)), sds((T, D)), sds((1, D))], [full((1, 1)), tl.row(D), full((1, D))],
                   [True, False, True], tl.n)


def dn_prep_fwd(projA, pba, cw8, alog, dtb, H, name, tm=256):
    T = projA.shape[0]
    W3 = 3 * H * HEAD_DIM
    tl = Tiles(T, _pick(T, tm, DN_CHUNK))
    R = tl.tm

    def fn(i, xp, x, ba, cw, alog, dtb):
        xwin = jnp.concatenate([jnp.where(i > 0, xp, 0.0), x], axis=0)
        q, k, v = dn_qkv(causal_conv(xwin, cw, 4, R), H)
        return jnp.stack(q), jnp.stack(k), jnp.stack(v), dn_gates(ba, alog, dtb, H, R)

    hs = sds((H, T, HEAD_DIM))
    return rowcall(name, fn, [projA, projA, pba, cw8, alog, dtb],
                   [tl.prev(W3), tl.row(W3), tl.row(LANES), full((SUBLANES, W3)), full((1, LANES)), full((1, LANES))],
                   [hs, hs, hs, sds((T, LANES))], [tl.heads(H)] * 3 + [tl.row(LANES)], [False] * 4, tl.n)


def dn_prep_bwd(projA, pba, cw8, alog, dtb, dq, dk, dv, dbg1, dbg2, H, name, tm=256):
    T = projA.shape[0]
    W3 = 3 * H * HEAD_DIM
    tl = Tiles(T, _pick(T, tm, DN_CHUNK))
    R = tl.tm
    RE = R + SUBLANES

    def fn(i, xp, x, xn, ba, cw, alog, dtb, dq, dk, dv, dqn, dkn, dvn, dbg1, dbg2):
        last = i == tl.n - 1
        xwin = jnp.concatenate([jnp.where(i > 0, xp, 0.0), x, jnp.where(last, 0.0, xn)], axis=0)
        pre = causal_conv(xwin, cw, 4, RE)
        ext = lambda d, dn: [jnp.concatenate([d[h], jnp.where(last, 0.0, dn[h])], axis=0) for h in range(H)]
        _, vj = jax.vjp(lambda p: dn_qkv(p, H), pre)
        (dpre,) = vj((ext(dq, dqn), ext(dk, dkn), ext(dv, dvn)))
        dx = cw[0:1, :] * dpre[3:3 + R, :]
        for j in range(1, 4):
            dx = dx + cw[j:j + 1, :] * dpre[3 - j:3 - j + R, :]
        dcw = rows_to8([jnp.sum(dpre[0:R, :] * xwin[5 + j:5 + j + R, :], axis=0, keepdims=True) for j in range(4)], W3)
        _, vjg = jax.vjp(lambda ba, alog, dtb: dn_gates(ba, alog, dtb, H, R), ba, alog, dtb)
        dba, dalog, ddtb = vjg(dbg1 + dbg2)
        return dx, dba, dcw, dalog, ddtb

    return rowcall(name, fn, [projA, projA, projA, pba, cw8, alog, dtb, dq, dk, dv, dq, dk, dv, dbg1, dbg2],
                   [tl.prev(W3), tl.row(W3), tl.nxt(W3), tl.row(LANES), full((SUBLANES, W3)), full((1, LANES)), full((1, LANES))]
                   + [tl.heads(H)] * 3 + [tl.heads_nxt(H)] * 3 + [tl.row(LANES)] * 2,
                   [sds((T, W3), BF16), sds((T, LANES), BF16), sds((SUBLANES, W3)), sds((1, LANES)), sds((1, LANES))],
                   [tl.row(W3), tl.row(LANES), full((SUBLANES, W3)), full((1, LANES)), full((1, LANES))],
                   [False, False, True, True, True], tl.n)


def _chunk_specs(H, C):
    hs = pl.BlockSpec((H, C, HEAD_DIM), lambda n: (0, n, 0))
    col = pl.BlockSpec((H, 1, C, 1), lambda n: (0, n, 0, 0))
    rw = pl.BlockSpec((H, 1, 1, C), lambda n: (0, n, 0, 0))
    at = pl.BlockSpec((H, C, C), lambda n: (0, n, 0))
    one = pl.BlockSpec((H, 1, 1, 1), lambda n: (0, n, 0, 0))
    return hs, col, rw, at, one


def dn_chunk_fwd(q, k, v, beta_c, gc_c, gc_r, name):
    H, T, _ = q.shape
    C = DN_CHUNK
    N = T // C
    hs, col, rw, at, one = _chunk_specs(H, C)

    def body(q, k, v, bc, gc, gr, u, w, a, qd, kd, gl):
        for h in range(H):
            res = dn_chunk(q[h], k[h], v[h], bc[h, 0], gc[h, 0], gr[h, 0])
            for ref, val in zip((u, w, a, qd, kd), res[:5]):
                ref[h] = val
            gl[h, 0] = res[5]

    big = sds((H, T, HEAD_DIM))
    return pl.pallas_call(
        body, name=name, grid=(N,), in_specs=[hs, hs, hs, col, col, rw], out_specs=[hs, hs, at, hs, hs, one],
        out_shape=[big, big, sds((H, T, C)), big, big, sds((H, N, 1, 1))], compiler_params=_params(("parallel",)),
    )(q, k, v, beta_c, gc_c, gc_r)


def dn_chunk_bwd(q, k, v, beta_c, gc_c, gc_r, du, dw, da, dqd, dkd, dgl, name):
    H, T, _ = q.shape
    C = DN_CHUNK
    N = T // C
    hs, col, rw, at, one = _chunk_specs(H, C)

    def body(q, k, v, bc, gc, gr, du, dw, da, dqd, dkd, dgl, dq, dk, dv, dbc, dgc, dgr):
        for h in range(H):
            _, vj = jax.vjp(dn_chunk, q[h], k[h], v[h], bc[h, 0], gc[h, 0], gr[h, 0])
            res = vj((du[h], dw[h], da[h], dqd[h], dkd[h], dgl[h, 0]))
            dq[h], dk[h], dv[h] = res[0], res[1], res[2]
            dbc[h, 0], dgc[h, 0], dgr[h, 0] = res[3], res[4], res[5]

    big = sds((H, T, HEAD_DIM))
    return pl.pallas_call(
        body, name=name, grid=(N,), in_specs=[hs, hs, hs, col, col, rw, hs, hs, at, hs, hs, one],
        out_specs=[hs, hs, hs, col, col, rw],
        out_shape=[big, big, big, sds((H, N, C, 1)), sds((H, N, C, 1)), sds((H, N, 1, C))],
        compiler_params=_params(("parallel",)),
    )(q, k, v, beta_c, gc_c, gc_r, du, dw, da, dqd, dkd, dgl)


def dn_scan_fwd(u, w, a, qd, kd, gl, name):
    H, T, _ = u.shape
    C = DN_CHUNK
    N = T // C
    hs, _, _, at, one = _chunk_specs(H, C)
    st = pl.BlockSpec((1, H, HEAD_DIM, HEAD_DIM), lambda n: (n, 0, 0, 0))

    def body(u, w, a, qd, kd, gl, o, s_in, S):
        @pl.when(pl.program_id(0) == 0)
        def _():
            S[...] = jnp.zeros_like(S)

        for h in range(H):
            s = S[h]
            s_in[0, h] = s
            o[h], S[h] = dn_step(u[h], w[h], a[h], qd[h], kd[h], gl[h, 0], s)

    return pl.pallas_call(
        body, name=name, grid=(N,), in_specs=[hs, hs, at, hs, hs, one], out_specs=[hs, st],
        out_shape=[sds((H, T, HEAD_DIM)), sds((N, H, HEAD_DIM, HEAD_DIM))],
        scratch_shapes=[pltpu.VMEM((H, HEAD_DIM, HEAD_DIM), F32)], compiler_params=_params(("arbitrary",)),
    )(u, w, a, qd, kd, gl)


def dn_scan_bwd(u, w, a, qd, kd, gl, s_in, do, name):
    H, T, _ = u.shape
    C = DN_CHUNK
    N = T // C
    rev = lambda spec_shape, f: pl.BlockSpec(spec_shape, f)
    hs = rev((H, C, HEAD_DIM), lambda n: (0, N - 1 - n, 0))
    at = rev((H, C, C), lambda n: (0, N - 1 - n, 0))
    one = rev((H, 1, 1, 1), lambda n: (0, N - 1 - n, 0, 0))
    st = rev((1, H, HEAD_DIM, HEAD_DIM), lambda n: (N - 1 - n, 0, 0, 0))

    def body(u, w, a, qd, kd, gl, s_in, do, du, dw, da, dqd, dkd, dgl, dS):
        @pl.when(pl.program_id(0) == 0)
        def _():
            dS[...] = jnp.zeros_like(dS)

        for h in range(H):
            _, vj = jax.vjp(dn_step, u[h], w[h], a[h], qd[h], kd[h], gl[h, 0], s_in[0, h])
            res = vj((do[h], dS[h]))
            du[h], dw[h], da[h], dqd[h], dkd[h] = res[:5]
            dgl[h, 0] = res[5]
            dS[h] = res[6]

    big = sds((H, T, HEAD_DIM))
    return pl.pallas_call(
        body, name=name, grid=(N,), in_specs=[hs, hs, at, hs, hs, one, st, hs], out_specs=[hs, hs, at, hs, hs, one],
        out_shape=[big, big, sds((H, T, C)), big, big, sds((H, N, 1, 1))],
        scratch_shapes=[pltpu.VMEM((H, HEAD_DIM, HEAD_DIM), F32)], compiler_params=_params(("arbitrary",)),
    )(u, w, a, qd, kd, gl, s_in, do)


def dn_post_fwd(o, projA, g, name, tm=256):
    H, T, _ = o.shape
    W = H * HEAD_DIM
    tl = Tiles(T, _pick(T, tm))
    (y,) = rowcall(name, lambda i, o, z, g: (dn_post(o, z, g),), [o, projA, g], [tl.heads(H), tl.row(W, 3), full((1, HEAD_DIM))],
                   [sds((T, W), BF16)], [tl.row(W)], [False], tl.n)
    return y


def dn_post_bwd(o, projA, g, dy, name, tm=256):
    H, T, _ = o.shape
    W = H * HEAD_DIM
    tl = Tiles(T, _pick(T, tm))

    def fn(i, o, z, g, dy):
        _, vj = jax.vjp(dn_post, o, z, g)
        return vj(dy.astype(F32))

    return rowcall(name, fn, [o, projA, g, dy], [tl.heads(H), tl.row(W, 3), full((1, HEAD_DIM)), tl.row(W)],
                   [sds((H, T, HEAD_DIM)), sds((T, W), BF16), sds((1, HEAD_DIM))],
                   [tl.heads(H), tl.row(W), full((1, HEAD_DIM))], [False, False, True], tl.n)


def gmlp_fwd(projB, ln_g, ln_b, sgw, sgbT, name, tm=512):
    T = projB.shape[0]
    G = sgw.shape[0]
    W = G * HEAD_DIM
    tl = Tiles(T, _pick(T, tm))
    (y,) = rowcall(name, lambda i, *a: (gmlp(*a),), [projB, projB, ln_g, ln_b, sgw, sgbT],
                   [tl.row(W, 0), tl.row(W, 1), full((1, W)), full((1, W)), full(sgw.shape), full(sgbT.shape)],
                   [sds((T, W), BF16)], [tl.row(W)], [False], tl.n)
    return y


def gmlp_bwd(projB, ln_g, ln_b, sgw, sgbT, dy, name, tm=512):
    T = projB.shape[0]
    G = sgw.shape[0]
    W = G * HEAD_DIM
    tl = Tiles(T, _pick(T, tm))

    def fn(i, u_raw, v_raw, ln_g, ln_b, sgw, sgbT, dy):
        _, vj = jax.vjp(gmlp, u_raw, v_raw, ln_g, ln_b, sgw, sgbT)
        return vj(dy.astype(F32))

    return rowcall(name, fn, [projB, projB, ln_g, ln_b, sgw, sgbT, dy],
                   [tl.row(W, 0), tl.row(W, 1), full((1, W)), full((1, W)), full(sgw.shape), full(sgbT.shape), tl.row(W)],
                   [sds((T, W), BF16), sds((T, W), BF16), sds((1, W)), sds((1, W)), sds(sgw.shape), sds(sgbT.shape)],
                   [tl.row(W), tl.row(W), full((1, W)), full((1, W)), full(sgw.shape), full(sgbT.shape)],
                   [False, False, True, True, True, True], tl.n)


def merge_fwd(projB, ap, bp, cb_a, name, tm=256):
    T, D = ap.shape
    tl = Tiles(T, _pick(T, tm))
    (m,) = rowcall(name, lambda i, *a: (merge(*a),), [projB, projB, ap, bp],
                   [tl.row(D, cb_a), tl.row(D, cb_a + 1), tl.row(D), tl.row(D)], [sds((T, D), BF16)], [tl.row(D)], [False], tl.n)
    return m


def merge_bwd(projB, ap, bp, dm, cb_a, name, tm=256):
    T, D = ap.shape
    tl = Tiles(T, _pick(T, tm))

    def fn(i, ga, gb, ap, bp, dm):
        _, vj = jax.vjp(merge, ga, gb, ap, bp)
        return vj(dm.astype(F32))

    return rowcall(name, fn, [projB, projB, ap, bp, dm], [tl.row(D, cb_a), tl.row(D, cb_a + 1), tl.row(D), tl.row(D), tl.row(D)],
                   [sds((T, D), BF16)] * 4, [tl.row(D)] * 4, [False] * 4, tl.n)


def ffn_act_fwd(gp, up, fcw8, fcb, name, tm=256, tc=512):
    T, F = gp.shape
    tl = Tiles(T, _pick(T, tm))
    tc = _pick(F, tc)
    R = tl.tm

    def fn(i, gprev, g, up, cw, cb):
        xwin = jnp.concatenate([jnp.where(i > 0, gprev, 0.0), g], axis=0)
        return (silu(causal_conv(xwin, cw, 3, R) + cb) * up,)

    (act,) = rowcall(name, fn, [gp, gp, up, fcw8, fcb], [tl.prevj(tc), tl.rowj(tc), tl.rowj(tc), constj(SUBLANES, tc), constj(1, tc)],
                     [sds((T, F), BF16)], [tl.rowj(tc)], [False], tl.n, F // tc)
    return act


def ffn_act_bwd(gp, up, fcw8, fcb, dact, name, tm=256, tc=512):
    T, F = gp.shape
    tl = Tiles(T, _pick(T, tm))
    tc = _pick(F, tc)
    R = tl.tm
    RE = R + SUBLANES

    def fn(i, gprev, g, gnext, up, upn, da, dan, cw, cb):
        last = i == tl.n - 1
        xwin = jnp.concatenate([jnp.where(i > 0, gprev, 0.0), g, jnp.where(last, 0.0, gnext)], axis=0)
        gate = causal_conv(xwin, cw, 3, RE) + cb
        upe = jnp.concatenate([up, upn], axis=0)
        dae = jnp.concatenate([da, jnp.where(last, 0.0, dan)], axis=0)
        s = jax.nn.sigmoid(gate)
        dgate = dae * upe * (s * (1.0 + gate * (1.0 - s)))
        dup = da * (gate[0:R, :] * s[0:R, :])
        dgp = cw[0:1, :] * dgate[2:2 + R, :] + cw[1:2, :] * dgate[1:1 + R, :] + cw[2:3, :] * dgate[0:R, :]
        dcw = rows_to8([jnp.sum(dgate[0:R, :] * xwin[6 + j:6 + j + R, :], axis=0, keepdims=True) for j in range(3)], tc)
        dcb = jnp.sum(dgate[0:R, :], axis=0, keepdims=True)
        return dgp, dup, dcw, dcb

    return rowcall(name, fn, [gp, gp, gp, up, up, dact, dact, fcw8, fcb],
                   [tl.prevj(tc), tl.rowj(tc), tl.nxtj(tc), tl.rowj(tc), tl.nxtj(tc), tl.rowj(tc), tl.nxtj(tc),
                    constj(SUBLANES, tc), constj(1, tc)],
                   [sds((T, F), BF16), sds((T, F), BF16), sds((SUBLANES, F)), sds((1, F))],
                   [tl.rowj(tc), tl.rowj(tc), constj(SUBLANES, tc), constj(1, tc)], [False, False, True, True], tl.n, F // tc)


def _me():
    return lax.axis_index("x"), lax.axis_index("y"), lax.axis_index("c")


def all_gather(shards, name):
    nt = len(shards)

    def body(*refs):
        xs, outs = refs[:nt], refs[nt:2 * nt]
        send_sems, recv_sems, local_sems = refs[2 * nt:]
        x, y, c = _me()
        me, sibling = (x, y, c), (x, y, 1 - c)
        chips = [(1 - x, y), (x, 1 - y), (1 - x, 1 - y)]

        def slot(t, p):
            return outs[t].at[4 * p[0] + 2 * p[1] + p[2]]

        def copy(t, k, block, to, src=None):
            return pltpu.make_async_remote_copy(
                src_ref=slot(t, block) if src is None else src, dst_ref=slot(t, block),
                send_sem=send_sems.at[t, k], recv_sem=recv_sems.at[t, k], device_id=to, device_id_type=MESH)

        mine = [pltpu.make_async_copy(xs[t], slot(t, me), local_sems.at[t]) for t in range(nt)]
        first = []
        for t in range(nt):
            mine[t].start()
            first.append(copy(t, 0, me, sibling, src=xs[t]))
            first += [copy(t, 1 + j, me, (*chip, c), src=xs[t]) for j, chip in enumerate(chips)]
        for cp in first:
            cp.start()
        passed = []
        for j, chip in enumerate(chips):
            for t in range(nt):
                copy(t, 1 + j, (*chip, c), me).wait_recv()
                cp = copy(t, 4 + j, (*chip, c), sibling)
                cp.start()
                passed.append(cp)
        for t in range(nt):
            copy(t, 0, sibling, me).wait_recv()
            for j, chip in enumerate(chips):
                copy(t, 4 + j, (*chip, 1 - c), me).wait_recv()
        for cp in first + passed:
            cp.wait_send()
        for t in range(nt):
            mine[t].wait()

    any_spec = pl.BlockSpec(memory_space=pl.ANY)
    return pl.pallas_call(
        body, name=name, in_specs=[any_spec] * nt, out_specs=[any_spec] * nt,
        out_shape=[jax.ShapeDtypeStruct((N_DEV,) + s.shape, s.dtype) for s in shards],
        scratch_shapes=[pltpu.SemaphoreType.DMA((nt, 7)), pltpu.SemaphoreType.DMA((nt, 7)), pltpu.SemaphoreType.DMA((nt,))],
    )(*shards)


def all_to_all(parts, name):
    nt = len(parts)

    def body(*refs):
        xs, outs = refs[:nt], refs[nt:2 * nt]
        send_sems, recv_sems, local_sems = refs[2 * nt:]
        x, y, c = _me()
        my = 4 * x + 2 * y + c
        copies = []
        for t in range(nt):
            own = pltpu.make_async_copy(xs[t].at[my], outs[t].at[my], local_sems.at[t])
            own.start()
            copies.append(own)
        remote = []
        for k in range(1, N_DEV):
            px, py, pc = x ^ (k >> 2), y ^ ((k >> 1) & 1), c ^ (k & 1)
            peer = 4 * px + 2 * py + pc
            for t in range(nt):
                cp = pltpu.make_async_remote_copy(
                    src_ref=xs[t].at[peer], dst_ref=outs[t].at[my], send_sem=send_sems.at[t, k - 1],
                    recv_sem=recv_sems.at[t, k - 1], device_id=(px, py, pc), device_id_type=MESH)
                cp.start()
                remote.append((cp, t, k, peer))
        for cp, t, k, peer in remote:
            pltpu.make_async_remote_copy(
                src_ref=xs[t].at[my], dst_ref=outs[t].at[peer], send_sem=send_sems.at[t, k - 1],
                recv_sem=recv_sems.at[t, k - 1], device_id=(x, y, c), device_id_type=MESH).wait_recv()
        for cp, t, k, peer in remote:
            cp.wait_send()
        for own in copies:
            own.wait()

    any_spec = pl.BlockSpec(memory_space=pl.ANY)
    return pl.pallas_call(
        body, name=name, in_specs=[any_spec] * nt, out_specs=[any_spec] * nt,
        out_shape=[jax.ShapeDtypeStruct(p.shape, p.dtype) for p in parts],
        scratch_shapes=[pltpu.SemaphoreType.DMA((nt, 7)), pltpu.SemaphoreType.DMA((nt, 7)), pltpu.SemaphoreType.DMA((nt,))],
    )(*parts)


def sum_adamw(parts, w, m, v, name, tr=256):
    _, R, C = parts.shape
    tr = _pick(R, tr, SUBLANES)
    c1 = 1.0 - ADAM_B1 ** ADAM_STEP
    c2 = 1.0 - ADAM_B2 ** ADAM_STEP

    def body(p, w, m, v, g_o, d_o, m_o, v_o):
        g = p[0].astype(F32)
        for d in range(1, N_DEV):
            g = g + p[d].astype(F32)
        mn = ADAM_B1 * m[...] + (1.0 - ADAM_B1) * g
        vn = ADAM_B2 * v[...] + (1.0 - ADAM_B2) * (g * g)
        m_hat = mn / c1
        v_hat = vn / c2
        g_o[...] = g
        d_o[...] = -ADAM_LR * (m_hat / (jnp.sqrt(v_hat) + ADAM_EPS) + ADAM_WD * w[...])
        m_o[...] = mn
        v_o[...] = vn

    blk = pl.BlockSpec((tr, C), lambda i: (i, 0))
    return pl.pallas_call(
        body, name=name, grid=(R // tr,), in_specs=[pl.BlockSpec((N_DEV, tr, C), lambda i: (0, i, 0)), blk, blk, blk],
        out_specs=[blk] * 4, out_shape=[sds((R, C))] * 4, compiler_params=_params(("parallel",)),
    )(parts, w, m, v)


SHARDED = ("w_in", "dn_conv_w", "w_branch_a", "w_branch_b", "w_out", "ffn_w_gate", "ffn_w_up", "ffn_conv_w", "ffn_w_down")
COL_SHARDED = ("w_in", "dn_conv_w", "w_branch_a", "w_branch_b", "ffn_w_gate", "ffn_w_up", "ffn_conv_w")
CONV_WEIGHTS = ("dn_conv_w", "ffn_conv_w")
REPLICATED = ("norm1_g", "dn_a_log", "dn_dt_bias", "dn_onorm_g", "sg_ln_g", "sg_ln_b", "sg_w", "sg_b", "norm2_g",
              "ffn_conv_b", "final_norm_g")
WEIGHTS = ("norm1_g", "w_in", "dn_conv_w", "dn_a_log", "dn_dt_bias", "dn_onorm_g", "sg_ln_g", "sg_ln_b", "sg_w", "sg_b",
           "w_branch_a", "w_branch_b", "w_out", "norm2_g", "ffn_w_gate", "ffn_w_up", "ffn_conv_w", "ffn_conv_b",
           "ffn_w_down", "final_norm_g")


def _assemble(name, g):
    if name in COL_SHARDED:
        L, r, cs = g.shape[1:]
        return jnp.transpose(g, (1, 2, 0, 3)).reshape(L, r, N_DEV * cs)
    L, rs, cdim = g.shape[1:]
    return jnp.transpose(g, (1, 0, 2, 3)).reshape(L, N_DEV * rs, cdim)


def _split(name, full_grad):
    if name in COL_SHARDED:
        L, r, cfull = full_grad.shape
        return jnp.transpose(full_grad.reshape(L, r, N_DEV, cfull // N_DEV), (2, 0, 1, 3))
    L, rfull, cdim = full_grad.shape
    return jnp.transpose(full_grad.reshape(L, N_DEV, rfull // N_DEV, cdim), (1, 0, 2, 3))


def _pad_lanes(a, lo, width=LANES):
    return jnp.pad(a, ((0, 0), (lo, width - lo - a.shape[1])))


def _pad_rows(a, rows=SUBLANES):
    return jnp.pad(a, ((0, rows - a.shape[0]), (0, 0)))


def kernel(x, norm1_g, w_in, dn_conv_w, dn_a_log, dn_dt_bias, dn_onorm_g, sg_ln_g, sg_ln_b, sg_w, sg_b, w_branch_a, w_branch_b, w_out, norm2_g, ffn_w_gate, ffn_w_up, ffn_conv_w, ffn_conv_b, ffn_w_down, final_norm_g, loss_target, m_norm1_g, m_w_in, m_dn_conv_w, m_dn_a_log, m_dn_dt_bias, m_dn_onorm_g, m_sg_ln_g, m_sg_ln_b, m_sg_w, m_sg_b, m_w_branch_a, m_w_branch_b, m_w_out, m_norm2_g, m_ffn_w_gate, m_ffn_w_up, m_ffn_conv_w, m_ffn_conv_b, m_ffn_w_down, m_final_norm_g, v_norm1_g, v_w_in, v_dn_conv_w, v_dn_a_log, v_dn_dt_bias, v_dn_onorm_g, v_sg_ln_g, v_sg_ln_b, v_sg_w, v_sg_b, v_w_branch_a, v_w_branch_b, v_w_out, v_norm2_g, v_ffn_w_gate, v_ffn_w_up, v_ffn_conv_w, v_ffn_conv_b, v_ffn_w_down, v_final_norm_g):
    W = dict(norm1_g=norm1_g, w_in=w_in, dn_conv_w=dn_conv_w, dn_a_log=dn_a_log, dn_dt_bias=dn_dt_bias, dn_onorm_g=dn_onorm_g,
             sg_ln_g=sg_ln_g, sg_ln_b=sg_ln_b, sg_w=sg_w, sg_b=sg_b, w_branch_a=w_branch_a, w_branch_b=w_branch_b, w_out=w_out,
             norm2_g=norm2_g, ffn_w_gate=ffn_w_gate, ffn_w_up=ffn_w_up, ffn_conv_w=ffn_conv_w, ffn_conv_b=ffn_conv_b,
             ffn_w_down=ffn_w_down, final_norm_g=final_norm_g)
    Mo = dict(norm1_g=m_norm1_g, w_in=m_w_in, dn_conv_w=m_dn_conv_w, dn_a_log=m_dn_a_log, dn_dt_bias=m_dn_dt_bias,
              dn_onorm_g=m_dn_onorm_g, sg_ln_g=m_sg_ln_g, sg_ln_b=m_sg_ln_b, sg_w=m_sg_w, sg_b=m_sg_b, w_branch_a=m_w_branch_a,
              w_branch_b=m_w_branch_b, w_out=m_w_out, norm2_g=m_norm2_g, ffn_w_gate=m_ffn_w_gate, ffn_w_up=m_ffn_w_up,
              ffn_conv_w=m_ffn_conv_w, ffn_conv_b=m_ffn_conv_b, ffn_w_down=m_ffn_w_down, final_norm_g=m_final_norm_g)
    Vo = dict(norm1_g=v_norm1_g, w_in=v_w_in, dn_conv_w=v_dn_conv_w, dn_a_log=v_dn_a_log, dn_dt_bias=v_dn_dt_bias,
              dn_onorm_g=v_dn_onorm_g, sg_ln_g=v_sg_ln_g, sg_ln_b=v_sg_ln_b, sg_w=v_sg_w, sg_b=v_sg_b, w_branch_a=v_w_branch_a,
              w_branch_b=v_w_branch_b, w_out=v_w_out, norm2_g=v_norm2_g, ffn_w_gate=v_ffn_w_gate, ffn_w_up=v_ffn_w_up,
              ffn_conv_w=v_ffn_conv_w, ffn_conv_b=v_ffn_conv_b, ffn_w_down=v_ffn_w_down, final_norm_g=v_final_norm_g)

    xs = x[0]
    tgt = loss_target[0]
    T, D = xs.shape
    depth = norm1_g.shape[0]
    H = dn_a_log.shape[1]
    G = sg_w.shape[1]
    WA = H * HEAD_DIM
    WB = G * HEAD_DIM
    N = T // DN_CHUNK
    colA = 4 * WA
    colB0 = colA + 2 * H
    cb_a = (2 * WB) // D

    gathered = all_gather([W[n] if n in CONV_WEIGHTS else W[n].astype(BF16) for n in SHARDED], "gather_weights")
    full_w = {n: _assemble(n, g) for n, g in zip(SHARDED, gathered)}

    def layer_weights(l):
        w_in_l = full_w["w_in"][l]
        return dict(
            wA=w_in_l[:, :colA], wba=_pad_lanes(w_in_l[:, colA:colB0], 0), wB=w_in_l[:, colB0:],
            cw8=_pad_rows(full_w["dn_conv_w"][l]), wa=full_w["w_branch_a"][l], wb=full_w["w_branch_b"][l],
            wo=full_w["w_out"][l], wg=full_w["ffn_w_gate"][l], wu=full_w["ffn_w_up"][l],
            fcw8=_pad_rows(full_w["ffn_conv_w"][l]), wd=full_w["ffn_w_down"][l],
            g1=norm1_g[l][None], g2=norm2_g[l][None], alog=_pad_lanes(dn_a_log[l][None], H), dtb=_pad_lanes(dn_dt_bias[l][None], H),
            og=dn_onorm_g[l][None], lng=sg_ln_g[l][None], lnb=sg_ln_b[l][None], sgw=sg_w[l], sgbT=sg_b[l].T, fcb=ffn_conv_b[l][None])

    def to_chunks(bg):
        bt = bg[:, :2 * H].T
        beta_c = bt[:H].reshape(H, N, DN_CHUNK, 1)
        gc_c = bt[H:].reshape(H, N, DN_CHUNK, 1)
        return beta_c, gc_c, bt[H:].reshape(H, N, 1, DN_CHUNK)

    saved = []
    cur = xs
    for l in range(depth):
        p = layer_weights(l)
        t = f"l{l}_"
        h = norm_fwd(cur, p["g1"], t + "norm1")
        projA = matmul(h, p["wA"], "nn", t + "projA")
        pba = matmul(h, p["wba"], "nn", t + "proj_ba")
        projB = matmul(h, p["wB"], "nn", t + "projB")
        q, k, v, bg = dn_prep_fwd(projA, pba, p["cw8"], p["alog"], p["dtb"], H, t + "dn_prep")
        beta_c, gc_c, gc_r = to_chunks(bg)
        u, w, a, qd, kd, gl = dn_chunk_fwd(q, k, v, beta_c, gc_c, gc_r, t + "dn_chunk")
        o, s_in = dn_scan_fwd(u, w, a, qd, kd, gl, t + "dn_scan")
        y_a = dn_post_fwd(o, projA, p["og"], t + "dn_post")
        y_b = gmlp_fwd(projB, p["lng"], p["lnb"], p["sgw"], p["sgbT"], t + "gmlp")
        ap = matmul(y_a, p["wa"], "nn", t + "branch_a")
        bp = matmul(y_b, p["wb"], "nn", t + "branch_b")
        merged = merge_fwd(projB, ap, bp, cb_a, t + "merge")
        x1 = matmul(merged, p["wo"], "nn", t + "out_proj", c=cur)
        h2 = norm_fwd(x1, p["g2"], t + "norm2")
        gp = matmul(h2, p["wg"], "nn", t + "ffn_gate")
        up = matmul(h2, p["wu"], "nn", t + "ffn_up")
        act = ffn_act_fwd(gp, up, p["fcw8"], p["fcb"], t + "ffn_act")
        x2 = matmul(act, p["wd"], "nn", t + "ffn_down", c=x1)
        saved.append(dict(p=p, x0=cur, h=h, projA=projA, pba=pba, projB=projB, q=q, k=k, v=v, chunks=(beta_c, gc_c, gc_r),
                          scan=(u, w, a, qd, kd, gl), s_in=s_in, o=o, y_a=y_a, y_b=y_b, ap=ap, bp=bp, merged=merged, x1=x1,
                          h2=h2, gp=gp, up=up, act=act))
        cur = x2

    loss_part, dx, d_final = head_fwd_bwd(cur, final_norm_g[None], tgt, "loss_head")
    loss = lax.psum(loss_part[0, 0], ("x", "y", "c"))

    grads_sh = {n: [None] * depth for n in SHARDED}
    grads_rep = {n: [None] * depth for n in REPLICATED if n != "final_norm_g"}
    for l in reversed(range(depth)):
        s = saved[l]
        p = s["p"]
        t = f"l{l}_b_"
        dact = matmul(dx, p["wd"], "nt", t + "d_act")
        grads_sh["ffn_w_down"][l] = matmul(s["act"], dx, "tn", t + "dw_down")
        dgp, dup, dfcw, dfcb = ffn_act_bwd(s["gp"], s["up"], p["fcw8"], p["fcb"], dact, t + "ffn_act")
        dh2 = matmul(dgp, p["wg"], "nt", t + "dh2_gate")
        dh2 = matmul(dup, p["wu"], "nt", t + "dh2_up", c=dh2)
        grads_sh["ffn_w_gate"][l] = matmul(s["h2"], dgp, "tn", t + "dw_gate")
        grads_sh["ffn_w_up"][l] = matmul(s["h2"], dup, "tn", t + "dw_up")
        grads_sh["ffn_conv_w"][l] = dfcw[:3]
        grads_rep["ffn_conv_b"][l] = dfcb[0]
        dx1, dg2 = norm_bwd(s["x1"], p["g2"], dh2, dx, t + "norm2")
        grads_rep["norm2_g"][l] = dg2[0]
        dmerged = matmul(dx1, p["wo"], "nt", t + "d_merged")
        grads_sh["w_out"][l] = matmul(s["merged"], dx1, "tn", t + "dw_out")
        dga, dgb, dap, dbp = merge_bwd(s["projB"], s["ap"], s["bp"], dmerged, cb_a, t + "merge")
        dya = matmul(dap, p["wa"], "nt", t + "d_ya")
        dyb = matmul(dbp, p["wb"], "nt", t + "d_yb")
        grads_sh["w_branch_a"][l] = matmul(s["y_a"], dap, "tn", t + "dw_a")
        grads_sh["w_branch_b"][l] = matmul(s["y_b"], dbp, "tn", t + "dw_b")
        du_raw, dv_raw, dlng, dlnb, dsgw, dsgbT = gmlp_bwd(s["projB"], p["lng"], p["lnb"], p["sgw"], p["sgbT"], dyb, t + "gmlp")
        grads_rep["sg_ln_g"][l], grads_rep["sg_ln_b"][l] = dlng[0], dlnb[0]
        grads_rep["sg_w"][l], grads_rep["sg_b"][l] = dsgw, dsgbT.T
        do, dz, dog = dn_post_bwd(s["o"], s["projA"], p["og"], dya, t + "dn_post")
        grads_rep["dn_onorm_g"][l] = dog[0]
        du, dw, da, dqd, dkd, dgl = dn_scan_bwd(*s["scan"], s["s_in"], do, t + "dn_scan")
        dq, dk, dv, dbc, dgc, dgr = dn_chunk_bwd(s["q"], s["k"], s["v"], *s["chunks"], du, dw, da, dqd, dkd, dgl, t + "dn_chunk")
        dbg1 = _pad_lanes(jnp.concatenate([dbc.reshape(H, T), dgc.reshape(H, T)], axis=0).T, 0)
        dbg2 = _pad_lanes(dgr.reshape(H, T).T, H)
        dqkv, dba, dcw, dalog, ddtb = dn_prep_bwd(s["projA"], s["pba"], p["cw8"], p["alog"], p["dtb"], dq, dk, dv, dbg1, dbg2, H,
                                                  t + "dn_prep")
        grads_sh["dn_conv_w"][l] = dcw[:4]
        grads_rep["dn_a_log"][l], grads_rep["dn_dt_bias"][l] = dalog[0, H:2 * H], ddtb[0, H:2 * H]
        dprojA = jnp.concatenate([dqkv, dz], axis=1)
        dprojB = jnp.concatenate([du_raw, dv_raw, dga, dgb], axis=1)
        dh = matmul(dprojA, p["wA"], "nt", t + "dh_A")
        dh = matmul(dba, p["wba"], "nt", t + "dh_ba", c=dh)
        dh = matmul(dprojB, p["wB"], "nt", t + "dh_B", c=dh)
        dwA = matmul(s["h"], dprojA, "tn", t + "dw_A")
        dwba = matmul(s["h"], dba, "tn", t + "dw_ba")
        dwB = matmul(s["h"], dprojB, "tn", t + "dw_B")
        grads_sh["w_in"][l] = jnp.concatenate([dwA, dwba[:, :2 * H], dwB], axis=1)
        dx, dg1 = norm_bwd(s["x0"], p["g1"], dh, dx1, t + "norm1")
        grads_rep["norm1_g"][l] = dg1[0]

    parts = all_to_all([_split(n, jnp.stack(grads_sh[n])).astype(F32 if n in CONV_WEIGHTS else BF16) for n in SHARDED],
                       "exchange_grads")
    out = {}
    for n, pt in zip(SHARDED, parts):
        shp = W[n].shape
        r2 = lambda a: a.reshape(-1, shp[-1])
        res = sum_adamw(pt.reshape(N_DEV, -1, shp[-1]), r2(W[n]), r2(Mo[n]), r2(Vo[n]), "adamw_" + n)
        out[n] = [r.reshape(shp) for r in res]

    rep_full = {n: (jnp.stack(grads_rep[n]) if n != "final_norm_g" else d_final[0]) for n in REPLICATED}
    sizes = [math.prod(W[n].shape) for n in REPLICATED]
    total = sum(sizes)
    rows = -(-total // LANES)
    rows = -(-rows // SUBLANES) * SUBLANES

    def pack(d):
        flat = jnp.concatenate([d[n].reshape(-1).astype(F32) for n in REPLICATED])
        return jnp.pad(flat, (0, rows * LANES - total)).reshape(rows, LANES)

    (rep_parts,) = all_gather([pack(rep_full)], "gather_small_grads")
    res = sum_adamw(rep_parts, pack(W), pack(Mo), pack(Vo), "adamw_small")
    offs = 0
    for n, sz in zip(REPLICATED, sizes):
        out[n] = [r.reshape(-1)[offs:offs + sz].reshape(W[n].shape) for r in res]
        offs += sz

    return (loss, dx[None], *[out[n][0] for n in WEIGHTS], *[out[n][1] for n in WEIGHTS],
            *[out[n][2] for n in WEIGHTS], *[out[n][3] for n in WEIGHTS])
```

```python
import functools
import math

import jax
import jax.numpy as jnp
from jax import lax
from jax.experimental import pallas as pl
from jax.experimental.pallas import tpu as pltpu

F32 = jnp.float32
BF16 = jnp.bfloat16
EPS = 1e-6
N_DEV = 8
LANES = 128
SUBLANES = 8
HEAD_DIM = 128
DN_CHUNK = 64
SG_CHUNK = 128
VMEM_LIMIT = 56 * 1024 * 1024
MESH = pl.DeviceIdType.MESH
HIGHEST = lax.Precision.HIGHEST

ADAM_LR = 0.001
ADAM_B1 = 0.9
ADAM_B2 = 0.999
ADAM_EPS = 1e-08
ADAM_WD = 0.01
ADAM_STEP = 10


def _pick(n, target, mult=LANES):
    best = None
    d = mult
    while d <= min(n, target):
        if n % d == 0:
            best = d
        d += mult
    return n if best is None else best


def _params(sem):
    return pltpu.CompilerParams(dimension_semantics=sem, vmem_limit_bytes=VMEM_LIMIT)


_NN = (((1,), (0,)), ((), ()))
_NT = (((1,), (1,)), ((), ()))
_TN = (((0,), (0,)), ((), ()))


def _dg(a, b, dims, hi):
    if hi == 2:
        return lax.dot_general(a.astype(F32), b.astype(F32), dims, precision=HIGHEST, preferred_element_type=F32)
    if hi == 1:
        a_hi, b_hi = a.astype(BF16), b.astype(BF16)
        a_lo, b_lo = (a - a_hi.astype(F32)).astype(BF16), (b - b_hi.astype(F32)).astype(BF16)
        ax, bx = dims[0][0][0], dims[0][1][0]
        a = jnp.concatenate([a_hi, a_hi, a_lo], axis=ax)
        b = jnp.concatenate([b_hi, b_lo, b_hi], axis=bx)
        return lax.dot_general(a, b, dims, preferred_element_type=F32)
    return lax.dot_general(a.astype(BF16), b.astype(BF16), dims, preferred_element_type=F32)


@functools.partial(jax.custom_vjp, nondiff_argnums=(2,))
def mm_nn(a, b, hi=False):
    return _dg(a, b, _NN, hi)


def _mm_nn_f(a, b, hi):
    return _dg(a, b, _NN, hi), (a, b)


def _mm_nn_b(hi, res, g):
    a, b = res
    return mm_nt(g, b, hi), mm_tn(a, g, hi)


@functools.partial(jax.custom_vjp, nondiff_argnums=(2,))
def mm_nt(a, b, hi=False):
    return _dg(a, b, _NT, hi)


def _mm_nt_f(a, b, hi):
    return _dg(a, b, _NT, hi), (a, b)


def _mm_nt_b(hi, res, g):
    a, b = res
    return mm_nn(g, b, hi), mm_tn(g, a, hi)


@functools.partial(jax.custom_vjp, nondiff_argnums=(2,))
def mm_tn(a, b, hi=False):
    return _dg(a, b, _TN, hi)


def _mm_tn_f(a, b, hi):
    return _dg(a, b, _TN, hi), (a, b)


def _mm_tn_b(hi, res, g):
    a, b = res
    return mm_nt(b, g, hi), mm_nn(a, g, hi)


mm_nn.defvjp(_mm_nn_f, _mm_nn_b)
mm_nt.defvjp(_mm_nt_f, _mm_nt_b)
mm_tn.defvjp(_mm_tn_f, _mm_tn_b)


def matmul(a, b, mode, name, c=None, tm=1024, tn=1024, tk=2048):
    if mode == "nn":
        (M, K), N = a.shape, b.shape[1]
    elif mode == "nt":
        (M, K), N = a.shape, b.shape[0]
    else:
        (K, M), N = a.shape, b.shape[1]
    tm, tn, tk = _pick(M, tm), _pick(N, tn), _pick(K, tk)
    nk = K // tk
    dims = {"nn": _NN, "nt": _NT, "tn": _TN}[mode]
    a_spec = pl.BlockSpec((tk, tm), lambda i, j, k: (k, i)) if mode == "tn" else pl.BlockSpec((tm, tk), lambda i, j, k: (i, k))
    b_spec = pl.BlockSpec((tn, tk), lambda i, j, k: (j, k)) if mode == "nt" else pl.BlockSpec((tk, tn), lambda i, j, k: (k, j))
    o_spec = pl.BlockSpec((tm, tn), lambda i, j, k: (i, j))
    has_c = c is not None

    def body(*refs):
        a_ref, b_ref = refs[:2]
        c_ref = refs[2] if has_c else None
        o_ref = refs[-1]

        def dot():
            return lax.dot_general(a_ref[...].astype(BF16), b_ref[...].astype(BF16), dims, preferred_element_type=F32)

        if nk == 1:
            o_ref[...] = dot() + c_ref[...] if has_c else dot()
        else:
            @pl.when(pl.program_id(2) == 0)
            def _():
                o_ref[...] = c_ref[...] if has_c else jnp.zeros_like(o_ref)

            o_ref[...] += dot()

    ins = [a, b] + ([c] if has_c else [])
    specs = [a_spec, b_spec] + ([o_spec] if has_c else [])
    return pl.pallas_call(
        body, name=name, grid=(M // tm, N // tn, nk), in_specs=specs, out_specs=o_spec,
        out_shape=jax.ShapeDtypeStruct((M, N), F32), compiler_params=_params(("parallel", "parallel", "arbitrary")),
    )(*ins)


def rowcall(name, fn, ins, in_specs, outs, out_specs, acc, nrow, ncol=1):
    n_in = len(ins)

    def body(*refs):
        i = pl.program_id(1)
        res = fn(i, *[r[...] for r in refs[:n_in]])
        for r, v, is_acc in zip(refs[n_in:], res, acc):
            if is_acc:
                @pl.when(i == 0)
                def _(r=r, v=v):
                    r[...] = v.astype(r.dtype)

                @pl.when(i > 0)
                def _(r=r, v=v):
                    r[...] += v.astype(r.dtype)
            else:
                r[...] = v.astype(r.dtype)

    return pl.pallas_call(
        body, name=name, grid=(ncol, nrow), in_specs=list(in_specs), out_specs=list(out_specs), out_shape=list(outs),
        compiler_params=_params(("parallel", "arbitrary")),
    )(*ins)


class Tiles:
    def __init__(self, T, tm):
        self.T, self.tm, self.n = T, tm, T // tm
        self.r8 = tm // SUBLANES

    def row(self, w, cb=0):
        return pl.BlockSpec((self.tm, w), lambda j, i: (i, cb))

    def rowj(self, tc):
        return pl.BlockSpec((self.tm, tc), lambda j, i: (i, j))

    def prev(self, w, cb=0):
        return pl.BlockSpec((SUBLANES, w), lambda j, i: (jnp.maximum(i * self.r8 - 1, 0), cb))

    def prevj(self, tc):
        return pl.BlockSpec((SUBLANES, tc), lambda j, i: (jnp.maximum(i * self.r8 - 1, 0), j))

    def nxt(self, w, cb=0):
        last = self.T // SUBLANES - 1
        return pl.BlockSpec((SUBLANES, w), lambda j, i: (jnp.minimum((i + 1) * self.r8, last), cb))

    def nxtj(self, tc):
        last = self.T // SUBLANES - 1
        return pl.BlockSpec((SUBLANES, tc), lambda j, i: (jnp.minimum((i + 1) * self.r8, last), j))

    def heads(self, H):
        return pl.BlockSpec((H, self.tm, HEAD_DIM), lambda j, i: (0, i, 0))

    def heads_nxt(self, H):
        last = self.T // SUBLANES - 1
        return pl.BlockSpec((H, SUBLANES, HEAD_DIM), lambda j, i: (0, jnp.minimum((i + 1) * self.r8, last), 0))


def full(shape):
    return pl.BlockSpec(tuple(shape), lambda j, i: (0,) * len(shape))


def constj(r, tc):
    return pl.BlockSpec((r, tc), lambda j, i: (0, j))


def sds(shape, dtype=F32):
    return jax.ShapeDtypeStruct(tuple(shape), dtype)


def rms(x, g):
    return x * lax.rsqrt(jnp.mean(x * x, axis=-1, keepdims=True) + EPS) * g


def silu(x):
    return x * jax.nn.sigmoid(x)


def gelu(x):
    return 0.5 * x * (1.0 + lax.erf(x * (2.0 ** -0.5)))


def causal_conv(xwin, w, K, R):
    base = SUBLANES - (K - 1)
    out = w[0:1, :] * xwin[base:base + R, :]
    for j in range(1, K):
        out = out + w[j:j + 1, :] * xwin[base + j:base + j + R, :]
    return out


def rows_to8(rows, C):
    rid = lax.broadcasted_iota(jnp.int32, (SUBLANES, C), 0)
    out = jnp.zeros((SUBLANES, C), F32)
    for k, r in enumerate(rows):
        out = out + jnp.where(rid == k, jnp.broadcast_to(r, (SUBLANES, C)), 0.0)
    return out


def dn_qkv(pre, H):
    a = silu(pre)
    W = H * HEAD_DIM

    def l2(t):
        return t * lax.rsqrt(jnp.sum(t * t, axis=-1, keepdims=True) + EPS)

    q = [l2(a[:, h * HEAD_DIM:(h + 1) * HEAD_DIM]) for h in range(H)]
    k = [l2(a[:, W + h * HEAD_DIM:W + (h + 1) * HEAD_DIM]) for h in range(H)]
    v = [a[:, 2 * W + h * HEAD_DIM:2 * W + (h + 1) * HEAD_DIM] for h in range(H)]
    return q, k, v


def dn_gates(ba, alog, dtb, H, R):
    lane = lax.broadcasted_iota(jnp.int32, (R, LANES), 1)
    beta = jax.nn.sigmoid(ba)
    g = -jnp.exp(alog) * jax.nn.softplus(ba + dtb)
    g = jnp.where((lane >= H) & (lane < 2 * H), g, 0.0)
    ri = lax.broadcasted_iota(jnp.int32, (R, R), 0)
    ci = lax.broadcasted_iota(jnp.int32, (R, R), 1)
    cum = jnp.where((ri // DN_CHUNK == ci // DN_CHUNK) & (ci <= ri), 1.0, 0.0).astype(F32)
    gc = mm_nn(cum, g, 2)
    return jnp.where(lane < H, beta, gc)


def neumann_inverse(Ls):
    C = Ls[0].shape[0]
    ri = lax.broadcasted_iota(jnp.int32, (C, C), 0)
    ci = lax.broadcasted_iota(jnp.int32, (C, C), 1)
    eye = jnp.where(ri == ci, 1.0, 0.0).astype(F32)
    P = [-L for L in Ls]
    R = [eye + p for p in P]
    for _ in range(int(math.log2(C)) - 1):
        P = [mm_nn(p, p, 1) for p in P]
        R = [r + mm_nn(r, p, 1) for r, p in zip(R, P)]
    return R


def dn_chunk(q, k, v, beta, gc, gr):
    n = len(q)
    C = q[0].shape[0]
    ri = lax.broadcasted_iota(jnp.int32, (C, C), 0)
    ci = lax.broadcasted_iota(jnp.int32, (C, C), 1)
    qs = [q[h] * (HEAD_DIM ** -0.5) for h in range(n)]
    kb = [k[h] * beta[h] for h in range(n)]
    vb = [v[h] * beta[h] for h in range(n)]
    decay = [jnp.exp(jnp.where(ri >= ci, gc[h] - gr[h], -jnp.inf)) for h in range(n)]
    L = [jnp.where(ri > ci, mm_nt(kb[h], k[h]) * decay[h], 0.0) for h in range(n)]
    attn = [jnp.where(ri >= ci, mm_nt(qs[h], k[h]) * decay[h], 0.0) for h in range(n)]
    Tinv = neumann_inverse(L)
    eg = [jnp.exp(gc[h]) for h in range(n)]
    u = [mm_nn(Tinv[h], vb[h]) for h in range(n)]
    w = [mm_nn(Tinv[h], kb[h] * eg[h]) for h in range(n)]
    qd = [qs[h] * eg[h] for h in range(n)]
    gl = [gc[h][C - 1:C, :] for h in range(n)]
    kd = [k[h] * jnp.exp(gl[h] - gc[h]) for h in range(n)]
    return u, w, attn, qd, kd, gl


def dn_step(u, w, a, qd, kd, gl, S):
    n = len(u)
    v_new = [u[h] - mm_nn(w[h], S[h]) for h in range(n)]
    o = [mm_nn(qd[h], S[h]) + mm_nn(a[h], v_new[h]) for h in range(n)]
    S_new = [S[h] * jnp.exp(gl[h]) + mm_tn(kd[h], v_new[h]) for h in range(n)]
    return o, S_new


def dn_post(o, z, g):
    H = o.shape[0]
    return jnp.concatenate([rms(o[h], g) * silu(z[:, h * HEAD_DIM:(h + 1) * HEAD_DIM]) for h in range(H)], axis=1)


def gmlp(u_raw, v_raw, ln_g, ln_b, sgw, sgbT):
    R = u_raw.shape[0]
    G = sgw.shape[0]
    nc = R // SG_CHUNK
    u = gelu(u_raw)
    vv = gelu(v_raw)
    xc = vv - jnp.mean(vv, axis=-1, keepdims=True)
    vg = xc * lax.rsqrt(jnp.mean(xc * xc, axis=-1, keepdims=True) + EPS) * ln_g + ln_b
    ri = lax.broadcasted_iota(jnp.int32, (SG_CHUNK, SG_CHUNK), 0)
    ci = lax.broadcasted_iota(jnp.int32, (SG_CHUNK, SG_CHUNK), 1)
    cols = []
    for g in range(G):
        ws = jnp.where(ri >= ci, sgw[g], 0.0)
        rhs = jnp.concatenate([vg[c * SG_CHUNK:(c + 1) * SG_CHUNK, g * HEAD_DIM:(g + 1) * HEAD_DIM] for c in range(nc)], axis=1)
        mixed = mm_nn(ws, rhs) + sgbT[:, g:g + 1]
        cols.append(jnp.concatenate([mixed[:, c * HEAD_DIM:(c + 1) * HEAD_DIM] for c in range(nc)], axis=0))
    return u * jnp.concatenate(cols, axis=1)


def merge(ga, gb, ap, bp):
    return jax.nn.sigmoid(ga) * ap + jax.nn.sigmoid(gb) * bp


def norm_fwd(x, g, name, tm=256):
    T, D = x.shape
    tl = Tiles(T, _pick(T, tm))
    (h,) = rowcall(name, lambda i, x, g: (rms(x, g),), [x, g], [tl.row(D), full((1, D))],
                   [sds((T, D), BF16)], [tl.row(D)], [False], tl.n)
    return h


def norm_bwd(x, g, dh, dres, name, tm=256):
    T, D = x.shape
    tl = Tiles(T, _pick(T, tm))

    def fn(i, x, g, dh, dres):
        _, vj = jax.vjp(rms, x, g)
        dx, dg = vj(dh.astype(F32))
        return dx + dres, dg

    return rowcall(name, fn, [x, g, dh, dres], [tl.row(D), full((1, D)), tl.row(D), tl.row(D)],
                   [sds((T, D)), sds((1, D))], [tl.row(D), full((1, D))], [False, True], tl.n)


def head_fwd_bwd(x, g, tgt, name, tm=256):
    T, D = x.shape
    tl = Tiles(T, _pick(T, tm))

    def fn(i, x, g, tgt):
        y, vj = jax.vjp(rms, x, g)
        e = y - tgt
        loss = 0.5 * jnp.sum(jnp.mean(e * e, axis=-1, keepdims=True), axis=0, keepdims=True)
        dx, dg = vj(e * (1.0 / D))
        return loss, dx, dg

    return rowcall(name, fn, [x, g, tgt], [tl.row(D), full((1, D)), tl.row(D)],
                   [sds((1, 1)), sds((T, D)), sds((1, D))], [full((1, 1)), tl.row(D), full((1, D))],
                   [True, False, True], tl.n)


def dn_prep_fwd(projA, pba, cw8, alog, dtb, H, name, tm=256):
    T = projA.shape[0]
    W3 = 3 * H * HEAD_DIM
    tl = Tiles(T, _pick(T, tm, DN_CHUNK))
    R = tl.tm

    def fn(i, xp, x, ba, cw, alog, dtb):
        xwin = jnp.concatenate([jnp.where(i > 0, xp, 0.0), x], axis=0)
        q, k, v = dn_qkv(causal_conv(xwin, cw, 4, R), H)
        return jnp.stack(q), jnp.stack(k), jnp.stack(v), dn_gates(ba, alog, dtb, H, R)

    hs = sds((H, T, HEAD_DIM))
    return rowcall(name, fn, [projA, projA, pba, cw8, alog, dtb],
                   [tl.prev(W3), tl.row(W3), tl.row(LANES), full((SUBLANES, W3)), full((1, LANES)), full((1, LANES))],
                   [hs, hs, hs, sds((T, LANES))], [tl.heads(H)] * 3 + [tl.row(LANES)], [False] * 4, tl.n)


def dn_prep_bwd(projA, pba, cw8, alog, dtb, dq, dk, dv, dbg1, dbg2, H, name, tm=256):
    T = projA.shape[0]
    W3 = 3 * H * HEAD_DIM
    tl = Tiles(T, _pick(T, tm, DN_CHUNK))
    R = tl.tm
    RE = R + SUBLANES

    def fn(i, xp, x, xn, ba, cw, alog, dtb, dq, dk, dv, dqn, dkn, dvn, dbg1, dbg2):
        last = i == tl.n - 1
        xwin = jnp.concatenate([jnp.where(i > 0, xp, 0.0), x, jnp.where(last, 0.0, xn)], axis=0)
        pre = causal_conv(xwin, cw, 4, RE)
        ext = lambda d, dn: [jnp.concatenate([d[h], jnp.where(last, 0.0, dn[h])], axis=0) for h in range(H)]
        _, vj = jax.vjp(lambda p: dn_qkv(p, H), pre)
        (dpre,) = vj((ext(dq, dqn), ext(dk, dkn), ext(dv, dvn)))
        dx = cw[0:1, :] * dpre[3:3 + R, :]
        for j in range(1, 4):
            dx = dx + cw[j:j + 1, :] * dpre[3 - j:3 - j + R, :]
        dcw = rows_to8([jnp.sum(dpre[0:R, :] * xwin[5 + j:5 + j + R, :], axis=0, keepdims=True) for j in range(4)], W3)
        _, vjg = jax.vjp(lambda ba, alog, dtb: dn_gates(ba, alog, dtb, H, R), ba, alog, dtb)
        dba, dalog, ddtb = vjg(dbg1 + dbg2)
        return dx, dba, dcw, dalog, ddtb

    return rowcall(name, fn, [projA, projA, projA, pba, cw8, alog, dtb, dq, dk, dv, dq, dk, dv, dbg1, dbg2],
                   [tl.prev(W3), tl.row(W3), tl.nxt(W3), tl.row(LANES), full((SUBLANES, W3)), full((1, LANES)), full((1, LANES))]
                   + [tl.heads(H)] * 3 + [tl.heads_nxt(H)] * 3 + [tl.row(LANES)] * 2,
                   [sds((T, W3), BF16), sds((T, LANES), BF16), sds((SUBLANES, W3)), sds((1, LANES)), sds((1, LANES))],
                   [tl.row(W3), tl.row(LANES), full((SUBLANES, W3)), full((1, LANES)), full((1, LANES))],
                   [False, False, True, True, True], tl.n)


def _chunk_specs(H, C):
    hs = pl.BlockSpec((H, C, HEAD_DIM), lambda n: (0, n, 0))
    col = pl.BlockSpec((H, 1, C, 1), lambda n: (0, n, 0, 0))
    rw = pl.BlockSpec((H, 1, 1, C), lambda n: (0, n, 0, 0))
    at = pl.BlockSpec((H, C, C), lambda n: (0, n, 0))
    one = pl.BlockSpec((H, 1, 1, 1), lambda n: (0, n, 0, 0))
    return hs, col, rw, at, one


def dn_chunk_fwd(q, k, v, beta_c, gc_c, gc_r, name):
    H, T, _ = q.shape
    C = DN_CHUNK
    N = T // C
    hs, col, rw, at, one = _chunk_specs(H, C)

    def body(q, k, v, bc, gc, gr, u, w, a, qd, kd, gl):
        hd = range(H)
        res = dn_chunk([q[h] for h in hd], [k[h] for h in hd], [v[h] for h in hd], [bc[h, 0] for h in hd],
                       [gc[h, 0] for h in hd], [gr[h, 0] for h in hd])
        for h in hd:
            for ref, val in zip((u, w, a, qd, kd), res[:5]):
                ref[h] = val[h]
            gl[h, 0] = res[5][h]

    big = sds((H, T, HEAD_DIM))
    return pl.pallas_call(
        body, name=name, grid=(N,), in_specs=[hs, hs, hs, col, col, rw], out_specs=[hs, hs, at, hs, hs, one],
        out_shape=[big, big, sds((H, T, C)), big, big, sds((H, N, 1, 1))], compiler_params=_params(("parallel",)),
    )(q, k, v, beta_c, gc_c, gc_r)


def dn_chunk_bwd(q, k, v, beta_c, gc_c, gc_r, du, dw, da, dqd, dkd, dgl, name):
    H, T, _ = q.shape
    C = DN_CHUNK
    N = T // C
    hs, col, rw, at, one = _chunk_specs(H, C)

    def body(q, k, v, bc, gc, gr, du, dw, da, dqd, dkd, dgl, dq, dk, dv, dbc, dgc, dgr):
        hd = range(H)
        _, vj = jax.vjp(dn_chunk, [q[h] for h in hd], [k[h] for h in hd], [v[h] for h in hd], [bc[h, 0] for h in hd],
                        [gc[h, 0] for h in hd], [gr[h, 0] for h in hd])
        res = vj(([du[h] for h in hd], [dw[h] for h in hd], [da[h] for h in hd], [dqd[h] for h in hd],
                  [dkd[h] for h in hd], [dgl[h, 0] for h in hd]))
        for h in hd:
            dq[h], dk[h], dv[h] = res[0][h], res[1][h], res[2][h]
            dbc[h, 0], dgc[h, 0], dgr[h, 0] = res[3][h], res[4][h], res[5][h]

    big = sds((H, T, HEAD_DIM))
    return pl.pallas_call(
        body, name=name, grid=(N,), in_specs=[hs, hs, hs, col, col, rw, hs, hs, at, hs, hs, one],
        out_specs=[hs, hs, hs, col, col, rw],
        out_shape=[big, big, big, sds((H, N, C, 1)), sds((H, N, C, 1)), sds((H, N, 1, C))],
        compiler_params=_params(("parallel",)),
    )(q, k, v, beta_c, gc_c, gc_r, du, dw, da, dqd, dkd, dgl)


def dn_scan_fwd(u, w, a, qd, kd, gl, name):
    H, T, _ = u.shape
    C = DN_CHUNK
    N = T // C
    hs, _, _, at, one = _chunk_specs(H, C)
    st = pl.BlockSpec((1, H, HEAD_DIM, HEAD_DIM), lambda n: (n, 0, 0, 0))

    def body(u, w, a, qd, kd, gl, o, s_in, S):
        @pl.when(pl.program_id(0) == 0)
        def _():
            S[...] = jnp.zeros_like(S)

        hd = range(H)
        s = [S[h] for h in hd]
        o_new, s_new = dn_step([u[h] for h in hd], [w[h] for h in hd], [a[h] for h in hd], [qd[h] for h in hd],
                               [kd[h] for h in hd], [gl[h, 0] for h in hd], s)
        for h in hd:
            s_in[0, h] = s[h]
            o[h] = o_new[h]
            S[h] = s_new[h]

    return pl.pallas_call(
        body, name=name, grid=(N,), in_specs=[hs, hs, at, hs, hs, one], out_specs=[hs, st],
        out_shape=[sds((H, T, HEAD_DIM)), sds((N, H, HEAD_DIM, HEAD_DIM))],
        scratch_shapes=[pltpu.VMEM((H, HEAD_DIM, HEAD_DIM), F32)], compiler_params=_params(("arbitrary",)),
    )(u, w, a, qd, kd, gl)


def dn_scan_bwd(u, w, a, qd, kd, gl, s_in, do, name):
    H, T, _ = u.shape
    C = DN_CHUNK
    N = T // C
    rev = lambda spec_shape, f: pl.BlockSpec(spec_shape, f)
    hs = rev((H, C, HEAD_DIM), lambda n: (0, N - 1 - n, 0))
    at = rev((H, C, C), lambda n: (0, N - 1 - n, 0))
    one = rev((H, 1, 1, 1), lambda n: (0, N - 1 - n, 0, 0))
    st = rev((1, H, HEAD_DIM, HEAD_DIM), lambda n: (N - 1 - n, 0, 0, 0))

    def body(u, w, a, qd, kd, gl, s_in, do, du, dw, da, dqd, dkd, dgl, dS):
        @pl.when(pl.program_id(0) == 0)
        def _():
            dS[...] = jnp.zeros_like(dS)

        hd = range(H)
        _, vj = jax.vjp(dn_step, [u[h] for h in hd], [w[h] for h in hd], [a[h] for h in hd], [qd[h] for h in hd],
                        [kd[h] for h in hd], [gl[h, 0] for h in hd], [s_in[0, h] for h in hd])
        res = vj(([do[h] for h in hd], [dS[h] for h in hd]))
        for h in hd:
            du[h], dw[h], da[h], dqd[h], dkd[h] = (res[j][h] for j in range(5))
            dgl[h, 0] = res[5][h]
            dS[h] = res[6][h]

    big = sds((H, T, HEAD_DIM))
    return pl.pallas_call(
        body, name=name, grid=(N,), in_specs=[hs, hs, at, hs, hs, one, st, hs], out_specs=[hs, hs, at, hs, hs, one],
        out_shape=[big, big, sds((H, T, C)), big, big, sds((H, N, 1, 1))],
        scratch_shapes=[pltpu.VMEM((H, HEAD_DIM, HEAD_DIM), F32)], compiler_params=_params(("arbitrary",)),
    )(u, w, a, qd, kd, gl, s_in, do)


def dn_post_fwd(o, projA, g, name, tm=256):
    H, T, _ = o.shape
    W = H * HEAD_DIM
    tl = Tiles(T, _pick(T, tm))
    (y,) = rowcall(name, lambda i, o, z, g: (dn_post(o, z, g),), [o, projA, g], [tl.heads(H), tl.row(W, 3), full((1, HEAD_DIM))],
                   [sds((T, W), BF16)], [tl.row(W)], [False], tl.n)
    return y


def dn_post_bwd(o, projA, g, dy, name, tm=256):
    H, T, _ = o.shape
    W = H * HEAD_DIM
    tl = Tiles(T, _pick(T, tm))

    def fn(i, o, z, g, dy):
        _, vj = jax.vjp(dn_post, o, z, g)
        return vj(dy.astype(F32))

    return rowcall(name, fn, [o, projA, g, dy], [tl.heads(H), tl.row(W, 3), full((1, HEAD_DIM)), tl.row(W)],
                   [sds((H, T, HEAD_DIM)), sds((T, W), BF16), sds((1, HEAD_DIM))],
                   [tl.heads(H), tl.row(W), full((1, HEAD_DIM))], [False, False, True], tl.n)


def gmlp_fwd(projB, ln_g, ln_b, sgw, sgbT, name, tm=512):
    T = projB.shape[0]
    G = sgw.shape[0]
    W = G * HEAD_DIM
    tl = Tiles(T, _pick(T, tm))
    (y,) = rowcall(name, lambda i, *a: (gmlp(*a),), [projB, projB, ln_g, ln_b, sgw, sgbT],
                   [tl.row(W, 0), tl.row(W, 1), full((1, W)), full((1, W)), full(sgw.shape), full(sgbT.shape)],
                   [sds((T, W), BF16)], [tl.row(W)], [False], tl.n)
    return y


def gmlp_bwd(projB, ln_g, ln_b, sgw, sgbT, dy, name, tm=512):
    T = projB.shape[0]
    G = sgw.shape[0]
    W = G * HEAD_DIM
    tl = Tiles(T, _pick(T, tm))

    def fn(i, u_raw, v_raw, ln_g, ln_b, sgw, sgbT, dy):
        _, vj = jax.vjp(gmlp, u_raw, v_raw, ln_g, ln_b, sgw, sgbT)
        return vj(dy.astype(F32))

    return rowcall(name, fn, [projB, projB, ln_g, ln_b, sgw, sgbT, dy],
                   [tl.row(W, 0), tl.row(W, 1), full((1, W)), full((1, W)), full(sgw.shape), full(sgbT.shape), tl.row(W)],
                   [sds((T, W), BF16), sds((T, W), BF16), sds((1, W)), sds((1, W)), sds(sgw.shape), sds(sgbT.shape)],
                   [tl.row(W), tl.row(W), full((1, W)), full((1, W)), full(sgw.shape), full(sgbT.shape)],
                   [False, False, True, True, True, True], tl.n)


def merge_fwd(projB, ap, bp, cb_a, name, tm=256):
    T, D = ap.shape
    tl = Tiles(T, _pick(T, tm))
    (m,) = rowcall(name, lambda i, *a: (merge(*a),), [projB, projB, ap, bp],
                   [tl.row(D, cb_a), tl.row(D, cb_a + 1), tl.row(D), tl.row(D)], [sds((T, D), BF16)], [tl.row(D)], [False], tl.n)
    return m


def merge_bwd(projB, ap, bp, dm, cb_a, name, tm=256):
    T, D = ap.shape
    tl = Tiles(T, _pick(T, tm))

    def fn(i, ga, gb, ap, bp, dm):
        _, vj = jax.vjp(merge, ga, gb, ap, bp)
        return vj(dm.astype(F32))

    return rowcall(name, fn, [projB, projB, ap, bp, dm], [tl.row(D, cb_a), tl.row(D, cb_a + 1), tl.row(D), tl.row(D), tl.row(D)],
                   [sds((T, D), BF16)] * 4, [tl.row(D)] * 4, [False] * 4, tl.n)


def ffn_act_fwd(gp, up, fcw8, fcb, name, tm=256, tc=512):
    T, F = gp.shape
    tl = Tiles(T, _pick(T, tm))
    tc = _pick(F, tc)
    R = tl.tm

    def fn(i, gprev, g, up, cw, cb):
        xwin = jnp.concatenate([jnp.where(i > 0, gprev, 0.0), g], axis=0)
        return (silu(causal_conv(xwin, cw, 3, R) + cb) * up,)

    (act,) = rowcall(name, fn, [gp, gp, up, fcw8, fcb], [tl.prevj(tc), tl.rowj(tc), tl.rowj(tc), constj(SUBLANES, tc), constj(1, tc)],
                     [sds((T, F), BF16)], [tl.rowj(tc)], [False], tl.n, F // tc)
    return act


def ffn_act_bwd(gp, up, fcw8, fcb, dact, name, tm=256, tc=512):
    T, F = gp.shape
    tl = Tiles(T, _pick(T, tm))
    tc = _pick(F, tc)
    R = tl.tm
    RE = R + SUBLANES

    def fn(i, gprev, g, gnext, up, upn, da, dan, cw, cb):
        last = i == tl.n - 1
        xwin = jnp.concatenate([jnp.where(i > 0, gprev, 0.0), g, jnp.where(last, 0.0, gnext)], axis=0)
        gate = causal_conv(xwin, cw, 3, RE) + cb
        upe = jnp.concatenate([up, upn], axis=0)
        dae = jnp.concatenate([da, jnp.where(last, 0.0, dan)], axis=0)
        s = jax.nn.sigmoid(gate)
        dgate = dae * upe * (s * (1.0 + gate * (1.0 - s)))
        dup = da * (gate[0:R, :] * s[0:R, :])
        dgp = cw[0:1, :] * dgate[2:2 + R, :] + cw[1:2, :] * dgate[1:1 + R, :] + cw[2:3, :] * dgate[0:R, :]
        dcw = rows_to8([jnp.sum(dgate[0:R, :] * xwin[6 + j:6 + j + R, :], axis=0, keepdims=True) for j in range(3)], tc)
        dcb = jnp.sum(dgate[0:R, :], axis=0, keepdims=True)
        return dgp, dup, dcw, dcb

    return rowcall(name, fn, [gp, gp, gp, up, up, dact, dact, fcw8, fcb],
                   [tl.prevj(tc), tl.rowj(tc), tl.nxtj(tc), tl.rowj(tc), tl.nxtj(tc), tl.rowj(tc), tl.nxtj(tc),
                    constj(SUBLANES, tc), constj(1, tc)],
                   [sds((T, F), BF16), sds((T, F), BF16), sds((SUBLANES, F)), sds((1, F))],
                   [tl.rowj(tc), tl.rowj(tc), constj(SUBLANES, tc), constj(1, tc)], [False, False, True, True], tl.n, F // tc)


def _me():
    return lax.axis_index("x"), lax.axis_index("y"), lax.axis_index("c")


def all_gather(shards, name):
    nt = len(shards)

    def body(*refs):
        xs, outs = refs[:nt], refs[nt:2 * nt]
        send_sems, recv_sems, local_sems = refs[2 * nt:]
        x, y, c = _me()
        me, sibling = (x, y, c), (x, y, 1 - c)
        chips = [(1 - x, y), (x, 1 - y), (1 - x, 1 - y)]

        def slot(t, p):
            return outs[t].at[4 * p[0] + 2 * p[1] + p[2]]

        def copy(t, k, block, to, src=None):
            return pltpu.make_async_remote_copy(
                src_ref=slot(t, block) if src is None else src, dst_ref=slot(t, block),
                send_sem=send_sems.at[t, k], recv_sem=recv_sems.at[t, k], device_id=to, device_id_type=MESH)

        mine = [pltpu.make_async_copy(xs[t], slot(t, me), local_sems.at[t]) for t in range(nt)]
        first = []
        for t in range(nt):
            mine[t].start()
            first.append(copy(t, 0, me, sibling, src=xs[t]))
            first += [copy(t, 1 + j, me, (*chip, c), src=xs[t]) for j, chip in enumerate(chips)]
        for cp in first:
            cp.start()
        passed = []
        for j, chip in enumerate(chips):
            for t in range(nt):
                copy(t, 1 + j, (*chip, c), me).wait_recv()
                cp = copy(t, 4 + j, (*chip, c), sibling)
                cp.start()
                passed.append(cp)
        for t in range(nt):
            copy(t, 0, sibling, me).wait_recv()
            for j, chip in enumerate(chips):
                copy(t, 4 + j, (*chip, 1 - c), me).wait_recv()
        for cp in first + passed:
            cp.wait_send()
        for t in range(nt):
            mine[t].wait()

    any_spec = pl.BlockSpec(memory_space=pl.ANY)
    return pl.pallas_call(
        body, name=name, in_specs=[any_spec] * nt, out_specs=[any_spec] * nt,
        out_shape=[jax.ShapeDtypeStruct((N_DEV,) + s.shape, s.dtype) for s in shards],
        scratch_shapes=[pltpu.SemaphoreType.DMA((nt, 7)), pltpu.SemaphoreType.DMA((nt, 7)), pltpu.SemaphoreType.DMA((nt,))],
    )(*shards)


def all_to_all(parts, name):
    nt = len(parts)

    def body(*refs):
        xs, outs = refs[:nt], refs[nt:2 * nt]
        send_sems, recv_sems, local_sems = refs[2 * nt:]
        x, y, c = _me()
        my = 4 * x + 2 * y + c
        copies = []
        for t in range(nt):
            own = pltpu.make_async_copy(xs[t].at[my], outs[t].at[my], local_sems.at[t])
            own.start()
            copies.append(own)
        remote = []
        for k in range(1, N_DEV):
            px, py, pc = x ^ (k >> 2), y ^ ((k >> 1) & 1), c ^ (k & 1)
            peer = 4 * px + 2 * py + pc
            for t in range(nt):
                cp = pltpu.make_async_remote_copy(
                    src_ref=xs[t].at[peer], dst_ref=outs[t].at[my], send_sem=send_sems.at[t, k - 1],
                    recv_sem=recv_sems.at[t, k - 1], device_id=(px, py, pc), device_id_type=MESH)
                cp.start()
                remote.append((cp, t, k, peer))
        for cp, t, k, peer in remote:
            pltpu.make_async_remote_copy(
                src_ref=xs[t].at[my], dst_ref=outs[t].at[peer], send_sem=send_sems.at[t, k - 1],
                recv_sem=recv_sems.at[t, k - 1], device_id=(x, y, c), device_id_type=MESH).wait_recv()
        for cp, t, k, peer in remote:
            cp.wait_send()
        for own in copies:
            own.wait()

    any_spec = pl.BlockSpec(memory_space=pl.ANY)
    return pl.pallas_call(
        body, name=name, in_specs=[any_spec] * nt, out_specs=[any_spec] * nt,
        out_shape=[jax.ShapeDtypeStruct(p.shape, p.dtype) for p in parts],
        scratch_shapes=[pltpu.SemaphoreType.DMA((nt, 7)), pltpu.SemaphoreType.DMA((nt, 7)), pltpu.SemaphoreType.DMA((nt,))],
    )(*parts)


def sum_adamw(parts, w, m, v, name, tr=256):
    _, R, C = parts.shape
    tr = _pick(R, tr, SUBLANES)
    c1 = 1.0 - ADAM_B1 ** ADAM_STEP
    c2 = 1.0 - ADAM_B2 ** ADAM_STEP

    def body(p, w, m, v, g_o, d_o, m_o, v_o):
        g = p[0].astype(F32)
        for d in range(1, N_DEV):
            g = g + p[d].astype(F32)
        mn = ADAM_B1 * m[...] + (1.0 - ADAM_B1) * g
        vn = ADAM_B2 * v[...] + (1.0 - ADAM_B2) * (g * g)
        m_hat = mn / c1
        v_hat = vn / c2
        g_o[...] = g
        d_o[...] = -ADAM_LR * (m_hat / (jnp.sqrt(v_hat) + ADAM_EPS) + ADAM_WD * w[...])
        m_o[...] = mn
        v_o[...] = vn

    blk = pl.BlockSpec((tr, C), lambda i: (i, 0))
    return pl.pallas_call(
        body, name=name, grid=(R // tr,), in_specs=[pl.BlockSpec((N_DEV, tr, C), lambda i: (0, i, 0)), blk, blk, blk],
        out_specs=[blk] * 4, out_shape=[sds((R, C))] * 4, compiler_params=_params(("parallel",)),
    )(parts, w, m, v)


SHARDED = ("w_in", "dn_conv_w", "w_branch_a", "w_branch_b", "w_out", "ffn_w_gate", "ffn_w_up", "ffn_conv_w", "ffn_w_down")
COL_SHARDED = ("w_in", "dn_conv_w", "w_branch_a", "w_branch_b", "ffn_w_gate", "ffn_w_up", "ffn_conv_w")
CONV_WEIGHTS = ("dn_conv_w", "ffn_conv_w")
REPLICATED = ("norm1_g", "dn_a_log", "dn_dt_bias", "dn_onorm_g", "sg_ln_g", "sg_ln_b", "sg_w", "sg_b", "norm2_g",
              "ffn_conv_b", "final_norm_g")
WEIGHTS = ("norm1_g", "w_in", "dn_conv_w", "dn_a_log", "dn_dt_bias", "dn_onorm_g", "sg_ln_g", "sg_ln_b", "sg_w", "sg_b",
           "w_branch_a", "w_branch_b", "w_out", "norm2_g", "ffn_w_gate", "ffn_w_up", "ffn_conv_w", "ffn_conv_b",
           "ffn_w_down", "final_norm_g")


def _assemble(name, g):
    if name in COL_SHARDED:
        L, r, cs = g.shape[1:]
        return jnp.transpose(g, (1, 2, 0, 3)).reshape(L, r, N_DEV * cs)
    L, rs, cdim = g.shape[1:]
    return jnp.transpose(g, (1, 0, 2, 3)).reshape(L, N_DEV * rs, cdim)


def _split(name, full_grad):
    if name in COL_SHARDED:
        L, r, cfull = full_grad.shape
        return jnp.transpose(full_grad.reshape(L, r, N_DEV, cfull // N_DEV), (2, 0, 1, 3))
    L, rfull, cdim = full_grad.shape
    return jnp.transpose(full_grad.reshape(L, N_DEV, rfull // N_DEV, cdim), (1, 0, 2, 3))


def _pad_lanes(a, lo, width=LANES):
    return jnp.pad(a, ((0, 0), (lo, width - lo - a.shape[1])))


def _pad_rows(a, rows=SUBLANES):
    return jnp.pad(a, ((0, rows - a.shape[0]), (0, 0)))


def kernel(x, norm1_g, w_in, dn_conv_w, dn_a_log, dn_dt_bias, dn_onorm_g, sg_ln_g, sg_ln_b, sg_w, sg_b, w_branch_a, w_branch_b, w_out, norm2_g, ffn_w_gate, ffn_w_up, ffn_conv_w, ffn_conv_b, ffn_w_down, final_norm_g, loss_target, m_norm1_g, m_w_in, m_dn_conv_w, m_dn_a_log, m_dn_dt_bias, m_dn_onorm_g, m_sg_ln_g, m_sg_ln_b, m_sg_w, m_sg_b, m_w_branch_a, m_w_branch_b, m_w_out, m_norm2_g, m_ffn_w_gate, m_ffn_w_up, m_ffn_conv_w, m_ffn_conv_b, m_ffn_w_down, m_final_norm_g, v_norm1_g, v_w_in, v_dn_conv_w, v_dn_a_log, v_dn_dt_bias, v_dn_onorm_g, v_sg_ln_g, v_sg_ln_b, v_sg_w, v_sg_b, v_w_branch_a, v_w_branch_b, v_w_out, v_norm2_g, v_ffn_w_gate, v_ffn_w_up, v_ffn_conv_w, v_ffn_conv_b, v_ffn_w_down, v_final_norm_g):
    W = dict(norm1_g=norm1_g, w_in=w_in, dn_conv_w=dn_conv_w, dn_a_log=dn_a_log, dn_dt_bias=dn_dt_bias, dn_onorm_g=dn_onorm_g,
             sg_ln_g=sg_ln_g, sg_ln_b=sg_ln_b, sg_w=sg_w, sg_b=sg_b, w_branch_a=w_branch_a, w_branch_b=w_branch_b, w_out=w_out,
             norm2_g=norm2_g, ffn_w_gate=ffn_w_gate, ffn_w_up=ffn_w_up, ffn_conv_w=ffn_conv_w, ffn_conv_b=ffn_conv_b,
             ffn_w_down=ffn_w_down, final_norm_g=final_norm_g)
    Mo = dict(norm1_g=m_norm1_g, w_in=m_w_in, dn_conv_w=m_dn_conv_w, dn_a_log=m_dn_a_log, dn_dt_bias=m_dn_dt_bias,
              dn_onorm_g=m_dn_onorm_g, sg_ln_g=m_sg_ln_g, sg_ln_b=m_sg_ln_b, sg_w=m_sg_w, sg_b=m_sg_b, w_branch_a=m_w_branch_a,
              w_branch_b=m_w_branch_b, w_out=m_w_out, norm2_g=m_norm2_g, ffn_w_gate=m_ffn_w_gate, ffn_w_up=m_ffn_w_up,
              ffn_conv_w=m_ffn_conv_w, ffn_conv_b=m_ffn_conv_b, ffn_w_down=m_ffn_w_down, final_norm_g=m_final_norm_g)
    Vo = dict(norm1_g=v_norm1_g, w_in=v_w_in, dn_conv_w=v_dn_conv_w, dn_a_log=v_dn_a_log, dn_dt_bias=v_dn_dt_bias,
              dn_onorm_g=v_dn_onorm_g, sg_ln_g=v_sg_ln_g, sg_ln_b=v_sg_ln_b, sg_w=v_sg_w, sg_b=v_sg_b, w_branch_a=v_w_branch_a,
              w_branch_b=v_w_branch_b, w_out=v_w_out, norm2_g=v_norm2_g, ffn_w_gate=v_ffn_w_gate, ffn_w_up=v_ffn_w_up,
              ffn_conv_w=v_ffn_conv_w, ffn_conv_b=v_ffn_conv_b, ffn_w_down=v_ffn_w_down, final_norm_g=v_final_norm_g)

    xs = x[0]
    tgt = loss_target[0]
    T, D = xs.shape
    depth = norm1_g.shape[0]
    H = dn_a_log.shape[1]
    G = sg_w.shape[1]
    WA = H * HEAD_DIM
    WB = G * HEAD_DIM
    N = T // DN_CHUNK
    colA = 4 * WA
    colB0 = colA + 2 * H
    cb_a = (2 * WB) // D

    gathered = all_gather([W[n] if n in CONV_WEIGHTS else W[n].astype(BF16) for n in SHARDED], "gather_weights")
    full_w = {n: _assemble(n, g) for n, g in zip(SHARDED, gathered)}

    def layer_weights(l):
        w_in_l = full_w["w_in"][l]
        return dict(
            wA=w_in_l[:, :colA], wba=_pad_lanes(w_in_l[:, colA:colB0], 0), wB=w_in_l[:, colB0:],
            cw8=_pad_rows(full_w["dn_conv_w"][l]), wa=full_w["w_branch_a"][l], wb=full_w["w_branch_b"][l],
            wo=full_w["w_out"][l], wg=full_w["ffn_w_gate"][l], wu=full_w["ffn_w_up"][l],
            fcw8=_pad_rows(full_w["ffn_conv_w"][l]), wd=full_w["ffn_w_down"][l],
            g1=norm1_g[l][None], g2=norm2_g[l][None], alog=_pad_lanes(dn_a_log[l][None], H), dtb=_pad_lanes(dn_dt_bias[l][None], H),
            og=dn_onorm_g[l][None], lng=sg_ln_g[l][None], lnb=sg_ln_b[l][None], sgw=sg_w[l], sgbT=sg_b[l].T, fcb=ffn_conv_b[l][None])

    def to_chunks(bg):
        bt = bg[:, :2 * H].T
        beta_c = bt[:H].reshape(H, N, DN_CHUNK, 1)
        gc_c = bt[H:].reshape(H, N, DN_CHUNK, 1)
        return beta_c, gc_c, bt[H:].reshape(H, N, 1, DN_CHUNK)

    saved = []
    cur = xs
    for l in range(depth):
        p = layer_weights(l)
        t = f"l{l}_"
        h = norm_fwd(cur, p["g1"], t + "norm1")
        projA = matmul(h, p["wA"], "nn", t + "projA")
        pba = matmul(h, p["wba"], "nn", t + "proj_ba")
        projB = matmul(h, p["wB"], "nn", t + "projB")
        q, k, v, bg = dn_prep_fwd(projA, pba, p["cw8"], p["alog"], p["dtb"], H, t + "dn_prep")
        beta_c, gc_c, gc_r = to_chunks(bg)
        u, w, a, qd, kd, gl = dn_chunk_fwd(q, k, v, beta_c, gc_c, gc_r, t + "dn_chunk")
        o, s_in = dn_scan_fwd(u, w, a, qd, kd, gl, t + "dn_scan")
        y_a = dn_post_fwd(o, projA, p["og"], t + "dn_post")
        y_b = gmlp_fwd(projB, p["lng"], p["lnb"], p["sgw"], p["sgbT"], t + "gmlp")
        ap = matmul(y_a, p["wa"], "nn", t + "branch_a")
        bp = matmul(y_b, p["wb"], "nn", t + "branch_b")
        merged = merge_fwd(projB, ap, bp, cb_a, t + "merge")
        x1 = matmul(merged, p["wo"], "nn", t + "out_proj", c=cur)
        h2 = norm_fwd(x1, p["g2"], t + "norm2")
        gp = matmul(h2, p["wg"], "nn", t + "ffn_gate")
        up = matmul(h2, p["wu"], "nn", t + "ffn_up")
        act = ffn_act_fwd(gp, up, p["fcw8"], p["fcb"], t + "ffn_act")
        x2 = matmul(act, p["wd"], "nn", t + "ffn_down", c=x1)
        saved.append(dict(p=p, x0=cur, h=h, projA=projA, pba=pba, projB=projB, q=q, k=k, v=v, chunks=(beta_c, gc_c, gc_r),
                          scan=(u, w, a, qd, kd, gl), s_in=s_in, o=o, y_a=y_a, y_b=y_b, ap=ap, bp=bp, merged=merged, x1=x1,
                          h2=h2, gp=gp, up=up, act=act))
        cur = x2

    loss_part, dx, d_final = head_fwd_bwd(cur, final_norm_g[None], tgt, "loss_head")
    loss = lax.psum(loss_part[0, 0], ("x", "y", "c"))

    grads_sh = {n: [None] * depth for n in SHARDED}
    grads_rep = {n: [None] * depth for n in REPLICATED if n != "final_norm_g"}
    for l in reversed(range(depth)):
        s = saved[l]
        p = s["p"]
        t = f"l{l}_b_"
        dact = matmul(dx, p["wd"], "nt", t + "d_act")
        grads_sh["ffn_w_down"][l] = matmul(s["act"], dx, "tn", t + "dw_down")
        dgp, dup, dfcw, dfcb = ffn_act_bwd(s["gp"], s["up"], p["fcw8"], p["fcb"], dact, t + "ffn_act")
        dh2 = matmul(dgp, p["wg"], "nt", t + "dh2_gate")
        dh2 = matmul(dup, p["wu"], "nt", t + "dh2_up", c=dh2)
        grads_sh["ffn_w_gate"][l] = matmul(s["h2"], dgp, "tn", t + "dw_gate")
        grads_sh["ffn_w_up"][l] = matmul(s["h2"], dup, "tn", t + "dw_up")
        grads_sh["ffn_conv_w"][l] = dfcw[:3]
        grads_rep["ffn_conv_b"][l] = dfcb[0]
        dx1, dg2 = norm_bwd(s["x1"], p["g2"], dh2, dx, t + "norm2")
        grads_rep["norm2_g"][l] = dg2[0]
        dmerged = matmul(dx1, p["wo"], "nt", t + "d_merged")
        grads_sh["w_out"][l] = matmul(s["merged"], dx1, "tn", t + "dw_out")
        dga, dgb, dap, dbp = merge_bwd(s["projB"], s["ap"], s["bp"], dmerged, cb_a, t + "merge")
        dya = matmul(dap, p["wa"], "nt", t + "d_ya")
        dyb = matmul(dbp, p["wb"], "nt", t + "d_yb")
        grads_sh["w_branch_a"][l] = matmul(s["y_a"], dap, "tn", t + "dw_a")
        grads_sh["w_branch_b"][l] = matmul(s["y_b"], dbp, "tn", t + "dw_b")
        du_raw, dv_raw, dlng, dlnb, dsgw, dsgbT = gmlp_bwd(s["projB"], p["lng"], p["lnb"], p["sgw"], p["sgbT"], dyb, t + "gmlp")
        grads_rep["sg_ln_g"][l], grads_rep["sg_ln_b"][l] = dlng[0], dlnb[0]
        grads_rep["sg_w"][l], grads_rep["sg_b"][l] = dsgw, dsgbT.T
        do, dz, dog = dn_post_bwd(s["o"], s["projA"], p["og"], dya, t + "dn_post")
        grads_rep["dn_onorm_g"][l] = dog[0]
        du, dw, da, dqd, dkd, dgl = dn_scan_bwd(*s["scan"], s["s_in"], do, t + "dn_scan")
        dq, dk, dv, dbc, dgc, dgr = dn_chunk_bwd(s["q"], s["k"], s["v"], *s["chunks"], du, dw, da, dqd, dkd, dgl, t + "dn_chunk")
        dbg1 = _pad_lanes(jnp.concatenate([dbc.reshape(H, T), dgc.reshape(H, T)], axis=0).T, 0)
        dbg2 = _pad_lanes(dgr.reshape(H, T).T, H)
        dqkv, dba, dcw, dalog, ddtb = dn_prep_bwd(s["projA"], s["pba"], p["cw8"], p["alog"], p["dtb"], dq, dk, dv, dbg1, dbg2, H,
                                                  t + "dn_prep")
        grads_sh["dn_conv_w"][l] = dcw[:4]
        grads_rep["dn_a_log"][l], grads_rep["dn_dt_bias"][l] = dalog[0, H:2 * H], ddtb[0, H:2 * H]
        dprojA = jnp.concatenate([dqkv, dz], axis=1)
        dprojB = jnp.concatenate([du_raw, dv_raw, dga, dgb], axis=1)
        dh = matmul(dprojA, p["wA"], "nt", t + "dh_A")
        dh = matmul(dba, p["wba"], "nt", t + "dh_ba", c=dh)
        dh = matmul(dprojB, p["wB"], "nt", t + "dh_B", c=dh)
        dwA = matmul(s["h"], dprojA, "tn", t + "dw_A")
        dwba = matmul(s["h"], dba, "tn", t + "dw_ba")
        dwB = matmul(s["h"], dprojB, "tn", t + "dw_B")
        grads_sh["w_in"][l] = jnp.concatenate([dwA, dwba[:, :2 * H], dwB], axis=1)
        dx, dg1 = norm_bwd(s["x0"], p["g1"], dh, dx1, t + "norm1")
        grads_rep["norm1_g"][l] = dg1[0]

    parts = all_to_all([_split(n, jnp.stack(grads_sh[n])).astype(F32 if n in CONV_WEIGHTS else BF16) for n in SHARDED],
                       "exchange_grads")
    out = {}
    for n, pt in zip(SHARDED, parts):
        shp = W[n].shape
        r2 = lambda a: a.reshape(-1, shp[-1])
        res = sum_adamw(pt.reshape(N_DEV, -1, shp[-1]), r2(W[n]), r2(Mo[n]), r2(Vo[n]), "adamw_" + n)
        out[n] = [r.reshape(shp) for r in res]

    rep_full = {n: (jnp.stack(grads_rep[n]) if n != "final_norm_g" else d_final[0]) for n in REPLICATED}
    sizes = [math.prod(W[n].shape) for n in REPLICATED]
    total = sum(sizes)
    rows = -(-total // LANES)
    rows = -(-rows // SUBLANES) * SUBLANES

    def pack(d):
        flat = jnp.concatenate([d[n].reshape(-1).astype(F32) for n in REPLICATED])
        return jnp.pad(flat, (0, rows * LANES - total)).reshape(rows, LANES)

    (rep_parts,) = all_gather([pack(rep_full)], "gather_small_grads")
    res = sum_adamw(rep_parts, pack(W), pack(Mo), pack(Vo), "adamw_small")
    offs = 0
    for n, sz in zip(REPLICATED, sizes):
        out[n] = [r.reshape(-1)[offs:offs + sz].reshape(W[n].shape) for r in res]
        offs += sz

    return (loss, dx[None], *[out[n][0] for n in WEIGHTS], *[out[n][1] for n in WEIGHTS],
            *[out[n][2] for n in WEIGHTS], *[out[n][3] for n in WEIGHTS])
```

```python
import functools
import math

import jax
import jax.numpy as jnp
from jax import lax
from jax.experimental import pallas as pl
from jax.experimental.pallas import tpu as pltpu

F32 = jnp.float32
BF16 = jnp.bfloat16
EPS = 1e-6
N_DEV = 8
LANES = 128
SUBLANES = 8
HEAD_DIM = 128
DN_CHUNK = 64
SG_CHUNK = 128
VMEM_LIMIT = 56 * 1024 * 1024
MESH = pl.DeviceIdType.MESH
HIGHEST = lax.Precision.HIGHEST

ADAM_LR = 0.001
ADAM_B1 = 0.9
ADAM_B2 = 0.999
ADAM_EPS = 1e-08
ADAM_WD = 0.01
ADAM_STEP = 10


def _pick(n, target, mult=LANES):
    best = None
    d = mult
    while d <= min(n, target):
        if n % d == 0:
            best = d
        d += mult
    return n if best is None else best


def _params(sem):
    return pltpu.CompilerParams(dimension_semantics=sem, vmem_limit_bytes=VMEM_LIMIT)


_NN = (((1,), (0,)), ((), ()))
_NT = (((1,), (1,)), ((), ()))
_TN = (((0,), (0,)), ((), ()))


def _dg(a, b, dims, hi):
    if hi == 2:
        return lax.dot_general(a.astype(F32), b.astype(F32), dims, precision=HIGHEST, preferred_element_type=F32)
    if hi == 1:
        a_hi, b_hi = a.astype(BF16), b.astype(BF16)
        a_lo, b_lo = (a - a_hi.astype(F32)).astype(BF16), (b - b_hi.astype(F32)).astype(BF16)
        ax, bx = dims[0][0][0], dims[0][1][0]
        a = jnp.concatenate([a_hi, a_hi, a_lo], axis=ax)
        b = jnp.concatenate([b_hi, b_lo, b_hi], axis=bx)
        return lax.dot_general(a, b, dims, preferred_element_type=F32)
    return lax.dot_general(a.astype(BF16), b.astype(BF16), dims, preferred_element_type=F32)


@functools.partial(jax.custom_vjp, nondiff_argnums=(2,))
def mm_nn(a, b, hi=False):
    return _dg(a, b, _NN, hi)


def _mm_nn_f(a, b, hi):
    return _dg(a, b, _NN, hi), (a, b)


def _mm_nn_b(hi, res, g):
    a, b = res
    return mm_nt(g, b, hi), mm_tn(a, g, hi)


@functools.partial(jax.custom_vjp, nondiff_argnums=(2,))
def mm_nt(a, b, hi=False):
    return _dg(a, b, _NT, hi)


def _mm_nt_f(a, b, hi):
    return _dg(a, b, _NT, hi), (a, b)


def _mm_nt_b(hi, res, g):
    a, b = res
    return mm_nn(g, b, hi), mm_tn(g, a, hi)


@functools.partial(jax.custom_vjp, nondiff_argnums=(2,))
def mm_tn(a, b, hi=False):
    return _dg(a, b, _TN, hi)


def _mm_tn_f(a, b, hi):
    return _dg(a, b, _TN, hi), (a, b)


def _mm_tn_b(hi, res, g):
    a, b = res
    return mm_nt(b, g, hi), mm_nn(a, g, hi)


mm_nn.defvjp(_mm_nn_f, _mm_nn_b)
mm_nt.defvjp(_mm_nt_f, _mm_nt_b)
mm_tn.defvjp(_mm_tn_f, _mm_tn_b)


def matmul(a, b, mode, name, c=None, tm=1024, tn=1024, tk=2048):
    if mode == "nn":
        (M, K), N = a.shape, b.shape[1]
    elif mode == "nt":
        (M, K), N = a.shape, b.shape[0]
    else:
        (K, M), N = a.shape, b.shape[1]
    tm, tn, tk = _pick(M, tm), _pick(N, tn), _pick(K, tk)
    nk = K // tk
    dims = {"nn": _NN, "nt": _NT, "tn": _TN}[mode]
    a_spec = pl.BlockSpec((tk, tm), lambda i, j, k: (k, i)) if mode == "tn" else pl.BlockSpec((tm, tk), lambda i, j, k: (i, k))
    b_spec = pl.BlockSpec((tn, tk), lambda i, j, k: (j, k)) if mode == "nt" else pl.BlockSpec((tk, tn), lambda i, j, k: (k, j))
    o_spec = pl.BlockSpec((tm, tn), lambda i, j, k: (i, j))
    has_c = c is not None

    def body(*refs):
        a_ref, b_ref = refs[:2]
        c_ref = refs[2] if has_c else None
        o_ref = refs[-1]

        def dot():
            return lax.dot_general(a_ref[...].astype(BF16), b_ref[...].astype(BF16), dims, preferred_element_type=F32)

        if nk == 1:
            o_ref[...] = dot() + c_ref[...] if has_c else dot()
        else:
            @pl.when(pl.program_id(2) == 0)
            def _():
                o_ref[...] = c_ref[...] if has_c else jnp.zeros_like(o_ref)

            o_ref[...] += dot()

    ins = [a, b] + ([c] if has_c else [])
    specs = [a_spec, b_spec] + ([o_spec] if has_c else [])
    return pl.pallas_call(
        body, name=name, grid=(M // tm, N // tn, nk), in_specs=specs, out_specs=o_spec,
        out_shape=jax.ShapeDtypeStruct((M, N), F32), compiler_params=_params(("parallel", "parallel", "arbitrary")),
    )(*ins)


def rowcall(name, fn, ins, in_specs, outs, out_specs, acc, nrow, ncol=1):
    n_in = len(ins)

    def body(*refs):
        i = pl.program_id(1)
        res = fn(i, *[r[...] for r in refs[:n_in]])
        for r, v, is_acc in zip(refs[n_in:], res, acc):
            if is_acc:
                @pl.when(i == 0)
                def _(r=r, v=v):
                    r[...] = v.astype(r.dtype)

                @pl.when(i > 0)
                def _(r=r, v=v):
                    r[...] += v.astype(r.dtype)
            else:
                r[...] = v.astype(r.dtype)

    return pl.pallas_call(
        body, name=name, grid=(ncol, nrow), in_specs=list(in_specs), out_specs=list(out_specs), out_shape=list(outs),
        compiler_params=_params(("parallel", "arbitrary")),
    )(*ins)


class Tiles:
    def __init__(self, T, tm):
        self.T, self.tm, self.n = T, tm, T // tm
        self.r8 = tm // SUBLANES

    def row(self, w, cb=0):
        return pl.BlockSpec((self.tm, w), lambda j, i: (i, cb))

    def rowj(self, tc):
        return pl.BlockSpec((self.tm, tc), lambda j, i: (i, j))

    def prev(self, w, cb=0):
        return pl.BlockSpec((SUBLANES, w), lambda j, i: (jnp.maximum(i * self.r8 - 1, 0), cb))

    def prevj(self, tc):
        return pl.BlockSpec((SUBLANES, tc), lambda j, i: (jnp.maximum(i * self.r8 - 1, 0), j))

    def nxt(self, w, cb=0):
        last = self.T // SUBLANES - 1
        return pl.BlockSpec((SUBLANES, w), lambda j, i: (jnp.minimum((i + 1) * self.r8, last), cb))

    def nxtj(self, tc):
        last = self.T // SUBLANES - 1
        return pl.BlockSpec((SUBLANES, tc), lambda j, i: (jnp.minimum((i + 1) * self.r8, last), j))

    def heads(self, H):
        return pl.BlockSpec((H, self.tm, HEAD_DIM), lambda j, i: (0, i, 0))

    def heads_nxt(self, H):
        last = self.T // SUBLANES - 1
        return pl.BlockSpec((H, SUBLANES, HEAD_DIM), lambda j, i: (0, jnp.minimum((i + 1) * self.r8, last), 0))


def full(shape):
    return pl.BlockSpec(tuple(shape), lambda j, i: (0,) * len(shape))


def constj(r, tc):
    return pl.BlockSpec((r, tc), lambda j, i: (0, j))


def sds(shape, dtype=F32):
    return jax.ShapeDtypeStruct(tuple(shape), dtype)


def rms(x, g):
    return x * lax.rsqrt(jnp.mean(x * x, axis=-1, keepdims=True) + EPS) * g


def silu(x):
    return x * jax.nn.sigmoid(x)


def gelu(x):
    return 0.5 * x * (1.0 + lax.erf(x * (2.0 ** -0.5)))


def causal_conv(xwin, w, K, R):
    base = SUBLANES - (K - 1)
    out = w[0:1, :] * xwin[base:base + R, :]
    for j in range(1, K):
        out = out + w[j:j + 1, :] * xwin[base + j:base + j + R, :]
    return out


def rows_to8(rows, C):
    rid = lax.broadcasted_iota(jnp.int32, (SUBLANES, C), 0)
    out = jnp.zeros((SUBLANES, C), F32)
    for k, r in enumerate(rows):
        out = out + jnp.where(rid == k, jnp.broadcast_to(r, (SUBLANES, C)), 0.0)
    return out


def dn_qkv(pre, H):
    a = silu(pre)
    W = H * HEAD_DIM

    def l2(t):
        return t * lax.rsqrt(jnp.sum(t * t, axis=-1, keepdims=True) + EPS)

    q = [l2(a[:, h * HEAD_DIM:(h + 1) * HEAD_DIM]) for h in range(H)]
    k = [l2(a[:, W + h * HEAD_DIM:W + (h + 1) * HEAD_DIM]) for h in range(H)]
    v = [a[:, 2 * W + h * HEAD_DIM:2 * W + (h + 1) * HEAD_DIM] for h in range(H)]
    return q, k, v


def dn_gates(ba, alog, dtb, H, R):
    lane = lax.broadcasted_iota(jnp.int32, (R, LANES), 1)
    beta = jax.nn.sigmoid(ba)
    g = -jnp.exp(alog) * jax.nn.softplus(ba + dtb)
    g = jnp.where((lane >= H) & (lane < 2 * H), g, 0.0)
    ri = lax.broadcasted_iota(jnp.int32, (R, R), 0)
    ci = lax.broadcasted_iota(jnp.int32, (R, R), 1)
    cum = jnp.where((ri // DN_CHUNK == ci // DN_CHUNK) & (ci <= ri), 1.0, 0.0).astype(F32)
    gc = mm_nn(cum, g, 2)
    return jnp.where(lane < H, beta, gc)


def neumann_inverse(Ls):
    C = Ls[0].shape[0]
    ri = lax.broadcasted_iota(jnp.int32, (C, C), 0)
    ci = lax.broadcasted_iota(jnp.int32, (C, C), 1)
    eye = jnp.where(ri == ci, 1.0, 0.0).astype(F32)
    P = [-L for L in Ls]
    R = [eye + p for p in P]
    for _ in range(int(math.log2(C)) - 1):
        P = [mm_nn(p, p, 1) for p in P]
        R = [r + mm_nn(r, p, 1) for r, p in zip(R, P)]
    return R


def dn_chunk(q, k, v, beta, gc, gr):
    n = len(q)
    C = q[0].shape[0]
    ri = lax.broadcasted_iota(jnp.int32, (C, C), 0)
    ci = lax.broadcasted_iota(jnp.int32, (C, C), 1)
    qs = [q[h] * (HEAD_DIM ** -0.5) for h in range(n)]
    kb = [k[h] * beta[h] for h in range(n)]
    vb = [v[h] * beta[h] for h in range(n)]
    decay = [jnp.exp(jnp.where(ri >= ci, gc[h] - gr[h], -jnp.inf)) for h in range(n)]
    L = [jnp.where(ri > ci, mm_nt(kb[h], k[h]) * decay[h], 0.0) for h in range(n)]
    attn = [jnp.where(ri >= ci, mm_nt(qs[h], k[h]) * decay[h], 0.0) for h in range(n)]
    Tinv = neumann_inverse(L)
    eg = [jnp.exp(gc[h]) for h in range(n)]
    u = [mm_nn(Tinv[h], vb[h]) for h in range(n)]
    w = [mm_nn(Tinv[h], kb[h] * eg[h]) for h in range(n)]
    qd = [qs[h] * eg[h] for h in range(n)]
    gl = [gc[h][C - 1:C, :] for h in range(n)]
    kd = [k[h] * jnp.exp(gl[h] - gc[h]) for h in range(n)]
    return u, w, attn, qd, kd, gl


def dn_step(u, w, a, qd, kd, gl, S):
    n = len(u)
    v_new = [u[h] - mm_nn(w[h], S[h]) for h in range(n)]
    o = [mm_nn(qd[h], S[h]) + mm_nn(a[h], v_new[h]) for h in range(n)]
    S_new = [S[h] * jnp.exp(gl[h]) + mm_tn(kd[h], v_new[h]) for h in range(n)]
    return o, S_new


def dn_post(o, z, g):
    H = o.shape[0]
    return jnp.concatenate([rms(o[h], g) * silu(z[:, h * HEAD_DIM:(h + 1) * HEAD_DIM]) for h in range(H)], axis=1)


def gmlp(u_raw, v_raw, ln_g, ln_b, sgw, sgbT):
    R = u_raw.shape[0]
    G = sgw.shape[0]
    nc = R // SG_CHUNK
    u = gelu(u_raw)
    vv = gelu(v_raw)
    xc = vv - jnp.mean(vv, axis=-1, keepdims=True)
    vg = xc * lax.rsqrt(jnp.mean(xc * xc, axis=-1, keepdims=True) + EPS) * ln_g + ln_b
    ri = lax.broadcasted_iota(jnp.int32, (SG_CHUNK, SG_CHUNK), 0)
    ci = lax.broadcasted_iota(jnp.int32, (SG_CHUNK, SG_CHUNK), 1)
    cols = []
    for g in range(G):
        ws = jnp.where(ri >= ci, sgw[g], 0.0)
        rhs = jnp.concatenate([vg[c * SG_CHUNK:(c + 1) * SG_CHUNK, g * HEAD_DIM:(g + 1) * HEAD_DIM] for c in range(nc)], axis=1)
        mixed = mm_nn(ws, rhs) + sgbT[:, g:g + 1]
        cols.append(jnp.concatenate([mixed[:, c * HEAD_DIM:(c + 1) * HEAD_DIM] for c in range(nc)], axis=0))
    return u * jnp.concatenate(cols, axis=1)


def merge(ga, gb, ap, bp):
    return jax.nn.sigmoid(ga) * ap + jax.nn.sigmoid(gb) * bp


def norm_fwd(x, g, name, tm=256):
    T, D = x.shape
    tl = Tiles(T, _pick(T, tm))
    (h,) = rowcall(name, lambda i, x, g: (rms(x, g),), [x, g], [tl.row(D), full((1, D))],
                   [sds((T, D), BF16)], [tl.row(D)], [False], tl.n)
    return h


def norm_bwd(x, g, dh, dres, name, tm=256):
    T, D = x.shape
    tl = Tiles(T, _pick(T, tm))

    def fn(i, x, g, dh, dres):
        _, vj = jax.vjp(rms, x, g)
        dx, dg = vj(dh.astype(F32))
        return dx + dres, dg

    return rowcall(name, fn, [x, g, dh, dres], [tl.row(D), full((1, D)), tl.row(D), tl.row(D)],
                   [sds((T, D)), sds((1, D))], [tl.row(D), full((1, D))], [False, True], tl.n)


def head_fwd_bwd(x, g, tgt, name, tm=256):
    T, D = x.shape
    tl = Tiles(T, _pick(T, tm))

    def fn(i, x, g, tgt):
        y, vj = jax.vjp(rms, x, g)
        e = y - tgt
        loss = 0.5 * jnp.sum(jnp.mean(e * e, axis=-1, keepdims=True), axis=0, keepdims=True)
        dx, dg = vj(e * (1.0 / D))
        return loss, dx, dg

    return rowcall(name, fn, [x, g, tgt], [tl.row(D), full((1, D)), tl.row(D)],
                   [sds((1, 1)), sds((T, D)), sds((1, D))], [full((1, 1)), tl.row(D), full((1, D))],
                   [True, False, True], tl.n)


def dn_prep_fwd(projA, pba, cw8, alog, dtb, H, name, tm=256):
    T = projA.shape[0]
    W3 = 3 * H * HEAD_DIM
    tl = Tiles(T, _pick(T, tm, DN_CHUNK))
    R = tl.tm

    def fn(i, xp, x, ba, cw, alog, dtb):
        xwin = jnp.concatenate([jnp.where(i > 0, xp, 0.0), x], axis=0)
        q, k, v = dn_qkv(causal_conv(xwin, cw, 4, R), H)
        return jnp.stack(q), jnp.stack(k), jnp.stack(v), dn_gates(ba, alog, dtb, H, R)

    hs = sds((H, T, HEAD_DIM))
    return rowcall(name, fn, [projA, projA, pba, cw8, alog, dtb],
                   [tl.prev(W3), tl.row(W3), tl.row(LANES), full((SUBLANES, W3)), full((1, LANES)), full((1, LANES))],
                   [hs, hs, hs, sds((T, LANES))], [tl.heads(H)] * 3 + [tl.row(LANES)], [False] * 4, tl.n)


def dn_prep_bwd(projA, pba, cw8, alog, dtb, dq, dk, dv, dbg1, dbg2, H, name, tm=256):
    T = projA.shape[0]
    W3 = 3 * H * HEAD_DIM
    tl = Tiles(T, _pick(T, tm, DN_CHUNK))
    R = tl.tm
    RE = R + SUBLANES

    def fn(i, xp, x, xn, ba, cw, alog, dtb, dq, dk, dv, dqn, dkn, dvn, dbg1, dbg2):
        last = i == tl.n - 1
        xwin = jnp.concatenate([jnp.where(i > 0, xp, 0.0), x, jnp.where(last, 0.0, xn)], axis=0)
        pre = causal_conv(xwin, cw, 4, RE)
        ext = lambda d, dn: [jnp.concatenate([d[h], jnp.where(last, 0.0, dn[h])], axis=0) for h in range(H)]
        _, vj = jax.vjp(lambda p: dn_qkv(p, H), pre)
        (dpre,) = vj((ext(dq, dqn), ext(dk, dkn), ext(dv, dvn)))
        dx = cw[0:1, :] * dpre[3:3 + R, :]
        for j in range(1, 4):
            dx = dx + cw[j:j + 1, :] * dpre[3 - j:3 - j + R, :]
        dcw = rows_to8([jnp.sum(dpre[0:R, :] * xwin[5 + j:5 + j + R, :], axis=0, keepdims=True) for j in range(4)], W3)
        _, vjg = jax.vjp(lambda ba, alog, dtb: dn_gates(ba, alog, dtb, H, R), ba, alog, dtb)
        dba, dalog, ddtb = vjg(dbg1 + dbg2)
        return dx, dba, dcw, dalog, ddtb

    return rowcall(name, fn, [projA, projA, projA, pba, cw8, alog, dtb, dq, dk, dv, dq, dk, dv, dbg1, dbg2],
                   [tl.prev(W3), tl.row(W3), tl.nxt(W3), tl.row(LANES), full((SUBLANES, W3)), full((1, LANES)), full((1, LANES))]
                   + [tl.heads(H)] * 3 + [tl.heads_nxt(H)] * 3 + [tl.row(LANES)] * 2,
                   [sds((T, W3), BF16), sds((T, LANES), BF16), sds((SUBLANES, W3)), sds((1, LANES)), sds((1, LANES))],
                   [tl.row(W3), tl.row(LANES), full((SUBLANES, W3)), full((1, LANES)), full((1, LANES))],
                   [False, False, True, True, True], tl.n)


def _chunk_specs(H, C):
    hs = pl.BlockSpec((H, C, HEAD_DIM), lambda n: (0, n, 0))
    col = pl.BlockSpec((H, 1, C, 1), lambda n: (0, n, 0, 0))
    rw = pl.BlockSpec((H, 1, 1, C), lambda n: (0, n, 0, 0))
    at = pl.BlockSpec((H, C, C), lambda n: (0, n, 0))
    one = pl.BlockSpec((H, 1, 1, 1), lambda n: (0, n, 0, 0))
    return hs, col, rw, at, one


def dn_chunk_fwd(q, k, v, beta_c, gc_c, gc_r, name):
    H, T, _ = q.shape
    C = DN_CHUNK
    N = T // C
    hs, col, rw, at, one = _chunk_specs(H, C)

    def body(q, k, v, bc, gc, gr, u, w, a, qd, kd, gl):
        hd = range(H)
        res = dn_chunk([q[h] for h in hd], [k[h] for h in hd], [v[h] for h in hd], [bc[h, 0] for h in hd],
                       [gc[h, 0] for h in hd], [gr[h, 0] for h in hd])
        for h in hd:
            for ref, val in zip((u, w, a, qd, kd), res[:5]):
                ref[h] = val[h]
            gl[h, 0] = res[5][h]

    big = sds((H, T, HEAD_DIM))
    return pl.pallas_call(
        body, name=name, grid=(N,), in_specs=[hs, hs, hs, col, col, rw], out_specs=[hs, hs, at, hs, hs, one],
        out_shape=[big, big, sds((H, T, C)), big, big, sds((H, N, 1, 1))], compiler_params=_params(("parallel",)),
    )(q, k, v, beta_c, gc_c, gc_r)


def dn_chunk_bwd(q, k, v, beta_c, gc_c, gc_r, du, dw, da, dqd, dkd, dgl, name):
    H, T, _ = q.shape
    C = DN_CHUNK
    N = T // C
    hs, col, rw, at, one = _chunk_specs(H, C)

    def body(q, k, v, bc, gc, gr, du, dw, da, dqd, dkd, dgl, dq, dk, dv, dbc, dgc, dgr):
        hd = range(H)
        _, vj = jax.vjp(dn_chunk, [q[h] for h in hd], [k[h] for h in hd], [v[h] for h in hd], [bc[h, 0] for h in hd],
                        [gc[h, 0] for h in hd], [gr[h, 0] for h in hd])
        res = vj(([du[h] for h in hd], [dw[h] for h in hd], [da[h] for h in hd], [dqd[h] for h in hd],
                  [dkd[h] for h in hd], [dgl[h, 0] for h in hd]))
        for h in hd:
            dq[h], dk[h], dv[h] = res[0][h], res[1][h], res[2][h]
            dbc[h, 0], dgc[h, 0], dgr[h, 0] = res[3][h], res[4][h], res[5][h]

    big = sds((H, T, HEAD_DIM))
    return pl.pallas_call(
        body, name=name, grid=(N,), in_specs=[hs, hs, hs, col, col, rw, hs, hs, at, hs, hs, one],
        out_specs=[hs, hs, hs, col, col, rw],
        out_shape=[big, big, big, sds((H, N, C, 1)), sds((H, N, C, 1)), sds((H, N, 1, C))],
        compiler_params=_params(("parallel",)),
    )(q, k, v, beta_c, gc_c, gc_r, du, dw, da, dqd, dkd, dgl)


def dn_scan_fwd(u, w, a, qd, kd, gl, name):
    H, T, _ = u.shape
    C = DN_CHUNK
    N = T // C
    hs, _, _, at, one = _chunk_specs(H, C)
    st = pl.BlockSpec((1, H, HEAD_DIM, HEAD_DIM), lambda n: (n, 0, 0, 0))

    def body(u, w, a, qd, kd, gl, o, s_in, S):
        @pl.when(pl.program_id(0) == 0)
        def _():
            S[...] = jnp.zeros_like(S)

        hd = range(H)
        s = [S[h] for h in hd]
        o_new, s_new = dn_step([u[h] for h in hd], [w[h] for h in hd], [a[h] for h in hd], [qd[h] for h in hd],
                               [kd[h] for h in hd], [gl[h, 0] for h in hd], s)
        for h in hd:
            s_in[0, h] = s[h]
            o[h] = o_new[h]
            S[h] = s_new[h]

    return pl.pallas_call(
        body, name=name, grid=(N,), in_specs=[hs, hs, at, hs, hs, one], out_specs=[hs, st],
        out_shape=[sds((H, T, HEAD_DIM)), sds((N, H, HEAD_DIM, HEAD_DIM))],
        scratch_shapes=[pltpu.VMEM((H, HEAD_DIM, HEAD_DIM), F32)], compiler_params=_params(("arbitrary",)),
    )(u, w, a, qd, kd, gl)


def dn_scan_bwd(u, w, a, qd, kd, gl, s_in, do, name):
    H, T, _ = u.shape
    C = DN_CHUNK
    N = T // C
    rev = lambda spec_shape, f: pl.BlockSpec(spec_shape, f)
    hs = rev((H, C, HEAD_DIM), lambda n: (0, N - 1 - n, 0))
    at = rev((H, C, C), lambda n: (0, N - 1 - n, 0))
    one = rev((H, 1, 1, 1), lambda n: (0, N - 1 - n, 0, 0))
    st = rev((1, H, HEAD_DIM, HEAD_DIM), lambda n: (N - 1 - n, 0, 0, 0))

    def body(u, w, a, qd, kd, gl, s_in, do, du, dw, da, dqd, dkd, dgl, dS):
        @pl.when(pl.program_id(0) == 0)
        def _():
            dS[...] = jnp.zeros_like(dS)

        hd = range(H)
        _, vj = jax.vjp(dn_step, [u[h] for h in hd], [w[h] for h in hd], [a[h] for h in hd], [qd[h] for h in hd],
                        [kd[h] for h in hd], [gl[h, 0] for h in hd], [s_in[0, h] for h in hd])
        res = vj(([do[h] for h in hd], [dS[h] for h in hd]))
        for h in hd:
            du[h], dw[h], da[h], dqd[h], dkd[h] = (res[j][h] for j in range(5))
            dgl[h, 0] = res[5][h]
            dS[h] = res[6][h]

    big = sds((H, T, HEAD_DIM))
    return pl.pallas_call(
        body, name=name, grid=(N,), in_specs=[hs, hs, at, hs, hs, one, st, hs], out_specs=[hs, hs, at, hs, hs, one],
        out_shape=[big, big, sds((H, T, C)), big, big, sds((H, N, 1, 1))],
        scratch_shapes=[pltpu.VMEM((H, HEAD_DIM, HEAD_DIM), F32)], compiler_params=_params(("arbitrary",)),
    )(u, w, a, qd, kd, gl, s_in, do)


def dn_post_fwd(o, projA, g, name, tm=256):
    H, T, _ = o.shape
    W = H * HEAD_DIM
    tl = Tiles(T, _pick(T, tm))
    (y,) = rowcall(name, lambda i, o, z, g: (dn_post(o, z, g),), [o, projA, g], [tl.heads(H), tl.row(W, 3), full((1, HEAD_DIM))],
                   [sds((T, W), BF16)], [tl.row(W)], [False], tl.n)
    return y


def dn_post_bwd(o, projA, g, dy, name, tm=256):
    H, T, _ = o.shape
    W = H * HEAD_DIM
    tl = Tiles(T, _pick(T, tm))

    def fn(i, o, z, g, dy):
        _, vj = jax.vjp(dn_post, o, z, g)
        return vj(dy.astype(F32))

    return rowcall(name, fn, [o, projA, g, dy], [tl.heads(H), tl.row(W, 3), full((1, HEAD_DIM)), tl.row(W)],
                   [sds((H, T, HEAD_DIM)), sds((T, W), BF16), sds((1, HEAD_DIM))],
                   [tl.heads(H), tl.row(W), full((1, HEAD_DIM))], [False, False, True], tl.n)


def gmlp_fwd(projB, ln_g, ln_b, sgw, sgbT, name, tm=512):
    T = projB.shape[0]
    G = sgw.shape[0]
    W = G * HEAD_DIM
    tl = Tiles(T, _pick(T, tm))
    (y,) = rowcall(name, lambda i, *a: (gmlp(*a),), [projB, projB, ln_g, ln_b, sgw, sgbT],
                   [tl.row(W, 0), tl.row(W, 1), full((1, W)), full((1, W)), full(sgw.shape), full(sgbT.shape)],
                   [sds((T, W), BF16)], [tl.row(W)], [False], tl.n)
    return y


def gmlp_bwd(projB, ln_g, ln_b, sgw, sgbT, dy, name, tm=512):
    T = projB.shape[0]
    G = sgw.shape[0]
    W = G * HEAD_DIM
    tl = Tiles(T, _pick(T, tm))

    def fn(i, u_raw, v_raw, ln_g, ln_b, sgw, sgbT, dy):
        _, vj = jax.vjp(gmlp, u_raw, v_raw, ln_g, ln_b, sgw, sgbT)
        return vj(dy.astype(F32))

    return rowcall(name, fn, [projB, projB, ln_g, ln_b, sgw, sgbT, dy],
                   [tl.row(W, 0), tl.row(W, 1), full((1, W)), full((1, W)), full(sgw.shape), full(sgbT.shape), tl.row(W)],
                   [sds((T, W), BF16), sds((T, W), BF16), sds((1, W)), sds((1, W)), sds(sgw.shape), sds(sgbT.shape)],
                   [tl.row(W), tl.row(W), full((1, W)), full((1, W)), full(sgw.shape), full(sgbT.shape)],
                   [False, False, True, True, True, True], tl.n)


def merge_fwd(projB, ap, bp, cb_a, name, tm=256):
    T, D = ap.shape
    tl = Tiles(T, _pick(T, tm))
    (m,) = rowcall(name, lambda i, *a: (merge(*a),), [projB, projB, ap, bp],
                   [tl.row(D, cb_a), tl.row(D, cb_a + 1), tl.row(D), tl.row(D)], [sds((T, D), BF16)], [tl.row(D)], [False], tl.n)
    return m


def merge_bwd(projB, ap, bp, dm, cb_a, name, tm=256):
    T, D = ap.shape
    tl = Tiles(T, _pick(T, tm))

    def fn(i, ga, gb, ap, bp, dm):
        _, vj = jax.vjp(merge, ga, gb, ap, bp)
        return vj(dm.astype(F32))

    return rowcall(name, fn, [projB, projB, ap, bp, dm], [tl.row(D, cb_a), tl.row(D, cb_a + 1), tl.row(D), tl.row(D), tl.row(D)],
                   [sds((T, D), BF16)] * 4, [tl.row(D)] * 4, [False] * 4, tl.n)


def ffn_act_fwd(gp, up, fcw8, fcb, name, tm=256, tc=512):
    T, F = gp.shape
    tl = Tiles(T, _pick(T, tm))
    tc = _pick(F, tc)
    R = tl.tm

    def fn(i, gprev, g, up, cw, cb):
        xwin = jnp.concatenate([jnp.where(i > 0, gprev, 0.0), g], axis=0)
        return (silu(causal_conv(xwin, cw, 3, R) + cb) * up,)

    (act,) = rowcall(name, fn, [gp, gp, up, fcw8, fcb], [tl.prevj(tc), tl.rowj(tc), tl.rowj(tc), constj(SUBLANES, tc), constj(1, tc)],
                     [sds((T, F), BF16)], [tl.rowj(tc)], [False], tl.n, F // tc)
    return act


def ffn_act_bwd(gp, up, fcw8, fcb, dact, name, tm=256, tc=512):
    T, F = gp.shape
    tl = Tiles(T, _pick(T, tm))
    tc = _pick(F, tc)
    R = tl.tm
    RE = R + SUBLANES

    def fn(i, gprev, g, gnext, up, upn, da, dan, cw, cb):
        last = i == tl.n - 1
        xwin = jnp.concatenate([jnp.where(i > 0, gprev, 0.0), g, jnp.where(last, 0.0, gnext)], axis=0)
        gate = causal_conv(xwin, cw, 3, RE) + cb
        upe = jnp.concatenate([up, upn], axis=0)
        dae = jnp.concatenate([da, jnp.where(last, 0.0, dan)], axis=0)
        s = jax.nn.sigmoid(gate)
        dgate = dae * upe * (s * (1.0 + gate * (1.0 - s)))
        dup = da * (gate[0:R, :] * s[0:R, :])
        dgp = cw[0:1, :] * dgate[2:2 + R, :] + cw[1:2, :] * dgate[1:1 + R, :] + cw[2:3, :] * dgate[0:R, :]
        dcw = rows_to8([jnp.sum(dgate[0:R, :] * xwin[6 + j:6 + j + R, :], axis=0, keepdims=True) for j in range(3)], tc)
        dcb = jnp.sum(dgate[0:R, :], axis=0, keepdims=True)
        return dgp, dup, dcw, dcb

    return rowcall(name, fn, [gp, gp, gp, up, up, dact, dact, fcw8, fcb],
                   [tl.prevj(tc), tl.rowj(tc), tl.nxtj(tc), tl.rowj(tc), tl.nxtj(tc), tl.rowj(tc), tl.nxtj(tc),
                    constj(SUBLANES, tc), constj(1, tc)],
                   [sds((T, F), BF16), sds((T, F), BF16), sds((SUBLANES, F)), sds((1, F))],
                   [tl.rowj(tc), tl.rowj(tc), constj(SUBLANES, tc), constj(1, tc)], [False, False, True, True], tl.n, F // tc)


def _me():
    return lax.axis_index("x"), lax.axis_index("y"), lax.axis_index("c")


def all_gather(shards, name):
    nt = len(shards)

    def body(*refs):
        xs, outs = refs[:nt], refs[nt:2 * nt]
        send_sems, recv_sems, local_sems = refs[2 * nt:]
        x, y, c = _me()
        me, sibling = (x, y, c), (x, y, 1 - c)
        chips = [(1 - x, y), (x, 1 - y), (1 - x, 1 - y)]

        def slot(t, p):
            return outs[t].at[4 * p[0] + 2 * p[1] + p[2]]

        def copy(t, k, block, to, src=None):
            return pltpu.make_async_remote_copy(
                src_ref=slot(t, block) if src is None else src, dst_ref=slot(t, block),
                send_sem=send_sems.at[t, k], recv_sem=recv_sems.at[t, k], device_id=to, device_id_type=MESH)

        mine = [pltpu.make_async_copy(xs[t], slot(t, me), local_sems.at[t]) for t in range(nt)]
        first = []
        for t in range(nt):
            mine[t].start()
            first.append(copy(t, 0, me, sibling, src=xs[t]))
            first += [copy(t, 1 + j, me, (*chip, c), src=xs[t]) for j, chip in enumerate(chips)]
        for cp in first:
            cp.start()
        passed = []
        for j, chip in enumerate(chips):
            for t in range(nt):
                copy(t, 1 + j, (*chip, c), me).wait_recv()
                cp = copy(t, 4 + j, (*chip, c), sibling)
                cp.start()
                passed.append(cp)
        for t in range(nt):
            copy(t, 0, sibling, me).wait_recv()
            for j, chip in enumerate(chips):
                copy(t, 4 + j, (*chip, 1 - c), me).wait_recv()
        for cp in first + passed:
            cp.wait_send()
        for t in range(nt):
            mine[t].wait()

    any_spec = pl.BlockSpec(memory_space=pl.ANY)
    return pl.pallas_call(
        body, name=name, in_specs=[any_spec] * nt, out_specs=[any_spec] * nt,
        out_shape=[jax.ShapeDtypeStruct((N_DEV,) + s.shape, s.dtype) for s in shards],
        scratch_shapes=[pltpu.SemaphoreType.DMA((nt, 7)), pltpu.SemaphoreType.DMA((nt, 7)), pltpu.SemaphoreType.DMA((nt,))],
    )(*shards)


_HBM = pl.BlockSpec(memory_space=pltpu.HBM)
_SEM = pl.BlockSpec(memory_space=pltpu.SEMAPHORE)
_ANY = pl.BlockSpec(memory_space=pl.ANY)
_DATAFLOW = pltpu.SideEffectType.DATAFLOW_SIDE_EFFECTING


def _peers():
    x, y, c = _me()
    out = []
    for k in range(1, N_DEV):
        p = (x ^ (k >> 2), y ^ ((k >> 1) & 1), c ^ (k & 1))
        out.append((k, p, 4 * p[0] + 2 * p[1] + p[2]))
    return out


def _split_copy(src, land, send_sems, recv_sems, t, k, peer, slot, my, scatter, receiving):
    return pltpu.make_async_remote_copy(
        src_ref=src.at[slot] if scatter else src, dst_ref=land.at[slot if receiving else my],
        send_sem=send_sems.at[t * (N_DEV - 1) + k - 1], recv_sem=recv_sems.at[t * (N_DEV - 1) + k - 1],
        device_id=peer, device_id_type=MESH)


def comm_start(groups, scatter, name):
    flat = [a for g in groups for a in g]
    nt = len(flat)
    lands = [lax.empty(a.shape if scatter else (N_DEV,) + a.shape, a.dtype) for a in flat]
    ng = len(groups)

    def body(*refs):
        src, land = refs[:nt], refs[nt:2 * nt]
        sems = refs[2 * nt:2 * nt + 2 * ng]
        token = refs[-1]
        x, y, c = _me()
        my = 4 * x + 2 * y + c
        t0 = 0
        for gi, g in enumerate(groups):
            for k, peer, slot in _peers():
                for t in range(len(g)):
                    _split_copy(src[t0 + t], land[t0 + t], sems[2 * gi], sems[2 * gi + 1], t, k, peer, slot, my, scatter,
                                False).start()
            t0 += len(g)
        token[...] = jnp.zeros_like(token)

    sem_shapes = []
    for g in groups:
        sem_shapes += [pltpu.SemaphoreType.DMA((len(g) * (N_DEV - 1),))] * 2
    res = pl.pallas_call(
        body, name=name, in_specs=[_HBM] * (2 * nt),
        out_specs=[_SEM] * (2 * ng) + [_HBM] * (2 * nt) + [pl.BlockSpec(memory_space=pltpu.VMEM)],
        out_shape=sem_shapes + [pltpu.HBM(a.shape, a.dtype) for a in flat + lands] + [sds((SUBLANES, LANES))],
        input_output_aliases={i: 2 * ng + i for i in range(2 * nt)},
        compiler_params=pltpu.CompilerParams(has_side_effects=_DATAFLOW),
    )(*[pltpu.with_memory_space_constraint(a, pltpu.HBM) for a in flat + lands])
    handles = []
    t0 = 0
    for gi, g in enumerate(groups):
        n = len(g)
        handles.append(dict(sems=(res[2 * gi], res[2 * gi + 1]), src=res[2 * ng + t0:2 * ng + t0 + n],
                            land=res[2 * ng + nt + t0:2 * ng + nt + t0 + n], scatter=scatter))
        t0 += n
    return handles, res[-1]


def comm_wait(handle, after, name):
    src, land, scatter = handle["src"], handle["land"], handle["scatter"]
    nt = len(src)

    def body(*refs):
        src_r, land_r = refs[:nt], refs[nt:2 * nt]
        send_sems, recv_sems = refs[2 * nt], refs[2 * nt + 1]
        x, y, c = _me()
        my = 4 * x + 2 * y + c
        for k, peer, slot in _peers():
            for t in range(nt):
                _split_copy(src_r[t], land_r[t], send_sems, recv_sems, t, k, peer, slot, my, scatter, False).wait_send()
                _split_copy(src_r[t], land_r[t], send_sems, recv_sems, t, k, peer, slot, my, scatter, True).wait_recv()

    res = pl.pallas_call(
        body, name=name, in_specs=[_HBM] * (2 * nt) + [_SEM, _SEM, _ANY], out_specs=[_HBM] * (2 * nt),
        out_shape=[pltpu.HBM(a.shape, a.dtype) for a in list(src) + list(land)],
        input_output_aliases={i: i for i in range(2 * nt)},
        compiler_params=pltpu.CompilerParams(has_side_effects=_DATAFLOW),
    )(*src, *land, *handle["sems"], after)
    return res[:nt], res[nt:]


def sum_adamw_shard(own_src, land, me, w, m, v, l, prev, name, tr=256):
    L, R, C = w.shape
    tr = _pick(R, tr, SUBLANES)
    c1 = 1.0 - ADAM_B1 ** ADAM_STEP
    c2 = 1.0 - ADAM_B2 ** ADAM_STEP
    n_prev = 0 if prev is None else 4

    def body(me_ref, *refs):
        parts = refs[:N_DEV]
        w_r, m_r, v_r = refs[N_DEV:N_DEV + 3]
        g_o, d_o, m_o, v_o = refs[N_DEV + 3 + n_prev:]
        g = parts[0][0].astype(F32)
        for k in range(1, N_DEV):
            g = g + parts[k][0].astype(F32)
        mn = ADAM_B1 * m_r[0] + (1.0 - ADAM_B1) * g
        vn = ADAM_B2 * v_r[0] + (1.0 - ADAM_B2) * (g * g)
        g_o[0] = g
        d_o[0] = -ADAM_LR * ((mn / c1) / (jnp.sqrt(vn / c2) + ADAM_EPS) + ADAM_WD * w_r[0])
        m_o[0] = mn
        v_o[0] = vn

    part_specs = [pl.BlockSpec((1, tr, C), lambda i, me, k=k: (me[0] ^ k, i, 0)) for k in range(N_DEV)]
    lay = pl.BlockSpec((1, tr, C), lambda i, me: (l, i, 0))
    grid_spec = pltpu.PrefetchScalarGridSpec(
        num_scalar_prefetch=1, grid=(R // tr,), in_specs=part_specs + [lay] * 3 + [_ANY] * n_prev, out_specs=[lay] * 4)
    return pl.pallas_call(
        body, name=name, grid_spec=grid_spec, out_shape=[sds((L, R, C))] * 4,
        input_output_aliases={1 + N_DEV + 3 + j: j for j in range(n_prev)}, compiler_params=_params(("parallel",)),
    )(me, own_src, *[land] * (N_DEV - 1), w, m, v, *([] if prev is None else prev))


def sum_adamw(parts, w, m, v, name, tr=256):
    _, R, C = parts.shape
    tr = _pick(R, tr, SUBLANES)
    c1 = 1.0 - ADAM_B1 ** ADAM_STEP
    c2 = 1.0 - ADAM_B2 ** ADAM_STEP

    def body(p, w, m, v, g_o, d_o, m_o, v_o):
        g = p[0].astype(F32)
        for d in range(1, N_DEV):
            g = g + p[d].astype(F32)
        mn = ADAM_B1 * m[...] + (1.0 - ADAM_B1) * g
        vn = ADAM_B2 * v[...] + (1.0 - ADAM_B2) * (g * g)
        m_hat = mn / c1
        v_hat = vn / c2
        g_o[...] = g
        d_o[...] = -ADAM_LR * (m_hat / (jnp.sqrt(v_hat) + ADAM_EPS) + ADAM_WD * w[...])
        m_o[...] = mn
        v_o[...] = vn

    blk = pl.BlockSpec((tr, C), lambda i: (i, 0))
    return pl.pallas_call(
        body, name=name, grid=(R // tr,), in_specs=[pl.BlockSpec((N_DEV, tr, C), lambda i: (0, i, 0)), blk, blk, blk],
        out_specs=[blk] * 4, out_shape=[sds((R, C))] * 4, compiler_params=_params(("parallel",)),
    )(parts, w, m, v)


SHARDED = ("w_in", "dn_conv_w", "w_branch_a", "w_branch_b", "w_out", "ffn_w_gate", "ffn_w_up", "ffn_conv_w", "ffn_w_down")
COL_SHARDED = ("w_in", "dn_conv_w", "w_branch_a", "w_branch_b", "ffn_w_gate", "ffn_w_up", "ffn_conv_w")
CONV_WEIGHTS = ("dn_conv_w", "ffn_conv_w")
REPLICATED = ("norm1_g", "dn_a_log", "dn_dt_bias", "dn_onorm_g", "sg_ln_g", "sg_ln_b", "sg_w", "sg_b", "norm2_g",
              "ffn_conv_b", "final_norm_g")
WEIGHTS = ("norm1_g", "w_in", "dn_conv_w", "dn_a_log", "dn_dt_bias", "dn_onorm_g", "sg_ln_g", "sg_ln_b", "sg_w", "sg_b",
           "w_branch_a", "w_branch_b", "w_out", "norm2_g", "ffn_w_gate", "ffn_w_up", "ffn_conv_w", "ffn_conv_b",
           "ffn_w_down", "final_norm_g")


def _assemble(name, g):
    if name in COL_SHARDED:
        L, r, cs = g.shape[1:]
        return jnp.transpose(g, (1, 2, 0, 3)).reshape(L, r, N_DEV * cs)
    L, rs, cdim = g.shape[1:]
    return jnp.transpose(g, (1, 0, 2, 3)).reshape(L, N_DEV * rs, cdim)


def _split(name, full_grad):
    if name in COL_SHARDED:
        L, r, cfull = full_grad.shape
        return jnp.transpose(full_grad.reshape(L, r, N_DEV, cfull // N_DEV), (2, 0, 1, 3))
    L, rfull, cdim = full_grad.shape
    return jnp.transpose(full_grad.reshape(L, N_DEV, rfull // N_DEV, cdim), (1, 0, 2, 3))


def _pad_lanes(a, lo, width=LANES):
    return jnp.pad(a, ((0, 0), (lo, width - lo - a.shape[1])))


def _pad_rows(a, rows=SUBLANES):
    return jnp.pad(a, ((0, rows - a.shape[0]), (0, 0)))


def kernel(x, norm1_g, w_in, dn_conv_w, dn_a_log, dn_dt_bias, dn_onorm_g, sg_ln_g, sg_ln_b, sg_w, sg_b, w_branch_a, w_branch_b, w_out, norm2_g, ffn_w_gate, ffn_w_up, ffn_conv_w, ffn_conv_b, ffn_w_down, final_norm_g, loss_target, m_norm1_g, m_w_in, m_dn_conv_w, m_dn_a_log, m_dn_dt_bias, m_dn_onorm_g, m_sg_ln_g, m_sg_ln_b, m_sg_w, m_sg_b, m_w_branch_a, m_w_branch_b, m_w_out, m_norm2_g, m_ffn_w_gate, m_ffn_w_up, m_ffn_conv_w, m_ffn_conv_b, m_ffn_w_down, m_final_norm_g, v_norm1_g, v_w_in, v_dn_conv_w, v_dn_a_log, v_dn_dt_bias, v_dn_onorm_g, v_sg_ln_g, v_sg_ln_b, v_sg_w, v_sg_b, v_w_branch_a, v_w_branch_b, v_w_out, v_norm2_g, v_ffn_w_gate, v_ffn_w_up, v_ffn_conv_w, v_ffn_conv_b, v_ffn_w_down, v_final_norm_g):
    W = dict(norm1_g=norm1_g, w_in=w_in, dn_conv_w=dn_conv_w, dn_a_log=dn_a_log, dn_dt_bias=dn_dt_bias, dn_onorm_g=dn_onorm_g,
             sg_ln_g=sg_ln_g, sg_ln_b=sg_ln_b, sg_w=sg_w, sg_b=sg_b, w_branch_a=w_branch_a, w_branch_b=w_branch_b, w_out=w_out,
             norm2_g=norm2_g, ffn_w_gate=ffn_w_gate, ffn_w_up=ffn_w_up, ffn_conv_w=ffn_conv_w, ffn_conv_b=ffn_conv_b,
             ffn_w_down=ffn_w_down, final_norm_g=final_norm_g)
    Mo = dict(norm1_g=m_norm1_g, w_in=m_w_in, dn_conv_w=m_dn_conv_w, dn_a_log=m_dn_a_log, dn_dt_bias=m_dn_dt_bias,
              dn_onorm_g=m_dn_onorm_g, sg_ln_g=m_sg_ln_g, sg_ln_b=m_sg_ln_b, sg_w=m_sg_w, sg_b=m_sg_b, w_branch_a=m_w_branch_a,
              w_branch_b=m_w_branch_b, w_out=m_w_out, norm2_g=m_norm2_g, ffn_w_gate=m_ffn_w_gate, ffn_w_up=m_ffn_w_up,
              ffn_conv_w=m_ffn_conv_w, ffn_conv_b=m_ffn_conv_b, ffn_w_down=m_ffn_w_down, final_norm_g=m_final_norm_g)
    Vo = dict(norm1_g=v_norm1_g, w_in=v_w_in, dn_conv_w=v_dn_conv_w, dn_a_log=v_dn_a_log, dn_dt_bias=v_dn_dt_bias,
              dn_onorm_g=v_dn_onorm_g, sg_ln_g=v_sg_ln_g, sg_ln_b=v_sg_ln_b, sg_w=v_sg_w, sg_b=v_sg_b, w_branch_a=v_w_branch_a,
              w_branch_b=v_w_branch_b, w_out=v_w_out, norm2_g=v_norm2_g, ffn_w_gate=v_ffn_w_gate, ffn_w_up=v_ffn_w_up,
              ffn_conv_w=v_ffn_conv_w, ffn_conv_b=v_ffn_conv_b, ffn_w_down=v_ffn_w_down, final_norm_g=v_final_norm_g)

    xs = x[0]
    tgt = loss_target[0]
    T, D = xs.shape
    depth = norm1_g.shape[0]
    H = dn_a_log.shape[1]
    G = sg_w.shape[1]
    WA = H * HEAD_DIM
    WB = G * HEAD_DIM
    N = T // DN_CHUNK
    colA = 4 * WA
    colB0 = colA + 2 * H
    cb_a = (2 * WB) // D

    my = 4 * lax.axis_index("x") + 2 * lax.axis_index("y") + lax.axis_index("c")
    me_arr = my.astype(jnp.int32).reshape(1)

    def shard(n, l):
        return W[n][l] if n in CONV_WEIGHTS else W[n][l].astype(BF16)

    gather_names = []
    for l in range(depth):
        first = [("w_in", l), ("dn_conv_w", l)]
        rest = [("w_branch_a", l), ("w_branch_b", l), ("w_out", l)]
        gather_names += ([first, rest] if l == 0 else [first + rest])
        gather_names.append([("ffn_w_gate", l), ("ffn_w_up", l), ("ffn_conv_w", l), ("ffn_w_down", l)])
    gather_handles, _ = comm_start([[shard(n, l) for n, l in g] for g in gather_names], False, "gather_start")
    full_w = {}

    def need(n, l, after):
        if (n, l) not in full_w:
            gi = [i for i, g in enumerate(gather_names) if (n, l) in g][0]
            src, land = comm_wait(gather_handles[gi], after, f"gather_wait{gi}")
            for (gn, gl), s, ld in zip(gather_names[gi], src, land):
                ld = lax.dynamic_update_index_in_dim(ld, s, my, 0)
                full_w[(gn, gl)] = _assemble(gn, ld[:, None])[0]
        return full_w[(n, l)]

    def layer_weights(l):
        return dict(
            g1=norm1_g[l][None], g2=norm2_g[l][None], alog=_pad_lanes(dn_a_log[l][None], H), dtb=_pad_lanes(dn_dt_bias[l][None], H),
            og=dn_onorm_g[l][None], lng=sg_ln_g[l][None], lnb=sg_ln_b[l][None], sgw=sg_w[l], sgbT=sg_b[l].T, fcb=ffn_conv_b[l][None])

    def mixer_in_weights(p, l, after):
        w_in_l = need("w_in", l, after)
        p.update(wA=w_in_l[:, :colA], wba=_pad_lanes(w_in_l[:, colA:colB0], 0), wB=w_in_l[:, colB0:],
                 cw8=_pad_rows(need("dn_conv_w", l, after)))

    def mixer_out_weights(p, l, after):
        p.update(wa=need("w_branch_a", l, after), wb=need("w_branch_b", l, after), wo=need("w_out", l, after))

    def ffn_weights(p, l, after):
        p.update(wg=need("ffn_w_gate", l, after), wu=need("ffn_w_up", l, after), fcw8=_pad_rows(need("ffn_conv_w", l, after)),
                 wd=need("ffn_w_down", l, after))

    def to_chunks(bg):
        bt = bg[:, :2 * H].T
        beta_c = bt[:H].reshape(H, N, DN_CHUNK, 1)
        gc_c = bt[H:].reshape(H, N, DN_CHUNK, 1)
        return beta_c, gc_c, bt[H:].reshape(H, N, 1, DN_CHUNK)

    saved = []
    cur = xs
    for l in range(depth):
        p = layer_weights(l)
        t = f"l{l}_"
        h = norm_fwd(cur, p["g1"], t + "norm1")
        mixer_in_weights(p, l, h)
        projA = matmul(h, p["wA"], "nn", t + "projA")
        pba = matmul(h, p["wba"], "nn", t + "proj_ba")
        projB = matmul(h, p["wB"], "nn", t + "projB")
        q, k, v, bg = dn_prep_fwd(projA, pba, p["cw8"], p["alog"], p["dtb"], H, t + "dn_prep")
        beta_c, gc_c, gc_r = to_chunks(bg)
        u, w, a, qd, kd, gl = dn_chunk_fwd(q, k, v, beta_c, gc_c, gc_r, t + "dn_chunk")
        o, s_in = dn_scan_fwd(u, w, a, qd, kd, gl, t + "dn_scan")
        y_a = dn_post_fwd(o, projA, p["og"], t + "dn_post")
        y_b = gmlp_fwd(projB, p["lng"], p["lnb"], p["sgw"], p["sgbT"], t + "gmlp")
        mixer_out_weights(p, l, y_b)
        ap = matmul(y_a, p["wa"], "nn", t + "branch_a")
        bp = matmul(y_b, p["wb"], "nn", t + "branch_b")
        merged = merge_fwd(projB, ap, bp, cb_a, t + "merge")
        x1 = matmul(merged, p["wo"], "nn", t + "out_proj", c=cur)
        h2 = norm_fwd(x1, p["g2"], t + "norm2")
        ffn_weights(p, l, h2)
        gp = matmul(h2, p["wg"], "nn", t + "ffn_gate")
        up = matmul(h2, p["wu"], "nn", t + "ffn_up")
        act = ffn_act_fwd(gp, up, p["fcw8"], p["fcb"], t + "ffn_act")
        x2 = matmul(act, p["wd"], "nn", t + "ffn_down", c=x1)
        saved.append(dict(p=p, x0=cur, h=h, projA=projA, pba=pba, projB=projB, q=q, k=k, v=v, chunks=(beta_c, gc_c, gc_r),
                          scan=(u, w, a, qd, kd, gl), s_in=s_in, o=o, y_a=y_a, y_b=y_b, ap=ap, bp=bp, merged=merged, x1=x1,
                          h2=h2, gp=gp, up=up, act=act))
        cur = x2

    loss_part, dx, d_final = head_fwd_bwd(cur, final_norm_g[None], tgt, "loss_head")
    loss = lax.psum(loss_part[0, 0], ("x", "y", "c"))

    grads_sh = {n: [None] * depth for n in SHARDED}
    grads_rep = {n: [None] * depth for n in REPLICATED if n != "final_norm_g"}
    exchanges = []

    def exchange(names, l, name):
        srcs = [_split(n, grads_sh[n][l][None])[:, 0].astype(F32 if n in CONV_WEIGHTS else BF16) for n in names]
        (handle,), tok = comm_start([srcs], True, name)
        exchanges.append((names, l, handle))
        return tok

    mixer_tok = None
    for l in reversed(range(depth)):
        s = saved[l]
        p = s["p"]
        t = f"l{l}_b_"
        dact = matmul(dx, p["wd"], "nt", t + "d_act")
        grads_sh["ffn_w_down"][l] = matmul(s["act"], dx, "tn", t + "dw_down")
        fcb = p["fcb"] if mixer_tok is None else p["fcb"] + mixer_tok[0, 0]
        dgp, dup, dfcw, dfcb = ffn_act_bwd(s["gp"], s["up"], p["fcw8"], fcb, dact, t + "ffn_act")
        dh2 = matmul(dgp, p["wg"], "nt", t + "dh2_gate")
        dh2 = matmul(dup, p["wu"], "nt", t + "dh2_up", c=dh2)
        grads_sh["ffn_w_gate"][l] = matmul(s["h2"], dgp, "tn", t + "dw_gate")
        grads_sh["ffn_w_up"][l] = matmul(s["h2"], dup, "tn", t + "dw_up")
        grads_sh["ffn_conv_w"][l] = dfcw[:3]
        grads_rep["ffn_conv_b"][l] = dfcb[0]
        tok = exchange(("ffn_w_down", "ffn_w_gate", "ffn_w_up", "ffn_conv_w"), l, t + "ffn_grads_start")
        dx1, dg2 = norm_bwd(s["x1"], p["g2"] + tok[0, 0], dh2, dx, t + "norm2")
        grads_rep["norm2_g"][l] = dg2[0]
        dmerged = matmul(dx1, p["wo"], "nt", t + "d_merged")
        grads_sh["w_out"][l] = matmul(s["merged"], dx1, "tn", t + "dw_out")
        dga, dgb, dap, dbp = merge_bwd(s["projB"], s["ap"], s["bp"], dmerged, cb_a, t + "merge")
        dya = matmul(dap, p["wa"], "nt", t + "d_ya")
        dyb = matmul(dbp, p["wb"], "nt", t + "d_yb")
        grads_sh["w_branch_a"][l] = matmul(s["y_a"], dap, "tn", t + "dw_a")
        grads_sh["w_branch_b"][l] = matmul(s["y_b"], dbp, "tn", t + "dw_b")
        du_raw, dv_raw, dlng, dlnb, dsgw, dsgbT = gmlp_bwd(s["projB"], p["lng"], p["lnb"], p["sgw"], p["sgbT"], dyb, t + "gmlp")
        grads_rep["sg_ln_g"][l], grads_rep["sg_ln_b"][l] = dlng[0], dlnb[0]
        grads_rep["sg_w"][l], grads_rep["sg_b"][l] = dsgw, dsgbT.T
        do, dz, dog = dn_post_bwd(s["o"], s["projA"], p["og"], dya, t + "dn_post")
        grads_rep["dn_onorm_g"][l] = dog[0]
        du, dw, da, dqd, dkd, dgl = dn_scan_bwd(*s["scan"], s["s_in"], do, t + "dn_scan")
        dq, dk, dv, dbc, dgc, dgr = dn_chunk_bwd(s["q"], s["k"], s["v"], *s["chunks"], du, dw, da, dqd, dkd, dgl, t + "dn_chunk")
        dbg1 = _pad_lanes(jnp.concatenate([dbc.reshape(H, T), dgc.reshape(H, T)], axis=0).T, 0)
        dbg2 = _pad_lanes(dgr.reshape(H, T).T, H)
        dqkv, dba, dcw, dalog, ddtb = dn_prep_bwd(s["projA"], s["pba"], p["cw8"], p["alog"], p["dtb"], dq, dk, dv, dbg1, dbg2, H,
                                                  t + "dn_prep")
        grads_sh["dn_conv_w"][l] = dcw[:4]
        grads_rep["dn_a_log"][l], grads_rep["dn_dt_bias"][l] = dalog[0, H:2 * H], ddtb[0, H:2 * H]
        dprojA = jnp.concatenate([dqkv, dz], axis=1)
        dprojB = jnp.concatenate([du_raw, dv_raw, dga, dgb], axis=1)
        dh = matmul(dprojA, p["wA"], "nt", t + "dh_A")
        dh = matmul(dba, p["wba"], "nt", t + "dh_ba", c=dh)
        dh = matmul(dprojB, p["wB"], "nt", t + "dh_B", c=dh)
        dwA = matmul(s["h"], dprojA, "tn", t + "dw_A")
        dwba = matmul(s["h"], dba, "tn", t + "dw_ba")
        dwB = matmul(s["h"], dprojB, "tn", t + "dw_B")
        grads_sh["w_in"][l] = jnp.concatenate([dwA, dwba[:, :2 * H], dwB], axis=1)
        mixer_tok = exchange(("w_out", "w_branch_a", "w_branch_b", "dn_conv_w", "w_in"), l, t + "mixer_grads_start")
        dx, dg1 = norm_bwd(s["x0"], p["g1"] + mixer_tok[0, 0], dh, dx1, t + "norm1")
        grads_rep["norm1_g"][l] = dg1[0]

    out = {}
    after = dx
    for gi, (names, l, handle) in enumerate(exchanges):
        src, land = comm_wait(handle, after, f"grads_wait{gi}")
        for n, s_, ld in zip(names, src, land):
            res = sum_adamw_shard(s_, ld, me_arr, W[n], Mo[n], Vo[n], l, out.get(n), f"adamw_{n}_{l}")
            out[n] = list(res)
            after = res[0]

    rep_full = {n: (jnp.stack(grads_rep[n]) if n != "final_norm_g" else d_final[0]) for n in REPLICATED}
    sizes = [math.prod(W[n].shape) for n in REPLICATED]
    total = sum(sizes)
    rows = -(-total // LANES)
    rows = -(-rows // SUBLANES) * SUBLANES

    def pack(d):
        flat = jnp.concatenate([d[n].reshape(-1).astype(F32) for n in REPLICATED])
        return jnp.pad(flat, (0, rows * LANES - total)).reshape(rows, LANES)

    (rep_parts,) = all_gather([pack(rep_full)], "gather_small_grads")
    res = sum_adamw(rep_parts, pack(W), pack(Mo), pack(Vo), "adamw_small")
    offs = 0
    for n, sz in zip(REPLICATED, sizes):
        out[n] = [r.reshape(-1)[offs:offs + sz].reshape(W[n].shape) for r in res]
        offs += sz

    return (loss, dx[None], *[out[n][0] for n in WEIGHTS], *[out[n][1] for n in WEIGHTS],
            *[out[n][2] for n in WEIGHTS], *[out[n][3] for n in WEIGHTS])
```

```python
import functools
import math

import jax
import jax.numpy as jnp
from jax import lax
from jax.experimental import pallas as pl
from jax.experimental.pallas import tpu as pltpu

F32 = jnp.float32
BF16 = jnp.bfloat16
EPS = 1e-6
N_DEV = 8
LANES = 128
SUBLANES = 8
HEAD_DIM = 128
DN_CHUNK = 64
SG_CHUNK = 128
VMEM_LIMIT = 56 * 1024 * 1024
MESH = pl.DeviceIdType.MESH
HIGHEST = lax.Precision.HIGHEST

ADAM_LR = 0.001
ADAM_B1 = 0.9
ADAM_B2 = 0.999
ADAM_EPS = 1e-08
ADAM_WD = 0.01
ADAM_STEP = 10


def _pick(n, target, mult=LANES):
    best = None
    d = mult
    while d <= min(n, target):
        if n % d == 0:
            best = d
        d += mult
    return n if best is None else best


def _params(sem):
    return pltpu.CompilerParams(dimension_semantics=sem, vmem_limit_bytes=VMEM_LIMIT)


_NN = (((1,), (0,)), ((), ()))
_NT = (((1,), (1,)), ((), ()))
_TN = (((0,), (0,)), ((), ()))


def _dg(a, b, dims, hi):
    if hi == 2:
        return lax.dot_general(a.astype(F32), b.astype(F32), dims, precision=HIGHEST, preferred_element_type=F32)
    if hi == 1:
        a_hi, b_hi = a.astype(BF16), b.astype(BF16)
        a_lo, b_lo = (a - a_hi.astype(F32)).astype(BF16), (b - b_hi.astype(F32)).astype(BF16)
        ax, bx = dims[0][0][0], dims[0][1][0]
        a = jnp.concatenate([a_hi, a_hi, a_lo], axis=ax)
        b = jnp.concatenate([b_hi, b_lo, b_hi], axis=bx)
        return lax.dot_general(a, b, dims, preferred_element_type=F32)
    return lax.dot_general(a.astype(BF16), b.astype(BF16), dims, preferred_element_type=F32)


@functools.partial(jax.custom_vjp, nondiff_argnums=(2,))
def mm_nn(a, b, hi=False):
    return _dg(a, b, _NN, hi)


def _mm_nn_f(a, b, hi):
    return _dg(a, b, _NN, hi), (a, b)


def _mm_nn_b(hi, res, g):
    a, b = res
    return mm_nt(g, b, hi), mm_tn(a, g, hi)


@functools.partial(jax.custom_vjp, nondiff_argnums=(2,))
def mm_nt(a, b, hi=False):
    return _dg(a, b, _NT, hi)


def _mm_nt_f(a, b, hi):
    return _dg(a, b, _NT, hi), (a, b)


def _mm_nt_b(hi, res, g):
    a, b = res
    return mm_nn(g, b, hi), mm_tn(g, a, hi)


@functools.partial(jax.custom_vjp, nondiff_argnums=(2,))
def mm_tn(a, b, hi=False):
    return _dg(a, b, _TN, hi)


def _mm_tn_f(a, b, hi):
    return _dg(a, b, _TN, hi), (a, b)


def _mm_tn_b(hi, res, g):
    a, b = res
    return mm_nt(b, g, hi), mm_nn(a, g, hi)


mm_nn.defvjp(_mm_nn_f, _mm_nn_b)
mm_nt.defvjp(_mm_nt_f, _mm_nt_b)
mm_tn.defvjp(_mm_tn_f, _mm_tn_b)


def matmul(a, b, mode, name, c=None, out_dtype=F32, tm=1024, tn=1024, tk=2048):
    if mode == "nn":
        (M, K), N = a.shape, b.shape[1]
    elif mode == "nt":
        (M, K), N = a.shape, b.shape[0]
    else:
        (K, M), N = a.shape, b.shape[1]
    tm, tn, tk = _pick(M, tm), _pick(N, tn), _pick(K, tk)
    nk = K // tk
    dims = {"nn": _NN, "nt": _NT, "tn": _TN}[mode]
    a_spec = pl.BlockSpec((tk, tm), lambda i, j, k: (k, i)) if mode == "tn" else pl.BlockSpec((tm, tk), lambda i, j, k: (i, k))
    b_spec = pl.BlockSpec((tn, tk), lambda i, j, k: (j, k)) if mode == "nt" else pl.BlockSpec((tk, tn), lambda i, j, k: (k, j))
    o_spec = pl.BlockSpec((tm, tn), lambda i, j, k: (i, j))
    has_c = c is not None

    own_acc = nk > 1 and out_dtype != F32

    def body(*refs):
        a_ref, b_ref = refs[:2]
        c_ref = refs[2] if has_c else None
        o_ref = refs[3] if has_c else refs[2]
        acc_ref = refs[-1] if own_acc else o_ref

        def dot():
            return lax.dot_general(a_ref[...].astype(BF16), b_ref[...].astype(BF16), dims, preferred_element_type=F32)

        if nk == 1:
            o_ref[...] = (dot() + c_ref[...] if has_c else dot()).astype(o_ref.dtype)
        else:
            @pl.when(pl.program_id(2) == 0)
            def _():
                acc_ref[...] = c_ref[...] if has_c else jnp.zeros_like(acc_ref)

            acc_ref[...] += dot()
            if own_acc:
                @pl.when(pl.program_id(2) == nk - 1)
                def _():
                    o_ref[...] = acc_ref[...].astype(o_ref.dtype)

    ins = [a, b] + ([c] if has_c else [])
    specs = [a_spec, b_spec] + ([o_spec] if has_c else [])
    return pl.pallas_call(
        body, name=name, grid=(M // tm, N // tn, nk), in_specs=specs, out_specs=o_spec,
        out_shape=jax.ShapeDtypeStruct((M, N), out_dtype), scratch_shapes=[pltpu.VMEM((tm, tn), F32)] if own_acc else [],
        compiler_params=_params(("parallel", "parallel", "arbitrary")),
    )(*ins)


def rowcall(name, fn, ins, in_specs, outs, out_specs, acc, nrow, ncol=1):
    n_in = len(ins)

    def body(*refs):
        i = pl.program_id(1)
        res = fn(i, *[r[...] for r in refs[:n_in]])
        for r, v, is_acc in zip(refs[n_in:], res, acc):
            if is_acc:
                @pl.when(i == 0)
                def _(r=r, v=v):
                    r[...] = v.astype(r.dtype)

                @pl.when(i > 0)
                def _(r=r, v=v):
                    r[...] += v.astype(r.dtype)
            else:
                r[...] = v.astype(r.dtype)

    return pl.pallas_call(
        body, name=name, grid=(ncol, nrow), in_specs=list(in_specs), out_specs=list(out_specs), out_shape=list(outs),
        compiler_params=_params(("parallel", "arbitrary")),
    )(*ins)


class Tiles:
    def __init__(self, T, tm):
        self.T, self.tm, self.n = T, tm, T // tm
        self.r8 = tm // SUBLANES

    def row(self, w, cb=0):
        return pl.BlockSpec((self.tm, w), lambda j, i: (i, cb))

    def rowj(self, tc):
        return pl.BlockSpec((self.tm, tc), lambda j, i: (i, j))

    def prev(self, w, cb=0):
        return pl.BlockSpec((SUBLANES, w), lambda j, i: (jnp.maximum(i * self.r8 - 1, 0), cb))

    def prevj(self, tc):
        return pl.BlockSpec((SUBLANES, tc), lambda j, i: (jnp.maximum(i * self.r8 - 1, 0), j))

    def nxt(self, w, cb=0):
        last = self.T // SUBLANES - 1
        return pl.BlockSpec((SUBLANES, w), lambda j, i: (jnp.minimum((i + 1) * self.r8, last), cb))

    def nxtj(self, tc):
        last = self.T // SUBLANES - 1
        return pl.BlockSpec((SUBLANES, tc), lambda j, i: (jnp.minimum((i + 1) * self.r8, last), j))

    def heads(self, H):
        return pl.BlockSpec((H, self.tm, HEAD_DIM), lambda j, i: (0, i, 0))

    def heads_nxt(self, H):
        last = self.T // SUBLANES - 1
        return pl.BlockSpec((H, SUBLANES, HEAD_DIM), lambda j, i: (0, jnp.minimum((i + 1) * self.r8, last), 0))


def full(shape):
    return pl.BlockSpec(tuple(shape), lambda j, i: (0,) * len(shape))


def constj(r, tc):
    return pl.BlockSpec((r, tc), lambda j, i: (0, j))


def sds(shape, dtype=F32):
    return jax.ShapeDtypeStruct(tuple(shape), dtype)


def rms(x, g):
    return x * lax.rsqrt(jnp.mean(x * x, axis=-1, keepdims=True) + EPS) * g


def silu(x):
    return x * jax.nn.sigmoid(x)


def gelu(x):
    return 0.5 * x * (1.0 + lax.erf(x * (2.0 ** -0.5)))


def causal_conv(xwin, w, K, R):
    base = SUBLANES - (K - 1)
    out = w[0:1, :] * xwin[base:base + R, :]
    for j in range(1, K):
        out = out + w[j:j + 1, :] * xwin[base + j:base + j + R, :]
    return out


def rows_to8(rows, C):
    rid = lax.broadcasted_iota(jnp.int32, (SUBLANES, C), 0)
    out = jnp.zeros((SUBLANES, C), F32)
    for k, r in enumerate(rows):
        out = out + jnp.where(rid == k, jnp.broadcast_to(r, (SUBLANES, C)), 0.0)
    return out


def dn_qkv(pre, H):
    a = silu(pre)
    W = H * HEAD_DIM

    def l2(t):
        return t * lax.rsqrt(jnp.sum(t * t, axis=-1, keepdims=True) + EPS)

    q = [l2(a[:, h * HEAD_DIM:(h + 1) * HEAD_DIM]) for h in range(H)]
    k = [l2(a[:, W + h * HEAD_DIM:W + (h + 1) * HEAD_DIM]) for h in range(H)]
    v = [a[:, 2 * W + h * HEAD_DIM:2 * W + (h + 1) * HEAD_DIM] for h in range(H)]
    return q, k, v


def dn_gates(ba, alog, dtb, H, R):
    lane = lax.broadcasted_iota(jnp.int32, (R, LANES), 1)
    beta = jax.nn.sigmoid(ba)
    g = -jnp.exp(alog) * jax.nn.softplus(ba + dtb)
    g = jnp.where((lane >= H) & (lane < 2 * H), g, 0.0)
    ri = lax.broadcasted_iota(jnp.int32, (R, R), 0)
    ci = lax.broadcasted_iota(jnp.int32, (R, R), 1)
    cum = jnp.where((ri // DN_CHUNK == ci // DN_CHUNK) & (ci <= ri), 1.0, 0.0).astype(F32)
    gc = mm_nn(cum, g, 2)
    return jnp.where(lane < H, beta, gc)


def neumann_inverse(Ls):
    C = Ls[0].shape[0]
    ri = lax.broadcasted_iota(jnp.int32, (C, C), 0)
    ci = lax.broadcasted_iota(jnp.int32, (C, C), 1)
    eye = jnp.where(ri == ci, 1.0, 0.0).astype(F32)
    P = [-L for L in Ls]
    R = [eye + p for p in P]
    for _ in range(int(math.log2(C)) - 1):
        P = [mm_nn(p, p, 1) for p in P]
        R = [r + mm_nn(r, p, 1) for r, p in zip(R, P)]
    return R


def dn_chunk(q, k, v, beta, gc, gr):
    n = len(q)
    C = q[0].shape[0]
    ri = lax.broadcasted_iota(jnp.int32, (C, C), 0)
    ci = lax.broadcasted_iota(jnp.int32, (C, C), 1)
    qs = [q[h] * (HEAD_DIM ** -0.5) for h in range(n)]
    kb = [k[h] * beta[h] for h in range(n)]
    vb = [v[h] * beta[h] for h in range(n)]
    decay = [jnp.exp(jnp.where(ri >= ci, gc[h] - gr[h], -jnp.inf)) for h in range(n)]
    L = [jnp.where(ri > ci, mm_nt(kb[h], k[h]) * decay[h], 0.0) for h in range(n)]
    attn = [jnp.where(ri >= ci, mm_nt(qs[h], k[h]) * decay[h], 0.0) for h in range(n)]
    Tinv = neumann_inverse(L)
    eg = [jnp.exp(gc[h]) for h in range(n)]
    u = [mm_nn(Tinv[h], vb[h]) for h in range(n)]
    w = [mm_nn(Tinv[h], kb[h] * eg[h]) for h in range(n)]
    qd = [qs[h] * eg[h] for h in range(n)]
    gl = [gc[h][C - 1:C, :] for h in range(n)]
    kd = [k[h] * jnp.exp(gl[h] - gc[h]) for h in range(n)]
    return u, w, attn, qd, kd, gl


def dn_step(u, w, a, qd, kd, gl, S):
    n = len(u)
    v_new = [u[h] - mm_nn(w[h], S[h]) for h in range(n)]
    o = [mm_nn(qd[h], S[h]) + mm_nn(a[h], v_new[h]) for h in range(n)]
    S_new = [S[h] * jnp.exp(gl[h]) + mm_tn(kd[h], v_new[h]) for h in range(n)]
    return o, S_new


def dn_post(o, z, g):
    H = o.shape[0]
    return jnp.concatenate([rms(o[h], g) * silu(z[:, h * HEAD_DIM:(h + 1) * HEAD_DIM]) for h in range(H)], axis=1)


def gmlp(u_raw, v_raw, ln_g, ln_b, sgw, sgbT):
    R = u_raw.shape[0]
    G = sgw.shape[0]
    nc = R // SG_CHUNK
    u = gelu(u_raw)
    vv = gelu(v_raw)
    xc = vv - jnp.mean(vv, axis=-1, keepdims=True)
    vg = xc * lax.rsqrt(jnp.mean(xc * xc, axis=-1, keepdims=True) + EPS) * ln_g + ln_b
    ri = lax.broadcasted_iota(jnp.int32, (SG_CHUNK, SG_CHUNK), 0)
    ci = lax.broadcasted_iota(jnp.int32, (SG_CHUNK, SG_CHUNK), 1)
    cols = []
    for g in range(G):
        ws = jnp.where(ri >= ci, sgw[g], 0.0)
        rhs = jnp.concatenate([vg[c * SG_CHUNK:(c + 1) * SG_CHUNK, g * HEAD_DIM:(g + 1) * HEAD_DIM] for c in range(nc)], axis=1)
        mixed = mm_nn(ws, rhs) + sgbT[:, g:g + 1]
        cols.append(jnp.concatenate([mixed[:, c * HEAD_DIM:(c + 1) * HEAD_DIM] for c in range(nc)], axis=0))
    return u * jnp.concatenate(cols, axis=1)


def merge(ga, gb, ap, bp):
    return jax.nn.sigmoid(ga) * ap + jax.nn.sigmoid(gb) * bp


def norm_fwd(x, g, name, tm=256):
    T, D = x.shape
    tl = Tiles(T, _pick(T, tm))
    (h,) = rowcall(name, lambda i, x, g: (rms(x, g),), [x, g], [tl.row(D), full((1, D))],
                   [sds((T, D), BF16)], [tl.row(D)], [False], tl.n)
    return h


def norm_bwd(x, g, dh, dres, name, tm=256):
    T, D = x.shape
    tl = Tiles(T, _pick(T, tm))

    def fn(i, x, g, dh, dres):
        _, vj = jax.vjp(rms, x, g)
        dx, dg = vj(dh.astype(F32))
        return dx + dres, dg

    return rowcall(name, fn, [x, g, dh, dres], [tl.row(D), full((1, D)), tl.row(D), tl.row(D)],
                   [sds((T, D)), sds((1, D))], [tl.row(D), full((1, D))], [False, True], tl.n)


def head_fwd_bwd(x, g, tgt, name, tm=256):
    T, D = x.shape
    tl = Tiles(T, _pick(T, tm))

    def fn(i, x, g, tgt):
        y, vj = jax.vjp(rms, x, g)
        e = y - tgt
        loss = 0.5 * jnp.sum(jnp.mean(e * e, axis=-1, keepdims=True), axis=0, keepdims=True)
        dx, dg = vj(e * (1.0 / D))
        return loss, dx, dg

    return rowcall(name, fn, [x, g, tgt], [tl.row(D), full((1, D)), tl.row(D)],
                   [sds((1, 1)), sds((T, D)), sds((1, D))], [full((1, 1)), tl.row(D), full((1, D))],
                   [True, False, True], tl.n)


def dn_prep_fwd(projA, pba, cw8, alog, dtb, H, name, tm=256):
    T = projA.shape[0]
    W3 = 3 * H * HEAD_DIM
    tl = Tiles(T, _pick(T, tm, DN_CHUNK))
    R = tl.tm

    def fn(i, xp, x, ba, cw, alog, dtb):
        xwin = jnp.concatenate([jnp.where(i > 0, xp, 0.0), x], axis=0)
        q, k, v = dn_qkv(causal_conv(xwin, cw, 4, R), H)
        return jnp.stack(q), jnp.stack(k), jnp.stack(v), dn_gates(ba, alog, dtb, H, R)

    hs = sds((H, T, HEAD_DIM))
    return rowcall(name, fn, [projA, projA, pba, cw8, alog, dtb],
                   [tl.prev(W3), tl.row(W3), tl.row(LANES), full((SUBLANES, W3)), full((1, LANES)), full((1, LANES))],
                   [hs, hs, hs, sds((T, LANES))], [tl.heads(H)] * 3 + [tl.row(LANES)], [False] * 4, tl.n)


def dn_prep_bwd(projA, pba, cw8, alog, dtb, dq, dk, dv, dbg1, dbg2, H, name, tm=256):
    T = projA.shape[0]
    W3 = 3 * H * HEAD_DIM
    tl = Tiles(T, _pick(T, tm, DN_CHUNK))
    R = tl.tm
    RE = R + SUBLANES

    def fn(i, xp, x, xn, ba, cw, alog, dtb, dq, dk, dv, dqn, dkn, dvn, dbg1, dbg2):
        last = i == tl.n - 1
        xwin = jnp.concatenate([jnp.where(i > 0, xp, 0.0), x, jnp.where(last, 0.0, xn)], axis=0)
        pre = causal_conv(xwin, cw, 4, RE)
        ext = lambda d, dn: [jnp.concatenate([d[h], jnp.where(last, 0.0, dn[h])], axis=0) for h in range(H)]
        _, vj = jax.vjp(lambda p: dn_qkv(p, H), pre)
        (dpre,) = vj((ext(dq, dqn), ext(dk, dkn), ext(dv, dvn)))
        dx = cw[0:1, :] * dpre[3:3 + R, :]
        for j in range(1, 4):
            dx = dx + cw[j:j + 1, :] * dpre[3 - j:3 - j + R, :]
        dcw = rows_to8([jnp.sum(dpre[0:R, :] * xwin[5 + j:5 + j + R, :], axis=0, keepdims=True) for j in range(4)], W3)
        _, vjg = jax.vjp(lambda ba, alog, dtb: dn_gates(ba, alog, dtb, H, R), ba, alog, dtb)
        dba, dalog, ddtb = vjg(dbg1 + dbg2)
        return dx, dba, dcw, dalog, ddtb

    return rowcall(name, fn, [projA, projA, projA, pba, cw8, alog, dtb, dq, dk, dv, dq, dk, dv, dbg1, dbg2],
                   [tl.prev(W3), tl.row(W3), tl.nxt(W3), tl.row(LANES), full((SUBLANES, W3)), full((1, LANES)), full((1, LANES))]
                   + [tl.heads(H)] * 3 + [tl.heads_nxt(H)] * 3 + [tl.row(LANES)] * 2,
                   [sds((T, W3), BF16), sds((T, LANES), BF16), sds((SUBLANES, W3)), sds((1, LANES)), sds((1, LANES))],
                   [tl.row(W3), tl.row(LANES), full((SUBLANES, W3)), full((1, LANES)), full((1, LANES))],
                   [False, False, True, True, True], tl.n)


def _chunk_specs(H, C):
    hs = pl.BlockSpec((H, C, HEAD_DIM), lambda n: (0, n, 0))
    col = pl.BlockSpec((H, 1, C, 1), lambda n: (0, n, 0, 0))
    rw = pl.BlockSpec((H, 1, 1, C), lambda n: (0, n, 0, 0))
    at = pl.BlockSpec((H, C, C), lambda n: (0, n, 0))
    one = pl.BlockSpec((H, 1, 1, 1), lambda n: (0, n, 0, 0))
    return hs, col, rw, at, one


def dn_chunk_fwd(q, k, v, beta_c, gc_c, gc_r, name):
    H, T, _ = q.shape
    C = DN_CHUNK
    N = T // C
    hs, col, rw, at, one = _chunk_specs(H, C)

    def body(q, k, v, bc, gc, gr, u, w, a, qd, kd, gl):
        hd = range(H)
        res = dn_chunk([q[h] for h in hd], [k[h] for h in hd], [v[h] for h in hd], [bc[h, 0] for h in hd],
                       [gc[h, 0] for h in hd], [gr[h, 0] for h in hd])
        for h in hd:
            for ref, val in zip((u, w, a, qd, kd), res[:5]):
                ref[h] = val[h]
            gl[h, 0] = res[5][h]

    big = sds((H, T, HEAD_DIM))
    return pl.pallas_call(
        body, name=name, grid=(N,), in_specs=[hs, hs, hs, col, col, rw], out_specs=[hs, hs, at, hs, hs, one],
        out_shape=[big, big, sds((H, T, C)), big, big, sds((H, N, 1, 1))], compiler_params=_params(("parallel",)),
    )(q, k, v, beta_c, gc_c, gc_r)


def dn_chunk_bwd(q, k, v, beta_c, gc_c, gc_r, du, dw, da, dqd, dkd, dgl, name):
    H, T, _ = q.shape
    C = DN_CHUNK
    N = T // C
    hs, col, rw, at, one = _chunk_specs(H, C)

    def body(q, k, v, bc, gc, gr, du, dw, da, dqd, dkd, dgl, dq, dk, dv, dbc, dgc, dgr):
        hd = range(H)
        _, vj = jax.vjp(dn_chunk, [q[h] for h in hd], [k[h] for h in hd], [v[h] for h in hd], [bc[h, 0] for h in hd],
                        [gc[h, 0] for h in hd], [gr[h, 0] for h in hd])
        res = vj(([du[h] for h in hd], [dw[h] for h in hd], [da[h] for h in hd], [dqd[h] for h in hd],
                  [dkd[h] for h in hd], [dgl[h, 0] for h in hd]))
        for h in hd:
            dq[h], dk[h], dv[h] = res[0][h], res[1][h], res[2][h]
            dbc[h, 0], dgc[h, 0], dgr[h, 0] = res[3][h], res[4][h], res[5][h]

    big = sds((H, T, HEAD_DIM))
    return pl.pallas_call(
        body, name=name, grid=(N,), in_specs=[hs, hs, hs, col, col, rw, hs, hs, at, hs, hs, one],
        out_specs=[hs, hs, hs, col, col, rw],
        out_shape=[big, big, big, sds((H, N, C, 1)), sds((H, N, C, 1)), sds((H, N, 1, C))],
        compiler_params=_params(("parallel",)),
    )(q, k, v, beta_c, gc_c, gc_r, du, dw, da, dqd, dkd, dgl)


def dn_scan_fwd(u, w, a, qd, kd, gl, name):
    H, T, _ = u.shape
    C = DN_CHUNK
    N = T // C
    hs, _, _, at, one = _chunk_specs(H, C)
    st = pl.BlockSpec((1, H, HEAD_DIM, HEAD_DIM), lambda n: (n, 0, 0, 0))

    def body(u, w, a, qd, kd, gl, o, s_in, S):
        @pl.when(pl.program_id(0) == 0)
        def _():
            S[...] = jnp.zeros_like(S)

        hd = range(H)
        s = [S[h] for h in hd]
        o_new, s_new = dn_step([u[h] for h in hd], [w[h] for h in hd], [a[h] for h in hd], [qd[h] for h in hd],
                               [kd[h] for h in hd], [gl[h, 0] for h in hd], s)
        for h in hd:
            s_in[0, h] = s[h]
            o[h] = o_new[h]
            S[h] = s_new[h]

    return pl.pallas_call(
        body, name=name, grid=(N,), in_specs=[hs, hs, at, hs, hs, one], out_specs=[hs, st],
        out_shape=[sds((H, T, HEAD_DIM)), sds((N, H, HEAD_DIM, HEAD_DIM))],
        scratch_shapes=[pltpu.VMEM((H, HEAD_DIM, HEAD_DIM), F32)], compiler_params=_params(("arbitrary",)),
    )(u, w, a, qd, kd, gl)


def dn_scan_bwd(u, w, a, qd, kd, gl, s_in, do, name):
    H, T, _ = u.shape
    C = DN_CHUNK
    N = T // C
    rev = lambda spec_shape, f: pl.BlockSpec(spec_shape, f)
    hs = rev((H, C, HEAD_DIM), lambda n: (0, N - 1 - n, 0))
    at = rev((H, C, C), lambda n: (0, N - 1 - n, 0))
    one = rev((H, 1, 1, 1), lambda n: (0, N - 1 - n, 0, 0))
    st = rev((1, H, HEAD_DIM, HEAD_DIM), lambda n: (N - 1 - n, 0, 0, 0))

    def body(u, w, a, qd, kd, gl, s_in, do, du, dw, da, dqd, dkd, dgl, dS):
        @pl.when(pl.program_id(0) == 0)
        def _():
            dS[...] = jnp.zeros_like(dS)

        hd = range(H)
        _, vj = jax.vjp(dn_step, [u[h] for h in hd], [w[h] for h in hd], [a[h] for h in hd], [qd[h] for h in hd],
                        [kd[h] for h in hd], [gl[h, 0] for h in hd], [s_in[0, h] for h in hd])
        res = vj(([do[h] for h in hd], [dS[h] for h in hd]))
        for h in hd:
            du[h], dw[h], da[h], dqd[h], dkd[h] = (res[j][h] for j in range(5))
            dgl[h, 0] = res[5][h]
            dS[h] = res[6][h]

    big = sds((H, T, HEAD_DIM))
    return pl.pallas_call(
        body, name=name, grid=(N,), in_specs=[hs, hs, at, hs, hs, one, st, hs], out_specs=[hs, hs, at, hs, hs, one],
        out_shape=[big, big, sds((H, T, C)), big, big, sds((H, N, 1, 1))],
        scratch_shapes=[pltpu.VMEM((H, HEAD_DIM, HEAD_DIM), F32)], compiler_params=_params(("arbitrary",)),
    )(u, w, a, qd, kd, gl, s_in, do)


def dn_post_fwd(o, projA, g, name, tm=256):
    H, T, _ = o.shape
    W = H * HEAD_DIM
    tl = Tiles(T, _pick(T, tm))
    (y,) = rowcall(name, lambda i, o, z, g: (dn_post(o, z, g),), [o, projA, g], [tl.heads(H), tl.row(W, 3), full((1, HEAD_DIM))],
                   [sds((T, W), BF16)], [tl.row(W)], [False], tl.n)
    return y


def dn_post_bwd(o, projA, g, dy, name, tm=256):
    H, T, _ = o.shape
    W = H * HEAD_DIM
    tl = Tiles(T, _pick(T, tm))

    def fn(i, o, z, g, dy):
        _, vj = jax.vjp(dn_post, o, z, g)
        return vj(dy.astype(F32))

    return rowcall(name, fn, [o, projA, g, dy], [tl.heads(H), tl.row(W, 3), full((1, HEAD_DIM)), tl.row(W)],
                   [sds((H, T, HEAD_DIM)), sds((T, W), BF16), sds((1, HEAD_DIM))],
                   [tl.heads(H), tl.row(W), full((1, HEAD_DIM))], [False, False, True], tl.n)


def gmlp_fwd(projB, ln_g, ln_b, sgw, sgbT, name, tm=512):
    T = projB.shape[0]
    G = sgw.shape[0]
    W = G * HEAD_DIM
    tl = Tiles(T, _pick(T, tm))
    (y,) = rowcall(name, lambda i, *a: (gmlp(*a),), [projB, projB, ln_g, ln_b, sgw, sgbT],
                   [tl.row(W, 0), tl.row(W, 1), full((1, W)), full((1, W)), full(sgw.shape), full(sgbT.shape)],
                   [sds((T, W), BF16)], [tl.row(W)], [False], tl.n)
    return y


def gmlp_bwd(projB, ln_g, ln_b, sgw, sgbT, dy, name, tm=512):
    T = projB.shape[0]
    G = sgw.shape[0]
    W = G * HEAD_DIM
    tl = Tiles(T, _pick(T, tm))

    def fn(i, u_raw, v_raw, ln_g, ln_b, sgw, sgbT, dy):
        _, vj = jax.vjp(gmlp, u_raw, v_raw, ln_g, ln_b, sgw, sgbT)
        return vj(dy.astype(F32))

    return rowcall(name, fn, [projB, projB, ln_g, ln_b, sgw, sgbT, dy],
                   [tl.row(W, 0), tl.row(W, 1), full((1, W)), full((1, W)), full(sgw.shape), full(sgbT.shape), tl.row(W)],
                   [sds((T, W), BF16), sds((T, W), BF16), sds((1, W)), sds((1, W)), sds(sgw.shape), sds(sgbT.shape)],
                   [tl.row(W), tl.row(W), full((1, W)), full((1, W)), full(sgw.shape), full(sgbT.shape)],
                   [False, False, True, True, True, True], tl.n)


def merge_fwd(projB, ap, bp, cb_a, name, tm=256):
    T, D = ap.shape
    tl = Tiles(T, _pick(T, tm))
    (m,) = rowcall(name, lambda i, *a: (merge(*a),), [projB, projB, ap, bp],
                   [tl.row(D, cb_a), tl.row(D, cb_a + 1), tl.row(D), tl.row(D)], [sds((T, D), BF16)], [tl.row(D)], [False], tl.n)
    return m


def merge_bwd(projB, ap, bp, dm, cb_a, name, tm=256):
    T, D = ap.shape
    tl = Tiles(T, _pick(T, tm))

    def fn(i, ga, gb, ap, bp, dm):
        _, vj = jax.vjp(merge, ga, gb, ap, bp)
        return vj(dm.astype(F32))

    return rowcall(name, fn, [projB, projB, ap, bp, dm], [tl.row(D, cb_a), tl.row(D, cb_a + 1), tl.row(D), tl.row(D), tl.row(D)],
                   [sds((T, D), BF16)] * 4, [tl.row(D)] * 4, [False] * 4, tl.n)


def ffn_act_fwd(gp, up, fcw8, fcb, name, tm=256, tc=512):
    T, F = gp.shape
    tl = Tiles(T, _pick(T, tm))
    tc = _pick(F, tc)
    R = tl.tm

    def fn(i, gprev, g, up, cw, cb):
        xwin = jnp.concatenate([jnp.where(i > 0, gprev, 0.0), g], axis=0)
        return (silu(causal_conv(xwin, cw, 3, R) + cb) * up,)

    (act,) = rowcall(name, fn, [gp, gp, up, fcw8, fcb], [tl.prevj(tc), tl.rowj(tc), tl.rowj(tc), constj(SUBLANES, tc), constj(1, tc)],
                     [sds((T, F), BF16)], [tl.rowj(tc)], [False], tl.n, F // tc)
    return act


def ffn_act_bwd(gp, up, fcw8, fcb, dact, name, tm=256, tc=512):
    T, F = gp.shape
    tl = Tiles(T, _pick(T, tm))
    tc = _pick(F, tc)
    R = tl.tm
    RE = R + SUBLANES

    def fn(i, gprev, g, gnext, up, upn, da, dan, cw, cb):
        last = i == tl.n - 1
        xwin = jnp.concatenate([jnp.where(i > 0, gprev, 0.0), g, jnp.where(last, 0.0, gnext)], axis=0)
        gate = causal_conv(xwin, cw, 3, RE) + cb
        upe = jnp.concatenate([up, upn], axis=0)
        dae = jnp.concatenate([da, jnp.where(last, 0.0, dan)], axis=0)
        s = jax.nn.sigmoid(gate)
        dgate = dae * upe * (s * (1.0 + gate * (1.0 - s)))
        dup = da * (gate[0:R, :] * s[0:R, :])
        dgp = cw[0:1, :] * dgate[2:2 + R, :] + cw[1:2, :] * dgate[1:1 + R, :] + cw[2:3, :] * dgate[0:R, :]
        dcw = rows_to8([jnp.sum(dgate[0:R, :] * xwin[6 + j:6 + j + R, :], axis=0, keepdims=True) for j in range(3)], tc)
        dcb = jnp.sum(dgate[0:R, :], axis=0, keepdims=True)
        return dgp, dup, dcw, dcb

    return rowcall(name, fn, [gp, gp, gp, up, up, dact, dact, fcw8, fcb],
                   [tl.prevj(tc), tl.rowj(tc), tl.nxtj(tc), tl.rowj(tc), tl.nxtj(tc), tl.rowj(tc), tl.nxtj(tc),
                    constj(SUBLANES, tc), constj(1, tc)],
                   [sds((T, F), BF16), sds((T, F), BF16), sds((SUBLANES, F)), sds((1, F))],
                   [tl.rowj(tc), tl.rowj(tc), constj(SUBLANES, tc), constj(1, tc)], [False, False, True, True], tl.n, F // tc)


def _me():
    return lax.axis_index("x"), lax.axis_index("y"), lax.axis_index("c")


def all_gather(shards, name):
    nt = len(shards)

    def body(*refs):
        xs, outs = refs[:nt], refs[nt:2 * nt]
        send_sems, recv_sems, local_sems = refs[2 * nt:]
        x, y, c = _me()
        me, sibling = (x, y, c), (x, y, 1 - c)
        chips = [(1 - x, y), (x, 1 - y), (1 - x, 1 - y)]

        def slot(t, p):
            return outs[t].at[4 * p[0] + 2 * p[1] + p[2]]

        def copy(t, k, block, to, src=None):
            return pltpu.make_async_remote_copy(
                src_ref=slot(t, block) if src is None else src, dst_ref=slot(t, block),
                send_sem=send_sems.at[t, k], recv_sem=recv_sems.at[t, k], device_id=to, device_id_type=MESH)

        mine = [pltpu.make_async_copy(xs[t], slot(t, me), local_sems.at[t]) for t in range(nt)]
        first = []
        for t in range(nt):
            mine[t].start()
            first.append(copy(t, 0, me, sibling, src=xs[t]))
            first += [copy(t, 1 + j, me, (*chip, c), src=xs[t]) for j, chip in enumerate(chips)]
        for cp in first:
            cp.start()
        passed = []
        for j, chip in enumerate(chips):
            for t in range(nt):
                copy(t, 1 + j, (*chip, c), me).wait_recv()
                cp = copy(t, 4 + j, (*chip, c), sibling)
                cp.start()
                passed.append(cp)
        for t in range(nt):
            copy(t, 0, sibling, me).wait_recv()
            for j, chip in enumerate(chips):
                copy(t, 4 + j, (*chip, 1 - c), me).wait_recv()
        for cp in first + passed:
            cp.wait_send()
        for t in range(nt):
            mine[t].wait()

    any_spec = pl.BlockSpec(memory_space=pl.ANY)
    return pl.pallas_call(
        body, name=name, in_specs=[any_spec] * nt, out_specs=[any_spec] * nt,
        out_shape=[jax.ShapeDtypeStruct((N_DEV,) + s.shape, s.dtype) for s in shards],
        scratch_shapes=[pltpu.SemaphoreType.DMA((nt, 7)), pltpu.SemaphoreType.DMA((nt, 7)), pltpu.SemaphoreType.DMA((nt,))],
    )(*shards)


_HBM = pl.BlockSpec(memory_space=pltpu.HBM)
_SEM = pl.BlockSpec(memory_space=pltpu.SEMAPHORE)
_ANY = pl.BlockSpec(memory_space=pl.ANY)
_DATAFLOW = pltpu.SideEffectType.DATAFLOW_SIDE_EFFECTING


def _peers():
    x, y, c = _me()
    out = []
    for k in range(1, N_DEV):
        p = (x ^ (k >> 2), y ^ ((k >> 1) & 1), c ^ (k & 1))
        out.append((k, p, 4 * p[0] + 2 * p[1] + p[2]))
    return out


def _split_copy(src, land, send_sems, recv_sems, t, k, peer, slot, my, scatter, receiving):
    return pltpu.make_async_remote_copy(
        src_ref=src.at[slot] if scatter else src, dst_ref=land.at[slot if receiving else my],
        send_sem=send_sems.at[t * (N_DEV - 1) + k - 1], recv_sem=recv_sems.at[t * (N_DEV - 1) + k - 1],
        device_id=peer, device_id_type=MESH)


def comm_start(groups, scatter, name, after=None):
    flat = [a for g in groups for a in g]
    nt = len(flat)
    lands = [lax.empty(a.shape if scatter else (N_DEV,) + a.shape, a.dtype) for a in flat]
    ng = len(groups)
    n_after = 0 if after is None else 1

    def body(*refs):
        src, land = refs[:nt], refs[nt:2 * nt]
        sems = refs[2 * nt + n_after:2 * nt + n_after + 2 * ng]
        token = refs[-1]
        x, y, c = _me()
        my = 4 * x + 2 * y + c
        t0 = 0
        for gi, g in enumerate(groups):
            for k, peer, slot in _peers():
                for t in range(len(g)):
                    _split_copy(src[t0 + t], land[t0 + t], sems[2 * gi], sems[2 * gi + 1], t, k, peer, slot, my, scatter,
                                False).start()
            t0 += len(g)
        token[...] = jnp.zeros_like(token)

    sem_shapes = []
    for g in groups:
        sem_shapes += [pltpu.SemaphoreType.DMA((len(g) * (N_DEV - 1),))] * 2
    res = pl.pallas_call(
        body, name=name, in_specs=[_HBM] * (2 * nt) + [_ANY] * n_after,
        out_specs=[_SEM] * (2 * ng) + [_HBM] * (2 * nt) + [pl.BlockSpec(memory_space=pltpu.VMEM)],
        out_shape=sem_shapes + [pltpu.HBM(a.shape, a.dtype) for a in flat + lands] + [sds((SUBLANES, LANES))],
        input_output_aliases={i: 2 * ng + i for i in range(2 * nt)},
        compiler_params=pltpu.CompilerParams(has_side_effects=_DATAFLOW),
    )(*[pltpu.with_memory_space_constraint(a, pltpu.HBM) for a in flat + lands], *([] if after is None else [after]))
    handles = []
    t0 = 0
    for gi, g in enumerate(groups):
        n = len(g)
        handles.append(dict(sems=(res[2 * gi], res[2 * gi + 1]), src=res[2 * ng + t0:2 * ng + t0 + n],
                            land=res[2 * ng + nt + t0:2 * ng + nt + t0 + n], scatter=scatter))
        t0 += n
    return handles, res[-1]


def comm_wait(handle, after, name):
    src, land, scatter = handle["src"], handle["land"], handle["scatter"]
    nt = len(src)

    def body(*refs):
        src_r, land_r = refs[:nt], refs[nt:2 * nt]
        send_sems, recv_sems = refs[2 * nt], refs[2 * nt + 1]
        x, y, c = _me()
        my = 4 * x + 2 * y + c
        for k, peer, slot in _peers():
            for t in range(nt):
                _split_copy(src_r[t], land_r[t], send_sems, recv_sems, t, k, peer, slot, my, scatter, False).wait_send()
                _split_copy(src_r[t], land_r[t], send_sems, recv_sems, t, k, peer, slot, my, scatter, True).wait_recv()

    res = pl.pallas_call(
        body, name=name, in_specs=[_HBM] * (2 * nt) + [_SEM, _SEM, _ANY], out_specs=[_HBM] * (2 * nt),
        out_shape=[pltpu.HBM(a.shape, a.dtype) for a in list(src) + list(land)],
        input_output_aliases={i: i for i in range(2 * nt)},
        compiler_params=pltpu.CompilerParams(has_side_effects=_DATAFLOW),
    )(*src, *land, *handle["sems"], after)
    return res[:nt], res[nt:]


def sum_adamw_shard(own_src, land, me, w, m, v, l, prev, name, tr=256):
    L, R, C = w.shape
    tr = _pick(R, tr, SUBLANES)
    c1 = 1.0 - ADAM_B1 ** ADAM_STEP
    c2 = 1.0 - ADAM_B2 ** ADAM_STEP
    n_prev = 0 if prev is None else 4

    def body(me_ref, *refs):
        parts = refs[:N_DEV]
        w_r, m_r, v_r = refs[N_DEV:N_DEV + 3]
        g_o, d_o, m_o, v_o = refs[N_DEV + 3 + n_prev:]
        g = parts[0][0].astype(F32)
        for k in range(1, N_DEV):
            g = g + parts[k][0].astype(F32)
        mn = ADAM_B1 * m_r[0] + (1.0 - ADAM_B1) * g
        vn = ADAM_B2 * v_r[0] + (1.0 - ADAM_B2) * (g * g)
        g_o[0] = g
        d_o[0] = -ADAM_LR * ((mn / c1) / (jnp.sqrt(vn / c2) + ADAM_EPS) + ADAM_WD * w_r[0])
        m_o[0] = mn
        v_o[0] = vn

    part_specs = [pl.BlockSpec((1, tr, C), lambda i, me, k=k: (me[0] ^ k, i, 0)) for k in range(N_DEV)]
    lay = pl.BlockSpec((1, tr, C), lambda i, me: (l, i, 0))
    grid_spec = pltpu.PrefetchScalarGridSpec(
        num_scalar_prefetch=1, grid=(R // tr,), in_specs=part_specs + [lay] * 3 + [_ANY] * n_prev, out_specs=[lay] * 4)
    return pl.pallas_call(
        body, name=name, grid_spec=grid_spec, out_shape=[sds((L, R, C))] * 4,
        input_output_aliases={1 + N_DEV + 3 + j: j for j in range(n_prev)}, compiler_params=_params(("parallel",)),
    )(me, own_src, *[land] * (N_DEV - 1), w, m, v, *([] if prev is None else prev))


def sum_adamw(parts, w, m, v, name, tr=256):
    _, R, C = parts.shape
    tr = _pick(R, tr, SUBLANES)
    c1 = 1.0 - ADAM_B1 ** ADAM_STEP
    c2 = 1.0 - ADAM_B2 ** ADAM_STEP

    def body(p, w, m, v, g_o, d_o, m_o, v_o):
        g = p[0].astype(F32)
        for d in range(1, N_DEV):
            g = g + p[d].astype(F32)
        mn = ADAM_B1 * m[...] + (1.0 - ADAM_B1) * g
        vn = ADAM_B2 * v[...] + (1.0 - ADAM_B2) * (g * g)
        m_hat = mn / c1
        v_hat = vn / c2
        g_o[...] = g
        d_o[...] = -ADAM_LR * (m_hat / (jnp.sqrt(v_hat) + ADAM_EPS) + ADAM_WD * w[...])
        m_o[...] = mn
        v_o[...] = vn

    blk = pl.BlockSpec((tr, C), lambda i: (i, 0))
    return pl.pallas_call(
        body, name=name, grid=(R // tr,), in_specs=[pl.BlockSpec((N_DEV, tr, C), lambda i: (0, i, 0)), blk, blk, blk],
        out_specs=[blk] * 4, out_shape=[sds((R, C))] * 4, compiler_params=_params(("parallel",)),
    )(parts, w, m, v)


SHARDED = ("w_in", "dn_conv_w", "w_branch_a", "w_branch_b", "w_out", "ffn_w_gate", "ffn_w_up", "ffn_conv_w", "ffn_w_down")
COL_SHARDED = ("w_in", "dn_conv_w", "w_branch_a", "w_branch_b", "ffn_w_gate", "ffn_w_up", "ffn_conv_w")
CONV_WEIGHTS = ("dn_conv_w", "ffn_conv_w")
REPLICATED = ("norm1_g", "dn_a_log", "dn_dt_bias", "dn_onorm_g", "sg_ln_g", "sg_ln_b", "sg_w", "sg_b", "norm2_g",
              "ffn_conv_b", "final_norm_g")
WEIGHTS = ("norm1_g", "w_in", "dn_conv_w", "dn_a_log", "dn_dt_bias", "dn_onorm_g", "sg_ln_g", "sg_ln_b", "sg_w", "sg_b",
           "w_branch_a", "w_branch_b", "w_out", "norm2_g", "ffn_w_gate", "ffn_w_up", "ffn_conv_w", "ffn_conv_b",
           "ffn_w_down", "final_norm_g")


def _columns(pieces, lo, hi):
    out = []
    for a, start, width in pieces:
        s, e = max(lo, start), min(hi, start + width)
        if s < e:
            out.append(a[:, s - start:e - start])
    return out[0] if len(out) == 1 else jnp.concatenate(out, axis=1)


def _assemble(name, g):
    if name in COL_SHARDED:
        return jnp.concatenate([g[d] for d in range(N_DEV)], axis=1)
    return g.reshape(N_DEV * g.shape[1], g.shape[2])


def _split(name, pieces, dtype):
    total = sum(w for _, _, w in pieces)
    if name in COL_SHARDED:
        cs = total // N_DEV
        return jnp.stack([_columns(pieces, d * cs, (d + 1) * cs).astype(dtype) for d in range(N_DEV)])
    (a, _, _), = pieces
    return a.reshape(N_DEV, a.shape[0] // N_DEV, a.shape[1]).astype(dtype)


def _pad_lanes(a, lo, width=LANES):
    return jnp.pad(a, ((0, 0), (lo, width - lo - a.shape[1])))


def _pad_rows(a, rows=SUBLANES):
    return jnp.pad(a, ((0, rows - a.shape[0]), (0, 0)))


def kernel(x, norm1_g, w_in, dn_conv_w, dn_a_log, dn_dt_bias, dn_onorm_g, sg_ln_g, sg_ln_b, sg_w, sg_b, w_branch_a, w_branch_b, w_out, norm2_g, ffn_w_gate, ffn_w_up, ffn_conv_w, ffn_conv_b, ffn_w_down, final_norm_g, loss_target, m_norm1_g, m_w_in, m_dn_conv_w, m_dn_a_log, m_dn_dt_bias, m_dn_onorm_g, m_sg_ln_g, m_sg_ln_b, m_sg_w, m_sg_b, m_w_branch_a, m_w_branch_b, m_w_out, m_norm2_g, m_ffn_w_gate, m_ffn_w_up, m_ffn_conv_w, m_ffn_conv_b, m_ffn_w_down, m_final_norm_g, v_norm1_g, v_w_in, v_dn_conv_w, v_dn_a_log, v_dn_dt_bias, v_dn_onorm_g, v_sg_ln_g, v_sg_ln_b, v_sg_w, v_sg_b, v_w_branch_a, v_w_branch_b, v_w_out, v_norm2_g, v_ffn_w_gate, v_ffn_w_up, v_ffn_conv_w, v_ffn_conv_b, v_ffn_w_down, v_final_norm_g):
    W = dict(norm1_g=norm1_g, w_in=w_in, dn_conv_w=dn_conv_w, dn_a_log=dn_a_log, dn_dt_bias=dn_dt_bias, dn_onorm_g=dn_onorm_g,
             sg_ln_g=sg_ln_g, sg_ln_b=sg_ln_b, sg_w=sg_w, sg_b=sg_b, w_branch_a=w_branch_a, w_branch_b=w_branch_b, w_out=w_out,
             norm2_g=norm2_g, ffn_w_gate=ffn_w_gate, ffn_w_up=ffn_w_up, ffn_conv_w=ffn_conv_w, ffn_conv_b=ffn_conv_b,
             ffn_w_down=ffn_w_down, final_norm_g=final_norm_g)
    Mo = dict(norm1_g=m_norm1_g, w_in=m_w_in, dn_conv_w=m_dn_conv_w, dn_a_log=m_dn_a_log, dn_dt_bias=m_dn_dt_bias,
              dn_onorm_g=m_dn_onorm_g, sg_ln_g=m_sg_ln_g, sg_ln_b=m_sg_ln_b, sg_w=m_sg_w, sg_b=m_sg_b, w_branch_a=m_w_branch_a,
              w_branch_b=m_w_branch_b, w_out=m_w_out, norm2_g=m_norm2_g, ffn_w_gate=m_ffn_w_gate, ffn_w_up=m_ffn_w_up,
              ffn_conv_w=m_ffn_conv_w, ffn_conv_b=m_ffn_conv_b, ffn_w_down=m_ffn_w_down, final_norm_g=m_final_norm_g)
    Vo = dict(norm1_g=v_norm1_g, w_in=v_w_in, dn_conv_w=v_dn_conv_w, dn_a_log=v_dn_a_log, dn_dt_bias=v_dn_dt_bias,
              dn_onorm_g=v_dn_onorm_g, sg_ln_g=v_sg_ln_g, sg_ln_b=v_sg_ln_b, sg_w=v_sg_w, sg_b=v_sg_b, w_branch_a=v_w_branch_a,
              w_branch_b=v_w_branch_b, w_out=v_w_out, norm2_g=v_norm2_g, ffn_w_gate=v_ffn_w_gate, ffn_w_up=v_ffn_w_up,
              ffn_conv_w=v_ffn_conv_w, ffn_conv_b=v_ffn_conv_b, ffn_w_down=v_ffn_w_down, final_norm_g=v_final_norm_g)

    xs = x[0]
    tgt = loss_target[0]
    T, D = xs.shape
    depth = norm1_g.shape[0]
    H = dn_a_log.shape[1]
    G = sg_w.shape[1]
    WA = H * HEAD_DIM
    WB = G * HEAD_DIM
    N = T // DN_CHUNK
    colA = 4 * WA
    colB0 = colA + 2 * H
    cb_a = (2 * WB) // D

    my = 4 * lax.axis_index("x") + 2 * lax.axis_index("y") + lax.axis_index("c")
    me_arr = my.astype(jnp.int32).reshape(1)

    def shard(n, l):
        return W[n][l] if n in CONV_WEIGHTS else W[n][l].astype(BF16)

    first = [("w_in", 0), ("dn_conv_w", 0)]
    first_blocks = all_gather([shard(n, l) for n, l in first], "gather_first")
    gathered = dict(zip(first, first_blocks))
    gather_names = [[("w_branch_a", 0), ("w_branch_b", 0), ("w_out", 0)]]
    for l in range(depth):
        if l > 0:
            gather_names.append([("w_in", l), ("dn_conv_w", l), ("w_branch_a", l), ("w_branch_b", l), ("w_out", l)])
        gather_names.append([("ffn_w_gate", l), ("ffn_w_up", l), ("ffn_conv_w", l), ("ffn_w_down", l)])
    gather_handles, _ = comm_start([[shard(n, l) for n, l in g] for g in gather_names], False, "gather_start",
                                   after=first_blocks[0])

    def need(n, l, after):
        if (n, l) not in gathered:
            gi = [i for i, g in enumerate(gather_names) if (n, l) in g][0]
            src, land = comm_wait(gather_handles[gi], after, f"gather_wait{gi}")
            for key, s, ld in zip(gather_names[gi], src, land):
                gathered[key] = lax.dynamic_update_index_in_dim(ld, s, my, 0)
        return gathered[(n, l)]

    def full(n, l, after):
        return _assemble(n, need(n, l, after))

    def layer_weights(l):
        return dict(
            g1=norm1_g[l][None], g2=norm2_g[l][None], alog=_pad_lanes(dn_a_log[l][None], H), dtb=_pad_lanes(dn_dt_bias[l][None], H),
            og=dn_onorm_g[l][None], lng=sg_ln_g[l][None], lnb=sg_ln_b[l][None], sgw=sg_w[l], sgbT=sg_b[l].T, fcb=ffn_conv_b[l][None])

    def mixer_in_weights(p, l, after):
        g = need("w_in", l, after)
        cs = g.shape[2]
        blocks = [(g[d], d * cs, cs) for d in range(N_DEV)]
        p.update(wA=_columns(blocks, 0, colA), wba=_pad_lanes(_columns(blocks, colA, colB0), 0),
                 wB=_columns(blocks, colB0, N_DEV * cs), cw8=_pad_rows(full("dn_conv_w", l, after)))

    def mixer_out_weights(p, l, after):
        p.update(wa=full("w_branch_a", l, after), wb=full("w_branch_b", l, after), wo=full("w_out", l, after))

    def ffn_weights(p, l, after):
        p.update(wg=full("ffn_w_gate", l, after), wu=full("ffn_w_up", l, after), fcw8=_pad_rows(full("ffn_conv_w", l, after)),
                 wd=full("ffn_w_down", l, after))

    def to_chunks(bg):
        bt = bg[:, :2 * H].T
        beta_c = bt[:H].reshape(H, N, DN_CHUNK, 1)
        gc_c = bt[H:].reshape(H, N, DN_CHUNK, 1)
        return beta_c, gc_c, bt[H:].reshape(H, N, 1, DN_CHUNK)

    saved = []
    cur = xs
    for l in range(depth):
        p = layer_weights(l)
        t = f"l{l}_"
        h = norm_fwd(cur, p["g1"], t + "norm1")
        mixer_in_weights(p, l, h)
        projA = matmul(h, p["wA"], "nn", t + "projA")
        pba = matmul(h, p["wba"], "nn", t + "proj_ba")
        projB = matmul(h, p["wB"], "nn", t + "projB")
        q, k, v, bg = dn_prep_fwd(projA, pba, p["cw8"], p["alog"], p["dtb"], H, t + "dn_prep")
        beta_c, gc_c, gc_r = to_chunks(bg)
        u, w, a, qd, kd, gl = dn_chunk_fwd(q, k, v, beta_c, gc_c, gc_r, t + "dn_chunk")
        o, s_in = dn_scan_fwd(u, w, a, qd, kd, gl, t + "dn_scan")
        y_a = dn_post_fwd(o, projA, p["og"], t + "dn_post")
        y_b = gmlp_fwd(projB, p["lng"], p["lnb"], p["sgw"], p["sgbT"], t + "gmlp")
        mixer_out_weights(p, l, y_b)
        ap = matmul(y_a, p["wa"], "nn", t + "branch_a")
        bp = matmul(y_b, p["wb"], "nn", t + "branch_b")
        merged = merge_fwd(projB, ap, bp, cb_a, t + "merge")
        x1 = matmul(merged, p["wo"], "nn", t + "out_proj", c=cur)
        h2 = norm_fwd(x1, p["g2"], t + "norm2")
        ffn_weights(p, l, h2)
        gp = matmul(h2, p["wg"], "nn", t + "ffn_gate")
        up = matmul(h2, p["wu"], "nn", t + "ffn_up")
        act = ffn_act_fwd(gp, up, p["fcw8"], p["fcb"], t + "ffn_act")
        x2 = matmul(act, p["wd"], "nn", t + "ffn_down", c=x1)
        saved.append(dict(p=p, x0=cur, h=h, projA=projA, pba=pba, projB=projB, q=q, k=k, v=v, chunks=(beta_c, gc_c, gc_r),
                          scan=(u, w, a, qd, kd, gl), s_in=s_in, o=o, y_a=y_a, y_b=y_b, ap=ap, bp=bp, merged=merged, x1=x1,
                          h2=h2, gp=gp, up=up, act=act))
        cur = x2

    loss_part, dx, d_final = head_fwd_bwd(cur, final_norm_g[None], tgt, "loss_head")
    loss = lax.psum(loss_part[0, 0], ("x", "y", "c"))

    grads_sh = {n: [None] * depth for n in SHARDED}
    grads_rep = {n: [None] * depth for n in REPLICATED if n != "final_norm_g"}
    exchanges = []

    def exchange(names, l, name, after=None):
        srcs = [_split(n, grads_sh[n][l], F32 if n in CONV_WEIGHTS else BF16) for n in names]
        (handle,), tok = comm_start([srcs], True, name, after=after)
        exchanges.append((names, l, handle))
        return tok

    def whole(a):
        return [(a, 0, a.shape[1])]

    sizes = [math.prod(W[n].shape) for n in REPLICATED]
    total = sum(sizes)
    rows = -(-total // LANES)
    rows = -(-rows // SUBLANES) * SUBLANES

    def pack(d):
        flat = jnp.concatenate([d[n].reshape(-1).astype(F32) for n in REPLICATED])
        return jnp.pad(flat, (0, rows * LANES - total)).reshape(rows, LANES)

    mixer_tok = None
    for l in reversed(range(depth)):
        s = saved[l]
        p = s["p"]
        t = f"l{l}_b_"
        dact = matmul(dx, p["wd"], "nt", t + "d_act")
        grads_sh["ffn_w_down"][l] = whole(matmul(s["act"], dx, "tn", t + "dw_down", out_dtype=BF16))
        fcb = p["fcb"] if mixer_tok is None else p["fcb"] + mixer_tok[0, 0]
        dgp, dup, dfcw, dfcb = ffn_act_bwd(s["gp"], s["up"], p["fcw8"], fcb, dact, t + "ffn_act")
        dh2 = matmul(dgp, p["wg"], "nt", t + "dh2_gate")
        dh2 = matmul(dup, p["wu"], "nt", t + "dh2_up", c=dh2)
        grads_sh["ffn_w_gate"][l] = whole(matmul(s["h2"], dgp, "tn", t + "dw_gate", out_dtype=BF16))
        grads_sh["ffn_w_up"][l] = whole(matmul(s["h2"], dup, "tn", t + "dw_up", out_dtype=BF16))
        grads_sh["ffn_conv_w"][l] = whole(dfcw[:3])
        grads_rep["ffn_conv_b"][l] = dfcb[0]
        tok = exchange(("ffn_w_down", "ffn_w_gate", "ffn_w_up", "ffn_conv_w"), l, t + "ffn_grads_start")
        dx1, dg2 = norm_bwd(s["x1"], p["g2"] + tok[0, 0], dh2, dx, t + "norm2")
        grads_rep["norm2_g"][l] = dg2[0]
        dmerged = matmul(dx1, p["wo"], "nt", t + "d_merged")
        grads_sh["w_out"][l] = whole(matmul(s["merged"], dx1, "tn", t + "dw_out", out_dtype=BF16))
        dga, dgb, dap, dbp = merge_bwd(s["projB"], s["ap"], s["bp"], dmerged, cb_a, t + "merge")
        dya = matmul(dap, p["wa"], "nt", t + "d_ya")
        dyb = matmul(dbp, p["wb"], "nt", t + "d_yb")
        grads_sh["w_branch_a"][l] = whole(matmul(s["y_a"], dap, "tn", t + "dw_a", out_dtype=BF16))
        grads_sh["w_branch_b"][l] = whole(matmul(s["y_b"], dbp, "tn", t + "dw_b", out_dtype=BF16))
        du_raw, dv_raw, dlng, dlnb, dsgw, dsgbT = gmlp_bwd(s["projB"], p["lng"], p["lnb"], p["sgw"], p["sgbT"], dyb, t + "gmlp")
        grads_rep["sg_ln_g"][l], grads_rep["sg_ln_b"][l] = dlng[0], dlnb[0]
        grads_rep["sg_w"][l], grads_rep["sg_b"][l] = dsgw, dsgbT.T
        do, dz, dog = dn_post_bwd(s["o"], s["projA"], p["og"], dya, t + "dn_post")
        grads_rep["dn_onorm_g"][l] = dog[0]
        du, dw, da, dqd, dkd, dgl = dn_scan_bwd(*s["scan"], s["s_in"], do, t + "dn_scan")
        dq, dk, dv, dbc, dgc, dgr = dn_chunk_bwd(s["q"], s["k"], s["v"], *s["chunks"], du, dw, da, dqd, dkd, dgl, t + "dn_chunk")
        dbg1 = _pad_lanes(jnp.concatenate([dbc.reshape(H, T), dgc.reshape(H, T)], axis=0).T, 0)
        dbg2 = _pad_lanes(dgr.reshape(H, T).T, H)
        dqkv, dba, dcw, dalog, ddtb = dn_prep_bwd(s["projA"], s["pba"], p["cw8"], p["alog"], p["dtb"], dq, dk, dv, dbg1, dbg2, H,
                                                  t + "dn_prep")
        grads_sh["dn_conv_w"][l] = whole(dcw[:4])
        grads_rep["dn_a_log"][l], grads_rep["dn_dt_bias"][l] = dalog[0, H:2 * H], ddtb[0, H:2 * H]
        tok = exchange(("w_out", "w_branch_a", "w_branch_b", "dn_conv_w"), l, t + "mixer_grads_start")
        dprojA = jnp.concatenate([dqkv, dz], axis=1)
        dprojB = jnp.concatenate([du_raw, dv_raw, dga, dgb], axis=1)
        dh = matmul(dprojA, p["wA"], "nt", t + "dh_A")
        dh = matmul(dba, p["wba"], "nt", t + "dh_ba", c=dh)
        dh = matmul(dprojB, p["wB"], "nt", t + "dh_B", c=dh)
        dx, dg1 = norm_bwd(s["x0"], p["g1"] + tok[0, 0], dh, dx1, t + "norm1")
        grads_rep["norm1_g"][l] = dg1[0]
        if l == 0:
            rep_full = {n: (jnp.stack(grads_rep[n]) if n != "final_norm_g" else d_final[0]) for n in REPLICATED}
            (small_handle,), small_tok = comm_start([[pack(rep_full)]], False, "small_grads_start")
        dwA = matmul(s["h"], dprojA, "tn", t + "dw_A", out_dtype=BF16)
        dwba = matmul(s["h"], dba, "tn", t + "dw_ba", out_dtype=BF16)
        dwB = matmul(s["h"], dprojB, "tn", t + "dw_B", out_dtype=BF16)
        grads_sh["w_in"][l] = [(dwA, 0, colA), (dwba, colA, 2 * H), (dwB, colB0, dwB.shape[1])]
        mixer_tok = exchange(("w_in",), l, t + "w_in_grads_start", after=small_tok if l == 0 else None)

    out = {}
    after = dx

    def update_group(gi, after):
        names, l, handle = exchanges[gi]
        src, land = comm_wait(handle, after, f"grads_wait{gi}")
        for n, s_, ld in zip(names, src, land):
            res = sum_adamw_shard(s_, ld, me_arr, W[n], Mo[n], Vo[n], l, out.get(n), f"adamw_{n}_{l}")
            out[n] = list(res)
        return res[0]

    for gi in range(len(exchanges) - 1):
        after = update_group(gi, after)

    (small_src,), (small_land,) = comm_wait(small_handle, after, "small_grads_wait")
    rep_parts = lax.dynamic_update_index_in_dim(small_land, small_src, my, 0)
    res = sum_adamw(rep_parts, pack(W), pack(Mo), pack(Vo), "adamw_small")
    update_group(len(exchanges) - 1, res[0])
    offs = 0
    for n, sz in zip(REPLICATED, sizes):
        out[n] = [r.reshape(-1)[offs:offs + sz].reshape(W[n].shape) for r in res]
        offs += sz

    return (loss, dx[None], *[out[n][0] for n in WEIGHTS], *[out[n][1] for n in WEIGHTS],
            *[out[n][2] for n in WEIGHTS], *[out[n][3] for n in WEIGHTS])
```

```python
import functools
import math

import jax
import jax.numpy as jnp
from jax import lax
from jax.experimental import pallas as pl
from jax.experimental.pallas import tpu as pltpu

F32 = jnp.float32
BF16 = jnp.bfloat16
EPS = 1e-6
N_DEV = 8
LANES = 128
SUBLANES = 8
HEAD_DIM = 128
DN_CHUNK = 64
SG_CHUNK = 128
VMEM_LIMIT = 56 * 1024 * 1024
MESH = pl.DeviceIdType.MESH
HIGHEST = lax.Precision.HIGHEST

ADAM_LR = 0.001
ADAM_B1 = 0.9
ADAM_B2 = 0.999
ADAM_EPS = 1e-08
ADAM_WD = 0.01
ADAM_STEP = 10


def _pick(n, target, mult=LANES):
    best = None
    d = mult
    while d <= min(n, target):
        if n % d == 0:
            best = d
        d += mult
    return n if best is None else best


def _params(sem):
    return pltpu.CompilerParams(dimension_semantics=sem, vmem_limit_bytes=VMEM_LIMIT)


_NN = (((1,), (0,)), ((), ()))
_NT = (((1,), (1,)), ((), ()))
_TN = (((0,), (0,)), ((), ()))


def _dg(a, b, dims, hi):
    if hi == 2:
        return lax.dot_general(a.astype(F32), b.astype(F32), dims, precision=HIGHEST, preferred_element_type=F32)
    if hi == 1:
        a_hi, b_hi = a.astype(BF16), b.astype(BF16)
        a_lo, b_lo = (a - a_hi.astype(F32)).astype(BF16), (b - b_hi.astype(F32)).astype(BF16)
        ax, bx = dims[0][0][0], dims[0][1][0]
        a = jnp.concatenate([a_hi, a_hi, a_lo], axis=ax)
        b = jnp.concatenate([b_hi, b_lo, b_hi], axis=bx)
        return lax.dot_general(a, b, dims, preferred_element_type=F32)
    return lax.dot_general(a.astype(BF16), b.astype(BF16), dims, preferred_element_type=F32)


@functools.partial(jax.custom_vjp, nondiff_argnums=(2,))
def mm_nn(a, b, hi=False):
    return _dg(a, b, _NN, hi)


def _mm_nn_f(a, b, hi):
    return _dg(a, b, _NN, hi), (a, b)


def _mm_nn_b(hi, res, g):
    a, b = res
    return mm_nt(g, b, hi), mm_tn(a, g, hi)


@functools.partial(jax.custom_vjp, nondiff_argnums=(2,))
def mm_nt(a, b, hi=False):
    return _dg(a, b, _NT, hi)


def _mm_nt_f(a, b, hi):
    return _dg(a, b, _NT, hi), (a, b)


def _mm_nt_b(hi, res, g):
    a, b = res
    return mm_nn(g, b, hi), mm_tn(g, a, hi)


@functools.partial(jax.custom_vjp, nondiff_argnums=(2,))
def mm_tn(a, b, hi=False):
    return _dg(a, b, _TN, hi)


def _mm_tn_f(a, b, hi):
    return _dg(a, b, _TN, hi), (a, b)


def _mm_tn_b(hi, res, g):
    a, b = res
    return mm_nt(b, g, hi), mm_nn(a, g, hi)


mm_nn.defvjp(_mm_nn_f, _mm_nn_b)
mm_nt.defvjp(_mm_nt_f, _mm_nt_b)
mm_tn.defvjp(_mm_tn_f, _mm_tn_b)


def matmul(a, b, mode, name, c=None, out_dtype=F32, tm=1024, tn=1024, tk=2048):
    if mode == "nn":
        (M, K), N = a.shape, b.shape[1]
    elif mode == "nt":
        (M, K), N = a.shape, b.shape[0]
    else:
        (K, M), N = a.shape, b.shape[1]
    tm, tn, tk = _pick(M, tm), _pick(N, tn), _pick(K, tk)
    nk = K // tk
    dims = {"nn": _NN, "nt": _NT, "tn": _TN}[mode]
    a_spec = pl.BlockSpec((tk, tm), lambda i, j, k: (k, i)) if mode == "tn" else pl.BlockSpec((tm, tk), lambda i, j, k: (i, k))
    b_spec = pl.BlockSpec((tn, tk), lambda i, j, k: (j, k)) if mode == "nt" else pl.BlockSpec((tk, tn), lambda i, j, k: (k, j))
    o_spec = pl.BlockSpec((tm, tn), lambda i, j, k: (i, j))
    has_c = c is not None

    own_acc = nk > 1 and out_dtype != F32

    def body(*refs):
        a_ref, b_ref = refs[:2]
        c_ref = refs[2] if has_c else None
        o_ref = refs[3] if has_c else refs[2]
        acc_ref = refs[-1] if own_acc else o_ref

        def dot():
            return lax.dot_general(a_ref[...].astype(BF16), b_ref[...].astype(BF16), dims, preferred_element_type=F32)

        if nk == 1:
            o_ref[...] = (dot() + c_ref[...] if has_c else dot()).astype(o_ref.dtype)
        else:
            @pl.when(pl.program_id(2) == 0)
            def _():
                acc_ref[...] = c_ref[...] if has_c else jnp.zeros_like(acc_ref)

            acc_ref[...] += dot()
            if own_acc:
                @pl.when(pl.program_id(2) == nk - 1)
                def _():
                    o_ref[...] = acc_ref[...].astype(o_ref.dtype)

    ins = [a, b] + ([c] if has_c else [])
    specs = [a_spec, b_spec] + ([o_spec] if has_c else [])
    return pl.pallas_call(
        body, name=name, grid=(M // tm, N // tn, nk), in_specs=specs, out_specs=o_spec,
        out_shape=jax.ShapeDtypeStruct((M, N), out_dtype), scratch_shapes=[pltpu.VMEM((tm, tn), F32)] if own_acc else [],
        compiler_params=_params(("parallel", "parallel", "arbitrary")),
    )(*ins)


def rowcall(name, fn, ins, in_specs, outs, out_specs, acc, nrow, ncol=1):
    n_in = len(ins)

    def body(*refs):
        i = pl.program_id(1)
        res = fn(i, *[r[...] for r in refs[:n_in]])
        for r, v, is_acc in zip(refs[n_in:], res, acc):
            if is_acc:
                @pl.when(i == 0)
                def _(r=r, v=v):
                    r[...] = v.astype(r.dtype)

                @pl.when(i > 0)
                def _(r=r, v=v):
                    r[...] += v.astype(r.dtype)
            else:
                r[...] = v.astype(r.dtype)

    return pl.pallas_call(
        body, name=name, grid=(ncol, nrow), in_specs=list(in_specs), out_specs=list(out_specs), out_shape=list(outs),
        compiler_params=_params(("parallel", "arbitrary")),
    )(*ins)


class Tiles:
    def __init__(self, T, tm):
        self.T, self.tm, self.n = T, tm, T // tm
        self.r8 = tm // SUBLANES

    def row(self, w, cb=0):
        return pl.BlockSpec((self.tm, w), lambda j, i: (i, cb))

    def rowj(self, tc):
        return pl.BlockSpec((self.tm, tc), lambda j, i: (i, j))

    def prev(self, w, cb=0):
        return pl.BlockSpec((SUBLANES, w), lambda j, i: (jnp.maximum(i * self.r8 - 1, 0), cb))

    def prevj(self, tc):
        return pl.BlockSpec((SUBLANES, tc), lambda j, i: (jnp.maximum(i * self.r8 - 1, 0), j))

    def nxt(self, w, cb=0):
        last = self.T // SUBLANES - 1
        return pl.BlockSpec((SUBLANES, w), lambda j, i: (jnp.minimum((i + 1) * self.r8, last), cb))

    def nxtj(self, tc):
        last = self.T // SUBLANES - 1
        return pl.BlockSpec((SUBLANES, tc), lambda j, i: (jnp.minimum((i + 1) * self.r8, last), j))

    def heads(self, H):
        return pl.BlockSpec((H, self.tm, HEAD_DIM), lambda j, i: (0, i, 0))

    def heads_nxt(self, H):
        last = self.T // SUBLANES - 1
        return pl.BlockSpec((H, SUBLANES, HEAD_DIM), lambda j, i: (0, jnp.minimum((i + 1) * self.r8, last), 0))


def full(shape):
    return pl.BlockSpec(tuple(shape), lambda j, i: (0,) * len(shape))


def constj(r, tc):
    return pl.BlockSpec((r, tc), lambda j, i: (0, j))


def sds(shape, dtype=F32):
    return jax.ShapeDtypeStruct(tuple(shape), dtype)


def rms(x, g):
    return x * lax.rsqrt(jnp.mean(x * x, axis=-1, keepdims=True) + EPS) * g


def silu(x):
    return x * jax.nn.sigmoid(x)


def gelu(x):
    return 0.5 * x * (1.0 + lax.erf(x * (2.0 ** -0.5)))


def causal_conv(xwin, w, K, R):
    base = SUBLANES - (K - 1)
    out = w[0:1, :] * xwin[base:base + R, :]
    for j in range(1, K):
        out = out + w[j:j + 1, :] * xwin[base + j:base + j + R, :]
    return out


def rows_to8(rows, C):
    rid = lax.broadcasted_iota(jnp.int32, (SUBLANES, C), 0)
    out = jnp.zeros((SUBLANES, C), F32)
    for k, r in enumerate(rows):
        out = out + jnp.where(rid == k, jnp.broadcast_to(r, (SUBLANES, C)), 0.0)
    return out


def dn_qkv(pre, H):
    a = silu(pre)
    W = H * HEAD_DIM

    def l2(t):
        return t * lax.rsqrt(jnp.sum(t * t, axis=-1, keepdims=True) + EPS)

    q = [l2(a[:, h * HEAD_DIM:(h + 1) * HEAD_DIM]) for h in range(H)]
    k = [l2(a[:, W + h * HEAD_DIM:W + (h + 1) * HEAD_DIM]) for h in range(H)]
    v = [a[:, 2 * W + h * HEAD_DIM:2 * W + (h + 1) * HEAD_DIM] for h in range(H)]
    return q, k, v


def dn_gates(ba, alog, dtb, H, R):
    lane = lax.broadcasted_iota(jnp.int32, (R, LANES), 1)
    beta = jax.nn.sigmoid(ba)
    g = -jnp.exp(alog) * jax.nn.softplus(ba + dtb)
    g = jnp.where((lane >= H) & (lane < 2 * H), g, 0.0)
    ri = lax.broadcasted_iota(jnp.int32, (R, R), 0)
    ci = lax.broadcasted_iota(jnp.int32, (R, R), 1)
    cum = jnp.where((ri // DN_CHUNK == ci // DN_CHUNK) & (ci <= ri), 1.0, 0.0).astype(F32)
    gc = mm_nn(cum, g, 2)
    return jnp.where(lane < H, beta, gc)


def neumann_inverse(Ls):
    C = Ls[0].shape[0]
    ri = lax.broadcasted_iota(jnp.int32, (C, C), 0)
    ci = lax.broadcasted_iota(jnp.int32, (C, C), 1)
    eye = jnp.where(ri == ci, 1.0, 0.0).astype(F32)
    P = [-L for L in Ls]
    R = [eye + p for p in P]
    for _ in range(int(math.log2(C)) - 1):
        P = [mm_nn(p, p, 1) for p in P]
        R = [r + mm_nn(r, p, 1) for r, p in zip(R, P)]
    return R


def dn_chunk(q, k, v, beta, gc, gr):
    n = len(q)
    C = q[0].shape[0]
    ri = lax.broadcasted_iota(jnp.int32, (C, C), 0)
    ci = lax.broadcasted_iota(jnp.int32, (C, C), 1)
    qs = [q[h] * (HEAD_DIM ** -0.5) for h in range(n)]
    kb = [k[h] * beta[h] for h in range(n)]
    vb = [v[h] * beta[h] for h in range(n)]
    decay = [jnp.exp(jnp.where(ri >= ci, gc[h] - gr[h], -jnp.inf)) for h in range(n)]
    L = [jnp.where(ri > ci, mm_nt(kb[h], k[h]) * decay[h], 0.0) for h in range(n)]
    attn = [jnp.where(ri >= ci, mm_nt(qs[h], k[h]) * decay[h], 0.0) for h in range(n)]
    Tinv = neumann_inverse(L)
    eg = [jnp.exp(gc[h]) for h in range(n)]
    u = [mm_nn(Tinv[h], vb[h]) for h in range(n)]
    w = [mm_nn(Tinv[h], kb[h] * eg[h]) for h in range(n)]
    qd = [qs[h] * eg[h] for h in range(n)]
    gl = [gc[h][C - 1:C, :] for h in range(n)]
    kd = [k[h] * jnp.exp(gl[h] - gc[h]) for h in range(n)]
    return u, w, attn, qd, kd, gl


def dn_step(u, w, a, qd, kd, gl, S):
    n = len(u)
    v_new = [u[h] - mm_nn(w[h], S[h]) for h in range(n)]
    o = [mm_nn(qd[h], S[h]) + mm_nn(a[h], v_new[h]) for h in range(n)]
    S_new = [S[h] * jnp.exp(gl[h]) + mm_tn(kd[h], v_new[h]) for h in range(n)]
    return o, S_new


def dn_post(o, z, g):
    H = o.shape[0]
    return jnp.concatenate([rms(o[h], g) * silu(z[:, h * HEAD_DIM:(h + 1) * HEAD_DIM]) for h in range(H)], axis=1)


def gmlp(u_raw, v_raw, ln_g, ln_b, sgw, sgbT):
    R = u_raw.shape[0]
    G = sgw.shape[0]
    nc = R // SG_CHUNK
    u = gelu(u_raw)
    vv = gelu(v_raw)
    xc = vv - jnp.mean(vv, axis=-1, keepdims=True)
    vg = xc * lax.rsqrt(jnp.mean(xc * xc, axis=-1, keepdims=True) + EPS) * ln_g + ln_b
    ri = lax.broadcasted_iota(jnp.int32, (SG_CHUNK, SG_CHUNK), 0)
    ci = lax.broadcasted_iota(jnp.int32, (SG_CHUNK, SG_CHUNK), 1)
    cols = []
    for g in range(G):
        ws = jnp.where(ri >= ci, sgw[g], 0.0)
        rhs = jnp.concatenate([vg[c * SG_CHUNK:(c + 1) * SG_CHUNK, g * HEAD_DIM:(g + 1) * HEAD_DIM] for c in range(nc)], axis=1)
        mixed = mm_nn(ws, rhs) + sgbT[:, g:g + 1]
        cols.append(jnp.concatenate([mixed[:, c * HEAD_DIM:(c + 1) * HEAD_DIM] for c in range(nc)], axis=0))
    return u * jnp.concatenate(cols, axis=1)


def merge(ga, gb, ap, bp):
    return jax.nn.sigmoid(ga) * ap + jax.nn.sigmoid(gb) * bp


def norm_fwd(x, g, name, tm=256):
    T, D = x.shape
    tl = Tiles(T, _pick(T, tm))
    (h,) = rowcall(name, lambda i, x, g: (rms(x, g),), [x, g], [tl.row(D), full((1, D))],
                   [sds((T, D), BF16)], [tl.row(D)], [False], tl.n)
    return h


def norm_bwd(x, g, dh, dres, name, tm=256):
    T, D = x.shape
    tl = Tiles(T, _pick(T, tm))

    def fn(i, x, g, dh, dres):
        _, vj = jax.vjp(rms, x, g)
        dx, dg = vj(dh.astype(F32))
        return dx + dres, dg

    return rowcall(name, fn, [x, g, dh, dres], [tl.row(D), full((1, D)), tl.row(D), tl.row(D)],
                   [sds((T, D)), sds((1, D))], [tl.row(D), full((1, D))], [False, True], tl.n)


def head_fwd_bwd(x, g, tgt, name, tm=256):
    T, D = x.shape
    tl = Tiles(T, _pick(T, tm))

    def fn(i, x, g, tgt):
        y, vj = jax.vjp(rms, x, g)
        e = y - tgt
        loss = 0.5 * jnp.sum(jnp.mean(e * e, axis=-1, keepdims=True), axis=0, keepdims=True)
        dx, dg = vj(e * (1.0 / D))
        return loss, dx, dg

    return rowcall(name, fn, [x, g, tgt], [tl.row(D), full((1, D)), tl.row(D)],
                   [sds((1, 1)), sds((T, D)), sds((1, D))], [full((1, 1)), tl.row(D), full((1, D))],
                   [True, False, True], tl.n)


def dn_prep_fwd(projA, pba, cw8, alog, dtb, H, name, tm=256):
    T = projA.shape[0]
    W3 = 3 * H * HEAD_DIM
    tl = Tiles(T, _pick(T, tm, DN_CHUNK))
    R = tl.tm

    def fn(i, xp, x, ba, cw, alog, dtb):
        xwin = jnp.concatenate([jnp.where(i > 0, xp, 0.0), x], axis=0)
        q, k, v = dn_qkv(causal_conv(xwin, cw, 4, R), H)
        return jnp.stack(q), jnp.stack(k), jnp.stack(v), dn_gates(ba, alog, dtb, H, R)

    hs = sds((H, T, HEAD_DIM))
    return rowcall(name, fn, [projA, projA, pba, cw8, alog, dtb],
                   [tl.prev(W3), tl.row(W3), tl.row(LANES), full((SUBLANES, W3)), full((1, LANES)), full((1, LANES))],
                   [hs, hs, hs, sds((T, LANES))], [tl.heads(H)] * 3 + [tl.row(LANES)], [False] * 4, tl.n)


def dn_prep_bwd(projA, pba, cw8, alog, dtb, dq, dk, dv, dbg1, dbg2, H, name, tm=256):
    T = projA.shape[0]
    W3 = 3 * H * HEAD_DIM
    tl = Tiles(T, _pick(T, tm, DN_CHUNK))
    R = tl.tm
    RE = R + SUBLANES

    def fn(i, xp, x, xn, ba, cw, alog, dtb, dq, dk, dv, dqn, dkn, dvn, dbg1, dbg2):
        last = i == tl.n - 1
        xwin = jnp.concatenate([jnp.where(i > 0, xp, 0.0), x, jnp.where(last, 0.0, xn)], axis=0)
        pre = causal_conv(xwin, cw, 4, RE)
        ext = lambda d, dn: [jnp.concatenate([d[h], jnp.where(last, 0.0, dn[h])], axis=0) for h in range(H)]
        _, vj = jax.vjp(lambda p: dn_qkv(p, H), pre)
        (dpre,) = vj((ext(dq, dqn), ext(dk, dkn), ext(dv, dvn)))
        dx = cw[0:1, :] * dpre[3:3 + R, :]
        for j in range(1, 4):
            dx = dx + cw[j:j + 1, :] * dpre[3 - j:3 - j + R, :]
        dcw = rows_to8([jnp.sum(dpre[0:R, :] * xwin[5 + j:5 + j + R, :], axis=0, keepdims=True) for j in range(4)], W3)
        _, vjg = jax.vjp(lambda ba, alog, dtb: dn_gates(ba, alog, dtb, H, R), ba, alog, dtb)
        dba, dalog, ddtb = vjg(dbg1 + dbg2)
        return dx, dba, dcw, dalog, ddtb

    return rowcall(name, fn, [projA, projA, projA, pba, cw8, alog, dtb, dq, dk, dv, dq, dk, dv, dbg1, dbg2],
                   [tl.prev(W3), tl.row(W3), tl.nxt(W3), tl.row(LANES), full((SUBLANES, W3)), full((1, LANES)), full((1, LANES))]
                   + [tl.heads(H)] * 3 + [tl.heads_nxt(H)] * 3 + [tl.row(LANES)] * 2,
                   [sds((T, W3), BF16), sds((T, LANES), BF16), sds((SUBLANES, W3)), sds((1, LANES)), sds((1, LANES))],
                   [tl.row(W3), tl.row(LANES), full((SUBLANES, W3)), full((1, LANES)), full((1, LANES))],
                   [False, False, True, True, True], tl.n)


def _chunk_specs(H, C):
    hs = pl.BlockSpec((H, C, HEAD_DIM), lambda n: (0, n, 0))
    col = pl.BlockSpec((H, 1, C, 1), lambda n: (0, n, 0, 0))
    rw = pl.BlockSpec((H, 1, 1, C), lambda n: (0, n, 0, 0))
    at = pl.BlockSpec((H, C, C), lambda n: (0, n, 0))
    one = pl.BlockSpec((H, 1, 1, 1), lambda n: (0, n, 0, 0))
    return hs, col, rw, at, one


def dn_chunk_fwd(q, k, v, beta_c, gc_c, gc_r, name):
    H, T, _ = q.shape
    C = DN_CHUNK
    N = T // C
    hs, col, rw, at, one = _chunk_specs(H, C)

    def body(q, k, v, bc, gc, gr, u, w, a, qd, kd, gl):
        hd = range(H)
        res = dn_chunk([q[h] for h in hd], [k[h] for h in hd], [v[h] for h in hd], [bc[h, 0] for h in hd],
                       [gc[h, 0] for h in hd], [gr[h, 0] for h in hd])
        for h in hd:
            for ref, val in zip((u, w, a, qd, kd), res[:5]):
                ref[h] = val[h]
            gl[h, 0] = res[5][h]

    big = sds((H, T, HEAD_DIM))
    return pl.pallas_call(
        body, name=name, grid=(N,), in_specs=[hs, hs, hs, col, col, rw], out_specs=[hs, hs, at, hs, hs, one],
        out_shape=[big, big, sds((H, T, C)), big, big, sds((H, N, 1, 1))], compiler_params=_params(("parallel",)),
    )(q, k, v, beta_c, gc_c, gc_r)


def dn_chunk_bwd(q, k, v, beta_c, gc_c, gc_r, du, dw, da, dqd, dkd, dgl, name):
    H, T, _ = q.shape
    C = DN_CHUNK
    N = T // C
    hs, col, rw, at, one = _chunk_specs(H, C)

    def body(q, k, v, bc, gc, gr, du, dw, da, dqd, dkd, dgl, dq, dk, dv, dbc, dgc, dgr):
        hd = range(H)
        _, vj = jax.vjp(dn_chunk, [q[h] for h in hd], [k[h] for h in hd], [v[h] for h in hd], [bc[h, 0] for h in hd],
                        [gc[h, 0] for h in hd], [gr[h, 0] for h in hd])
        res = vj(([du[h] for h in hd], [dw[h] for h in hd], [da[h] for h in hd], [dqd[h] for h in hd],
                  [dkd[h] for h in hd], [dgl[h, 0] for h in hd]))
        for h in hd:
            dq[h], dk[h], dv[h] = res[0][h], res[1][h], res[2][h]
            dbc[h, 0], dgc[h, 0], dgr[h, 0] = res[3][h], res[4][h], res[5][h]

    big = sds((H, T, HEAD_DIM))
    return pl.pallas_call(
        body, name=name, grid=(N,), in_specs=[hs, hs, hs, col, col, rw, hs, hs, at, hs, hs, one],
        out_specs=[hs, hs, hs, col, col, rw],
        out_shape=[big, big, big, sds((H, N, C, 1)), sds((H, N, C, 1)), sds((H, N, 1, C))],
        compiler_params=_params(("parallel",)),
    )(q, k, v, beta_c, gc_c, gc_r, du, dw, da, dqd, dkd, dgl)


def dn_scan_fwd(u, w, a, qd, kd, gl, name):
    H, T, _ = u.shape
    C = DN_CHUNK
    N = T // C
    hs, _, _, at, one = _chunk_specs(H, C)
    st = pl.BlockSpec((1, H, HEAD_DIM, HEAD_DIM), lambda n: (n, 0, 0, 0))

    def body(u, w, a, qd, kd, gl, o, s_in, S):
        @pl.when(pl.program_id(0) == 0)
        def _():
            S[...] = jnp.zeros_like(S)

        hd = range(H)
        s = [S[h] for h in hd]
        o_new, s_new = dn_step([u[h] for h in hd], [w[h] for h in hd], [a[h] for h in hd], [qd[h] for h in hd],
                               [kd[h] for h in hd], [gl[h, 0] for h in hd], s)
        for h in hd:
            s_in[0, h] = s[h]
            o[h] = o_new[h]
            S[h] = s_new[h]

    return pl.pallas_call(
        body, name=name, grid=(N,), in_specs=[hs, hs, at, hs, hs, one], out_specs=[hs, st],
        out_shape=[sds((H, T, HEAD_DIM)), sds((N, H, HEAD_DIM, HEAD_DIM))],
        scratch_shapes=[pltpu.VMEM((H, HEAD_DIM, HEAD_DIM), F32)], compiler_params=_params(("arbitrary",)),
    )(u, w, a, qd, kd, gl)


def dn_scan_bwd(u, w, a, qd, kd, gl, s_in, do, name):
    H, T, _ = u.shape
    C = DN_CHUNK
    N = T // C
    rev = lambda spec_shape, f: pl.BlockSpec(spec_shape, f)
    hs = rev((H, C, HEAD_DIM), lambda n: (0, N - 1 - n, 0))
    at = rev((H, C, C), lambda n: (0, N - 1 - n, 0))
    one = rev((H, 1, 1, 1), lambda n: (0, N - 1 - n, 0, 0))
    st = rev((1, H, HEAD_DIM, HEAD_DIM), lambda n: (N - 1 - n, 0, 0, 0))

    def body(u, w, a, qd, kd, gl, s_in, do, du, dw, da, dqd, dkd, dgl, dS):
        @pl.when(pl.program_id(0) == 0)
        def _():
            dS[...] = jnp.zeros_like(dS)

        hd = range(H)
        _, vj = jax.vjp(dn_step, [u[h] for h in hd], [w[h] for h in hd], [a[h] for h in hd], [qd[h] for h in hd],
                        [kd[h] for h in hd], [gl[h, 0] for h in hd], [s_in[0, h] for h in hd])
        res = vj(([do[h] for h in hd], [dS[h] for h in hd]))
        for h in hd:
            du[h], dw[h], da[h], dqd[h], dkd[h] = (res[j][h] for j in range(5))
            dgl[h, 0] = res[5][h]
            dS[h] = res[6][h]

    big = sds((H, T, HEAD_DIM))
    return pl.pallas_call(
        body, name=name, grid=(N,), in_specs=[hs, hs, at, hs, hs, one, st, hs], out_specs=[hs, hs, at, hs, hs, one],
        out_shape=[big, big, sds((H, T, C)), big, big, sds((H, N, 1, 1))],
        scratch_shapes=[pltpu.VMEM((H, HEAD_DIM, HEAD_DIM), F32)], compiler_params=_params(("arbitrary",)),
    )(u, w, a, qd, kd, gl, s_in, do)


def dn_post_fwd(o, projA, g, name, tm=256):
    H, T, _ = o.shape
    W = H * HEAD_DIM
    tl = Tiles(T, _pick(T, tm))
    (y,) = rowcall(name, lambda i, o, z, g: (dn_post(o, z, g),), [o, projA, g], [tl.heads(H), tl.row(W, 3), full((1, HEAD_DIM))],
                   [sds((T, W), BF16)], [tl.row(W)], [False], tl.n)
    return y


def dn_post_bwd(o, projA, g, dy, name, tm=256):
    H, T, _ = o.shape
    W = H * HEAD_DIM
    tl = Tiles(T, _pick(T, tm))

    def fn(i, o, z, g, dy):
        _, vj = jax.vjp(dn_post, o, z, g)
        return vj(dy.astype(F32))

    return rowcall(name, fn, [o, projA, g, dy], [tl.heads(H), tl.row(W, 3), full((1, HEAD_DIM)), tl.row(W)],
                   [sds((H, T, HEAD_DIM)), sds((T, W), BF16), sds((1, HEAD_DIM))],
                   [tl.heads(H), tl.row(W), full((1, HEAD_DIM))], [False, False, True], tl.n)


def gmlp_fwd(projB, ln_g, ln_b, sgw, sgbT, name, tm=512):
    T = projB.shape[0]
    G = sgw.shape[0]
    W = G * HEAD_DIM
    tl = Tiles(T, _pick(T, tm))
    (y,) = rowcall(name, lambda i, *a: (gmlp(*a),), [projB, projB, ln_g, ln_b, sgw, sgbT],
                   [tl.row(W, 0), tl.row(W, 1), full((1, W)), full((1, W)), full(sgw.shape), full(sgbT.shape)],
                   [sds((T, W), BF16)], [tl.row(W)], [False], tl.n)
    return y


def gmlp_bwd(projB, ln_g, ln_b, sgw, sgbT, dy, name, tm=512):
    T = projB.shape[0]
    G = sgw.shape[0]
    W = G * HEAD_DIM
    tl = Tiles(T, _pick(T, tm))

    def fn(i, u_raw, v_raw, ln_g, ln_b, sgw, sgbT, dy):
        _, vj = jax.vjp(gmlp, u_raw, v_raw, ln_g, ln_b, sgw, sgbT)
        return vj(dy.astype(F32))

    return rowcall(name, fn, [projB, projB, ln_g, ln_b, sgw, sgbT, dy],
                   [tl.row(W, 0), tl.row(W, 1), full((1, W)), full((1, W)), full(sgw.shape), full(sgbT.shape), tl.row(W)],
                   [sds((T, W), BF16), sds((T, W), BF16), sds((1, W)), sds((1, W)), sds(sgw.shape), sds(sgbT.shape)],
                   [tl.row(W), tl.row(W), full((1, W)), full((1, W)), full(sgw.shape), full(sgbT.shape)],
                   [False, False, True, True, True, True], tl.n)


def merge_fwd(projB, ap, bp, cb_a, name, tm=256):
    T, D = ap.shape
    tl = Tiles(T, _pick(T, tm))
    (m,) = rowcall(name, lambda i, *a: (merge(*a),), [projB, projB, ap, bp],
                   [tl.row(D, cb_a), tl.row(D, cb_a + 1), tl.row(D), tl.row(D)], [sds((T, D), BF16)], [tl.row(D)], [False], tl.n)
    return m


def merge_bwd(projB, ap, bp, dm, cb_a, name, tm=256):
    T, D = ap.shape
    tl = Tiles(T, _pick(T, tm))

    def fn(i, ga, gb, ap, bp, dm):
        _, vj = jax.vjp(merge, ga, gb, ap, bp)
        return vj(dm.astype(F32))

    return rowcall(name, fn, [projB, projB, ap, bp, dm], [tl.row(D, cb_a), tl.row(D, cb_a + 1), tl.row(D), tl.row(D), tl.row(D)],
                   [sds((T, D), BF16)] * 4, [tl.row(D)] * 4, [False] * 4, tl.n)


def ffn_act_fwd(gp, up, fcw8, fcb, name, tm=256, tc=512):
    T, F = gp.shape
    tl = Tiles(T, _pick(T, tm))
    tc = _pick(F, tc)
    R = tl.tm

    def fn(i, gprev, g, up, cw, cb):
        xwin = jnp.concatenate([jnp.where(i > 0, gprev, 0.0), g], axis=0)
        return (silu(causal_conv(xwin, cw, 3, R) + cb) * up,)

    (act,) = rowcall(name, fn, [gp, gp, up, fcw8, fcb], [tl.prevj(tc), tl.rowj(tc), tl.rowj(tc), constj(SUBLANES, tc), constj(1, tc)],
                     [sds((T, F), BF16)], [tl.rowj(tc)], [False], tl.n, F // tc)
    return act


def ffn_act_bwd(gp, up, fcw8, fcb, dact, name, tm=256, tc=512):
    T, F = gp.shape
    tl = Tiles(T, _pick(T, tm))
    tc = _pick(F, tc)
    R = tl.tm
    RE = R + SUBLANES

    def fn(i, gprev, g, gnext, up, upn, da, dan, cw, cb):
        last = i == tl.n - 1
        xwin = jnp.concatenate([jnp.where(i > 0, gprev, 0.0), g, jnp.where(last, 0.0, gnext)], axis=0)
        gate = causal_conv(xwin, cw, 3, RE) + cb
        upe = jnp.concatenate([up, upn], axis=0)
        dae = jnp.concatenate([da, jnp.where(last, 0.0, dan)], axis=0)
        s = jax.nn.sigmoid(gate)
        dgate = dae * upe * (s * (1.0 + gate * (1.0 - s)))
        dup = da * (gate[0:R, :] * s[0:R, :])
        dgp = cw[0:1, :] * dgate[2:2 + R, :] + cw[1:2, :] * dgate[1:1 + R, :] + cw[2:3, :] * dgate[0:R, :]
        dcw = rows_to8([jnp.sum(dgate[0:R, :] * xwin[6 + j:6 + j + R, :], axis=0, keepdims=True) for j in range(3)], tc)
        dcb = jnp.sum(dgate[0:R, :], axis=0, keepdims=True)
        return dgp, dup, dcw, dcb

    return rowcall(name, fn, [gp, gp, gp, up, up, dact, dact, fcw8, fcb],
                   [tl.prevj(tc), tl.rowj(tc), tl.nxtj(tc), tl.rowj(tc), tl.nxtj(tc), tl.rowj(tc), tl.nxtj(tc),
                    constj(SUBLANES, tc), constj(1, tc)],
                   [sds((T, F), BF16), sds((T, F), BF16), sds((SUBLANES, F)), sds((1, F))],
                   [tl.rowj(tc), tl.rowj(tc), constj(SUBLANES, tc), constj(1, tc)], [False, False, True, True], tl.n, F // tc)


def _me():
    return lax.axis_index("x"), lax.axis_index("y"), lax.axis_index("c")


def all_gather(shards, name):
    nt = len(shards)

    def body(*refs):
        xs, outs = refs[:nt], refs[nt:2 * nt]
        send_sems, recv_sems, local_sems = refs[2 * nt:]
        x, y, c = _me()
        me, sibling = (x, y, c), (x, y, 1 - c)
        chips = [(1 - x, y), (x, 1 - y), (1 - x, 1 - y)]

        def slot(t, p):
            return outs[t].at[4 * p[0] + 2 * p[1] + p[2]]

        def copy(t, k, block, to, src=None):
            return pltpu.make_async_remote_copy(
                src_ref=slot(t, block) if src is None else src, dst_ref=slot(t, block),
                send_sem=send_sems.at[t, k], recv_sem=recv_sems.at[t, k], device_id=to, device_id_type=MESH)

        mine = [pltpu.make_async_copy(xs[t], slot(t, me), local_sems.at[t]) for t in range(nt)]
        first = []
        for t in range(nt):
            mine[t].start()
            first.append(copy(t, 0, me, sibling, src=xs[t]))
            first += [copy(t, 1 + j, me, (*chip, c), src=xs[t]) for j, chip in enumerate(chips)]
        for cp in first:
            cp.start()
        passed = []
        for j, chip in enumerate(chips):
            for t in range(nt):
                copy(t, 1 + j, (*chip, c), me).wait_recv()
                cp = copy(t, 4 + j, (*chip, c), sibling)
                cp.start()
                passed.append(cp)
        for t in range(nt):
            copy(t, 0, sibling, me).wait_recv()
            for j, chip in enumerate(chips):
                copy(t, 4 + j, (*chip, 1 - c), me).wait_recv()
        for cp in first + passed:
            cp.wait_send()
        for t in range(nt):
            mine[t].wait()

    any_spec = pl.BlockSpec(memory_space=pl.ANY)
    return pl.pallas_call(
        body, name=name, in_specs=[any_spec] * nt, out_specs=[any_spec] * nt,
        out_shape=[jax.ShapeDtypeStruct((N_DEV,) + s.shape, s.dtype) for s in shards],
        scratch_shapes=[pltpu.SemaphoreType.DMA((nt, 7)), pltpu.SemaphoreType.DMA((nt, 7)), pltpu.SemaphoreType.DMA((nt,))],
    )(*shards)


_HBM = pl.BlockSpec(memory_space=pltpu.HBM)
_SEM = pl.BlockSpec(memory_space=pltpu.SEMAPHORE)
_ANY = pl.BlockSpec(memory_space=pl.ANY)
_DATAFLOW = pltpu.SideEffectType.DATAFLOW_SIDE_EFFECTING


def _peers():
    x, y, c = _me()
    out = []
    for k in range(1, N_DEV):
        p = (x ^ (k >> 2), y ^ ((k >> 1) & 1), c ^ (k & 1))
        out.append((k, p, 4 * p[0] + 2 * p[1] + p[2]))
    return out


def _split_copy(src, land, send_sems, recv_sems, t, k, peer, slot, my, scatter, receiving):
    return pltpu.make_async_remote_copy(
        src_ref=src.at[slot] if scatter else src, dst_ref=land.at[slot if receiving else my],
        send_sem=send_sems.at[t * (N_DEV - 1) + k - 1], recv_sem=recv_sems.at[t * (N_DEV - 1) + k - 1],
        device_id=peer, device_id_type=MESH)


def comm_start(groups, scatter, name, after=None):
    flat = [a for g in groups for a in g]
    nt = len(flat)
    lands = [lax.empty(a.shape if scatter else (N_DEV,) + a.shape, a.dtype) for a in flat]
    ng = len(groups)
    n_after = 0 if after is None else 1

    def body(*refs):
        src, land = refs[:nt], refs[nt:2 * nt]
        sems = refs[2 * nt + n_after:2 * nt + n_after + 2 * ng]
        token = refs[-1]
        x, y, c = _me()
        my = 4 * x + 2 * y + c
        t0 = 0
        for gi, g in enumerate(groups):
            for k, peer, slot in _peers():
                for t in range(len(g)):
                    _split_copy(src[t0 + t], land[t0 + t], sems[2 * gi], sems[2 * gi + 1], t, k, peer, slot, my, scatter,
                                False).start()
            t0 += len(g)
        token[...] = jnp.zeros_like(token)

    sem_shapes = []
    for g in groups:
        sem_shapes += [pltpu.SemaphoreType.DMA((len(g) * (N_DEV - 1),))] * 2
    res = pl.pallas_call(
        body, name=name, in_specs=[_HBM] * (2 * nt) + [_HBM] * n_after,
        out_specs=[_SEM] * (2 * ng) + [_HBM] * (2 * nt) + [pl.BlockSpec(memory_space=pltpu.VMEM)],
        out_shape=sem_shapes + [pltpu.HBM(a.shape, a.dtype) for a in flat + lands] + [sds((SUBLANES, LANES))],
        input_output_aliases={i: 2 * ng + i for i in range(2 * nt)},
        compiler_params=pltpu.CompilerParams(has_side_effects=_DATAFLOW),
    )(*[pltpu.with_memory_space_constraint(a, pltpu.HBM) for a in flat + lands + ([] if after is None else [after])])
    handles = []
    t0 = 0
    for gi, g in enumerate(groups):
        n = len(g)
        handles.append(dict(sems=(res[2 * gi], res[2 * gi + 1]), src=res[2 * ng + t0:2 * ng + t0 + n],
                            land=res[2 * ng + nt + t0:2 * ng + nt + t0 + n], scatter=scatter))
        t0 += n
    return handles, res[-1]


def comm_wait(handle, after, name):
    src, land, scatter = handle["src"], handle["land"], handle["scatter"]
    nt = len(src)

    def body(*refs):
        src_r, land_r = refs[:nt], refs[nt:2 * nt]
        send_sems, recv_sems = refs[2 * nt], refs[2 * nt + 1]
        x, y, c = _me()
        my = 4 * x + 2 * y + c
        for k, peer, slot in _peers():
            for t in range(nt):
                _split_copy(src_r[t], land_r[t], send_sems, recv_sems, t, k, peer, slot, my, scatter, False).wait_send()
                _split_copy(src_r[t], land_r[t], send_sems, recv_sems, t, k, peer, slot, my, scatter, True).wait_recv()

    after = list(after) if isinstance(after, (list, tuple)) else [after]
    res = pl.pallas_call(
        body, name=name, in_specs=[_HBM] * (2 * nt) + [_SEM, _SEM] + [_HBM] * len(after), out_specs=[_HBM] * (2 * nt),
        out_shape=[pltpu.HBM(a.shape, a.dtype) for a in list(src) + list(land)],
        input_output_aliases={i: i for i in range(2 * nt)},
        compiler_params=pltpu.CompilerParams(has_side_effects=_DATAFLOW),
    )(*src, *land, *handle["sems"], *[pltpu.with_memory_space_constraint(a, pltpu.HBM) for a in after])
    return res[:nt], res[nt:]


def sum_adamw_shard(own_src, land, me, w, m, v, l, prev, name, tr=256):
    L, R, C = w.shape
    tr = _pick(R, tr, SUBLANES)
    c1 = 1.0 - ADAM_B1 ** ADAM_STEP
    c2 = 1.0 - ADAM_B2 ** ADAM_STEP
    n_prev = 0 if prev is None else 4

    def body(me_ref, *refs):
        parts = refs[:N_DEV]
        w_r, m_r, v_r = refs[N_DEV:N_DEV + 3]
        g_o, d_o, m_o, v_o = refs[N_DEV + 3 + n_prev:]
        g = parts[0][0].astype(F32)
        for k in range(1, N_DEV):
            g = g + parts[k][0].astype(F32)
        mn = ADAM_B1 * m_r[0] + (1.0 - ADAM_B1) * g
        vn = ADAM_B2 * v_r[0] + (1.0 - ADAM_B2) * (g * g)
        g_o[0] = g
        d_o[0] = -ADAM_LR * ((mn / c1) / (jnp.sqrt(vn / c2) + ADAM_EPS) + ADAM_WD * w_r[0])
        m_o[0] = mn
        v_o[0] = vn

    part_specs = [pl.BlockSpec((1, tr, C), lambda i, me, k=k: (me[0] ^ k, i, 0)) for k in range(N_DEV)]
    lay = pl.BlockSpec((1, tr, C), lambda i, me: (l, i, 0))
    grid_spec = pltpu.PrefetchScalarGridSpec(
        num_scalar_prefetch=1, grid=(R // tr,), in_specs=part_specs + [lay] * 3 + [_ANY] * n_prev, out_specs=[lay] * 4)
    return pl.pallas_call(
        body, name=name, grid_spec=grid_spec, out_shape=[sds((L, R, C))] * 4,
        input_output_aliases={1 + N_DEV + 3 + j: j for j in range(n_prev)}, compiler_params=_params(("parallel",)),
    )(me, own_src, *[land] * (N_DEV - 1), w, m, v, *([] if prev is None else prev))


def sum_adamw(parts, w, m, v, name, tr=256):
    _, R, C = parts.shape
    tr = _pick(R, tr, SUBLANES)
    c1 = 1.0 - ADAM_B1 ** ADAM_STEP
    c2 = 1.0 - ADAM_B2 ** ADAM_STEP

    def body(p, w, m, v, g_o, d_o, m_o, v_o):
        g = p[0].astype(F32)
        for d in range(1, N_DEV):
            g = g + p[d].astype(F32)
        mn = ADAM_B1 * m[...] + (1.0 - ADAM_B1) * g
        vn = ADAM_B2 * v[...] + (1.0 - ADAM_B2) * (g * g)
        m_hat = mn / c1
        v_hat = vn / c2
        g_o[...] = g
        d_o[...] = -ADAM_LR * (m_hat / (jnp.sqrt(v_hat) + ADAM_EPS) + ADAM_WD * w[...])
        m_o[...] = mn
        v_o[...] = vn

    blk = pl.BlockSpec((tr, C), lambda i: (i, 0))
    return pl.pallas_call(
        body, name=name, grid=(R // tr,), in_specs=[pl.BlockSpec((N_DEV, tr, C), lambda i: (0, i, 0)), blk, blk, blk],
        out_specs=[blk] * 4, out_shape=[sds((R, C))] * 4, compiler_params=_params(("parallel",)),
    )(parts, w, m, v)


SHARDED = ("w_in", "dn_conv_w", "w_branch_a", "w_branch_b", "w_out", "ffn_w_gate", "ffn_w_up", "ffn_conv_w", "ffn_w_down")
COL_SHARDED = ("w_in", "dn_conv_w", "w_branch_a", "w_branch_b", "ffn_w_gate", "ffn_w_up", "ffn_conv_w")
CONV_WEIGHTS = ("dn_conv_w", "ffn_conv_w")
REPLICATED = ("norm1_g", "dn_a_log", "dn_dt_bias", "dn_onorm_g", "sg_ln_g", "sg_ln_b", "sg_w", "sg_b", "norm2_g",
              "ffn_conv_b", "final_norm_g")
WEIGHTS = ("norm1_g", "w_in", "dn_conv_w", "dn_a_log", "dn_dt_bias", "dn_onorm_g", "sg_ln_g", "sg_ln_b", "sg_w", "sg_b",
           "w_branch_a", "w_branch_b", "w_out", "norm2_g", "ffn_w_gate", "ffn_w_up", "ffn_conv_w", "ffn_conv_b",
           "ffn_w_down", "final_norm_g")


def _columns(pieces, lo, hi):
    out = []
    for a, start, width in pieces:
        s, e = max(lo, start), min(hi, start + width)
        if s < e:
            out.append(a[:, s - start:e - start])
    return out[0] if len(out) == 1 else jnp.concatenate(out, axis=1)


def _assemble(name, g):
    if name in COL_SHARDED:
        return jnp.concatenate([g[d] for d in range(N_DEV)], axis=1)
    return g.reshape(N_DEV * g.shape[1], g.shape[2])


def _split(name, pieces, dtype):
    total = sum(w for _, _, w in pieces)
    if name in COL_SHARDED:
        cs = total // N_DEV
        return jnp.stack([_columns(pieces, d * cs, (d + 1) * cs).astype(dtype) for d in range(N_DEV)])
    (a, _, _), = pieces
    return a.reshape(N_DEV, a.shape[0] // N_DEV, a.shape[1]).astype(dtype)


def _pad_lanes(a, lo, width=LANES):
    return jnp.pad(a, ((0, 0), (lo, width - lo - a.shape[1])))


def _pad_rows(a, rows=SUBLANES):
    return jnp.pad(a, ((0, rows - a.shape[0]), (0, 0)))


def kernel(x, norm1_g, w_in, dn_conv_w, dn_a_log, dn_dt_bias, dn_onorm_g, sg_ln_g, sg_ln_b, sg_w, sg_b, w_branch_a, w_branch_b, w_out, norm2_g, ffn_w_gate, ffn_w_up, ffn_conv_w, ffn_conv_b, ffn_w_down, final_norm_g, loss_target, m_norm1_g, m_w_in, m_dn_conv_w, m_dn_a_log, m_dn_dt_bias, m_dn_onorm_g, m_sg_ln_g, m_sg_ln_b, m_sg_w, m_sg_b, m_w_branch_a, m_w_branch_b, m_w_out, m_norm2_g, m_ffn_w_gate, m_ffn_w_up, m_ffn_conv_w, m_ffn_conv_b, m_ffn_w_down, m_final_norm_g, v_norm1_g, v_w_in, v_dn_conv_w, v_dn_a_log, v_dn_dt_bias, v_dn_onorm_g, v_sg_ln_g, v_sg_ln_b, v_sg_w, v_sg_b, v_w_branch_a, v_w_branch_b, v_w_out, v_norm2_g, v_ffn_w_gate, v_ffn_w_up, v_ffn_conv_w, v_ffn_conv_b, v_ffn_w_down, v_final_norm_g):
    W = dict(norm1_g=norm1_g, w_in=w_in, dn_conv_w=dn_conv_w, dn_a_log=dn_a_log, dn_dt_bias=dn_dt_bias, dn_onorm_g=dn_onorm_g,
             sg_ln_g=sg_ln_g, sg_ln_b=sg_ln_b, sg_w=sg_w, sg_b=sg_b, w_branch_a=w_branch_a, w_branch_b=w_branch_b, w_out=w_out,
             norm2_g=norm2_g, ffn_w_gate=ffn_w_gate, ffn_w_up=ffn_w_up, ffn_conv_w=ffn_conv_w, ffn_conv_b=ffn_conv_b,
             ffn_w_down=ffn_w_down, final_norm_g=final_norm_g)
    Mo = dict(norm1_g=m_norm1_g, w_in=m_w_in, dn_conv_w=m_dn_conv_w, dn_a_log=m_dn_a_log, dn_dt_bias=m_dn_dt_bias,
              dn_onorm_g=m_dn_onorm_g, sg_ln_g=m_sg_ln_g, sg_ln_b=m_sg_ln_b, sg_w=m_sg_w, sg_b=m_sg_b, w_branch_a=m_w_branch_a,
              w_branch_b=m_w_branch_b, w_out=m_w_out, norm2_g=m_norm2_g, ffn_w_gate=m_ffn_w_gate, ffn_w_up=m_ffn_w_up,
              ffn_conv_w=m_ffn_conv_w, ffn_conv_b=m_ffn_conv_b, ffn_w_down=m_ffn_w_down, final_norm_g=m_final_norm_g)
    Vo = dict(norm1_g=v_norm1_g, w_in=v_w_in, dn_conv_w=v_dn_conv_w, dn_a_log=v_dn_a_log, dn_dt_bias=v_dn_dt_bias,
              dn_onorm_g=v_dn_onorm_g, sg_ln_g=v_sg_ln_g, sg_ln_b=v_sg_ln_b, sg_w=v_sg_w, sg_b=v_sg_b, w_branch_a=v_w_branch_a,
              w_branch_b=v_w_branch_b, w_out=v_w_out, norm2_g=v_norm2_g, ffn_w_gate=v_ffn_w_gate, ffn_w_up=v_ffn_w_up,
              ffn_conv_w=v_ffn_conv_w, ffn_conv_b=v_ffn_conv_b, ffn_w_down=v_ffn_w_down, final_norm_g=v_final_norm_g)

    xs = x[0]
    tgt = loss_target[0]
    T, D = xs.shape
    depth = norm1_g.shape[0]
    H = dn_a_log.shape[1]
    G = sg_w.shape[1]
    WA = H * HEAD_DIM
    WB = G * HEAD_DIM
    N = T // DN_CHUNK
    colA = 4 * WA
    colB0 = colA + 2 * H
    cb_a = (2 * WB) // D

    my = 4 * lax.axis_index("x") + 2 * lax.axis_index("y") + lax.axis_index("c")
    me_arr = my.astype(jnp.int32).reshape(1)

    def shard(n, l):
        return W[n][l] if n in CONV_WEIGHTS else W[n][l].astype(BF16)

    first = [("w_in", 0), ("dn_conv_w", 0)]
    first_blocks = all_gather([shard(n, l) for n, l in first], "gather_first")
    gathered = dict(zip(first, first_blocks))
    gather_names = [[("w_branch_a", 0), ("w_branch_b", 0), ("w_out", 0)]]
    for l in range(depth):
        if l > 0:
            gather_names.append([("w_in", l), ("dn_conv_w", l), ("w_branch_a", l), ("w_branch_b", l), ("w_out", l)])
        gather_names.append([("ffn_w_gate", l), ("ffn_w_up", l), ("ffn_conv_w", l), ("ffn_w_down", l)])
    gather_handles, gather_tok = comm_start([[shard(n, l) for n, l in g] for g in gather_names], False, "gather_start",
                                            after=first_blocks[0])

    def need(n, l, after):
        if (n, l) not in gathered:
            gi = [i for i, g in enumerate(gather_names) if (n, l) in g][0]
            src, land = comm_wait(gather_handles[gi], after, f"gather_wait{gi}")
            for key, s, ld in zip(gather_names[gi], src, land):
                gathered[key] = lax.dynamic_update_index_in_dim(ld, s, my, 0)
        return gathered[(n, l)]

    def full(n, l, after):
        return _assemble(n, need(n, l, after))

    def layer_weights(l):
        return dict(
            g1=norm1_g[l][None], g2=norm2_g[l][None], alog=_pad_lanes(dn_a_log[l][None], H), dtb=_pad_lanes(dn_dt_bias[l][None], H),
            og=dn_onorm_g[l][None], lng=sg_ln_g[l][None], lnb=sg_ln_b[l][None], sgw=sg_w[l], sgbT=sg_b[l].T, fcb=ffn_conv_b[l][None])

    def mixer_in_weights(p, l, after):
        g = need("w_in", l, after)
        cs = g.shape[2]
        blocks = [(g[d], d * cs, cs) for d in range(N_DEV)]
        p.update(wA=_columns(blocks, 0, colA), wba=_pad_lanes(_columns(blocks, colA, colB0), 0),
                 wB=_columns(blocks, colB0, N_DEV * cs), cw8=_pad_rows(full("dn_conv_w", l, after)))

    def mixer_out_weights(p, l, after):
        p.update(wa=full("w_branch_a", l, after), wb=full("w_branch_b", l, after), wo=full("w_out", l, after))

    def ffn_weights(p, l, after):
        p.update(wg=full("ffn_w_gate", l, after), wu=full("ffn_w_up", l, after), fcw8=_pad_rows(full("ffn_conv_w", l, after)),
                 wd=full("ffn_w_down", l, after))

    def to_chunks(bg):
        bt = bg[:, :2 * H].T
        beta_c = bt[:H].reshape(H, N, DN_CHUNK, 1)
        gc_c = bt[H:].reshape(H, N, DN_CHUNK, 1)
        return beta_c, gc_c, bt[H:].reshape(H, N, 1, DN_CHUNK)

    saved = []
    cur = xs
    for l in range(depth):
        p = layer_weights(l)
        t = f"l{l}_"
        h = norm_fwd(cur, p["g1"] + gather_tok[0, 0] if l == 0 else p["g1"], t + "norm1")
        mixer_in_weights(p, l, h)
        projA = matmul(h, p["wA"], "nn", t + "projA")
        pba = matmul(h, p["wba"], "nn", t + "proj_ba")
        projB = matmul(h, p["wB"], "nn", t + "projB")
        q, k, v, bg = dn_prep_fwd(projA, pba, p["cw8"], p["alog"], p["dtb"], H, t + "dn_prep")
        beta_c, gc_c, gc_r = to_chunks(bg)
        u, w, a, qd, kd, gl = dn_chunk_fwd(q, k, v, beta_c, gc_c, gc_r, t + "dn_chunk")
        o, s_in = dn_scan_fwd(u, w, a, qd, kd, gl, t + "dn_scan")
        y_a = dn_post_fwd(o, projA, p["og"], t + "dn_post")
        y_b = gmlp_fwd(projB, p["lng"], p["lnb"], p["sgw"], p["sgbT"], t + "gmlp")
        mixer_out_weights(p, l, y_b)
        ap = matmul(y_a, p["wa"], "nn", t + "branch_a")
        bp = matmul(y_b, p["wb"], "nn", t + "branch_b")
        merged = merge_fwd(projB, ap, bp, cb_a, t + "merge")
        x1 = matmul(merged, p["wo"], "nn", t + "out_proj", c=cur)
        h2 = norm_fwd(x1, p["g2"], t + "norm2")
        ffn_weights(p, l, h2)
        gp = matmul(h2, p["wg"], "nn", t + "ffn_gate")
        up = matmul(h2, p["wu"], "nn", t + "ffn_up")
        act = ffn_act_fwd(gp, up, p["fcw8"], p["fcb"], t + "ffn_act")
        x2 = matmul(act, p["wd"], "nn", t + "ffn_down", c=x1)
        saved.append(dict(p=p, x0=cur, h=h, projA=projA, pba=pba, projB=projB, q=q, k=k, v=v, chunks=(beta_c, gc_c, gc_r),
                          scan=(u, w, a, qd, kd, gl), s_in=s_in, o=o, y_a=y_a, y_b=y_b, ap=ap, bp=bp, merged=merged, x1=x1,
                          h2=h2, gp=gp, up=up, act=act))
        cur = x2

    loss_part, dx, d_final = head_fwd_bwd(cur, final_norm_g[None], tgt, "loss_head")
    loss = lax.psum(loss_part[0, 0], ("x", "y", "c"))

    grads_sh = {n: [None] * depth for n in SHARDED}
    grads_rep = {n: [None] * depth for n in REPLICATED if n != "final_norm_g"}
    exchanges = []

    def exchange(names, l, name, after=None):
        srcs = [_split(n, grads_sh[n][l], F32 if n in CONV_WEIGHTS else BF16) for n in names]
        (handle,), tok = comm_start([srcs], True, name, after=after)
        exchanges.append((names, l, handle))
        return tok

    def whole(a):
        return [(a, 0, a.shape[1])]

    sizes = [math.prod(W[n].shape) for n in REPLICATED]
    total = sum(sizes)
    rows = -(-total // LANES)
    rows = -(-rows // SUBLANES) * SUBLANES

    def pack(d):
        flat = jnp.concatenate([d[n].reshape(-1).astype(F32) for n in REPLICATED])
        return jnp.pad(flat, (0, rows * LANES - total)).reshape(rows, LANES)

    mixer_tok = None
    for l in reversed(range(depth)):
        s = saved[l]
        p = s["p"]
        t = f"l{l}_b_"
        dact = matmul(dx, p["wd"], "nt", t + "d_act")
        grads_sh["ffn_w_down"][l] = whole(matmul(s["act"], dx, "tn", t + "dw_down", out_dtype=BF16))
        fcb = p["fcb"] if mixer_tok is None else p["fcb"] + mixer_tok[0, 0]
        dgp, dup, dfcw, dfcb = ffn_act_bwd(s["gp"], s["up"], p["fcw8"], fcb, dact, t + "ffn_act")
        dh2 = matmul(dgp, p["wg"], "nt", t + "dh2_gate")
        dh2 = matmul(dup, p["wu"], "nt", t + "dh2_up", c=dh2)
        grads_sh["ffn_w_gate"][l] = whole(matmul(s["h2"], dgp, "tn", t + "dw_gate", out_dtype=BF16))
        grads_sh["ffn_w_up"][l] = whole(matmul(s["h2"], dup, "tn", t + "dw_up", out_dtype=BF16))
        grads_sh["ffn_conv_w"][l] = whole(dfcw[:3])
        grads_rep["ffn_conv_b"][l] = dfcb[0]
        tok = exchange(("ffn_w_down", "ffn_w_gate", "ffn_w_up", "ffn_conv_w"), l, t + "ffn_grads_start")
        dx1, dg2 = norm_bwd(s["x1"], p["g2"] + tok[0, 0], dh2, dx, t + "norm2")
        grads_rep["norm2_g"][l] = dg2[0]
        dmerged = matmul(dx1, p["wo"], "nt", t + "d_merged")
        grads_sh["w_out"][l] = whole(matmul(s["merged"], dx1, "tn", t + "dw_out", out_dtype=BF16))
        dga, dgb, dap, dbp = merge_bwd(s["projB"], s["ap"], s["bp"], dmerged, cb_a, t + "merge")
        dya = matmul(dap, p["wa"], "nt", t + "d_ya")
        dyb = matmul(dbp, p["wb"], "nt", t + "d_yb")
        grads_sh["w_branch_a"][l] = whole(matmul(s["y_a"], dap, "tn", t + "dw_a", out_dtype=BF16))
        grads_sh["w_branch_b"][l] = whole(matmul(s["y_b"], dbp, "tn", t + "dw_b", out_dtype=BF16))
        du_raw, dv_raw, dlng, dlnb, dsgw, dsgbT = gmlp_bwd(s["projB"], p["lng"], p["lnb"], p["sgw"], p["sgbT"], dyb, t + "gmlp")
        grads_rep["sg_ln_g"][l], grads_rep["sg_ln_b"][l] = dlng[0], dlnb[0]
        grads_rep["sg_w"][l], grads_rep["sg_b"][l] = dsgw, dsgbT.T
        do, dz, dog = dn_post_bwd(s["o"], s["projA"], p["og"], dya, t + "dn_post")
        grads_rep["dn_onorm_g"][l] = dog[0]
        du, dw, da, dqd, dkd, dgl = dn_scan_bwd(*s["scan"], s["s_in"], do, t + "dn_scan")
        dq, dk, dv, dbc, dgc, dgr = dn_chunk_bwd(s["q"], s["k"], s["v"], *s["chunks"], du, dw, da, dqd, dkd, dgl, t + "dn_chunk")
        dbg1 = _pad_lanes(jnp.concatenate([dbc.reshape(H, T), dgc.reshape(H, T)], axis=0).T, 0)
        dbg2 = _pad_lanes(dgr.reshape(H, T).T, H)
        dqkv, dba, dcw, dalog, ddtb = dn_prep_bwd(s["projA"], s["pba"], p["cw8"], p["alog"], p["dtb"], dq, dk, dv, dbg1, dbg2, H,
                                                  t + "dn_prep")
        grads_sh["dn_conv_w"][l] = whole(dcw[:4])
        grads_rep["dn_a_log"][l], grads_rep["dn_dt_bias"][l] = dalog[0, H:2 * H], ddtb[0, H:2 * H]
        tok = exchange(("w_out", "w_branch_a", "w_branch_b", "dn_conv_w"), l, t + "mixer_grads_start")
        dprojA = jnp.concatenate([dqkv, dz], axis=1)
        dprojB = jnp.concatenate([du_raw, dv_raw, dga, dgb], axis=1)
        dh = matmul(dprojA, p["wA"], "nt", t + "dh_A")
        dh = matmul(dba, p["wba"], "nt", t + "dh_ba", c=dh)
        dh = matmul(dprojB, p["wB"], "nt", t + "dh_B", c=dh)
        dx, dg1 = norm_bwd(s["x0"], p["g1"] + tok[0, 0], dh, dx1, t + "norm1")
        grads_rep["norm1_g"][l] = dg1[0]
        if l == 0:
            rep_full = {n: (jnp.stack(grads_rep[n]) if n != "final_norm_g" else d_final[0]) for n in REPLICATED}
            (small_handle,), small_tok = comm_start([[pack(rep_full)]], False, "small_grads_start")
        dwA = matmul(s["h"], dprojA, "tn", t + "dw_A", out_dtype=BF16)
        dwba = matmul(s["h"], dba, "tn", t + "dw_ba", out_dtype=BF16)
        dwB = matmul(s["h"], dprojB, "tn", t + "dw_B", out_dtype=BF16)
        grads_sh["w_in"][l] = [(dwA, 0, colA), (dwba, colA, 2 * H), (dwB, colB0, dwB.shape[1])]
        mixer_tok = exchange(("w_in",), l, t + "w_in_grads_start", after=small_tok if l == 0 else None)

    out = {}
    after = dx

    def update_group(gi, after):
        names, l, handle = exchanges[gi]
        src, land = comm_wait(handle, after, f"grads_wait{gi}")
        done = []
        for n, s_, ld in zip(names, src, land):
            res = sum_adamw_shard(s_, ld, me_arr, W[n], Mo[n], Vo[n], l, out.get(n), f"adamw_{n}_{l}")
            out[n] = list(res)
            done.append(res[0])
        return done

    for gi in range(len(exchanges) - 1):
        after = update_group(gi, after)

    (small_src,), (small_land,) = comm_wait(small_handle, after, "small_grads_wait")
    rep_parts = lax.dynamic_update_index_in_dim(small_land, small_src, my, 0)
    res = sum_adamw(rep_parts, pack(W), pack(Mo), pack(Vo), "adamw_small")
    update_group(len(exchanges) - 1, after + [res[0]])
    offs = 0
    for n, sz in zip(REPLICATED, sizes):
        out[n] = [r.reshape(-1)[offs:offs + sz].reshape(W[n].shape) for r in res]
        offs += sz

    return (loss, dx[None], *[out[n][0] for n in WEIGHTS], *[out[n][1] for n in WEIGHTS],
            *[out[n][2] for n in WEIGHTS], *[out[n][3] for n in WEIGHTS])
```

```python
import functools
import math

import jax
import jax.numpy as jnp
from jax import lax
from jax.experimental import pallas as pl
from jax.experimental.pallas import tpu as pltpu

F32 = jnp.float32
BF16 = jnp.bfloat16
EPS = 1e-6
N_DEV = 8
LANES = 128
SUBLANES = 8
HEAD_DIM = 128
DN_CHUNK = 64
SG_CHUNK = 128
VMEM_LIMIT = 56 * 1024 * 1024
MESH = pl.DeviceIdType.MESH
HIGHEST = lax.Precision.HIGHEST

ADAM_LR = 0.001
ADAM_B1 = 0.9
ADAM_B2 = 0.999
ADAM_EPS = 1e-08
ADAM_WD = 0.01
ADAM_STEP = 10


def _pick(n, target, mult=LANES):
    best = None
    d = mult
    while d <= min(n, target):
        if n % d == 0:
            best = d
        d += mult
    return n if best is None else best


def _params(sem):
    return pltpu.CompilerParams(dimension_semantics=sem, vmem_limit_bytes=VMEM_LIMIT)


_NN = (((1,), (0,)), ((), ()))
_NT = (((1,), (1,)), ((), ()))
_TN = (((0,), (0,)), ((), ()))


def _dg(a, b, dims, hi):
    if hi == 2:
        return lax.dot_general(a.astype(F32), b.astype(F32), dims, precision=HIGHEST, preferred_element_type=F32)
    if hi == 1:
        a_hi, b_hi = a.astype(BF16), b.astype(BF16)
        a_lo, b_lo = (a - a_hi.astype(F32)).astype(BF16), (b - b_hi.astype(F32)).astype(BF16)
        ax, bx = dims[0][0][0], dims[0][1][0]
        a = jnp.concatenate([a_hi, a_hi, a_lo], axis=ax)
        b = jnp.concatenate([b_hi, b_lo, b_hi], axis=bx)
        return lax.dot_general(a, b, dims, preferred_element_type=F32)
    return lax.dot_general(a.astype(BF16), b.astype(BF16), dims, preferred_element_type=F32)


@functools.partial(jax.custom_vjp, nondiff_argnums=(2,))
def mm_nn(a, b, hi=False):
    return _dg(a, b, _NN, hi)


def _mm_nn_f(a, b, hi):
    return _dg(a, b, _NN, hi), (a, b)


def _mm_nn_b(hi, res, g):
    a, b = res
    return mm_nt(g, b, hi), mm_tn(a, g, hi)


@functools.partial(jax.custom_vjp, nondiff_argnums=(2,))
def mm_nt(a, b, hi=False):
    return _dg(a, b, _NT, hi)


def _mm_nt_f(a, b, hi):
    return _dg(a, b, _NT, hi), (a, b)


def _mm_nt_b(hi, res, g):
    a, b = res
    return mm_nn(g, b, hi), mm_tn(g, a, hi)


@functools.partial(jax.custom_vjp, nondiff_argnums=(2,))
def mm_tn(a, b, hi=False):
    return _dg(a, b, _TN, hi)


def _mm_tn_f(a, b, hi):
    return _dg(a, b, _TN, hi), (a, b)


def _mm_tn_b(hi, res, g):
    a, b = res
    return mm_nt(b, g, hi), mm_nn(a, g, hi)


mm_nn.defvjp(_mm_nn_f, _mm_nn_b)
mm_nt.defvjp(_mm_nt_f, _mm_nt_b)
mm_tn.defvjp(_mm_tn_f, _mm_tn_b)


def matmul(a, b, mode, name, c=None, out_dtype=F32, tm=1024, tn=1024, tk=2048):
    if mode == "nn":
        (M, K), N = a.shape, b.shape[1]
    elif mode == "nt":
        (M, K), N = a.shape, b.shape[0]
    else:
        (K, M), N = a.shape, b.shape[1]
    tm, tn, tk = _pick(M, tm), _pick(N, tn), _pick(K, tk)
    nk = K // tk
    dims = {"nn": _NN, "nt": _NT, "tn": _TN}[mode]
    a_spec = pl.BlockSpec((tk, tm), lambda i, j, k: (k, i)) if mode == "tn" else pl.BlockSpec((tm, tk), lambda i, j, k: (i, k))
    b_spec = pl.BlockSpec((tn, tk), lambda i, j, k: (j, k)) if mode == "nt" else pl.BlockSpec((tk, tn), lambda i, j, k: (k, j))
    o_spec = pl.BlockSpec((tm, tn), lambda i, j, k: (i, j))
    has_c = c is not None

    own_acc = nk > 1 and out_dtype != F32

    def body(*refs):
        a_ref, b_ref = refs[:2]
        c_ref = refs[2] if has_c else None
        o_ref = refs[3] if has_c else refs[2]
        acc_ref = refs[-1] if own_acc else o_ref

        def dot():
            return lax.dot_general(a_ref[...].astype(BF16), b_ref[...].astype(BF16), dims, preferred_element_type=F32)

        if nk == 1:
            o_ref[...] = (dot() + c_ref[...] if has_c else dot()).astype(o_ref.dtype)
        else:
            @pl.when(pl.program_id(2) == 0)
            def _():
                acc_ref[...] = c_ref[...] if has_c else jnp.zeros_like(acc_ref)

            acc_ref[...] += dot()
            if own_acc:
                @pl.when(pl.program_id(2) == nk - 1)
                def _():
                    o_ref[...] = acc_ref[...].astype(o_ref.dtype)

    ins = [a, b] + ([c] if has_c else [])
    specs = [a_spec, b_spec] + ([o_spec] if has_c else [])
    return pl.pallas_call(
        body, name=name, grid=(M // tm, N // tn, nk), in_specs=specs, out_specs=o_spec,
        out_shape=jax.ShapeDtypeStruct((M, N), out_dtype), scratch_shapes=[pltpu.VMEM((tm, tn), F32)] if own_acc else [],
        compiler_params=_params(("parallel", "parallel", "arbitrary")),
    )(*ins)


def rowcall(name, fn, ins, in_specs, outs, out_specs, acc, nrow, ncol=1):
    n_in = len(ins)

    def body(*refs):
        i = pl.program_id(1)
        res = fn(i, *[r[...] for r in refs[:n_in]])
        for r, v, is_acc in zip(refs[n_in:], res, acc):
            if is_acc:
                @pl.when(i == 0)
                def _(r=r, v=v):
                    r[...] = v.astype(r.dtype)

                @pl.when(i > 0)
                def _(r=r, v=v):
                    r[...] += v.astype(r.dtype)
            else:
                r[...] = v.astype(r.dtype)

    return pl.pallas_call(
        body, name=name, grid=(ncol, nrow), in_specs=list(in_specs), out_specs=list(out_specs), out_shape=list(outs),
        compiler_params=_params(("parallel", "arbitrary")),
    )(*ins)


class Tiles:
    def __init__(self, T, tm):
        self.T, self.tm, self.n = T, tm, T // tm
        self.r8 = tm // SUBLANES

    def row(self, w, cb=0):
        return pl.BlockSpec((self.tm, w), lambda j, i: (i, cb))

    def rowj(self, tc):
        return pl.BlockSpec((self.tm, tc), lambda j, i: (i, j))

    def prev(self, w, cb=0):
        return pl.BlockSpec((SUBLANES, w), lambda j, i: (jnp.maximum(i * self.r8 - 1, 0), cb))

    def prevj(self, tc):
        return pl.BlockSpec((SUBLANES, tc), lambda j, i: (jnp.maximum(i * self.r8 - 1, 0), j))

    def nxt(self, w, cb=0):
        last = self.T // SUBLANES - 1
        return pl.BlockSpec((SUBLANES, w), lambda j, i: (jnp.minimum((i + 1) * self.r8, last), cb))

    def nxtj(self, tc):
        last = self.T // SUBLANES - 1
        return pl.BlockSpec((SUBLANES, tc), lambda j, i: (jnp.minimum((i + 1) * self.r8, last), j))

    def heads(self, H):
        return pl.BlockSpec((H, self.tm, HEAD_DIM), lambda j, i: (0, i, 0))

    def heads_nxt(self, H):
        last = self.T // SUBLANES - 1
        return pl.BlockSpec((H, SUBLANES, HEAD_DIM), lambda j, i: (0, jnp.minimum((i + 1) * self.r8, last), 0))


def full(shape):
    return pl.BlockSpec(tuple(shape), lambda j, i: (0,) * len(shape))


def constj(r, tc):
    return pl.BlockSpec((r, tc), lambda j, i: (0, j))


def sds(shape, dtype=F32):
    return jax.ShapeDtypeStruct(tuple(shape), dtype)


def rms(x, g):
    return x * lax.rsqrt(jnp.mean(x * x, axis=-1, keepdims=True) + EPS) * g


def silu(x):
    return x * jax.nn.sigmoid(x)


def gelu(x):
    return 0.5 * x * (1.0 + lax.erf(x * (2.0 ** -0.5)))


def causal_conv(xwin, w, K, R):
    base = SUBLANES - (K - 1)
    out = w[0:1, :] * xwin[base:base + R, :]
    for j in range(1, K):
        out = out + w[j:j + 1, :] * xwin[base + j:base + j + R, :]
    return out


def rows_to8(rows, C):
    rid = lax.broadcasted_iota(jnp.int32, (SUBLANES, C), 0)
    out = jnp.zeros((SUBLANES, C), F32)
    for k, r in enumerate(rows):
        out = out + jnp.where(rid == k, jnp.broadcast_to(r, (SUBLANES, C)), 0.0)
    return out


def dn_qkv(pre, H):
    a = silu(pre)
    W = H * HEAD_DIM

    def l2(t):
        return t * lax.rsqrt(jnp.sum(t * t, axis=-1, keepdims=True) + EPS)

    q = [l2(a[:, h * HEAD_DIM:(h + 1) * HEAD_DIM]) for h in range(H)]
    k = [l2(a[:, W + h * HEAD_DIM:W + (h + 1) * HEAD_DIM]) for h in range(H)]
    v = [a[:, 2 * W + h * HEAD_DIM:2 * W + (h + 1) * HEAD_DIM] for h in range(H)]
    return q, k, v


def dn_gates(ba, alog, dtb, H, R):
    lane = lax.broadcasted_iota(jnp.int32, (R, LANES), 1)
    beta = jax.nn.sigmoid(ba)
    g = -jnp.exp(alog) * jax.nn.softplus(ba + dtb)
    g = jnp.where((lane >= H) & (lane < 2 * H), g, 0.0)
    ri = lax.broadcasted_iota(jnp.int32, (R, R), 0)
    ci = lax.broadcasted_iota(jnp.int32, (R, R), 1)
    cum = jnp.where((ri // DN_CHUNK == ci // DN_CHUNK) & (ci <= ri), 1.0, 0.0).astype(F32)
    gc = mm_nn(cum, g, 2)
    return jnp.where(lane < H, beta, gc)


def neumann_inverse(Ls):
    C = Ls[0].shape[0]
    ri = lax.broadcasted_iota(jnp.int32, (C, C), 0)
    ci = lax.broadcasted_iota(jnp.int32, (C, C), 1)
    eye = jnp.where(ri == ci, 1.0, 0.0).astype(F32)
    P = [-L for L in Ls]
    R = [eye + p for p in P]
    for _ in range(int(math.log2(C)) - 1):
        P = [mm_nn(p, p, 1) for p in P]
        R = [r + mm_nn(r, p, 1) for r, p in zip(R, P)]
    return R


def dn_chunk(q, k, v, beta, gc, gr):
    n = len(q)
    C = q[0].shape[0]
    ri = lax.broadcasted_iota(jnp.int32, (C, C), 0)
    ci = lax.broadcasted_iota(jnp.int32, (C, C), 1)
    qs = [q[h] * (HEAD_DIM ** -0.5) for h in range(n)]
    kb = [k[h] * beta[h] for h in range(n)]
    vb = [v[h] * beta[h] for h in range(n)]
    decay = [jnp.exp(jnp.where(ri >= ci, gc[h] - gr[h], -jnp.inf)) for h in range(n)]
    L = [jnp.where(ri > ci, mm_nt(kb[h], k[h]) * decay[h], 0.0) for h in range(n)]
    attn = [jnp.where(ri >= ci, mm_nt(qs[h], k[h]) * decay[h], 0.0) for h in range(n)]
    Tinv = neumann_inverse(L)
    eg = [jnp.exp(gc[h]) for h in range(n)]
    u = [mm_nn(Tinv[h], vb[h]) for h in range(n)]
    w = [mm_nn(Tinv[h], kb[h] * eg[h]) for h in range(n)]
    qd = [qs[h] * eg[h] for h in range(n)]
    gl = [gc[h][C - 1:C, :] for h in range(n)]
    kd = [k[h] * jnp.exp(gl[h] - gc[h]) for h in range(n)]
    return u, w, attn, qd, kd, gl


def dn_step(u, w, a, qd, kd, gl, S):
    n = len(u)
    v_new = [u[h] - mm_nn(w[h], S[h]) for h in range(n)]
    o = [mm_nn(qd[h], S[h]) + mm_nn(a[h], v_new[h]) for h in range(n)]
    S_new = [S[h] * jnp.exp(gl[h]) + mm_tn(kd[h], v_new[h]) for h in range(n)]
    return o, S_new


def dn_post(o, z, g):
    H = o.shape[0]
    return jnp.concatenate([rms(o[h], g) * silu(z[:, h * HEAD_DIM:(h + 1) * HEAD_DIM]) for h in range(H)], axis=1)


def gmlp(u_raw, v_raw, ln_g, ln_b, sgw, sgbT):
    R = u_raw.shape[0]
    G = sgw.shape[0]
    nc = R // SG_CHUNK
    u = gelu(u_raw)
    vv = gelu(v_raw)
    xc = vv - jnp.mean(vv, axis=-1, keepdims=True)
    vg = xc * lax.rsqrt(jnp.mean(xc * xc, axis=-1, keepdims=True) + EPS) * ln_g + ln_b
    ri = lax.broadcasted_iota(jnp.int32, (SG_CHUNK, SG_CHUNK), 0)
    ci = lax.broadcasted_iota(jnp.int32, (SG_CHUNK, SG_CHUNK), 1)
    cols = []
    for g in range(G):
        ws = jnp.where(ri >= ci, sgw[g], 0.0)
        rhs = jnp.concatenate([vg[c * SG_CHUNK:(c + 1) * SG_CHUNK, g * HEAD_DIM:(g + 1) * HEAD_DIM] for c in range(nc)], axis=1)
        mixed = mm_nn(ws, rhs) + sgbT[:, g:g + 1]
        cols.append(jnp.concatenate([mixed[:, c * HEAD_DIM:(c + 1) * HEAD_DIM] for c in range(nc)], axis=0))
    return u * jnp.concatenate(cols, axis=1)


def merge(ga, gb, ap, bp):
    return jax.nn.sigmoid(ga) * ap + jax.nn.sigmoid(gb) * bp


def norm_fwd(x, g, name, tm=256):
    T, D = x.shape
    tl = Tiles(T, _pick(T, tm))
    (h,) = rowcall(name, lambda i, x, g: (rms(x, g),), [x, g], [tl.row(D), full((1, D))],
                   [sds((T, D), BF16)], [tl.row(D)], [False], tl.n)
    return h


def norm_bwd(x, g, dh, dres, name, tm=256):
    T, D = x.shape
    tl = Tiles(T, _pick(T, tm))

    def fn(i, x, g, dh, dres):
        _, vj = jax.vjp(rms, x, g)
        dx, dg = vj(dh.astype(F32))
        return dx + dres, dg

    return rowcall(name, fn, [x, g, dh, dres], [tl.row(D), full((1, D)), tl.row(D), tl.row(D)],
                   [sds((T, D)), sds((1, D))], [tl.row(D), full((1, D))], [False, True], tl.n)


def head_fwd_bwd(x, g, tgt, name, tm=256):
    T, D = x.shape
    tl = Tiles(T, _pick(T, tm))

    def fn(i, x, g, tgt):
        y, vj = jax.vjp(rms, x, g)
        e = y - tgt
        loss = 0.5 * jnp.sum(jnp.mean(e * e, axis=-1, keepdims=True), axis=0, keepdims=True)
        dx, dg = vj(e * (1.0 / D))
        return loss, dx, dg

    return rowcall(name, fn, [x, g, tgt], [tl.row(D), full((1, D)), tl.row(D)],
                   [sds((1, 1)), sds((T, D)), sds((1, D))], [full((1, 1)), tl.row(D), full((1, D))],
                   [True, False, True], tl.n)


def dn_prep_fwd(projA, pba, cw8, alog, dtb, H, name, tm=256):
    T = projA.shape[0]
    W3 = 3 * H * HEAD_DIM
    tl = Tiles(T, _pick(T, tm, DN_CHUNK))
    R = tl.tm

    def fn(i, xp, x, ba, cw, alog, dtb):
        xwin = jnp.concatenate([jnp.where(i > 0, xp, 0.0), x], axis=0)
        q, k, v = dn_qkv(causal_conv(xwin, cw, 4, R), H)
        return jnp.stack(q), jnp.stack(k), jnp.stack(v), dn_gates(ba, alog, dtb, H, R)

    hs = sds((H, T, HEAD_DIM))
    return rowcall(name, fn, [projA, projA, pba, cw8, alog, dtb],
                   [tl.prev(W3), tl.row(W3), tl.row(LANES), full((SUBLANES, W3)), full((1, LANES)), full((1, LANES))],
                   [hs, hs, hs, sds((T, LANES))], [tl.heads(H)] * 3 + [tl.row(LANES)], [False] * 4, tl.n)


def dn_prep_bwd(projA, pba, cw8, alog, dtb, dq, dk, dv, dbg1, dbg2, H, name, tm=256):
    T = projA.shape[0]
    W3 = 3 * H * HEAD_DIM
    tl = Tiles(T, _pick(T, tm, DN_CHUNK))
    R = tl.tm
    RE = R + SUBLANES

    def fn(i, xp, x, xn, ba, cw, alog, dtb, dq, dk, dv, dqn, dkn, dvn, dbg1, dbg2):
        last = i == tl.n - 1
        xwin = jnp.concatenate([jnp.where(i > 0, xp, 0.0), x, jnp.where(last, 0.0, xn)], axis=0)
        pre = causal_conv(xwin, cw, 4, RE)
        ext = lambda d, dn: [jnp.concatenate([d[h], jnp.where(last, 0.0, dn[h])], axis=0) for h in range(H)]
        _, vj = jax.vjp(lambda p: dn_qkv(p, H), pre)
        (dpre,) = vj((ext(dq, dqn), ext(dk, dkn), ext(dv, dvn)))
        dx = cw[0:1, :] * dpre[3:3 + R, :]
        for j in range(1, 4):
            dx = dx + cw[j:j + 1, :] * dpre[3 - j:3 - j + R, :]
        dcw = rows_to8([jnp.sum(dpre[0:R, :] * xwin[5 + j:5 + j + R, :], axis=0, keepdims=True) for j in range(4)], W3)
        _, vjg = jax.vjp(lambda ba, alog, dtb: dn_gates(ba, alog, dtb, H, R), ba, alog, dtb)
        dba, dalog, ddtb = vjg(dbg1 + dbg2)
        return dx, dba, dcw, dalog, ddtb

    return rowcall(name, fn, [projA, projA, projA, pba, cw8, alog, dtb, dq, dk, dv, dq, dk, dv, dbg1, dbg2],
                   [tl.prev(W3), tl.row(W3), tl.nxt(W3), tl.row(LANES), full((SUBLANES, W3)), full((1, LANES)), full((1, LANES))]
                   + [tl.heads(H)] * 3 + [tl.heads_nxt(H)] * 3 + [tl.row(LANES)] * 2,
                   [sds((T, W3), BF16), sds((T, LANES), BF16), sds((SUBLANES, W3)), sds((1, LANES)), sds((1, LANES))],
                   [tl.row(W3), tl.row(LANES), full((SUBLANES, W3)), full((1, LANES)), full((1, LANES))],
                   [False, False, True, True, True], tl.n)


def _chunk_specs(H, C):
    hs = pl.BlockSpec((H, C, HEAD_DIM), lambda n: (0, n, 0))
    col = pl.BlockSpec((H, 1, C, 1), lambda n: (0, n, 0, 0))
    rw = pl.BlockSpec((H, 1, 1, C), lambda n: (0, n, 0, 0))
    at = pl.BlockSpec((H, C, C), lambda n: (0, n, 0))
    one = pl.BlockSpec((H, 1, 1, 1), lambda n: (0, n, 0, 0))
    return hs, col, rw, at, one


def dn_chunk_fwd(q, k, v, beta_c, gc_c, gc_r, name):
    H, T, _ = q.shape
    C = DN_CHUNK
    N = T // C
    hs, col, rw, at, one = _chunk_specs(H, C)

    def body(q, k, v, bc, gc, gr, u, w, a, qd, kd, gl):
        hd = range(H)
        res = dn_chunk([q[h] for h in hd], [k[h] for h in hd], [v[h] for h in hd], [bc[h, 0] for h in hd],
                       [gc[h, 0] for h in hd], [gr[h, 0] for h in hd])
        for h in hd:
            for ref, val in zip((u, w, a, qd, kd), res[:5]):
                ref[h] = val[h]
            gl[h, 0] = res[5][h]

    big = sds((H, T, HEAD_DIM))
    return pl.pallas_call(
        body, name=name, grid=(N,), in_specs=[hs, hs, hs, col, col, rw], out_specs=[hs, hs, at, hs, hs, one],
        out_shape=[big, big, sds((H, T, C)), big, big, sds((H, N, 1, 1))], compiler_params=_params(("parallel",)),
    )(q, k, v, beta_c, gc_c, gc_r)


def dn_chunk_bwd(q, k, v, beta_c, gc_c, gc_r, du, dw, da, dqd, dkd, dgl, name):
    H, T, _ = q.shape
    C = DN_CHUNK
    N = T // C
    hs, col, rw, at, one = _chunk_specs(H, C)

    def body(q, k, v, bc, gc, gr, du, dw, da, dqd, dkd, dgl, dq, dk, dv, dbc, dgc, dgr):
        hd = range(H)
        _, vj = jax.vjp(dn_chunk, [q[h] for h in hd], [k[h] for h in hd], [v[h] for h in hd], [bc[h, 0] for h in hd],
                        [gc[h, 0] for h in hd], [gr[h, 0] for h in hd])
        res = vj(([du[h] for h in hd], [dw[h] for h in hd], [da[h] for h in hd], [dqd[h] for h in hd],
                  [dkd[h] for h in hd], [dgl[h, 0] for h in hd]))
        for h in hd:
            dq[h], dk[h], dv[h] = res[0][h], res[1][h], res[2][h]
            dbc[h, 0], dgc[h, 0], dgr[h, 0] = res[3][h], res[4][h], res[5][h]

    big = sds((H, T, HEAD_DIM))
    return pl.pallas_call(
        body, name=name, grid=(N,), in_specs=[hs, hs, hs, col, col, rw, hs, hs, at, hs, hs, one],
        out_specs=[hs, hs, hs, col, col, rw],
        out_shape=[big, big, big, sds((H, N, C, 1)), sds((H, N, C, 1)), sds((H, N, 1, C))],
        compiler_params=_params(("parallel",)),
    )(q, k, v, beta_c, gc_c, gc_r, du, dw, da, dqd, dkd, dgl)


def dn_scan_fwd(u, w, a, qd, kd, gl, name):
    H, T, _ = u.shape
    C = DN_CHUNK
    N = T // C
    hs, _, _, at, one = _chunk_specs(H, C)
    st = pl.BlockSpec((1, H, HEAD_DIM, HEAD_DIM), lambda n: (n, 0, 0, 0))

    def body(u, w, a, qd, kd, gl, o, s_in, S):
        @pl.when(pl.program_id(0) == 0)
        def _():
            S[...] = jnp.zeros_like(S)

        hd = range(H)
        s = [S[h] for h in hd]
        o_new, s_new = dn_step([u[h] for h in hd], [w[h] for h in hd], [a[h] for h in hd], [qd[h] for h in hd],
                               [kd[h] for h in hd], [gl[h, 0] for h in hd], s)
        for h in hd:
            s_in[0, h] = s[h]
            o[h] = o_new[h]
            S[h] = s_new[h]

    return pl.pallas_call(
        body, name=name, grid=(N,), in_specs=[hs, hs, at, hs, hs, one], out_specs=[hs, st],
        out_shape=[sds((H, T, HEAD_DIM)), sds((N, H, HEAD_DIM, HEAD_DIM))],
        scratch_shapes=[pltpu.VMEM((H, HEAD_DIM, HEAD_DIM), F32)], compiler_params=_params(("arbitrary",)),
    )(u, w, a, qd, kd, gl)


def dn_scan_bwd(u, w, a, qd, kd, gl, s_in, do, name):
    H, T, _ = u.shape
    C = DN_CHUNK
    N = T // C
    rev = lambda spec_shape, f: pl.BlockSpec(spec_shape, f)
    hs = rev((H, C, HEAD_DIM), lambda n: (0, N - 1 - n, 0))
    at = rev((H, C, C), lambda n: (0, N - 1 - n, 0))
    one = rev((H, 1, 1, 1), lambda n: (0, N - 1 - n, 0, 0))
    st = rev((1, H, HEAD_DIM, HEAD_DIM), lambda n: (N - 1 - n, 0, 0, 0))

    def body(u, w, a, qd, kd, gl, s_in, do, du, dw, da, dqd, dkd, dgl, dS):
        @pl.when(pl.program_id(0) == 0)
        def _():
            dS[...] = jnp.zeros_like(dS)

        hd = range(H)
        _, vj = jax.vjp(dn_step, [u[h] for h in hd], [w[h] for h in hd], [a[h] for h in hd], [qd[h] for h in hd],
                        [kd[h] for h in hd], [gl[h, 0] for h in hd], [s_in[0, h] for h in hd])
        res = vj(([do[h] for h in hd], [dS[h] for h in hd]))
        for h in hd:
            du[h], dw[h], da[h], dqd[h], dkd[h] = (res[j][h] for j in range(5))
            dgl[h, 0] = res[5][h]
            dS[h] = res[6][h]

    big = sds((H, T, HEAD_DIM))
    return pl.pallas_call(
        body, name=name, grid=(N,), in_specs=[hs, hs, at, hs, hs, one, st, hs], out_specs=[hs, hs, at, hs, hs, one],
        out_shape=[big, big, sds((H, T, C)), big, big, sds((H, N, 1, 1))],
        scratch_shapes=[pltpu.VMEM((H, HEAD_DIM, HEAD_DIM), F32)], compiler_params=_params(("arbitrary",)),
    )(u, w, a, qd, kd, gl, s_in, do)


def dn_post_fwd(o, projA, g, name, tm=256):
    H, T, _ = o.shape
    W = H * HEAD_DIM
    tl = Tiles(T, _pick(T, tm))
    (y,) = rowcall(name, lambda i, o, z, g: (dn_post(o, z, g),), [o, projA, g], [tl.heads(H), tl.row(W, 3), full((1, HEAD_DIM))],
                   [sds((T, W), BF16)], [tl.row(W)], [False], tl.n)
    return y


def dn_post_bwd(o, projA, g, dy, name, tm=256):
    H, T, _ = o.shape
    W = H * HEAD_DIM
    tl = Tiles(T, _pick(T, tm))

    def fn(i, o, z, g, dy):
        _, vj = jax.vjp(dn_post, o, z, g)
        return vj(dy.astype(F32))

    return rowcall(name, fn, [o, projA, g, dy], [tl.heads(H), tl.row(W, 3), full((1, HEAD_DIM)), tl.row(W)],
                   [sds((H, T, HEAD_DIM)), sds((T, W), BF16), sds((1, HEAD_DIM))],
                   [tl.heads(H), tl.row(W), full((1, HEAD_DIM))], [False, False, True], tl.n)


def gmlp_fwd(projB, ln_g, ln_b, sgw, sgbT, name, tm=512):
    T = projB.shape[0]
    G = sgw.shape[0]
    W = G * HEAD_DIM
    tl = Tiles(T, _pick(T, tm))
    (y,) = rowcall(name, lambda i, *a: (gmlp(*a),), [projB, projB, ln_g, ln_b, sgw, sgbT],
                   [tl.row(W, 0), tl.row(W, 1), full((1, W)), full((1, W)), full(sgw.shape), full(sgbT.shape)],
                   [sds((T, W), BF16)], [tl.row(W)], [False], tl.n)
    return y


def gmlp_bwd(projB, ln_g, ln_b, sgw, sgbT, dy, name, tm=512):
    T = projB.shape[0]
    G = sgw.shape[0]
    W = G * HEAD_DIM
    tl = Tiles(T, _pick(T, tm))

    def fn(i, u_raw, v_raw, ln_g, ln_b, sgw, sgbT, dy):
        _, vj = jax.vjp(gmlp, u_raw, v_raw, ln_g, ln_b, sgw, sgbT)
        return vj(dy.astype(F32))

    return rowcall(name, fn, [projB, projB, ln_g, ln_b, sgw, sgbT, dy],
                   [tl.row(W, 0), tl.row(W, 1), full((1, W)), full((1, W)), full(sgw.shape), full(sgbT.shape), tl.row(W)],
                   [sds((T, W), BF16), sds((T, W), BF16), sds((1, W)), sds((1, W)), sds(sgw.shape), sds(sgbT.shape)],
                   [tl.row(W), tl.row(W), full((1, W)), full((1, W)), full(sgw.shape), full(sgbT.shape)],
                   [False, False, True, True, True, True], tl.n)


def merge_fwd(projB, ap, bp, cb_a, name, tm=256):
    T, D = ap.shape
    tl = Tiles(T, _pick(T, tm))
    (m,) = rowcall(name, lambda i, *a: (merge(*a),), [projB, projB, ap, bp],
                   [tl.row(D, cb_a), tl.row(D, cb_a + 1), tl.row(D), tl.row(D)], [sds((T, D), BF16)], [tl.row(D)], [False], tl.n)
    return m


def merge_bwd(projB, ap, bp, dm, cb_a, name, tm=256):
    T, D = ap.shape
    tl = Tiles(T, _pick(T, tm))

    def fn(i, ga, gb, ap, bp, dm):
        _, vj = jax.vjp(merge, ga, gb, ap, bp)
        return vj(dm.astype(F32))

    return rowcall(name, fn, [projB, projB, ap, bp, dm], [tl.row(D, cb_a), tl.row(D, cb_a + 1), tl.row(D), tl.row(D), tl.row(D)],
                   [sds((T, D), BF16)] * 4, [tl.row(D)] * 4, [False] * 4, tl.n)


def ffn_act_fwd(gp, up, fcw8, fcb, name, tm=256, tc=512):
    T, F = gp.shape
    tl = Tiles(T, _pick(T, tm))
    tc = _pick(F, tc)
    R = tl.tm

    def fn(i, gprev, g, up, cw, cb):
        xwin = jnp.concatenate([jnp.where(i > 0, gprev, 0.0), g], axis=0)
        return (silu(causal_conv(xwin, cw, 3, R) + cb) * up,)

    (act,) = rowcall(name, fn, [gp, gp, up, fcw8, fcb], [tl.prevj(tc), tl.rowj(tc), tl.rowj(tc), constj(SUBLANES, tc), constj(1, tc)],
                     [sds((T, F), BF16)], [tl.rowj(tc)], [False], tl.n, F // tc)
    return act


def ffn_act_bwd(gp, up, fcw8, fcb, dact, name, tm=256, tc=512):
    T, F = gp.shape
    tl = Tiles(T, _pick(T, tm))
    tc = _pick(F, tc)
    R = tl.tm
    RE = R + SUBLANES

    def fn(i, gprev, g, gnext, up, upn, da, dan, cw, cb):
        last = i == tl.n - 1
        xwin = jnp.concatenate([jnp.where(i > 0, gprev, 0.0), g, jnp.where(last, 0.0, gnext)], axis=0)
        gate = causal_conv(xwin, cw, 3, RE) + cb
        upe = jnp.concatenate([up, upn], axis=0)
        dae = jnp.concatenate([da, jnp.where(last, 0.0, dan)], axis=0)
        s = jax.nn.sigmoid(gate)
        dgate = dae * upe * (s * (1.0 + gate * (1.0 - s)))
        dup = da * (gate[0:R, :] * s[0:R, :])
        dgp = cw[0:1, :] * dgate[2:2 + R, :] + cw[1:2, :] * dgate[1:1 + R, :] + cw[2:3, :] * dgate[0:R, :]
        dcw = rows_to8([jnp.sum(dgate[0:R, :] * xwin[6 + j:6 + j + R, :], axis=0, keepdims=True) for j in range(3)], tc)
        dcb = jnp.sum(dgate[0:R, :], axis=0, keepdims=True)
        return dgp, dup, dcw, dcb

    return rowcall(name, fn, [gp, gp, gp, up, up, dact, dact, fcw8, fcb],
                   [tl.prevj(tc), tl.rowj(tc), tl.nxtj(tc), tl.rowj(tc), tl.nxtj(tc), tl.rowj(tc), tl.nxtj(tc),
                    constj(SUBLANES, tc), constj(1, tc)],
                   [sds((T, F), BF16), sds((T, F), BF16), sds((SUBLANES, F)), sds((1, F))],
                   [tl.rowj(tc), tl.rowj(tc), constj(SUBLANES, tc), constj(1, tc)], [False, False, True, True], tl.n, F // tc)


def _me():
    return lax.axis_index("x"), lax.axis_index("y"), lax.axis_index("c")


def all_gather(shards, name):
    nt = len(shards)

    def body(*refs):
        xs, outs = refs[:nt], refs[nt:2 * nt]
        send_sems, recv_sems, local_sems = refs[2 * nt:]
        x, y, c = _me()
        me, sibling = (x, y, c), (x, y, 1 - c)
        chips = [(1 - x, y), (x, 1 - y), (1 - x, 1 - y)]

        def slot(t, p):
            return outs[t].at[4 * p[0] + 2 * p[1] + p[2]]

        def copy(t, k, block, to, src=None):
            return pltpu.make_async_remote_copy(
                src_ref=slot(t, block) if src is None else src, dst_ref=slot(t, block),
                send_sem=send_sems.at[t, k], recv_sem=recv_sems.at[t, k], device_id=to, device_id_type=MESH)

        mine = [pltpu.make_async_copy(xs[t], slot(t, me), local_sems.at[t]) for t in range(nt)]
        first = []
        for t in range(nt):
            mine[t].start()
            first.append(copy(t, 0, me, sibling, src=xs[t]))
            first += [copy(t, 1 + j, me, (*chip, c), src=xs[t]) for j, chip in enumerate(chips)]
        for cp in first:
            cp.start()
        passed = []
        for j, chip in enumerate(chips):
            for t in range(nt):
                copy(t, 1 + j, (*chip, c), me).wait_recv()
                cp = copy(t, 4 + j, (*chip, c), sibling)
                cp.start()
                passed.append(cp)
        for t in range(nt):
            copy(t, 0, sibling, me).wait_recv()
            for j, chip in enumerate(chips):
                copy(t, 4 + j, (*chip, 1 - c), me).wait_recv()
        for cp in first + passed:
            cp.wait_send()
        for t in range(nt):
            mine[t].wait()

    any_spec = pl.BlockSpec(memory_space=pl.ANY)
    return pl.pallas_call(
        body, name=name, in_specs=[any_spec] * nt, out_specs=[any_spec] * nt,
        out_shape=[jax.ShapeDtypeStruct((N_DEV,) + s.shape, s.dtype) for s in shards],
        scratch_shapes=[pltpu.SemaphoreType.DMA((nt, 7)), pltpu.SemaphoreType.DMA((nt, 7)), pltpu.SemaphoreType.DMA((nt,))],
    )(*shards)


_HBM = pl.BlockSpec(memory_space=pltpu.HBM)
_SEM = pl.BlockSpec(memory_space=pltpu.SEMAPHORE)
_ANY = pl.BlockSpec(memory_space=pl.ANY)
_DATAFLOW = pltpu.SideEffectType.DATAFLOW_SIDE_EFFECTING


def _peers():
    x, y, c = _me()
    out = []
    for k in range(1, N_DEV):
        p = (x ^ (k >> 2), y ^ ((k >> 1) & 1), c ^ (k & 1))
        out.append((k, p, 4 * p[0] + 2 * p[1] + p[2]))
    return out


def _split_copy(src, land, send_sems, recv_sems, t, k, peer, slot, my, scatter, receiving):
    return pltpu.make_async_remote_copy(
        src_ref=src.at[slot] if scatter else src, dst_ref=land.at[slot if receiving else my],
        send_sem=send_sems.at[t * (N_DEV - 1) + k - 1], recv_sem=recv_sems.at[t * (N_DEV - 1) + k - 1],
        device_id=peer, device_id_type=MESH)


def comm_start(groups, scatter, name, after=None):
    flat = [a for g in groups for a in g]
    nt = len(flat)
    lands = [lax.empty(a.shape if scatter else (N_DEV,) + a.shape, a.dtype) for a in flat]
    ng = len(groups)
    n_after = 0 if after is None else 1

    def body(*refs):
        src, land = refs[:nt], refs[nt:2 * nt]
        sems = refs[2 * nt + n_after:2 * nt + n_after + 2 * ng]
        token = refs[-1]
        x, y, c = _me()
        my = 4 * x + 2 * y + c
        t0 = 0
        for gi, g in enumerate(groups):
            for k, peer, slot in _peers():
                for t in range(len(g)):
                    _split_copy(src[t0 + t], land[t0 + t], sems[2 * gi], sems[2 * gi + 1], t, k, peer, slot, my, scatter,
                                False).start()
            t0 += len(g)
        token[...] = jnp.zeros_like(token)

    sem_shapes = []
    for g in groups:
        sem_shapes += [pltpu.SemaphoreType.DMA((len(g) * (N_DEV - 1),))] * 2
    res = pl.pallas_call(
        body, name=name, in_specs=[_HBM] * (2 * nt) + [_HBM] * n_after,
        out_specs=[_SEM] * (2 * ng) + [_HBM] * (2 * nt) + [pl.BlockSpec(memory_space=pltpu.VMEM)],
        out_shape=sem_shapes + [pltpu.HBM(a.shape, a.dtype) for a in flat + lands] + [sds((SUBLANES, LANES))],
        input_output_aliases={i: 2 * ng + i for i in range(2 * nt)},
        compiler_params=pltpu.CompilerParams(has_side_effects=_DATAFLOW),
    )(*[pltpu.with_memory_space_constraint(a, pltpu.HBM) for a in flat + lands + ([] if after is None else [after])])
    handles = []
    t0 = 0
    for gi, g in enumerate(groups):
        n = len(g)
        handles.append(dict(sems=(res[2 * gi], res[2 * gi + 1]), src=res[2 * ng + t0:2 * ng + t0 + n],
                            land=res[2 * ng + nt + t0:2 * ng + nt + t0 + n], scatter=scatter))
        t0 += n
    return handles, res[-1]


def comm_wait(handle, after, name):
    src, land, scatter = handle["src"], handle["land"], handle["scatter"]
    nt = len(src)

    def body(*refs):
        src_r, land_r = refs[:nt], refs[nt:2 * nt]
        send_sems, recv_sems = refs[2 * nt], refs[2 * nt + 1]
        x, y, c = _me()
        my = 4 * x + 2 * y + c
        for k, peer, slot in _peers():
            for t in range(nt):
                _split_copy(src_r[t], land_r[t], send_sems, recv_sems, t, k, peer, slot, my, scatter, False).wait_send()
                _split_copy(src_r[t], land_r[t], send_sems, recv_sems, t, k, peer, slot, my, scatter, True).wait_recv()

    after = list(after) if isinstance(after, (list, tuple)) else [after]
    res = pl.pallas_call(
        body, name=name, in_specs=[_HBM] * (2 * nt) + [_SEM, _SEM] + [_HBM] * len(after), out_specs=[_HBM] * (2 * nt),
        out_shape=[pltpu.HBM(a.shape, a.dtype) for a in list(src) + list(land)],
        input_output_aliases={i: i for i in range(2 * nt)},
        compiler_params=pltpu.CompilerParams(has_side_effects=_DATAFLOW),
    )(*src, *land, *handle["sems"], *[pltpu.with_memory_space_constraint(a, pltpu.HBM) for a in after])
    return res[:nt], res[nt:]


def sum_adamw_shard(own_src, land, me, w, m, v, l, prev, name, tr=256):
    L, R, C = w.shape
    by_rows = R % SUBLANES == 0 or C % LANES != 0
    tr, tc = (_pick(R, tr, SUBLANES), C) if by_rows else (R, _pick(C, 256))
    steps = R // tr if by_rows else C // tc
    c1 = 1.0 - ADAM_B1 ** ADAM_STEP
    c2 = 1.0 - ADAM_B2 ** ADAM_STEP
    n_prev = 0 if prev is None else 4

    def at(lead, i):
        return (lead, i, 0) if by_rows else (lead, 0, i)

    def body(me_ref, *refs):
        parts = refs[:N_DEV]
        w_r, m_r, v_r = refs[N_DEV:N_DEV + 3]
        g_o, d_o, m_o, v_o = refs[N_DEV + 3 + n_prev:]
        g = parts[0][0].astype(F32)
        for k in range(1, N_DEV):
            g = g + parts[k][0].astype(F32)
        mn = ADAM_B1 * m_r[0] + (1.0 - ADAM_B1) * g
        vn = ADAM_B2 * v_r[0] + (1.0 - ADAM_B2) * (g * g)
        g_o[0] = g
        d_o[0] = -ADAM_LR * ((mn / c1) / (jnp.sqrt(vn / c2) + ADAM_EPS) + ADAM_WD * w_r[0])
        m_o[0] = mn
        v_o[0] = vn

    part_specs = [pl.BlockSpec((1, tr, tc), lambda i, me, k=k: at(me[0] ^ k, i)) for k in range(N_DEV)]
    lay = pl.BlockSpec((1, tr, tc), lambda i, me: at(l, i))
    grid_spec = pltpu.PrefetchScalarGridSpec(
        num_scalar_prefetch=1, grid=(steps,), in_specs=part_specs + [lay] * 3 + [_ANY] * n_prev, out_specs=[lay] * 4)
    return pl.pallas_call(
        body, name=name, grid_spec=grid_spec, out_shape=[sds((L, R, C))] * 4,
        input_output_aliases={1 + N_DEV + 3 + j: j for j in range(n_prev)}, compiler_params=_params(("parallel",)),
    )(me, own_src, *[land] * (N_DEV - 1), w, m, v, *([] if prev is None else prev))


def sum_adamw(parts, w, m, v, name, tr=256):
    _, R, C = parts.shape
    tr = _pick(R, tr, SUBLANES)
    c1 = 1.0 - ADAM_B1 ** ADAM_STEP
    c2 = 1.0 - ADAM_B2 ** ADAM_STEP

    def body(p, w, m, v, g_o, d_o, m_o, v_o):
        g = p[0].astype(F32)
        for d in range(1, N_DEV):
            g = g + p[d].astype(F32)
        mn = ADAM_B1 * m[...] + (1.0 - ADAM_B1) * g
        vn = ADAM_B2 * v[...] + (1.0 - ADAM_B2) * (g * g)
        m_hat = mn / c1
        v_hat = vn / c2
        g_o[...] = g
        d_o[...] = -ADAM_LR * (m_hat / (jnp.sqrt(v_hat) + ADAM_EPS) + ADAM_WD * w[...])
        m_o[...] = mn
        v_o[...] = vn

    blk = pl.BlockSpec((tr, C), lambda i: (i, 0))
    return pl.pallas_call(
        body, name=name, grid=(R // tr,), in_specs=[pl.BlockSpec((N_DEV, tr, C), lambda i: (0, i, 0)), blk, blk, blk],
        out_specs=[blk] * 4, out_shape=[sds((R, C))] * 4, compiler_params=_params(("parallel",)),
    )(parts, w, m, v)


SHARDED = ("w_in", "dn_conv_w", "w_branch_a", "w_branch_b", "w_out", "ffn_w_gate", "ffn_w_up", "ffn_conv_w", "ffn_w_down")
TRANSPOSED = ("w_in", "ffn_w_gate", "ffn_w_up")
COL_SHARDED = ("dn_conv_w", "w_branch_a", "w_branch_b", "ffn_conv_w")
CONV_WEIGHTS = ("dn_conv_w", "ffn_conv_w")
REPLICATED = ("norm1_g", "dn_a_log", "dn_dt_bias", "dn_onorm_g", "sg_ln_g", "sg_ln_b", "sg_w", "sg_b", "norm2_g",
              "ffn_conv_b", "final_norm_g")
WEIGHTS = ("norm1_g", "w_in", "dn_conv_w", "dn_a_log", "dn_dt_bias", "dn_onorm_g", "sg_ln_g", "sg_ln_b", "sg_w", "sg_b",
           "w_branch_a", "w_branch_b", "w_out", "norm2_g", "ffn_w_gate", "ffn_w_up", "ffn_conv_w", "ffn_conv_b",
           "ffn_w_down", "final_norm_g")


def _columns(pieces, lo, hi):
    out = []
    for a, start, width in pieces:
        s, e = max(lo, start), min(hi, start + width)
        if s < e:
            out.append(a[:, s - start:e - start])
    return out[0] if len(out) == 1 else jnp.concatenate(out, axis=1)


def _assemble(name, g):
    if name in COL_SHARDED:
        return jnp.concatenate([g[d] for d in range(N_DEV)], axis=1)
    return g.reshape(N_DEV * g.shape[1], g.shape[2])


def _split(name, pieces, dtype):
    total = sum(w for _, _, w in pieces)
    if name in COL_SHARDED:
        cs = total // N_DEV
        return jnp.stack([_columns(pieces, d * cs, (d + 1) * cs).astype(dtype) for d in range(N_DEV)])
    a = pieces[0][0] if len(pieces) == 1 else jnp.concatenate([p[:w] for p, _, w in pieces], axis=0)
    return a.reshape(N_DEV, a.shape[0] // N_DEV, a.shape[1]).astype(dtype)


def _pad_lanes(a, lo, width=LANES):
    return jnp.pad(a, ((0, 0), (lo, width - lo - a.shape[1])))


def _pad_rows(a, rows=SUBLANES):
    return jnp.pad(a, ((0, rows - a.shape[0]), (0, 0)))


def kernel(x, norm1_g, w_in, dn_conv_w, dn_a_log, dn_dt_bias, dn_onorm_g, sg_ln_g, sg_ln_b, sg_w, sg_b, w_branch_a, w_branch_b, w_out, norm2_g, ffn_w_gate, ffn_w_up, ffn_conv_w, ffn_conv_b, ffn_w_down, final_norm_g, loss_target, m_norm1_g, m_w_in, m_dn_conv_w, m_dn_a_log, m_dn_dt_bias, m_dn_onorm_g, m_sg_ln_g, m_sg_ln_b, m_sg_w, m_sg_b, m_w_branch_a, m_w_branch_b, m_w_out, m_norm2_g, m_ffn_w_gate, m_ffn_w_up, m_ffn_conv_w, m_ffn_conv_b, m_ffn_w_down, m_final_norm_g, v_norm1_g, v_w_in, v_dn_conv_w, v_dn_a_log, v_dn_dt_bias, v_dn_onorm_g, v_sg_ln_g, v_sg_ln_b, v_sg_w, v_sg_b, v_w_branch_a, v_w_branch_b, v_w_out, v_norm2_g, v_ffn_w_gate, v_ffn_w_up, v_ffn_conv_w, v_ffn_conv_b, v_ffn_w_down, v_final_norm_g):
    W = dict(norm1_g=norm1_g, w_in=w_in, dn_conv_w=dn_conv_w, dn_a_log=dn_a_log, dn_dt_bias=dn_dt_bias, dn_onorm_g=dn_onorm_g,
             sg_ln_g=sg_ln_g, sg_ln_b=sg_ln_b, sg_w=sg_w, sg_b=sg_b, w_branch_a=w_branch_a, w_branch_b=w_branch_b, w_out=w_out,
             norm2_g=norm2_g, ffn_w_gate=ffn_w_gate, ffn_w_up=ffn_w_up, ffn_conv_w=ffn_conv_w, ffn_conv_b=ffn_conv_b,
             ffn_w_down=ffn_w_down, final_norm_g=final_norm_g)
    Mo = dict(norm1_g=m_norm1_g, w_in=m_w_in, dn_conv_w=m_dn_conv_w, dn_a_log=m_dn_a_log, dn_dt_bias=m_dn_dt_bias,
              dn_onorm_g=m_dn_onorm_g, sg_ln_g=m_sg_ln_g, sg_ln_b=m_sg_ln_b, sg_w=m_sg_w, sg_b=m_sg_b, w_branch_a=m_w_branch_a,
              w_branch_b=m_w_branch_b, w_out=m_w_out, norm2_g=m_norm2_g, ffn_w_gate=m_ffn_w_gate, ffn_w_up=m_ffn_w_up,
              ffn_conv_w=m_ffn_conv_w, ffn_conv_b=m_ffn_conv_b, ffn_w_down=m_ffn_w_down, final_norm_g=m_final_norm_g)
    Vo = dict(norm1_g=v_norm1_g, w_in=v_w_in, dn_conv_w=v_dn_conv_w, dn_a_log=v_dn_a_log, dn_dt_bias=v_dn_dt_bias,
              dn_onorm_g=v_dn_onorm_g, sg_ln_g=v_sg_ln_g, sg_ln_b=v_sg_ln_b, sg_w=v_sg_w, sg_b=v_sg_b, w_branch_a=v_w_branch_a,
              w_branch_b=v_w_branch_b, w_out=v_w_out, norm2_g=v_norm2_g, ffn_w_gate=v_ffn_w_gate, ffn_w_up=v_ffn_w_up,
              ffn_conv_w=v_ffn_conv_w, ffn_conv_b=v_ffn_conv_b, ffn_w_down=v_ffn_w_down, final_norm_g=v_final_norm_g)

    xs = x[0]
    tgt = loss_target[0]
    T, D = xs.shape
    depth = norm1_g.shape[0]
    H = dn_a_log.shape[1]
    G = sg_w.shape[1]
    WA = H * HEAD_DIM
    WB = G * HEAD_DIM
    N = T // DN_CHUNK
    colA = 4 * WA
    colB0 = colA + 2 * H
    cb_a = (2 * WB) // D

    my = 4 * lax.axis_index("x") + 2 * lax.axis_index("y") + lax.axis_index("c")
    me_arr = my.astype(jnp.int32).reshape(1)

    def view(d):
        return {n: (jnp.transpose(d[n], (0, 2, 1)) if n in TRANSPOSED else d[n]) for n in SHARDED}

    Wv, Mv, Vv = view(W), view(Mo), view(Vo)

    def shard(n, l):
        return Wv[n][l] if n in CONV_WEIGHTS else Wv[n][l].astype(BF16)

    first = [("w_in", 0), ("dn_conv_w", 0)]
    first_blocks = all_gather([shard(n, l) for n, l in first], "gather_first")
    gathered = dict(zip(first, first_blocks))
    gather_names = [[("w_branch_a", 0), ("w_branch_b", 0), ("w_out", 0)]]
    for l in range(depth):
        if l > 0:
            gather_names.append([("w_in", l), ("dn_conv_w", l), ("w_branch_a", l), ("w_branch_b", l), ("w_out", l)])
        gather_names.append([("ffn_w_gate", l), ("ffn_w_up", l), ("ffn_conv_w", l), ("ffn_w_down", l)])
    gather_handles, gather_tok = comm_start([[shard(n, l) for n, l in g] for g in gather_names], False, "gather_start",
                                            after=first_blocks[0])

    def need(n, l, after):
        if (n, l) not in gathered:
            gi = [i for i, g in enumerate(gather_names) if (n, l) in g][0]
            src, land = comm_wait(gather_handles[gi], after, f"gather_wait{gi}")
            for key, s, ld in zip(gather_names[gi], src, land):
                gathered[key] = lax.dynamic_update_index_in_dim(ld, s, my, 0)
        return gathered[(n, l)]

    def full(n, l, after):
        return _assemble(n, need(n, l, after))

    def layer_weights(l):
        return dict(
            g1=norm1_g[l][None], g2=norm2_g[l][None], alog=_pad_lanes(dn_a_log[l][None], H), dtb=_pad_lanes(dn_dt_bias[l][None], H),
            og=dn_onorm_g[l][None], lng=sg_ln_g[l][None], lnb=sg_ln_b[l][None], sgw=sg_w[l], sgbT=sg_b[l].T, fcb=ffn_conv_b[l][None])

    def mixer_in_weights(p, l, after):
        wt = full("w_in", l, after)
        p.update(wA=wt[:colA], wba=_pad_rows(wt[colA:colB0], LANES), wB=wt[colB0:], cw8=_pad_rows(full("dn_conv_w", l, after)))

    def mixer_out_weights(p, l, after):
        p.update(wa=full("w_branch_a", l, after), wb=full("w_branch_b", l, after), wo=full("w_out", l, after))

    def ffn_weights(p, l, after):
        p.update(wg=full("ffn_w_gate", l, after), wu=full("ffn_w_up", l, after), fcw8=_pad_rows(full("ffn_conv_w", l, after)),
                 wd=full("ffn_w_down", l, after))

    def to_chunks(bg):
        bt = bg[:, :2 * H].T
        beta_c = bt[:H].reshape(H, N, DN_CHUNK, 1)
        gc_c = bt[H:].reshape(H, N, DN_CHUNK, 1)
        return beta_c, gc_c, bt[H:].reshape(H, N, 1, DN_CHUNK)

    saved = []
    cur = xs
    for l in range(depth):
        p = layer_weights(l)
        t = f"l{l}_"
        h = norm_fwd(cur, p["g1"] + gather_tok[0, 0] if l == 0 else p["g1"], t + "norm1")
        mixer_in_weights(p, l, h)
        projA = matmul(h, p["wA"], "nt", t + "projA")
        pba = matmul(h, p["wba"], "nt", t + "proj_ba")
        projB = matmul(h, p["wB"], "nt", t + "projB")
        q, k, v, bg = dn_prep_fwd(projA, pba, p["cw8"], p["alog"], p["dtb"], H, t + "dn_prep")
        beta_c, gc_c, gc_r = to_chunks(bg)
        u, w, a, qd, kd, gl = dn_chunk_fwd(q, k, v, beta_c, gc_c, gc_r, t + "dn_chunk")
        o, s_in = dn_scan_fwd(u, w, a, qd, kd, gl, t + "dn_scan")
        y_a = dn_post_fwd(o, projA, p["og"], t + "dn_post")
        y_b = gmlp_fwd(projB, p["lng"], p["lnb"], p["sgw"], p["sgbT"], t + "gmlp")
        mixer_out_weights(p, l, y_b)
        ap = matmul(y_a, p["wa"], "nn", t + "branch_a")
        bp = matmul(y_b, p["wb"], "nn", t + "branch_b")
        merged = merge_fwd(projB, ap, bp, cb_a, t + "merge")
        x1 = matmul(merged, p["wo"], "nn", t + "out_proj", c=cur)
        h2 = norm_fwd(x1, p["g2"], t + "norm2")
        ffn_weights(p, l, h2)
        gp = matmul(h2, p["wg"], "nt", t + "ffn_gate")
        up = matmul(h2, p["wu"], "nt", t + "ffn_up")
        act = ffn_act_fwd(gp, up, p["fcw8"], p["fcb"], t + "ffn_act")
        x2 = matmul(act, p["wd"], "nn", t + "ffn_down", c=x1)
        saved.append(dict(p=p, x0=cur, h=h, projA=projA, pba=pba, projB=projB, q=q, k=k, v=v, chunks=(beta_c, gc_c, gc_r),
                          scan=(u, w, a, qd, kd, gl), s_in=s_in, o=o, y_a=y_a, y_b=y_b, ap=ap, bp=bp, merged=merged, x1=x1,
                          h2=h2, gp=gp, up=up, act=act))
        cur = x2

    loss_part, dx, d_final = head_fwd_bwd(cur, final_norm_g[None], tgt, "loss_head")
    loss = lax.psum(loss_part[0, 0], ("x", "y", "c"))

    grads_sh = {n: [None] * depth for n in SHARDED}
    grads_rep = {n: [None] * depth for n in REPLICATED if n != "final_norm_g"}
    exchanges = []

    def exchange(names, l, name, after=None):
        srcs = [_split(n, grads_sh[n][l], F32 if n in CONV_WEIGHTS else BF16) for n in names]
        (handle,), tok = comm_start([srcs], True, name, after=after)
        exchanges.append((names, l, handle))
        return tok

    def whole(a):
        return [(a, 0, a.shape[1])]

    sizes = [math.prod(W[n].shape) for n in REPLICATED]
    total = sum(sizes)
    rows = -(-total // LANES)
    rows = -(-rows // SUBLANES) * SUBLANES

    def pack(d):
        flat = jnp.concatenate([d[n].reshape(-1).astype(F32) for n in REPLICATED])
        return jnp.pad(flat, (0, rows * LANES - total)).reshape(rows, LANES)

    mixer_tok = None
    for l in reversed(range(depth)):
        s = saved[l]
        p = s["p"]
        t = f"l{l}_b_"
        dact = matmul(dx, p["wd"], "nt", t + "d_act")
        grads_sh["ffn_w_down"][l] = whole(matmul(s["act"], dx, "tn", t + "dw_down", out_dtype=BF16))
        fcb = p["fcb"] if mixer_tok is None else p["fcb"] + mixer_tok[0, 0]
        dgp, dup, dfcw, dfcb = ffn_act_bwd(s["gp"], s["up"], p["fcw8"], fcb, dact, t + "ffn_act")
        dh2 = matmul(dgp, p["wg"], "nn", t + "dh2_gate")
        dh2 = matmul(dup, p["wu"], "nn", t + "dh2_up", c=dh2)
        grads_sh["ffn_w_gate"][l] = whole(matmul(dgp, s["h2"], "tn", t + "dw_gate", out_dtype=BF16))
        grads_sh["ffn_w_up"][l] = whole(matmul(dup, s["h2"], "tn", t + "dw_up", out_dtype=BF16))
        grads_sh["ffn_conv_w"][l] = whole(dfcw[:3])
        grads_rep["ffn_conv_b"][l] = dfcb[0]
        tok = exchange(("ffn_w_down", "ffn_w_gate", "ffn_w_up", "ffn_conv_w"), l, t + "ffn_grads_start")
        dx1, dg2 = norm_bwd(s["x1"], p["g2"] + tok[0, 0], dh2, dx, t + "norm2")
        grads_rep["norm2_g"][l] = dg2[0]
        dmerged = matmul(dx1, p["wo"], "nt", t + "d_merged")
        grads_sh["w_out"][l] = whole(matmul(s["merged"], dx1, "tn", t + "dw_out", out_dtype=BF16))
        dga, dgb, dap, dbp = merge_bwd(s["projB"], s["ap"], s["bp"], dmerged, cb_a, t + "merge")
        dya = matmul(dap, p["wa"], "nt", t + "d_ya")
        dyb = matmul(dbp, p["wb"], "nt", t + "d_yb")
        grads_sh["w_branch_a"][l] = whole(matmul(s["y_a"], dap, "tn", t + "dw_a", out_dtype=BF16))
        grads_sh["w_branch_b"][l] = whole(matmul(s["y_b"], dbp, "tn", t + "dw_b", out_dtype=BF16))
        du_raw, dv_raw, dlng, dlnb, dsgw, dsgbT = gmlp_bwd(s["projB"], p["lng"], p["lnb"], p["sgw"], p["sgbT"], dyb, t + "gmlp")
        grads_rep["sg_ln_g"][l], grads_rep["sg_ln_b"][l] = dlng[0], dlnb[0]
        grads_rep["sg_w"][l], grads_rep["sg_b"][l] = dsgw, dsgbT.T
        do, dz, dog = dn_post_bwd(s["o"], s["projA"], p["og"], dya, t + "dn_post")
        grads_rep["dn_onorm_g"][l] = dog[0]
        du, dw, da, dqd, dkd, dgl = dn_scan_bwd(*s["scan"], s["s_in"], do, t + "dn_scan")
        dq, dk, dv, dbc, dgc, dgr = dn_chunk_bwd(s["q"], s["k"], s["v"], *s["chunks"], du, dw, da, dqd, dkd, dgl, t + "dn_chunk")
        dbg1 = _pad_lanes(jnp.concatenate([dbc.reshape(H, T), dgc.reshape(H, T)], axis=0).T, 0)
        dbg2 = _pad_lanes(dgr.reshape(H, T).T, H)
        dqkv, dba, dcw, dalog, ddtb = dn_prep_bwd(s["projA"], s["pba"], p["cw8"], p["alog"], p["dtb"], dq, dk, dv, dbg1, dbg2, H,
                                                  t + "dn_prep")
        grads_sh["dn_conv_w"][l] = whole(dcw[:4])
        grads_rep["dn_a_log"][l], grads_rep["dn_dt_bias"][l] = dalog[0, H:2 * H], ddtb[0, H:2 * H]
        tok = exchange(("w_out", "w_branch_a", "w_branch_b", "dn_conv_w"), l, t + "mixer_grads_start")
        dba = dba + tok[0, 0].astype(BF16)
        dprojA = jnp.concatenate([dqkv, dz], axis=1)
        dprojB = jnp.concatenate([du_raw, dv_raw, dga, dgb], axis=1)
        dwA = matmul(dprojA, s["h"], "tn", t + "dw_A", out_dtype=BF16)
        dwba = matmul(dba, s["h"], "tn", t + "dw_ba", out_dtype=BF16)
        dwB = matmul(dprojB, s["h"], "tn", t + "dw_B", out_dtype=BF16)
        grads_sh["w_in"][l] = [(dwA, 0, colA), (dwba, colA, 2 * H), (dwB, colB0, dwB.shape[0])]
        mixer_tok = exchange(("w_in",), l, t + "w_in_grads_start")
        dh = matmul(dba, p["wba"] + mixer_tok[0, 0].astype(BF16), "nn", t + "dh_ba")
        dh = matmul(dprojA, p["wA"], "nn", t + "dh_A", c=dh)
        dh = matmul(dprojB, p["wB"], "nn", t + "dh_B", c=dh)
        dx, dg1 = norm_bwd(s["x0"], p["g1"], dh, dx1, t + "norm1")
        grads_rep["norm1_g"][l] = dg1[0]
        if l == 0:
            rep_full = {n: (jnp.stack(grads_rep[n]) if n != "final_norm_g" else d_final[0]) for n in REPLICATED}
            (small_handle,), small_tok = comm_start([[pack(rep_full)]], False, "small_grads_start")

    out = {}
    after = [dx, small_tok]

    def update_group(gi, after):
        names, l, handle = exchanges[gi]
        src, land = comm_wait(handle, after, f"grads_wait{gi}")
        done = []
        for n, s_, ld in zip(names, src, land):
            res = sum_adamw_shard(s_, ld, me_arr, Wv[n], Mv[n], Vv[n], l, out.get(n), f"adamw_{n}_{l}")
            out[n] = list(res)
            done.append(res[0])
        return done

    for gi in range(len(exchanges) - 1):
        after = update_group(gi, after)

    (small_src,), (small_land,) = comm_wait(small_handle, after, "small_grads_wait")
    rep_parts = lax.dynamic_update_index_in_dim(small_land, small_src, my, 0)
    res = sum_adamw(rep_parts, pack(W), pack(Mo), pack(Vo), "adamw_small")
    update_group(len(exchanges) - 1, after + [res[0]])
    for n in TRANSPOSED:
        out[n] = [jnp.transpose(r, (0, 2, 1)) for r in out[n]]
    offs = 0
    for n, sz in zip(REPLICATED, sizes):
        out[n] = [r.reshape(-1)[offs:offs + sz].reshape(W[n].shape) for r in res]
        offs += sz

    return (loss, dx[None], *[out[n][0] for n in WEIGHTS], *[out[n][1] for n in WEIGHTS],
            *[out[n][2] for n in WEIGHTS], *[out[n][3] for n in WEIGHTS])
```

```python
import functools
import math

import jax
import jax.numpy as jnp
from jax import lax
from jax.experimental import pallas as pl
from jax.experimental.pallas import tpu as pltpu

F32 = jnp.float32
BF16 = jnp.bfloat16
EPS = 1e-6
N_DEV = 8
LANES = 128
SUBLANES = 8
HEAD_DIM = 128
DN_CHUNK = 64
SG_CHUNK = 128
VMEM_LIMIT = 56 * 1024 * 1024
MESH = pl.DeviceIdType.MESH
HIGHEST = lax.Precision.HIGHEST

ADAM_LR = 0.001
ADAM_B1 = 0.9
ADAM_B2 = 0.999
ADAM_EPS = 1e-08
ADAM_WD = 0.01
ADAM_STEP = 10


def _pick(n, target, mult=LANES):
    best = None
    d = mult
    while d <= min(n, target):
        if n % d == 0:
            best = d
        d += mult
    return n if best is None else best


def _params(sem):
    return pltpu.CompilerParams(dimension_semantics=sem, vmem_limit_bytes=VMEM_LIMIT)


_NN = (((1,), (0,)), ((), ()))
_NT = (((1,), (1,)), ((), ()))
_TN = (((0,), (0,)), ((), ()))


def _dg(a, b, dims, hi):
    if hi == 2:
        return lax.dot_general(a.astype(F32), b.astype(F32), dims, precision=HIGHEST, preferred_element_type=F32)
    if hi == 1:
        a_hi, b_hi = a.astype(BF16), b.astype(BF16)
        a_lo, b_lo = (a - a_hi.astype(F32)).astype(BF16), (b - b_hi.astype(F32)).astype(BF16)
        ax, bx = dims[0][0][0], dims[0][1][0]
        a = jnp.concatenate([a_hi, a_hi, a_lo], axis=ax)
        b = jnp.concatenate([b_hi, b_lo, b_hi], axis=bx)
        return lax.dot_general(a, b, dims, preferred_element_type=F32)
    return lax.dot_general(a.astype(BF16), b.astype(BF16), dims, preferred_element_type=F32)


@functools.partial(jax.custom_vjp, nondiff_argnums=(2,))
def mm_nn(a, b, hi=False):
    return _dg(a, b, _NN, hi)


def _mm_nn_f(a, b, hi):
    return _dg(a, b, _NN, hi), (a, b)


def _mm_nn_b(hi, res, g):
    a, b = res
    return mm_nt(g, b, hi), mm_tn(a, g, hi)


@functools.partial(jax.custom_vjp, nondiff_argnums=(2,))
def mm_nt(a, b, hi=False):
    return _dg(a, b, _NT, hi)


def _mm_nt_f(a, b, hi):
    return _dg(a, b, _NT, hi), (a, b)


def _mm_nt_b(hi, res, g):
    a, b = res
    return mm_nn(g, b, hi), mm_tn(g, a, hi)


@functools.partial(jax.custom_vjp, nondiff_argnums=(2,))
def mm_tn(a, b, hi=False):
    return _dg(a, b, _TN, hi)


def _mm_tn_f(a, b, hi):
    return _dg(a, b, _TN, hi), (a, b)


def _mm_tn_b(hi, res, g):
    a, b = res
    return mm_nt(b, g, hi), mm_nn(a, g, hi)


mm_nn.defvjp(_mm_nn_f, _mm_nn_b)
mm_nt.defvjp(_mm_nt_f, _mm_nt_b)
mm_tn.defvjp(_mm_tn_f, _mm_tn_b)


def matmul(a, b, mode, name, c=None, out_dtype=F32, tm=1024, tn=1024, tk=2048):
    if mode == "nn":
        (M, K), N = a.shape, b.shape[1]
    elif mode == "nt":
        (M, K), N = a.shape, b.shape[0]
    else:
        (K, M), N = a.shape, b.shape[1]
    tm, tn, tk = _pick(M, tm), _pick(N, tn), _pick(K, tk)
    nk = K // tk
    dims = {"nn": _NN, "nt": _NT, "tn": _TN}[mode]
    a_spec = pl.BlockSpec((tk, tm), lambda i, j, k: (k, i)) if mode == "tn" else pl.BlockSpec((tm, tk), lambda i, j, k: (i, k))
    b_spec = pl.BlockSpec((tn, tk), lambda i, j, k: (j, k)) if mode == "nt" else pl.BlockSpec((tk, tn), lambda i, j, k: (k, j))
    o_spec = pl.BlockSpec((tm, tn), lambda i, j, k: (i, j))
    has_c = c is not None

    own_acc = nk > 1 and out_dtype != F32

    def body(*refs):
        a_ref, b_ref = refs[:2]
        c_ref = refs[2] if has_c else None
        o_ref = refs[3] if has_c else refs[2]
        acc_ref = refs[-1] if own_acc else o_ref

        def dot():
            return lax.dot_general(a_ref[...].astype(BF16), b_ref[...].astype(BF16), dims, preferred_element_type=F32)

        if nk == 1:
            o_ref[...] = (dot() + c_ref[...] if has_c else dot()).astype(o_ref.dtype)
        else:
            @pl.when(pl.program_id(2) == 0)
            def _():
                acc_ref[...] = c_ref[...] if has_c else jnp.zeros_like(acc_ref)

            acc_ref[...] += dot()
            if own_acc:
                @pl.when(pl.program_id(2) == nk - 1)
                def _():
                    o_ref[...] = acc_ref[...].astype(o_ref.dtype)

    ins = [a, b] + ([c] if has_c else [])
    specs = [a_spec, b_spec] + ([o_spec] if has_c else [])
    return pl.pallas_call(
        body, name=name, grid=(M // tm, N // tn, nk), in_specs=specs, out_specs=o_spec,
        out_shape=jax.ShapeDtypeStruct((M, N), out_dtype), scratch_shapes=[pltpu.VMEM((tm, tn), F32)] if own_acc else [],
        compiler_params=_params(("parallel", "parallel", "arbitrary")),
    )(*ins)


def rowcall(name, fn, ins, in_specs, outs, out_specs, acc, nrow, ncol=1):
    n_in = len(ins)

    def body(*refs):
        i = pl.program_id(1)
        res = fn(i, *[r[...] for r in refs[:n_in]])
        for r, v, is_acc in zip(refs[n_in:], res, acc):
            if is_acc:
                @pl.when(i == 0)
                def _(r=r, v=v):
                    r[...] = v.astype(r.dtype)

                @pl.when(i > 0)
                def _(r=r, v=v):
                    r[...] += v.astype(r.dtype)
            else:
                r[...] = v.astype(r.dtype)

    return pl.pallas_call(
        body, name=name, grid=(ncol, nrow), in_specs=list(in_specs), out_specs=list(out_specs), out_shape=list(outs),
        compiler_params=_params(("parallel", "arbitrary")),
    )(*ins)


class Tiles:
    def __init__(self, T, tm):
        self.T, self.tm, self.n = T, tm, T // tm
        self.r8 = tm // SUBLANES

    def row(self, w, cb=0):
        return pl.BlockSpec((self.tm, w), lambda j, i: (i, cb))

    def rowj(self, tc):
        return pl.BlockSpec((self.tm, tc), lambda j, i: (i, j))

    def prev(self, w, cb=0):
        return pl.BlockSpec((SUBLANES, w), lambda j, i: (jnp.maximum(i * self.r8 - 1, 0), cb))

    def prevj(self, tc):
        return pl.BlockSpec((SUBLANES, tc), lambda j, i: (jnp.maximum(i * self.r8 - 1, 0), j))

    def nxt(self, w, cb=0):
        last = self.T // SUBLANES - 1
        return pl.BlockSpec((SUBLANES, w), lambda j, i: (jnp.minimum((i + 1) * self.r8, last), cb))

    def nxtj(self, tc):
        last = self.T // SUBLANES - 1
        return pl.BlockSpec((SUBLANES, tc), lambda j, i: (jnp.minimum((i + 1) * self.r8, last), j))

    def heads(self, H):
        return pl.BlockSpec((H, self.tm, HEAD_DIM), lambda j, i: (0, i, 0))

    def heads_nxt(self, H):
        last = self.T // SUBLANES - 1
        return pl.BlockSpec((H, SUBLANES, HEAD_DIM), lambda j, i: (0, jnp.minimum((i + 1) * self.r8, last), 0))


def full(shape):
    return pl.BlockSpec(tuple(shape), lambda j, i: (0,) * len(shape))


def constj(r, tc):
    return pl.BlockSpec((r, tc), lambda j, i: (0, j))


def sds(shape, dtype=F32):
    return jax.ShapeDtypeStruct(tuple(shape), dtype)


def rms(x, g):
    return x * lax.rsqrt(jnp.mean(x * x, axis=-1, keepdims=True) + EPS) * g


def silu(x):
    return x * jax.nn.sigmoid(x)


def gelu(x):
    return 0.5 * x * (1.0 + lax.erf(x * (2.0 ** -0.5)))


def causal_conv(xwin, w, K, R):
    base = SUBLANES - (K - 1)
    out = w[0:1, :] * xwin[base:base + R, :]
    for j in range(1, K):
        out = out + w[j:j + 1, :] * xwin[base + j:base + j + R, :]
    return out


def rows_to8(rows, C):
    rid = lax.broadcasted_iota(jnp.int32, (SUBLANES, C), 0)
    out = jnp.zeros((SUBLANES, C), F32)
    for k, r in enumerate(rows):
        out = out + jnp.where(rid == k, jnp.broadcast_to(r, (SUBLANES, C)), 0.0)
    return out


def dn_qkv(pre, H):
    a = silu(pre)
    W = H * HEAD_DIM

    def l2(t):
        return t * lax.rsqrt(jnp.sum(t * t, axis=-1, keepdims=True) + EPS)

    q = [l2(a[:, h * HEAD_DIM:(h + 1) * HEAD_DIM]) for h in range(H)]
    k = [l2(a[:, W + h * HEAD_DIM:W + (h + 1) * HEAD_DIM]) for h in range(H)]
    v = [a[:, 2 * W + h * HEAD_DIM:2 * W + (h + 1) * HEAD_DIM] for h in range(H)]
    return q, k, v


def dn_gates(ba, alog, dtb, H, R):
    lane = lax.broadcasted_iota(jnp.int32, (R, LANES), 1)
    beta = jax.nn.sigmoid(ba)
    g = -jnp.exp(alog) * jax.nn.softplus(ba + dtb)
    g = jnp.where((lane >= H) & (lane < 2 * H), g, 0.0)
    ri = lax.broadcasted_iota(jnp.int32, (R, R), 0)
    ci = lax.broadcasted_iota(jnp.int32, (R, R), 1)
    cum = jnp.where((ri // DN_CHUNK == ci // DN_CHUNK) & (ci <= ri), 1.0, 0.0).astype(F32)
    gc = mm_nn(cum, g, 2)
    return jnp.where(lane < H, beta, gc)


def neumann_inverse(Ls):
    C = Ls[0].shape[0]
    ri = lax.broadcasted_iota(jnp.int32, (C, C), 0)
    ci = lax.broadcasted_iota(jnp.int32, (C, C), 1)
    eye = jnp.where(ri == ci, 1.0, 0.0).astype(F32)
    P = [-L for L in Ls]
    R = [eye + p for p in P]
    for _ in range(int(math.log2(C)) - 1):
        P = [mm_nn(p, p, 1) for p in P]
        R = [r + mm_nn(r, p, 1) for r, p in zip(R, P)]
    return R


@jax.custom_vjp
def saved_inverse(L, T):
    return T


def _saved_inverse_f(L, T):
    return T, T


def _saved_inverse_b(T, g):
    return -mm_tn(T, mm_nt(g, T, 1), 1), jnp.zeros_like(T)


saved_inverse.defvjp(_saved_inverse_f, _saved_inverse_b)


def gate_columns(bg, H):
    bgT = bg.T
    return ([bg[:, h:h + 1] for h in range(H)], [bg[:, H + h:H + h + 1] for h in range(H)],
            [bgT[H + h:H + h + 1, :] for h in range(H)])


def dn_chunk(q, k, v, beta, gc, gr, tinv=None, with_inverse=False):
    n = len(q)
    C = q[0].shape[0]
    ri = lax.broadcasted_iota(jnp.int32, (C, C), 0)
    ci = lax.broadcasted_iota(jnp.int32, (C, C), 1)
    qs = [q[h] * (HEAD_DIM ** -0.5) for h in range(n)]
    kb = [k[h] * beta[h] for h in range(n)]
    vb = [v[h] * beta[h] for h in range(n)]
    decay = [jnp.exp(jnp.where(ri >= ci, gc[h] - gr[h], -jnp.inf)) for h in range(n)]
    L = [jnp.where(ri > ci, mm_nt(kb[h], k[h]) * decay[h], 0.0) for h in range(n)]
    attn = [jnp.where(ri >= ci, mm_nt(qs[h], k[h]) * decay[h], 0.0) for h in range(n)]
    Tinv = neumann_inverse(L) if tinv is None else [saved_inverse(L[h], tinv[h]) for h in range(n)]
    eg = [jnp.exp(gc[h]) for h in range(n)]
    u = [mm_nn(Tinv[h], vb[h]) for h in range(n)]
    w = [mm_nn(Tinv[h], kb[h] * eg[h]) for h in range(n)]
    qd = [qs[h] * eg[h] for h in range(n)]
    gl = [gc[h][C - 1:C, :] for h in range(n)]
    kd = [k[h] * jnp.exp(gl[h] - gc[h]) for h in range(n)]
    return (u, w, attn, qd, kd, gl, Tinv) if with_inverse else (u, w, attn, qd, kd, gl)


def dn_step(u, w, a, qd, kd, gl, S):
    n = len(u)
    v_new = [u[h] - mm_nn(w[h], S[h]) for h in range(n)]
    o = [mm_nn(qd[h], S[h]) + mm_nn(a[h], v_new[h]) for h in range(n)]
    S_new = [S[h] * jnp.exp(gl[h]) + mm_tn(kd[h], v_new[h]) for h in range(n)]
    return o, S_new


def dn_post(o, z, g):
    H = o.shape[0]
    return jnp.concatenate([rms(o[h], g) * silu(z[:, h * HEAD_DIM:(h + 1) * HEAD_DIM]) for h in range(H)], axis=1)


def gmlp(u_raw, v_raw, ln_g, ln_b, sgw, sgbT):
    R = u_raw.shape[0]
    G = sgw.shape[0]
    nc = R // SG_CHUNK
    u = gelu(u_raw)
    vv = gelu(v_raw)
    xc = vv - jnp.mean(vv, axis=-1, keepdims=True)
    vg = xc * lax.rsqrt(jnp.mean(xc * xc, axis=-1, keepdims=True) + EPS) * ln_g + ln_b
    ri = lax.broadcasted_iota(jnp.int32, (SG_CHUNK, SG_CHUNK), 0)
    ci = lax.broadcasted_iota(jnp.int32, (SG_CHUNK, SG_CHUNK), 1)
    cols = []
    for g in range(G):
        ws = jnp.where(ri >= ci, sgw[g], 0.0)
        rhs = jnp.concatenate([vg[c * SG_CHUNK:(c + 1) * SG_CHUNK, g * HEAD_DIM:(g + 1) * HEAD_DIM] for c in range(nc)], axis=1)
        mixed = mm_nn(ws, rhs) + sgbT[:, g:g + 1]
        cols.append(jnp.concatenate([mixed[:, c * HEAD_DIM:(c + 1) * HEAD_DIM] for c in range(nc)], axis=0))
    return u * jnp.concatenate(cols, axis=1)


def merge(ga, gb, ap, bp):
    return jax.nn.sigmoid(ga) * ap + jax.nn.sigmoid(gb) * bp


def norm_fwd(x, g, name, tm=256):
    T, D = x.shape
    tl = Tiles(T, _pick(T, tm))
    (h,) = rowcall(name, lambda i, x, g: (rms(x, g),), [x, g], [tl.row(D), full((1, D))],
                   [sds((T, D), BF16)], [tl.row(D)], [False], tl.n)
    return h


def norm_bwd(x, g, dh, dres, name, tm=256):
    T, D = x.shape
    tl = Tiles(T, _pick(T, tm))

    def fn(i, x, g, dh, dres):
        _, vj = jax.vjp(rms, x, g)
        dx, dg = vj(dh.astype(F32))
        return dx + dres, dg

    return rowcall(name, fn, [x, g, dh, dres], [tl.row(D), full((1, D)), tl.row(D), tl.row(D)],
                   [sds((T, D)), sds((1, D))], [tl.row(D), full((1, D))], [False, True], tl.n)


def head_fwd_bwd(x, g, tgt, name, tm=256):
    T, D = x.shape
    tl = Tiles(T, _pick(T, tm))

    def fn(i, x, g, tgt):
        y, vj = jax.vjp(rms, x, g)
        e = y - tgt
        loss = 0.5 * jnp.sum(jnp.mean(e * e, axis=-1, keepdims=True), axis=0, keepdims=True)
        dx, dg = vj(e * (1.0 / D))
        return loss, dx, dg

    return rowcall(name, fn, [x, g, tgt], [tl.row(D), full((1, D)), tl.row(D)],
                   [sds((1, 1)), sds((T, D)), sds((1, D))], [full((1, 1)), tl.row(D), full((1, D))],
                   [True, False, True], tl.n)


def dn_prep_fwd(projA, pba, cw8, alog, dtb, H, name, tm=256):
    T = projA.shape[0]
    W3 = 3 * H * HEAD_DIM
    tl = Tiles(T, _pick(T, tm, DN_CHUNK))
    R = tl.tm

    def fn(i, xp, x, ba, cw, alog, dtb):
        xwin = jnp.concatenate([jnp.where(i > 0, xp, 0.0), x], axis=0)
        q, k, v = dn_qkv(causal_conv(xwin, cw, 4, R), H)
        return jnp.stack(q), jnp.stack(k), jnp.stack(v), dn_gates(ba, alog, dtb, H, R)

    hs = sds((H, T, HEAD_DIM))
    return rowcall(name, fn, [projA, projA, pba, cw8, alog, dtb],
                   [tl.prev(W3), tl.row(W3), tl.row(LANES), full((SUBLANES, W3)), full((1, LANES)), full((1, LANES))],
                   [hs, hs, hs, sds((T, LANES))], [tl.heads(H)] * 3 + [tl.row(LANES)], [False] * 4, tl.n)


def dn_prep_bwd(projA, pba, cw8, alog, dtb, dq, dk, dv, dbg, H, name, tm=256):
    T = projA.shape[0]
    W3 = 3 * H * HEAD_DIM
    tl = Tiles(T, _pick(T, tm, DN_CHUNK))
    R = tl.tm
    RE = R + SUBLANES

    def fn(i, xp, x, xn, ba, cw, alog, dtb, dq, dk, dv, dqn, dkn, dvn, dbg):
        last = i == tl.n - 1
        xwin = jnp.concatenate([jnp.where(i > 0, xp, 0.0), x, jnp.where(last, 0.0, xn)], axis=0)
        pre = causal_conv(xwin, cw, 4, RE)
        ext = lambda d, dn: [jnp.concatenate([d[h], jnp.where(last, 0.0, dn[h])], axis=0) for h in range(H)]
        _, vj = jax.vjp(lambda p: dn_qkv(p, H), pre)
        (dpre,) = vj((ext(dq, dqn), ext(dk, dkn), ext(dv, dvn)))
        dx = cw[0:1, :] * dpre[3:3 + R, :]
        for j in range(1, 4):
            dx = dx + cw[j:j + 1, :] * dpre[3 - j:3 - j + R, :]
        dcw = rows_to8([jnp.sum(dpre[0:R, :] * xwin[5 + j:5 + j + R, :], axis=0, keepdims=True) for j in range(4)], W3)
        _, vjg = jax.vjp(lambda ba, alog, dtb: dn_gates(ba, alog, dtb, H, R), ba, alog, dtb)
        dba, dalog, ddtb = vjg(dbg)
        return dx, dba, dcw, dalog, ddtb

    return rowcall(name, fn, [projA, projA, projA, pba, cw8, alog, dtb, dq, dk, dv, dq, dk, dv, dbg],
                   [tl.prev(W3), tl.row(W3), tl.nxt(W3), tl.row(LANES), full((SUBLANES, W3)), full((1, LANES)), full((1, LANES))]
                   + [tl.heads(H)] * 3 + [tl.heads_nxt(H)] * 3 + [tl.row(LANES)],
                   [sds((T, W3), BF16), sds((T, LANES), BF16), sds((SUBLANES, W3)), sds((1, LANES)), sds((1, LANES))],
                   [tl.row(W3), tl.row(LANES), full((SUBLANES, W3)), full((1, LANES)), full((1, LANES))],
                   [False, False, True, True, True], tl.n)


def _chunk_specs(H, C):
    hs = pl.BlockSpec((H, C, HEAD_DIM), lambda n: (0, n, 0))
    col = pl.BlockSpec((H, 1, C, 1), lambda n: (0, n, 0, 0))
    rw = pl.BlockSpec((H, 1, 1, C), lambda n: (0, n, 0, 0))
    at = pl.BlockSpec((H, C, C), lambda n: (0, n, 0))
    one = pl.BlockSpec((H, 1, 1, 1), lambda n: (0, n, 0, 0))
    return hs, col, rw, at, one


def dn_chunk_fwd(q, k, v, bg, name):
    H, T, _ = q.shape
    C = DN_CHUNK
    N = T // C
    hs, _, _, at, one = _chunk_specs(H, C)
    gate = pl.BlockSpec((C, LANES), lambda n: (n, 0))

    def body(q, k, v, bg, u, w, a, qd, kd, gl, ti):
        hd = range(H)
        res = dn_chunk([q[h] for h in hd], [k[h] for h in hd], [v[h] for h in hd], *gate_columns(bg[...], H),
                       with_inverse=True)
        for h in hd:
            for ref, val in zip((u, w, a, qd, kd), res[:5]):
                ref[h] = val[h]
            gl[h, 0] = res[5][h]
            ti[h] = res[6][h]

    big = sds((H, T, HEAD_DIM))
    return pl.pallas_call(
        body, name=name, grid=(N,), in_specs=[hs, hs, hs, gate], out_specs=[hs, hs, at, hs, hs, one, at],
        out_shape=[big, big, sds((H, T, C)), big, big, sds((H, N, 1, 1)), sds((H, T, C))],
        compiler_params=_params(("parallel",)),
    )(q, k, v, bg)


def dn_chunk_bwd(q, k, v, bg, tinv, du, dw, da, dqd, dkd, dgl, name):
    H, T, _ = q.shape
    C = DN_CHUNK
    N = T // C
    hs, _, _, at, one = _chunk_specs(H, C)
    gate = pl.BlockSpec((C, LANES), lambda n: (n, 0))

    def body(q, k, v, bg, ti, du, dw, da, dqd, dkd, dgl, dq, dk, dv, dbg):
        hd = range(H)
        f = lambda q, k, v, b, gc, gr: dn_chunk(q, k, v, b, gc, gr, tinv=[ti[h] for h in hd])
        _, vj = jax.vjp(f, [q[h] for h in hd], [k[h] for h in hd], [v[h] for h in hd], *gate_columns(bg[...], H))
        res = vj(([du[h] for h in hd], [dw[h] for h in hd], [da[h] for h in hd], [dqd[h] for h in hd],
                  [dkd[h] for h in hd], [dgl[h, 0] for h in hd]))
        lane = lax.broadcasted_iota(jnp.int32, (C, LANES), 1)
        row = lax.broadcasted_iota(jnp.int32, (LANES, C), 0)
        cols = jnp.zeros((C, LANES), F32)
        rows = jnp.zeros((LANES, C), F32)
        for h in hd:
            dq[h], dk[h], dv[h] = res[0][h], res[1][h], res[2][h]
            cols = cols + jnp.where(lane == h, res[3][h], 0.0) + jnp.where(lane == H + h, res[4][h], 0.0)
            rows = rows + jnp.where(row == H + h, res[5][h], 0.0)
        dbg[...] = cols + rows.T

    big = sds((H, T, HEAD_DIM))
    return pl.pallas_call(
        body, name=name, grid=(N,), in_specs=[hs, hs, hs, gate, at, hs, hs, at, hs, hs, one],
        out_specs=[hs, hs, hs, gate], out_shape=[big, big, big, sds((T, LANES))], compiler_params=_params(("parallel",)),
    )(q, k, v, bg, tinv, du, dw, da, dqd, dkd, dgl)


def dn_scan_fwd(u, w, a, qd, kd, gl, name):
    H, T, _ = u.shape
    C = DN_CHUNK
    N = T // C
    hs, _, _, at, one = _chunk_specs(H, C)
    st = pl.BlockSpec((1, H, HEAD_DIM, HEAD_DIM), lambda n: (n, 0, 0, 0))

    def body(u, w, a, qd, kd, gl, o, s_in, S):
        @pl.when(pl.program_id(0) == 0)
        def _():
            S[...] = jnp.zeros_like(S)

        hd = range(H)
        s = [S[h] for h in hd]
        o_new, s_new = dn_step([u[h] for h in hd], [w[h] for h in hd], [a[h] for h in hd], [qd[h] for h in hd],
                               [kd[h] for h in hd], [gl[h, 0] for h in hd], s)
        for h in hd:
            s_in[0, h] = s[h]
            o[h] = o_new[h]
            S[h] = s_new[h]

    return pl.pallas_call(
        body, name=name, grid=(N,), in_specs=[hs, hs, at, hs, hs, one], out_specs=[hs, st],
        out_shape=[sds((H, T, HEAD_DIM)), sds((N, H, HEAD_DIM, HEAD_DIM))],
        scratch_shapes=[pltpu.VMEM((H, HEAD_DIM, HEAD_DIM), F32)], compiler_params=_params(("arbitrary",)),
    )(u, w, a, qd, kd, gl)


def dn_scan_bwd(u, w, a, qd, kd, gl, s_in, do, name):
    H, T, _ = u.shape
    C = DN_CHUNK
    N = T // C
    rev = lambda spec_shape, f: pl.BlockSpec(spec_shape, f)
    hs = rev((H, C, HEAD_DIM), lambda n: (0, N - 1 - n, 0))
    at = rev((H, C, C), lambda n: (0, N - 1 - n, 0))
    one = rev((H, 1, 1, 1), lambda n: (0, N - 1 - n, 0, 0))
    st = rev((1, H, HEAD_DIM, HEAD_DIM), lambda n: (N - 1 - n, 0, 0, 0))

    def body(u, w, a, qd, kd, gl, s_in, do, du, dw, da, dqd, dkd, dgl, dS):
        @pl.when(pl.program_id(0) == 0)
        def _():
            dS[...] = jnp.zeros_like(dS)

        hd = range(H)
        _, vj = jax.vjp(dn_step, [u[h] for h in hd], [w[h] for h in hd], [a[h] for h in hd], [qd[h] for h in hd],
                        [kd[h] for h in hd], [gl[h, 0] for h in hd], [s_in[0, h] for h in hd])
        res = vj(([do[h] for h in hd], [dS[h] for h in hd]))
        for h in hd:
            du[h], dw[h], da[h], dqd[h], dkd[h] = (res[j][h] for j in range(5))
            dgl[h, 0] = res[5][h]
            dS[h] = res[6][h]

    big = sds((H, T, HEAD_DIM))
    return pl.pallas_call(
        body, name=name, grid=(N,), in_specs=[hs, hs, at, hs, hs, one, st, hs], out_specs=[hs, hs, at, hs, hs, one],
        out_shape=[big, big, sds((H, T, C)), big, big, sds((H, N, 1, 1))],
        scratch_shapes=[pltpu.VMEM((H, HEAD_DIM, HEAD_DIM), F32)], compiler_params=_params(("arbitrary",)),
    )(u, w, a, qd, kd, gl, s_in, do)


def dn_post_fwd(o, projA, g, name, tm=256):
    H, T, _ = o.shape
    W = H * HEAD_DIM
    tl = Tiles(T, _pick(T, tm))
    (y,) = rowcall(name, lambda i, o, z, g: (dn_post(o, z, g),), [o, projA, g], [tl.heads(H), tl.row(W, 3), full((1, HEAD_DIM))],
                   [sds((T, W), BF16)], [tl.row(W)], [False], tl.n)
    return y


def dn_post_bwd(o, projA, g, dy, name, tm=256):
    H, T, _ = o.shape
    W = H * HEAD_DIM
    tl = Tiles(T, _pick(T, tm))

    def fn(i, o, z, g, dy):
        _, vj = jax.vjp(dn_post, o, z, g)
        return vj(dy.astype(F32))

    return rowcall(name, fn, [o, projA, g, dy], [tl.heads(H), tl.row(W, 3), full((1, HEAD_DIM)), tl.row(W)],
                   [sds((H, T, HEAD_DIM)), sds((T, W), BF16), sds((1, HEAD_DIM))],
                   [tl.heads(H), tl.row(W), full((1, HEAD_DIM))], [False, False, True], tl.n)


def gmlp_fwd(projB, ln_g, ln_b, sgw, sgbT, name, tm=512):
    T = projB.shape[0]
    G = sgw.shape[0]
    W = G * HEAD_DIM
    tl = Tiles(T, _pick(T, tm))
    (y,) = rowcall(name, lambda i, *a: (gmlp(*a),), [projB, projB, ln_g, ln_b, sgw, sgbT],
                   [tl.row(W, 0), tl.row(W, 1), full((1, W)), full((1, W)), full(sgw.shape), full(sgbT.shape)],
                   [sds((T, W), BF16)], [tl.row(W)], [False], tl.n)
    return y


def gmlp_bwd(projB, ln_g, ln_b, sgw, sgbT, dy, name, tm=512):
    T = projB.shape[0]
    G = sgw.shape[0]
    W = G * HEAD_DIM
    tl = Tiles(T, _pick(T, tm))

    def fn(i, u_raw, v_raw, ln_g, ln_b, sgw, sgbT, dy):
        _, vj = jax.vjp(gmlp, u_raw, v_raw, ln_g, ln_b, sgw, sgbT)
        return vj(dy.astype(F32))

    return rowcall(name, fn, [projB, projB, ln_g, ln_b, sgw, sgbT, dy],
                   [tl.row(W, 0), tl.row(W, 1), full((1, W)), full((1, W)), full(sgw.shape), full(sgbT.shape), tl.row(W)],
                   [sds((T, W), BF16), sds((T, W), BF16), sds((1, W)), sds((1, W)), sds(sgw.shape), sds(sgbT.shape)],
                   [tl.row(W), tl.row(W), full((1, W)), full((1, W)), full(sgw.shape), full(sgbT.shape)],
                   [False, False, True, True, True, True], tl.n)


def merge_fwd(projB, ap, bp, cb_a, name, tm=256):
    T, D = ap.shape
    tl = Tiles(T, _pick(T, tm))
    (m,) = rowcall(name, lambda i, *a: (merge(*a),), [projB, projB, ap, bp],
                   [tl.row(D, cb_a), tl.row(D, cb_a + 1), tl.row(D), tl.row(D)], [sds((T, D), BF16)], [tl.row(D)], [False], tl.n)
    return m


def merge_bwd(projB, ap, bp, dm, cb_a, name, tm=256):
    T, D = ap.shape
    tl = Tiles(T, _pick(T, tm))

    def fn(i, ga, gb, ap, bp, dm):
        _, vj = jax.vjp(merge, ga, gb, ap, bp)
        return vj(dm.astype(F32))

    return rowcall(name, fn, [projB, projB, ap, bp, dm], [tl.row(D, cb_a), tl.row(D, cb_a + 1), tl.row(D), tl.row(D), tl.row(D)],
                   [sds((T, D), BF16)] * 4, [tl.row(D)] * 4, [False] * 4, tl.n)


def ffn_act_fwd(gp, up, fcw8, fcb, name, tm=256, tc=512):
    T, F = gp.shape
    tl = Tiles(T, _pick(T, tm))
    tc = _pick(F, tc)
    R = tl.tm

    def fn(i, gprev, g, up, cw, cb):
        xwin = jnp.concatenate([jnp.where(i > 0, gprev, 0.0), g], axis=0)
        return (silu(causal_conv(xwin, cw, 3, R) + cb) * up,)

    (act,) = rowcall(name, fn, [gp, gp, up, fcw8, fcb], [tl.prevj(tc), tl.rowj(tc), tl.rowj(tc), constj(SUBLANES, tc), constj(1, tc)],
                     [sds((T, F), BF16)], [tl.rowj(tc)], [False], tl.n, F // tc)
    return act


def ffn_act_bwd(gp, up, fcw8, fcb, dact, name, tm=256, tc=512):
    T, F = gp.shape
    tl = Tiles(T, _pick(T, tm))
    tc = _pick(F, tc)
    R = tl.tm
    RE = R + SUBLANES

    def fn(i, gprev, g, gnext, up, upn, da, dan, cw, cb):
        last = i == tl.n - 1
        xwin = jnp.concatenate([jnp.where(i > 0, gprev, 0.0), g, jnp.where(last, 0.0, gnext)], axis=0)
        gate = causal_conv(xwin, cw, 3, RE) + cb
        upe = jnp.concatenate([up, upn], axis=0)
        dae = jnp.concatenate([da, jnp.where(last, 0.0, dan)], axis=0)
        s = jax.nn.sigmoid(gate)
        dgate = dae * upe * (s * (1.0 + gate * (1.0 - s)))
        dup = da * (gate[0:R, :] * s[0:R, :])
        dgp = cw[0:1, :] * dgate[2:2 + R, :] + cw[1:2, :] * dgate[1:1 + R, :] + cw[2:3, :] * dgate[0:R, :]
        dcw = rows_to8([jnp.sum(dgate[0:R, :] * xwin[6 + j:6 + j + R, :], axis=0, keepdims=True) for j in range(3)], tc)
        dcb = jnp.sum(dgate[0:R, :], axis=0, keepdims=True)
        return dgp, dup, dcw, dcb

    return rowcall(name, fn, [gp, gp, gp, up, up, dact, dact, fcw8, fcb],
                   [tl.prevj(tc), tl.rowj(tc), tl.nxtj(tc), tl.rowj(tc), tl.nxtj(tc), tl.rowj(tc), tl.nxtj(tc),
                    constj(SUBLANES, tc), constj(1, tc)],
                   [sds((T, F), BF16), sds((T, F), BF16), sds((SUBLANES, F)), sds((1, F))],
                   [tl.rowj(tc), tl.rowj(tc), constj(SUBLANES, tc), constj(1, tc)], [False, False, True, True], tl.n, F // tc)


def _me():
    return lax.axis_index("x"), lax.axis_index("y"), lax.axis_index("c")


def all_gather(shards, name):
    nt = len(shards)

    def body(*refs):
        xs, outs = refs[:nt], refs[nt:2 * nt]
        send_sems, recv_sems, local_sems = refs[2 * nt:]
        x, y, c = _me()
        me, sibling = (x, y, c), (x, y, 1 - c)
        chips = [(1 - x, y), (x, 1 - y), (1 - x, 1 - y)]

        def slot(t, p):
            return outs[t].at[4 * p[0] + 2 * p[1] + p[2]]

        def copy(t, k, block, to, src=None):
            return pltpu.make_async_remote_copy(
                src_ref=slot(t, block) if src is None else src, dst_ref=slot(t, block),
                send_sem=send_sems.at[t, k], recv_sem=recv_sems.at[t, k], device_id=to, device_id_type=MESH)

        mine = [pltpu.make_async_copy(xs[t], slot(t, me), local_sems.at[t]) for t in range(nt)]
        first = []
        for t in range(nt):
            mine[t].start()
            first.append(copy(t, 0, me, sibling, src=xs[t]))
            first += [copy(t, 1 + j, me, (*chip, c), src=xs[t]) for j, chip in enumerate(chips)]
        for cp in first:
            cp.start()
        passed = []
        for j, chip in enumerate(chips):
            for t in range(nt):
                copy(t, 1 + j, (*chip, c), me).wait_recv()
                cp = copy(t, 4 + j, (*chip, c), sibling)
                cp.start()
                passed.append(cp)
        for t in range(nt):
            copy(t, 0, sibling, me).wait_recv()
            for j, chip in enumerate(chips):
                copy(t, 4 + j, (*chip, 1 - c), me).wait_recv()
        for cp in first + passed:
            cp.wait_send()
        for t in range(nt):
            mine[t].wait()

    any_spec = pl.BlockSpec(memory_space=pl.ANY)
    return pl.pallas_call(
        body, name=name, in_specs=[any_spec] * nt, out_specs=[any_spec] * nt,
        out_shape=[jax.ShapeDtypeStruct((N_DEV,) + s.shape, s.dtype) for s in shards],
        scratch_shapes=[pltpu.SemaphoreType.DMA((nt, 7)), pltpu.SemaphoreType.DMA((nt, 7)), pltpu.SemaphoreType.DMA((nt,))],
    )(*shards)


_HBM = pl.BlockSpec(memory_space=pltpu.HBM)
_SEM = pl.BlockSpec(memory_space=pltpu.SEMAPHORE)
_ANY = pl.BlockSpec(memory_space=pl.ANY)
_DATAFLOW = pltpu.SideEffectType.DATAFLOW_SIDE_EFFECTING


def _peers():
    x, y, c = _me()
    out = []
    for k in range(1, N_DEV):
        p = (x ^ (k >> 2), y ^ ((k >> 1) & 1), c ^ (k & 1))
        out.append((k, p, 4 * p[0] + 2 * p[1] + p[2]))
    return out


def _split_copy(src, land, send_sems, recv_sems, t, k, peer, slot, my, scatter, receiving):
    return pltpu.make_async_remote_copy(
        src_ref=src.at[slot] if scatter else src, dst_ref=land.at[slot if receiving else my],
        send_sem=send_sems.at[t * (N_DEV - 1) + k - 1], recv_sem=recv_sems.at[t * (N_DEV - 1) + k - 1],
        device_id=peer, device_id_type=MESH)


def comm_start(groups, scatter, name, after=None):
    flat = [a for g in groups for a in g]
    nt = len(flat)
    lands = [lax.empty(a.shape if scatter else (N_DEV,) + a.shape, a.dtype) for a in flat]
    ng = len(groups)
    n_after = 0 if after is None else 1

    def body(*refs):
        src, land = refs[:nt], refs[nt:2 * nt]
        sems = refs[2 * nt + n_after:2 * nt + n_after + 2 * ng]
        token = refs[-1]
        x, y, c = _me()
        my = 4 * x + 2 * y + c
        t0 = 0
        for gi, g in enumerate(groups):
            for k, peer, slot in _peers():
                for t in range(len(g)):
                    _split_copy(src[t0 + t], land[t0 + t], sems[2 * gi], sems[2 * gi + 1], t, k, peer, slot, my, scatter,
                                False).start()
            t0 += len(g)
        token[...] = jnp.zeros_like(token)

    sem_shapes = []
    for g in groups:
        sem_shapes += [pltpu.SemaphoreType.DMA((len(g) * (N_DEV - 1),))] * 2
    res = pl.pallas_call(
        body, name=name, in_specs=[_HBM] * (2 * nt) + [_HBM] * n_after,
        out_specs=[_SEM] * (2 * ng) + [_HBM] * (2 * nt) + [pl.BlockSpec(memory_space=pltpu.VMEM)],
        out_shape=sem_shapes + [pltpu.HBM(a.shape, a.dtype) for a in flat + lands] + [sds((SUBLANES, LANES))],
        input_output_aliases={i: 2 * ng + i for i in range(2 * nt)},
        compiler_params=pltpu.CompilerParams(has_side_effects=_DATAFLOW),
    )(*[pltpu.with_memory_space_constraint(a, pltpu.HBM) for a in flat + lands + ([] if after is None else [after])])
    handles = []
    t0 = 0
    for gi, g in enumerate(groups):
        n = len(g)
        handles.append(dict(sems=(res[2 * gi], res[2 * gi + 1]), src=res[2 * ng + t0:2 * ng + t0 + n],
                            land=res[2 * ng + nt + t0:2 * ng + nt + t0 + n], scatter=scatter))
        t0 += n
    return handles, res[-1]


def comm_wait(handle, after, name):
    src, land, scatter = handle["src"], handle["land"], handle["scatter"]
    nt = len(src)

    def body(*refs):
        src_r, land_r = refs[:nt], refs[nt:2 * nt]
        send_sems, recv_sems = refs[2 * nt], refs[2 * nt + 1]
        x, y, c = _me()
        my = 4 * x + 2 * y + c
        for k, peer, slot in _peers():
            for t in range(nt):
                _split_copy(src_r[t], land_r[t], send_sems, recv_sems, t, k, peer, slot, my, scatter, False).wait_send()
                _split_copy(src_r[t], land_r[t], send_sems, recv_sems, t, k, peer, slot, my, scatter, True).wait_recv()

    after = list(after) if isinstance(after, (list, tuple)) else [after]
    res = pl.pallas_call(
        body, name=name, in_specs=[_HBM] * (2 * nt) + [_SEM, _SEM] + [_HBM] * len(after), out_specs=[_HBM] * (2 * nt),
        out_shape=[pltpu.HBM(a.shape, a.dtype) for a in list(src) + list(land)],
        input_output_aliases={i: i for i in range(2 * nt)},
        compiler_params=pltpu.CompilerParams(has_side_effects=_DATAFLOW),
    )(*src, *land, *handle["sems"], *[pltpu.with_memory_space_constraint(a, pltpu.HBM) for a in after])
    return res[:nt], res[nt:]


def sum_adamw_shard(own_src, land, me, w, m, v, l, prev, name, tr=256):
    L, R, C = w.shape
    by_rows = R % SUBLANES == 0 or C % LANES != 0
    tr, tc = (_pick(R, tr, SUBLANES), C) if by_rows else (R, _pick(C, 256))
    steps = R // tr if by_rows else C // tc
    c1 = 1.0 - ADAM_B1 ** ADAM_STEP
    c2 = 1.0 - ADAM_B2 ** ADAM_STEP
    n_prev = 0 if prev is None else 4

    def at(lead, i):
        return (lead, i, 0) if by_rows else (lead, 0, i)

    def body(me_ref, *refs):
        parts = refs[:N_DEV]
        w_r, m_r, v_r = refs[N_DEV:N_DEV + 3]
        g_o, d_o, m_o, v_o = refs[N_DEV + 3 + n_prev:]
        g = parts[0][0].astype(F32)
        for k in range(1, N_DEV):
            g = g + parts[k][0].astype(F32)
        mn = ADAM_B1 * m_r[0] + (1.0 - ADAM_B1) * g
        vn = ADAM_B2 * v_r[0] + (1.0 - ADAM_B2) * (g * g)
        g_o[0] = g
        d_o[0] = -ADAM_LR * ((mn / c1) / (jnp.sqrt(vn / c2) + ADAM_EPS) + ADAM_WD * w_r[0])
        m_o[0] = mn
        v_o[0] = vn

    part_specs = [pl.BlockSpec((1, tr, tc), lambda i, me, k=k: at(me[0] ^ k, i)) for k in range(N_DEV)]
    lay = pl.BlockSpec((1, tr, tc), lambda i, me: at(l, i))
    grid_spec = pltpu.PrefetchScalarGridSpec(
        num_scalar_prefetch=1, grid=(steps,), in_specs=part_specs + [lay] * 3 + [_ANY] * n_prev, out_specs=[lay] * 4)
    return pl.pallas_call(
        body, name=name, grid_spec=grid_spec, out_shape=[sds((L, R, C))] * 4,
        input_output_aliases={1 + N_DEV + 3 + j: j for j in range(n_prev)}, compiler_params=_params(("parallel",)),
    )(me, own_src, *[land] * (N_DEV - 1), w, m, v, *([] if prev is None else prev))


def sum_adamw(parts, w, m, v, name, tr=256):
    _, R, C = parts.shape
    tr = _pick(R, tr, SUBLANES)
    c1 = 1.0 - ADAM_B1 ** ADAM_STEP
    c2 = 1.0 - ADAM_B2 ** ADAM_STEP

    def body(p, w, m, v, g_o, d_o, m_o, v_o):
        g = p[0].astype(F32)
        for d in range(1, N_DEV):
            g = g + p[d].astype(F32)
        mn = ADAM_B1 * m[...] + (1.0 - ADAM_B1) * g
        vn = ADAM_B2 * v[...] + (1.0 - ADAM_B2) * (g * g)
        m_hat = mn / c1
        v_hat = vn / c2
        g_o[...] = g
        d_o[...] = -ADAM_LR * (m_hat / (jnp.sqrt(v_hat) + ADAM_EPS) + ADAM_WD * w[...])
        m_o[...] = mn
        v_o[...] = vn

    blk = pl.BlockSpec((tr, C), lambda i: (i, 0))
    return pl.pallas_call(
        body, name=name, grid=(R // tr,), in_specs=[pl.BlockSpec((N_DEV, tr, C), lambda i: (0, i, 0)), blk, blk, blk],
        out_specs=[blk] * 4, out_shape=[sds((R, C))] * 4, compiler_params=_params(("parallel",)),
    )(parts, w, m, v)


SHARDED = ("w_in", "dn_conv_w", "w_branch_a", "w_branch_b", "w_out", "ffn_w_gate", "ffn_w_up", "ffn_conv_w", "ffn_w_down")
TRANSPOSED = ("w_in", "ffn_w_gate", "ffn_w_up")
COL_SHARDED = ("dn_conv_w", "w_branch_a", "w_branch_b", "ffn_conv_w")
CONV_WEIGHTS = ("dn_conv_w", "ffn_conv_w")
REPLICATED = ("norm1_g", "dn_a_log", "dn_dt_bias", "dn_onorm_g", "sg_ln_g", "sg_ln_b", "sg_w", "sg_b", "norm2_g",
              "ffn_conv_b", "final_norm_g")
WEIGHTS = ("norm1_g", "w_in", "dn_conv_w", "dn_a_log", "dn_dt_bias", "dn_onorm_g", "sg_ln_g", "sg_ln_b", "sg_w", "sg_b",
           "w_branch_a", "w_branch_b", "w_out", "norm2_g", "ffn_w_gate", "ffn_w_up", "ffn_conv_w", "ffn_conv_b",
           "ffn_w_down", "final_norm_g")


def _columns(pieces, lo, hi):
    out = []
    for a, start, width in pieces:
        s, e = max(lo, start), min(hi, start + width)
        if s < e:
            out.append(a[:, s - start:e - start])
    return out[0] if len(out) == 1 else jnp.concatenate(out, axis=1)


def _assemble(name, g):
    if name in COL_SHARDED:
        return jnp.concatenate([g[d] for d in range(N_DEV)], axis=1)
    return g.reshape(N_DEV * g.shape[1], g.shape[2])


def _split(name, pieces, dtype):
    total = sum(w for _, _, w in pieces)
    if name in COL_SHARDED:
        cs = total // N_DEV
        return jnp.stack([_columns(pieces, d * cs, (d + 1) * cs).astype(dtype) for d in range(N_DEV)])
    a = pieces[0][0] if len(pieces) == 1 else jnp.concatenate([p[:w] for p, _, w in pieces], axis=0)
    return a.reshape(N_DEV, a.shape[0] // N_DEV, a.shape[1]).astype(dtype)


def _pad_lanes(a, lo, width=LANES):
    return jnp.pad(a, ((0, 0), (lo, width - lo - a.shape[1])))


def _pad_rows(a, rows=SUBLANES):
    return jnp.pad(a, ((0, rows - a.shape[0]), (0, 0)))


def kernel(x, norm1_g, w_in, dn_conv_w, dn_a_log, dn_dt_bias, dn_onorm_g, sg_ln_g, sg_ln_b, sg_w, sg_b, w_branch_a, w_branch_b, w_out, norm2_g, ffn_w_gate, ffn_w_up, ffn_conv_w, ffn_conv_b, ffn_w_down, final_norm_g, loss_target, m_norm1_g, m_w_in, m_dn_conv_w, m_dn_a_log, m_dn_dt_bias, m_dn_onorm_g, m_sg_ln_g, m_sg_ln_b, m_sg_w, m_sg_b, m_w_branch_a, m_w_branch_b, m_w_out, m_norm2_g, m_ffn_w_gate, m_ffn_w_up, m_ffn_conv_w, m_ffn_conv_b, m_ffn_w_down, m_final_norm_g, v_norm1_g, v_w_in, v_dn_conv_w, v_dn_a_log, v_dn_dt_bias, v_dn_onorm_g, v_sg_ln_g, v_sg_ln_b, v_sg_w, v_sg_b, v_w_branch_a, v_w_branch_b, v_w_out, v_norm2_g, v_ffn_w_gate, v_ffn_w_up, v_ffn_conv_w, v_ffn_conv_b, v_ffn_w_down, v_final_norm_g):
    W = dict(norm1_g=norm1_g, w_in=w_in, dn_conv_w=dn_conv_w, dn_a_log=dn_a_log, dn_dt_bias=dn_dt_bias, dn_onorm_g=dn_onorm_g,
             sg_ln_g=sg_ln_g, sg_ln_b=sg_ln_b, sg_w=sg_w, sg_b=sg_b, w_branch_a=w_branch_a, w_branch_b=w_branch_b, w_out=w_out,
             norm2_g=norm2_g, ffn_w_gate=ffn_w_gate, ffn_w_up=ffn_w_up, ffn_conv_w=ffn_conv_w, ffn_conv_b=ffn_conv_b,
             ffn_w_down=ffn_w_down, final_norm_g=final_norm_g)
    Mo = dict(norm1_g=m_norm1_g, w_in=m_w_in, dn_conv_w=m_dn_conv_w, dn_a_log=m_dn_a_log, dn_dt_bias=m_dn_dt_bias,
              dn_onorm_g=m_dn_onorm_g, sg_ln_g=m_sg_ln_g, sg_ln_b=m_sg_ln_b, sg_w=m_sg_w, sg_b=m_sg_b, w_branch_a=m_w_branch_a,
              w_branch_b=m_w_branch_b, w_out=m_w_out, norm2_g=m_norm2_g, ffn_w_gate=m_ffn_w_gate, ffn_w_up=m_ffn_w_up,
              ffn_conv_w=m_ffn_conv_w, ffn_conv_b=m_ffn_conv_b, ffn_w_down=m_ffn_w_down, final_norm_g=m_final_norm_g)
    Vo = dict(norm1_g=v_norm1_g, w_in=v_w_in, dn_conv_w=v_dn_conv_w, dn_a_log=v_dn_a_log, dn_dt_bias=v_dn_dt_bias,
              dn_onorm_g=v_dn_onorm_g, sg_ln_g=v_sg_ln_g, sg_ln_b=v_sg_ln_b, sg_w=v_sg_w, sg_b=v_sg_b, w_branch_a=v_w_branch_a,
              w_branch_b=v_w_branch_b, w_out=v_w_out, norm2_g=v_norm2_g, ffn_w_gate=v_ffn_w_gate, ffn_w_up=v_ffn_w_up,
              ffn_conv_w=v_ffn_conv_w, ffn_conv_b=v_ffn_conv_b, ffn_w_down=v_ffn_w_down, final_norm_g=v_final_norm_g)

    xs = x[0]
    tgt = loss_target[0]
    T, D = xs.shape
    depth = norm1_g.shape[0]
    H = dn_a_log.shape[1]
    G = sg_w.shape[1]
    WA = H * HEAD_DIM
    WB = G * HEAD_DIM
    N = T // DN_CHUNK
    colA = 4 * WA
    colB0 = colA + 2 * H
    cb_a = (2 * WB) // D

    my = 4 * lax.axis_index("x") + 2 * lax.axis_index("y") + lax.axis_index("c")
    me_arr = my.astype(jnp.int32).reshape(1)

    def view(d):
        return {n: (jnp.transpose(d[n], (0, 2, 1)) if n in TRANSPOSED else d[n]) for n in SHARDED}

    Wv, Mv, Vv = view(W), view(Mo), view(Vo)

    def shard(n, l):
        return Wv[n][l] if n in CONV_WEIGHTS else Wv[n][l].astype(BF16)

    first = [("w_in", 0), ("dn_conv_w", 0)]
    first_blocks = all_gather([shard(n, l) for n, l in first], "gather_first")
    gathered = dict(zip(first, first_blocks))
    gather_names = [[("w_branch_a", 0), ("w_branch_b", 0), ("w_out", 0)]]
    for l in range(depth):
        if l > 0:
            gather_names.append([("w_in", l), ("dn_conv_w", l), ("w_branch_a", l), ("w_branch_b", l), ("w_out", l)])
        gather_names.append([("ffn_w_gate", l), ("ffn_w_up", l), ("ffn_conv_w", l), ("ffn_w_down", l)])
    gather_handles, gather_tok = comm_start([[shard(n, l) for n, l in g] for g in gather_names], False, "gather_start",
                                            after=first_blocks[0])

    def need(n, l, after):
        if (n, l) not in gathered:
            gi = [i for i, g in enumerate(gather_names) if (n, l) in g][0]
            src, land = comm_wait(gather_handles[gi], after, f"gather_wait{gi}")
            for key, s, ld in zip(gather_names[gi], src, land):
                gathered[key] = lax.dynamic_update_index_in_dim(ld, s, my, 0)
        return gathered[(n, l)]

    def full(n, l, after):
        return _assemble(n, need(n, l, after))

    def layer_weights(l):
        return dict(
            g1=norm1_g[l][None], g2=norm2_g[l][None], alog=_pad_lanes(dn_a_log[l][None], H), dtb=_pad_lanes(dn_dt_bias[l][None], H),
            og=dn_onorm_g[l][None], lng=sg_ln_g[l][None], lnb=sg_ln_b[l][None], sgw=sg_w[l], sgbT=sg_b[l].T, fcb=ffn_conv_b[l][None])

    def mixer_in_weights(p, l, after):
        wt = full("w_in", l, after)
        p.update(wA=wt[:colA], wba=_pad_rows(wt[colA:colB0], LANES), wB=wt[colB0:], cw8=_pad_rows(full("dn_conv_w", l, after)))

    def mixer_out_weights(p, l, after):
        p.update(wa=full("w_branch_a", l, after), wb=full("w_branch_b", l, after), wo=full("w_out", l, after))

    def ffn_weights(p, l, after):
        p.update(wg=full("ffn_w_gate", l, after), wu=full("ffn_w_up", l, after), fcw8=_pad_rows(full("ffn_conv_w", l, after)),
                 wd=full("ffn_w_down", l, after))

    saved = []
    cur = xs
    for l in range(depth):
        p = layer_weights(l)
        t = f"l{l}_"
        h = norm_fwd(cur, p["g1"] + gather_tok[0, 0] if l == 0 else p["g1"], t + "norm1")
        mixer_in_weights(p, l, h)
        projA = matmul(h, p["wA"], "nt", t + "projA")
        pba = matmul(h, p["wba"], "nt", t + "proj_ba")
        projB = matmul(h, p["wB"], "nt", t + "projB")
        q, k, v, bg = dn_prep_fwd(projA, pba, p["cw8"], p["alog"], p["dtb"], H, t + "dn_prep")
        u, w, a, qd, kd, gl, tinv = dn_chunk_fwd(q, k, v, bg, t + "dn_chunk")
        o, s_in = dn_scan_fwd(u, w, a, qd, kd, gl, t + "dn_scan")
        y_a = dn_post_fwd(o, projA, p["og"], t + "dn_post")
        y_b = gmlp_fwd(projB, p["lng"], p["lnb"], p["sgw"], p["sgbT"], t + "gmlp")
        mixer_out_weights(p, l, y_b)
        ap = matmul(y_a, p["wa"], "nn", t + "branch_a")
        bp = matmul(y_b, p["wb"], "nn", t + "branch_b")
        merged = merge_fwd(projB, ap, bp, cb_a, t + "merge")
        x1 = matmul(merged, p["wo"], "nn", t + "out_proj", c=cur)
        h2 = norm_fwd(x1, p["g2"], t + "norm2")
        ffn_weights(p, l, h2)
        gp = matmul(h2, p["wg"], "nt", t + "ffn_gate")
        up = matmul(h2, p["wu"], "nt", t + "ffn_up")
        act = ffn_act_fwd(gp, up, p["fcw8"], p["fcb"], t + "ffn_act")
        x2 = matmul(act, p["wd"], "nn", t + "ffn_down", c=x1)
        saved.append(dict(p=p, x0=cur, h=h, projA=projA, pba=pba, projB=projB, q=q, k=k, v=v, bg=bg, tinv=tinv,
                          scan=(u, w, a, qd, kd, gl), s_in=s_in, o=o, y_a=y_a, y_b=y_b, ap=ap, bp=bp, merged=merged, x1=x1,
                          h2=h2, gp=gp, up=up, act=act))
        cur = x2

    loss_part, dx, d_final = head_fwd_bwd(cur, final_norm_g[None], tgt, "loss_head")
    loss = lax.psum(loss_part[0, 0], ("x", "y", "c"))

    grads_sh = {n: [None] * depth for n in SHARDED}
    grads_rep = {n: [None] * depth for n in REPLICATED if n != "final_norm_g"}
    exchanges = []

    def exchange(names, l, name, after=None):
        srcs = [_split(n, grads_sh[n][l], F32 if n in CONV_WEIGHTS else BF16) for n in names]
        (handle,), tok = comm_start([srcs], True, name, after=after)
        exchanges.append((names, l, handle))
        return tok

    def whole(a):
        return [(a, 0, a.shape[1])]

    sizes = [math.prod(W[n].shape) for n in REPLICATED]
    tile = SUBLANES * LANES
    nrows = [-(-sz // tile) * SUBLANES for sz in sizes]

    def pack(d):
        parts = [jnp.pad(d[n].reshape(-1).astype(F32), (0, r * LANES - sz)).reshape(r, LANES)
                 for n, sz, r in zip(REPLICATED, sizes, nrows)]
        return jnp.concatenate(parts, axis=0)

    mixer_tok = None
    for l in reversed(range(depth)):
        s = saved[l]
        p = s["p"]
        t = f"l{l}_b_"
        dact = matmul(dx, p["wd"], "nt", t + "d_act")
        grads_sh["ffn_w_down"][l] = whole(matmul(s["act"], dx, "tn", t + "dw_down", out_dtype=BF16))
        fcb = p["fcb"] if mixer_tok is None else p["fcb"] + mixer_tok[0, 0]
        dgp, dup, dfcw, dfcb = ffn_act_bwd(s["gp"], s["up"], p["fcw8"], fcb, dact, t + "ffn_act")
        dh2 = matmul(dgp, p["wg"], "nn", t + "dh2_gate")
        dh2 = matmul(dup, p["wu"], "nn", t + "dh2_up", c=dh2)
        grads_sh["ffn_w_gate"][l] = whole(matmul(dgp, s["h2"], "tn", t + "dw_gate", out_dtype=BF16))
        grads_sh["ffn_w_up"][l] = whole(matmul(dup, s["h2"], "tn", t + "dw_up", out_dtype=BF16))
        grads_sh["ffn_conv_w"][l] = whole(dfcw[:3])
        grads_rep["ffn_conv_b"][l] = dfcb[0]
        tok = exchange(("ffn_w_down", "ffn_w_gate", "ffn_w_up", "ffn_conv_w"), l, t + "ffn_grads_start")
        dx1, dg2 = norm_bwd(s["x1"], p["g2"] + tok[0, 0], dh2, dx, t + "norm2")
        grads_rep["norm2_g"][l] = dg2[0]
        dmerged = matmul(dx1, p["wo"], "nt", t + "d_merged")
        grads_sh["w_out"][l] = whole(matmul(s["merged"], dx1, "tn", t + "dw_out", out_dtype=BF16))
        dga, dgb, dap, dbp = merge_bwd(s["projB"], s["ap"], s["bp"], dmerged, cb_a, t + "merge")
        dya = matmul(dap, p["wa"], "nt", t + "d_ya")
        dyb = matmul(dbp, p["wb"], "nt", t + "d_yb")
        grads_sh["w_branch_a"][l] = whole(matmul(s["y_a"], dap, "tn", t + "dw_a", out_dtype=BF16))
        grads_sh["w_branch_b"][l] = whole(matmul(s["y_b"], dbp, "tn", t + "dw_b", out_dtype=BF16))
        du_raw, dv_raw, dlng, dlnb, dsgw, dsgbT = gmlp_bwd(s["projB"], p["lng"], p["lnb"], p["sgw"], p["sgbT"], dyb, t + "gmlp")
        grads_rep["sg_ln_g"][l], grads_rep["sg_ln_b"][l] = dlng[0], dlnb[0]
        grads_rep["sg_w"][l], grads_rep["sg_b"][l] = dsgw, dsgbT.T
        do, dz, dog = dn_post_bwd(s["o"], s["projA"], p["og"], dya, t + "dn_post")
        grads_rep["dn_onorm_g"][l] = dog[0]
        du, dw, da, dqd, dkd, dgl = dn_scan_bwd(*s["scan"], s["s_in"], do, t + "dn_scan")
        dq, dk, dv, dbg = dn_chunk_bwd(s["q"], s["k"], s["v"], s["bg"], s["tinv"], du, dw, da, dqd, dkd, dgl, t + "dn_chunk")
        dqkv, dba, dcw, dalog, ddtb = dn_prep_bwd(s["projA"], s["pba"], p["cw8"], p["alog"], p["dtb"], dq, dk, dv, dbg, H,
                                                  t + "dn_prep")
        grads_sh["dn_conv_w"][l] = whole(dcw[:4])
        grads_rep["dn_a_log"][l], grads_rep["dn_dt_bias"][l] = dalog[0, H:2 * H], ddtb[0, H:2 * H]
        tok = exchange(("w_out", "w_branch_a", "w_branch_b", "dn_conv_w"), l, t + "mixer_grads_start")
        dba = dba + tok[0, 0].astype(BF16)
        dprojA = jnp.concatenate([dqkv, dz], axis=1)
        dprojB = jnp.concatenate([du_raw, dv_raw, dga, dgb], axis=1)
        dwA = matmul(dprojA, s["h"], "tn", t + "dw_A", out_dtype=BF16)
        dwba = matmul(dba, s["h"], "tn", t + "dw_ba", out_dtype=BF16)
        dwB = matmul(dprojB, s["h"], "tn", t + "dw_B", out_dtype=BF16)
        grads_sh["w_in"][l] = [(dwA, 0, colA), (dwba, colA, 2 * H), (dwB, colB0, dwB.shape[0])]
        mixer_tok = exchange(("w_in",), l, t + "w_in_grads_start")
        dh = matmul(dba, p["wba"] + mixer_tok[0, 0].astype(BF16), "nn", t + "dh_ba")
        dh = matmul(dprojA, p["wA"], "nn", t + "dh_A", c=dh)
        dh = matmul(dprojB, p["wB"], "nn", t + "dh_B", c=dh)
        dx, dg1 = norm_bwd(s["x0"], p["g1"], dh, dx1, t + "norm1")
        grads_rep["norm1_g"][l] = dg1[0]
        if l == 0:
            rep_full = {n: (jnp.stack(grads_rep[n]) if n != "final_norm_g" else d_final[0]) for n in REPLICATED}
            (small_handle,), small_tok = comm_start([[pack(rep_full)]], False, "small_grads_start")

    out = {}
    after = [dx, small_tok]

    def update_group(gi, after):
        names, l, handle = exchanges[gi]
        src, land = comm_wait(handle, after, f"grads_wait{gi}")
        done = []
        for n, s_, ld in zip(names, src, land):
            res = sum_adamw_shard(s_, ld, me_arr, Wv[n], Mv[n], Vv[n], l, out.get(n), f"adamw_{n}_{l}")
            out[n] = list(res)
            done.append(res[0])
        return done

    for gi in range(len(exchanges) - 1):
        after = update_group(gi, after)

    (small_src,), (small_land,) = comm_wait(small_handle, after, "small_grads_wait")
    rep_parts = lax.dynamic_update_index_in_dim(small_land, small_src, my, 0)
    res = sum_adamw(rep_parts, pack(W), pack(Mo), pack(Vo), "adamw_small")
    update_group(len(exchanges) - 1, after + [res[0]])
    for n in TRANSPOSED:
        out[n] = [jnp.transpose(r, (0, 2, 1)) for r in out[n]]
    row0 = 0
    for n, sz, nr in zip(REPLICATED, sizes, nrows):
        out[n] = [r[row0:row0 + nr].reshape(-1)[:sz].reshape(W[n].shape) for r in res]
        row0 += nr

    return (loss, dx[None], *[out[n][0] for n in WEIGHTS], *[out[n][1] for n in WEIGHTS],
            *[out[n][2] for n in WEIGHTS], *[out[n][3] for n in WEIGHTS])
```

```python
import functools
import math

import jax
import jax.numpy as jnp
from jax import lax
from jax.experimental import pallas as pl
from jax.experimental.pallas import tpu as pltpu

F32 = jnp.float32
BF16 = jnp.bfloat16
EPS = 1e-6
N_DEV = 8
LANES = 128
SUBLANES = 8
HEAD_DIM = 128
DN_CHUNK = 64
SG_CHUNK = 128
VMEM_LIMIT = 56 * 1024 * 1024
MESH = pl.DeviceIdType.MESH
HIGHEST = lax.Precision.HIGHEST

ADAM_LR = 0.001
ADAM_B1 = 0.9
ADAM_B2 = 0.999
ADAM_EPS = 1e-08
ADAM_WD = 0.01
ADAM_STEP = 10


def _pick(n, target, mult=LANES):
    best = None
    d = mult
    while d <= min(n, target):
        if n % d == 0:
            best = d
        d += mult
    return n if best is None else best


def _params(sem):
    return pltpu.CompilerParams(dimension_semantics=sem, vmem_limit_bytes=VMEM_LIMIT)


_NN = (((1,), (0,)), ((), ()))
_NT = (((1,), (1,)), ((), ()))
_TN = (((0,), (0,)), ((), ()))


def _dg(a, b, dims, hi):
    if hi == 2:
        return lax.dot_general(a.astype(F32), b.astype(F32), dims, precision=HIGHEST, preferred_element_type=F32)
    if hi == 1:
        a_hi, b_hi = a.astype(BF16), b.astype(BF16)
        a_lo, b_lo = (a - a_hi.astype(F32)).astype(BF16), (b - b_hi.astype(F32)).astype(BF16)
        ax, bx = dims[0][0][0], dims[0][1][0]
        a = jnp.concatenate([a_hi, a_hi, a_lo], axis=ax)
        b = jnp.concatenate([b_hi, b_lo, b_hi], axis=bx)
        return lax.dot_general(a, b, dims, preferred_element_type=F32)
    return lax.dot_general(a.astype(BF16), b.astype(BF16), dims, preferred_element_type=F32)


@functools.partial(jax.custom_vjp, nondiff_argnums=(2,))
def mm_nn(a, b, hi=False):
    return _dg(a, b, _NN, hi)


def _mm_nn_f(a, b, hi):
    return _dg(a, b, _NN, hi), (a, b)


def _mm_nn_b(hi, res, g):
    a, b = res
    return mm_nt(g, b, hi), mm_tn(a, g, hi)


@functools.partial(jax.custom_vjp, nondiff_argnums=(2,))
def mm_nt(a, b, hi=False):
    return _dg(a, b, _NT, hi)


def _mm_nt_f(a, b, hi):
    return _dg(a, b, _NT, hi), (a, b)


def _mm_nt_b(hi, res, g):
    a, b = res
    return mm_nn(g, b, hi), mm_tn(g, a, hi)


@functools.partial(jax.custom_vjp, nondiff_argnums=(2,))
def mm_tn(a, b, hi=False):
    return _dg(a, b, _TN, hi)


def _mm_tn_f(a, b, hi):
    return _dg(a, b, _TN, hi), (a, b)


def _mm_tn_b(hi, res, g):
    a, b = res
    return mm_nt(b, g, hi), mm_nn(a, g, hi)


mm_nn.defvjp(_mm_nn_f, _mm_nn_b)
mm_nt.defvjp(_mm_nt_f, _mm_nt_b)
mm_tn.defvjp(_mm_tn_f, _mm_tn_b)


def matmul(a, b, mode, name, c=None, out_dtype=F32, tm=1024, tn=1024, tk=2048):
    a_list = list(a) if isinstance(a, (list, tuple)) else [a]
    b_list = list(b) if isinstance(b, (list, tuple)) else [b]
    nterm = len(a_list)

    def dims_of(a, b):
        if mode == "nn":
            return a.shape[0], a.shape[1], b.shape[1]
        if mode == "nt":
            return a.shape[0], a.shape[1], b.shape[0]
        return a.shape[1], a.shape[0], b.shape[1]

    M, _, N = dims_of(a_list[0], b_list[0])
    tm, tn = _pick(M, tm), _pick(N, tn)
    tks = [_pick(dims_of(x, y)[1], tk) for x, y in zip(a_list, b_list)]
    nks = [dims_of(x, y)[1] // t for x, y, t in zip(a_list, b_list, tks)]
    offs = [sum(nks[:t]) for t in range(nterm)]
    nk = sum(nks)
    dims = {"nn": _NN, "nt": _NT, "tn": _TN}[mode]

    def specs_of(t):
        kk = lambda k: jnp.clip(k - offs[t], 0, nks[t] - 1)
        a_spec = (pl.BlockSpec((tks[t], tm), lambda i, j, k: (kk(k), i)) if mode == "tn"
                  else pl.BlockSpec((tm, tks[t]), lambda i, j, k: (i, kk(k))))
        b_spec = (pl.BlockSpec((tn, tks[t]), lambda i, j, k: (j, kk(k))) if mode == "nt"
                  else pl.BlockSpec((tks[t], tn), lambda i, j, k: (kk(k), j)))
        return [a_spec, b_spec]

    o_spec = pl.BlockSpec((tm, tn), lambda i, j, k: (i, j))
    has_c = c is not None
    own_acc = nk > 1 and out_dtype != F32

    def body(*refs):
        ab = refs[:2 * nterm]
        c_ref = refs[2 * nterm] if has_c else None
        o_ref = refs[2 * nterm + (1 if has_c else 0)]
        acc_ref = refs[-1] if own_acc else o_ref

        def dot(t):
            return lax.dot_general(ab[2 * t][...].astype(BF16), ab[2 * t + 1][...].astype(BF16), dims,
                                   preferred_element_type=F32)

        if nk == 1:
            o_ref[...] = (dot(0) + c_ref[...] if has_c else dot(0)).astype(o_ref.dtype)
        else:
            k = pl.program_id(2)

            @pl.when(k == 0)
            def _():
                acc_ref[...] = c_ref[...] if has_c else jnp.zeros_like(acc_ref)

            for t in range(nterm):
                if nterm == 1:
                    acc_ref[...] += dot(t)
                else:
                    @pl.when((k >= offs[t]) & (k < offs[t] + nks[t]))
                    def _(t=t):
                        acc_ref[...] += dot(t)

            if own_acc:
                @pl.when(k == nk - 1)
                def _():
                    o_ref[...] = acc_ref[...].astype(o_ref.dtype)

    ins, specs = [], []
    for t in range(nterm):
        ins += [a_list[t], b_list[t]]
        specs += specs_of(t)
    if has_c:
        ins.append(c)
        specs.append(o_spec)
    return pl.pallas_call(
        body, name=name, grid=(M // tm, N // tn, nk), in_specs=specs, out_specs=o_spec,
        out_shape=jax.ShapeDtypeStruct((M, N), out_dtype), scratch_shapes=[pltpu.VMEM((tm, tn), F32)] if own_acc else [],
        compiler_params=_params(("parallel", "parallel", "arbitrary")),
    )(*ins)


def rowcall(name, fn, ins, in_specs, outs, out_specs, acc, nrow, ncol=1):
    n_in = len(ins)

    def body(*refs):
        i = pl.program_id(1)
        res = fn(i, *[r[...] for r in refs[:n_in]])
        for r, v, is_acc in zip(refs[n_in:], res, acc):
            if is_acc:
                @pl.when(i == 0)
                def _(r=r, v=v):
                    r[...] = v.astype(r.dtype)

                @pl.when(i > 0)
                def _(r=r, v=v):
                    r[...] += v.astype(r.dtype)
            else:
                r[...] = v.astype(r.dtype)

    return pl.pallas_call(
        body, name=name, grid=(ncol, nrow), in_specs=list(in_specs), out_specs=list(out_specs), out_shape=list(outs),
        compiler_params=_params(("parallel", "arbitrary")),
    )(*ins)


class Tiles:
    def __init__(self, T, tm):
        self.T, self.tm, self.n = T, tm, T // tm
        self.r8 = tm // SUBLANES

    def row(self, w, cb=0):
        return pl.BlockSpec((self.tm, w), lambda j, i: (i, cb))

    def rowj(self, tc):
        return pl.BlockSpec((self.tm, tc), lambda j, i: (i, j))

    def prev(self, w, cb=0):
        return pl.BlockSpec((SUBLANES, w), lambda j, i: (jnp.maximum(i * self.r8 - 1, 0), cb))

    def prevj(self, tc):
        return pl.BlockSpec((SUBLANES, tc), lambda j, i: (jnp.maximum(i * self.r8 - 1, 0), j))

    def nxt(self, w, cb=0):
        last = self.T // SUBLANES - 1
        return pl.BlockSpec((SUBLANES, w), lambda j, i: (jnp.minimum((i + 1) * self.r8, last), cb))

    def nxtj(self, tc):
        last = self.T // SUBLANES - 1
        return pl.BlockSpec((SUBLANES, tc), lambda j, i: (jnp.minimum((i + 1) * self.r8, last), j))

    def heads(self, H):
        return pl.BlockSpec((H, self.tm, HEAD_DIM), lambda j, i: (0, i, 0))

    def heads_nxt(self, H):
        last = self.T // SUBLANES - 1
        return pl.BlockSpec((H, SUBLANES, HEAD_DIM), lambda j, i: (0, jnp.minimum((i + 1) * self.r8, last), 0))


def full(shape):
    return pl.BlockSpec(tuple(shape), lambda j, i: (0,) * len(shape))


def constj(r, tc):
    return pl.BlockSpec((r, tc), lambda j, i: (0, j))


def sds(shape, dtype=F32):
    return jax.ShapeDtypeStruct(tuple(shape), dtype)


def rms(x, g):
    return x * lax.rsqrt(jnp.mean(x * x, axis=-1, keepdims=True) + EPS) * g


def silu(x):
    return x * jax.nn.sigmoid(x)


def gelu(x):
    return 0.5 * x * (1.0 + lax.erf(x * (2.0 ** -0.5)))


def causal_conv(xwin, w, K, R):
    base = SUBLANES - (K - 1)
    out = w[0:1, :] * xwin[base:base + R, :]
    for j in range(1, K):
        out = out + w[j:j + 1, :] * xwin[base + j:base + j + R, :]
    return out


def rows_to8(rows, C):
    rid = lax.broadcasted_iota(jnp.int32, (SUBLANES, C), 0)
    out = jnp.zeros((SUBLANES, C), F32)
    for k, r in enumerate(rows):
        out = out + jnp.where(rid == k, jnp.broadcast_to(r, (SUBLANES, C)), 0.0)
    return out


def dn_qkv(pre, H):
    a = silu(pre)
    W = H * HEAD_DIM

    def l2(t):
        return t * lax.rsqrt(jnp.sum(t * t, axis=-1, keepdims=True) + EPS)

    q = [l2(a[:, h * HEAD_DIM:(h + 1) * HEAD_DIM]) for h in range(H)]
    k = [l2(a[:, W + h * HEAD_DIM:W + (h + 1) * HEAD_DIM]) for h in range(H)]
    v = [a[:, 2 * W + h * HEAD_DIM:2 * W + (h + 1) * HEAD_DIM] for h in range(H)]
    return q, k, v


def dn_gates(ba, alog, dtb, H, R):
    lane = lax.broadcasted_iota(jnp.int32, (R, LANES), 1)
    beta = jax.nn.sigmoid(ba)
    g = -jnp.exp(alog) * jax.nn.softplus(ba + dtb)
    g = jnp.where((lane >= H) & (lane < 2 * H), g, 0.0)
    ri = lax.broadcasted_iota(jnp.int32, (R, R), 0)
    ci = lax.broadcasted_iota(jnp.int32, (R, R), 1)
    cum = jnp.where((ri // DN_CHUNK == ci // DN_CHUNK) & (ci <= ri), 1.0, 0.0).astype(F32)
    gc = mm_nn(cum, g, 2)
    return jnp.where(lane < H, beta, gc)


def neumann_inverse(Ls):
    C = Ls[0].shape[0]
    ri = lax.broadcasted_iota(jnp.int32, (C, C), 0)
    ci = lax.broadcasted_iota(jnp.int32, (C, C), 1)
    eye = jnp.where(ri == ci, 1.0, 0.0).astype(F32)
    P = [-L for L in Ls]
    R = [eye + p for p in P]
    for _ in range(int(math.log2(C)) - 1):
        P = [mm_nn(p, p, 1) for p in P]
        R = [r + mm_nn(r, p, 1) for r, p in zip(R, P)]
    return R


@jax.custom_vjp
def saved_inverse(L, T):
    return T


def _saved_inverse_f(L, T):
    return T, T


def _saved_inverse_b(T, g):
    return -mm_tn(T, mm_nt(g, T, 1), 1), jnp.zeros_like(T)


saved_inverse.defvjp(_saved_inverse_f, _saved_inverse_b)


def gate_columns(bg, H):
    bgT = bg.T
    return ([bg[:, h:h + 1] for h in range(H)], [bg[:, H + h:H + h + 1] for h in range(H)],
            [bgT[H + h:H + h + 1, :] for h in range(H)])


def dn_chunk(q, k, v, beta, gc, gr, tinv=None, with_inverse=False):
    n = len(q)
    C = q[0].shape[0]
    ri = lax.broadcasted_iota(jnp.int32, (C, C), 0)
    ci = lax.broadcasted_iota(jnp.int32, (C, C), 1)
    qs = [q[h] * (HEAD_DIM ** -0.5) for h in range(n)]
    kb = [k[h] * beta[h] for h in range(n)]
    vb = [v[h] * beta[h] for h in range(n)]
    decay = [jnp.exp(jnp.where(ri >= ci, gc[h] - gr[h], -jnp.inf)) for h in range(n)]
    L = [jnp.where(ri > ci, mm_nt(kb[h], k[h]) * decay[h], 0.0) for h in range(n)]
    attn = [jnp.where(ri >= ci, mm_nt(qs[h], k[h]) * decay[h], 0.0) for h in range(n)]
    Tinv = neumann_inverse(L) if tinv is None else [saved_inverse(L[h], tinv[h]) for h in range(n)]
    eg = [jnp.exp(gc[h]) for h in range(n)]
    u = [mm_nn(Tinv[h], vb[h]) for h in range(n)]
    w = [mm_nn(Tinv[h], kb[h] * eg[h]) for h in range(n)]
    qd = [qs[h] * eg[h] for h in range(n)]
    gl = [gc[h][C - 1:C, :] for h in range(n)]
    kd = [k[h] * jnp.exp(gl[h] - gc[h]) for h in range(n)]
    return (u, w, attn, qd, kd, gl, Tinv) if with_inverse else (u, w, attn, qd, kd, gl)


def dn_step(u, w, a, qd, kd, gl, S):
    n = len(u)
    v_new = [u[h] - mm_nn(w[h], S[h]) for h in range(n)]
    o = [mm_nn(qd[h], S[h]) + mm_nn(a[h], v_new[h]) for h in range(n)]
    S_new = [S[h] * jnp.exp(gl[h]) + mm_tn(kd[h], v_new[h]) for h in range(n)]
    return o, S_new


def dn_post(o, z, g):
    H = o.shape[0]
    return jnp.concatenate([rms(o[h], g) * silu(z[:, h * HEAD_DIM:(h + 1) * HEAD_DIM]) for h in range(H)], axis=1)


def gmlp(u_raw, v_raw, ln_g, ln_b, sgw, sgbT):
    R = u_raw.shape[0]
    G = sgw.shape[0]
    nc = R // SG_CHUNK
    u = gelu(u_raw)
    vv = gelu(v_raw)
    xc = vv - jnp.mean(vv, axis=-1, keepdims=True)
    vg = xc * lax.rsqrt(jnp.mean(xc * xc, axis=-1, keepdims=True) + EPS) * ln_g + ln_b
    ri = lax.broadcasted_iota(jnp.int32, (SG_CHUNK, SG_CHUNK), 0)
    ci = lax.broadcasted_iota(jnp.int32, (SG_CHUNK, SG_CHUNK), 1)
    cols = []
    for g in range(G):
        ws = jnp.where(ri >= ci, sgw[g], 0.0)
        rhs = jnp.concatenate([vg[c * SG_CHUNK:(c + 1) * SG_CHUNK, g * HEAD_DIM:(g + 1) * HEAD_DIM] for c in range(nc)], axis=1)
        mixed = mm_nn(ws, rhs) + sgbT[:, g:g + 1]
        cols.append(jnp.concatenate([mixed[:, c * HEAD_DIM:(c + 1) * HEAD_DIM] for c in range(nc)], axis=0))
    return u * jnp.concatenate(cols, axis=1)


def merge(ga, gb, ap, bp):
    return jax.nn.sigmoid(ga) * ap + jax.nn.sigmoid(gb) * bp


def norm_fwd(x, g, name, tm=256):
    T, D = x.shape
    tl = Tiles(T, _pick(T, tm))
    (h,) = rowcall(name, lambda i, x, g: (rms(x, g),), [x, g], [tl.row(D), full((1, D))],
                   [sds((T, D), BF16)], [tl.row(D)], [False], tl.n)
    return h


def norm_bwd(x, g, dh, dres, name, tm=256):
    T, D = x.shape
    tl = Tiles(T, _pick(T, tm))

    def fn(i, x, g, dh, dres):
        _, vj = jax.vjp(rms, x, g)
        dx, dg = vj(dh.astype(F32))
        dx = dx + dres
        return dx, dx, dg

    return rowcall(name, fn, [x, g, dh, dres], [tl.row(D), full((1, D)), tl.row(D), tl.row(D)],
                   [sds((T, D)), sds((T, D), BF16), sds((1, D))], [tl.row(D), tl.row(D), full((1, D))],
                   [False, False, True], tl.n)


def head_fwd_bwd(x, g, tgt, name, tm=256):
    T, D = x.shape
    tl = Tiles(T, _pick(T, tm))

    def fn(i, x, g, tgt):
        y, vj = jax.vjp(rms, x, g)
        e = y - tgt
        loss = 0.5 * jnp.sum(jnp.mean(e * e, axis=-1, keepdims=True), axis=0, keepdims=True)
        dx, dg = vj(e * (1.0 / D))
        return loss, dx, dx, dg

    return rowcall(name, fn, [x, g, tgt], [tl.row(D), full((1, D)), tl.row(D)],
                   [sds((1, 1)), sds((T, D)), sds((T, D), BF16), sds((1, D))],
                   [full((1, 1)), tl.row(D), tl.row(D), full((1, D))], [True, False, False, True], tl.n)


def dn_prep_fwd(projA, pba, cw8, alog, dtb, H, name, tm=256):
    T = projA.shape[0]
    W3 = 3 * H * HEAD_DIM
    tl = Tiles(T, _pick(T, tm, DN_CHUNK))
    R = tl.tm

    def fn(i, xp, x, ba, cw, alog, dtb):
        xwin = jnp.concatenate([jnp.where(i > 0, xp, 0.0), x], axis=0)
        q, k, v = dn_qkv(causal_conv(xwin, cw, 4, R), H)
        return jnp.stack(q), jnp.stack(k), jnp.stack(v), dn_gates(ba, alog, dtb, H, R)

    hs = sds((H, T, HEAD_DIM))
    return rowcall(name, fn, [projA, projA, pba, cw8, alog, dtb],
                   [tl.prev(W3), tl.row(W3), tl.row(LANES), full((SUBLANES, W3)), full((1, LANES)), full((1, LANES))],
                   [hs, hs, hs, sds((T, LANES))], [tl.heads(H)] * 3 + [tl.row(LANES)], [False] * 4, tl.n)


def dn_prep_bwd(projA, pba, cw8, alog, dtb, dq, dk, dv, dbg, H, name, tm=256):
    T = projA.shape[0]
    W3 = 3 * H * HEAD_DIM
    tl = Tiles(T, _pick(T, tm, DN_CHUNK))
    R = tl.tm
    RE = R + SUBLANES

    def fn(i, xp, x, xn, ba, cw, alog, dtb, dq, dk, dv, dqn, dkn, dvn, dbg):
        last = i == tl.n - 1
        xwin = jnp.concatenate([jnp.where(i > 0, xp, 0.0), x, jnp.where(last, 0.0, xn)], axis=0)
        pre = causal_conv(xwin, cw, 4, RE)
        ext = lambda d, dn: [jnp.concatenate([d[h], jnp.where(last, 0.0, dn[h])], axis=0) for h in range(H)]
        _, vj = jax.vjp(lambda p: dn_qkv(p, H), pre)
        (dpre,) = vj((ext(dq, dqn), ext(dk, dkn), ext(dv, dvn)))
        dx = cw[0:1, :] * dpre[3:3 + R, :]
        for j in range(1, 4):
            dx = dx + cw[j:j + 1, :] * dpre[3 - j:3 - j + R, :]
        dcw = rows_to8([jnp.sum(dpre[0:R, :] * xwin[5 + j:5 + j + R, :], axis=0, keepdims=True) for j in range(4)], W3)
        _, vjg = jax.vjp(lambda ba, alog, dtb: dn_gates(ba, alog, dtb, H, R), ba, alog, dtb)
        dba, dalog, ddtb = vjg(dbg)
        return dx, dba, dcw, dalog, ddtb

    return rowcall(name, fn, [projA, projA, projA, pba, cw8, alog, dtb, dq, dk, dv, dq, dk, dv, dbg],
                   [tl.prev(W3), tl.row(W3), tl.nxt(W3), tl.row(LANES), full((SUBLANES, W3)), full((1, LANES)), full((1, LANES))]
                   + [tl.heads(H)] * 3 + [tl.heads_nxt(H)] * 3 + [tl.row(LANES)],
                   [sds((T, W3), BF16), sds((T, LANES), BF16), sds((SUBLANES, W3)), sds((1, LANES)), sds((1, LANES))],
                   [tl.row(W3), tl.row(LANES), full((SUBLANES, W3)), full((1, LANES)), full((1, LANES))],
                   [False, False, True, True, True], tl.n)


def _chunk_specs(H, C):
    hs = pl.BlockSpec((H, C, HEAD_DIM), lambda n: (0, n, 0))
    col = pl.BlockSpec((H, 1, C, 1), lambda n: (0, n, 0, 0))
    rw = pl.BlockSpec((H, 1, 1, C), lambda n: (0, n, 0, 0))
    at = pl.BlockSpec((H, C, C), lambda n: (0, n, 0))
    one = pl.BlockSpec((H, 1, 1, 1), lambda n: (0, n, 0, 0))
    return hs, col, rw, at, one


def dn_chunk_fwd(q, k, v, bg, name):
    H, T, _ = q.shape
    C = DN_CHUNK
    N = T // C
    hs, _, _, at, one = _chunk_specs(H, C)
    gate = pl.BlockSpec((C, LANES), lambda n: (n, 0))

    def body(q, k, v, bg, u, w, a, qd, kd, gl, ti):
        hd = range(H)
        res = dn_chunk([q[h] for h in hd], [k[h] for h in hd], [v[h] for h in hd], *gate_columns(bg[...], H),
                       with_inverse=True)
        for h in hd:
            for ref, val in zip((u, w, a, qd, kd), res[:5]):
                ref[h] = val[h]
            gl[h, 0] = res[5][h]
            ti[h] = res[6][h]

    big = sds((H, T, HEAD_DIM))
    return pl.pallas_call(
        body, name=name, grid=(N,), in_specs=[hs, hs, hs, gate], out_specs=[hs, hs, at, hs, hs, one, at],
        out_shape=[big, big, sds((H, T, C)), big, big, sds((H, N, 1, 1)), sds((H, T, C))],
        compiler_params=_params(("parallel",)),
    )(q, k, v, bg)


def dn_chunk_bwd(q, k, v, bg, tinv, du, dw, da, dqd, dkd, dgl, name):
    H, T, _ = q.shape
    C = DN_CHUNK
    N = T // C
    hs, _, _, at, one = _chunk_specs(H, C)
    gate = pl.BlockSpec((C, LANES), lambda n: (n, 0))

    def body(q, k, v, bg, ti, du, dw, da, dqd, dkd, dgl, dq, dk, dv, dbg):
        hd = range(H)
        f = lambda q, k, v, b, gc, gr: dn_chunk(q, k, v, b, gc, gr, tinv=[ti[h] for h in hd])
        _, vj = jax.vjp(f, [q[h] for h in hd], [k[h] for h in hd], [v[h] for h in hd], *gate_columns(bg[...], H))
        res = vj(([du[h] for h in hd], [dw[h] for h in hd], [da[h] for h in hd], [dqd[h] for h in hd],
                  [dkd[h] for h in hd], [dgl[h, 0] for h in hd]))
        lane = lax.broadcasted_iota(jnp.int32, (C, LANES), 1)
        row = lax.broadcasted_iota(jnp.int32, (LANES, C), 0)
        cols = jnp.zeros((C, LANES), F32)
        rows = jnp.zeros((LANES, C), F32)
        for h in hd:
            dq[h], dk[h], dv[h] = res[0][h], res[1][h], res[2][h]
            cols = cols + jnp.where(lane == h, res[3][h], 0.0) + jnp.where(lane == H + h, res[4][h], 0.0)
            rows = rows + jnp.where(row == H + h, res[5][h], 0.0)
        dbg[...] = cols + rows.T

    big = sds((H, T, HEAD_DIM))
    return pl.pallas_call(
        body, name=name, grid=(N,), in_specs=[hs, hs, hs, gate, at, hs, hs, at, hs, hs, one],
        out_specs=[hs, hs, hs, gate], out_shape=[big, big, big, sds((T, LANES))], compiler_params=_params(("parallel",)),
    )(q, k, v, bg, tinv, du, dw, da, dqd, dkd, dgl)


def dn_scan_fwd(u, w, a, qd, kd, gl, name):
    H, T, _ = u.shape
    C = DN_CHUNK
    N = T // C
    hs, _, _, at, one = _chunk_specs(H, C)
    st = pl.BlockSpec((1, H, HEAD_DIM, HEAD_DIM), lambda n: (n, 0, 0, 0))

    def body(u, w, a, qd, kd, gl, o, s_in, S):
        @pl.when(pl.program_id(0) == 0)
        def _():
            S[...] = jnp.zeros_like(S)

        hd = range(H)
        s = [S[h] for h in hd]
        o_new, s_new = dn_step([u[h] for h in hd], [w[h] for h in hd], [a[h] for h in hd], [qd[h] for h in hd],
                               [kd[h] for h in hd], [gl[h, 0] for h in hd], s)
        for h in hd:
            s_in[0, h] = s[h]
            o[h] = o_new[h]
            S[h] = s_new[h]

    return pl.pallas_call(
        body, name=name, grid=(N,), in_specs=[hs, hs, at, hs, hs, one], out_specs=[hs, st],
        out_shape=[sds((H, T, HEAD_DIM)), sds((N, H, HEAD_DIM, HEAD_DIM))],
        scratch_shapes=[pltpu.VMEM((H, HEAD_DIM, HEAD_DIM), F32)], compiler_params=_params(("arbitrary",)),
    )(u, w, a, qd, kd, gl)


def dn_scan_bwd(u, w, a, qd, kd, gl, s_in, do, name):
    H, T, _ = u.shape
    C = DN_CHUNK
    N = T // C
    rev = lambda spec_shape, f: pl.BlockSpec(spec_shape, f)
    hs = rev((H, C, HEAD_DIM), lambda n: (0, N - 1 - n, 0))
    at = rev((H, C, C), lambda n: (0, N - 1 - n, 0))
    one = rev((H, 1, 1, 1), lambda n: (0, N - 1 - n, 0, 0))
    st = rev((1, H, HEAD_DIM, HEAD_DIM), lambda n: (N - 1 - n, 0, 0, 0))

    def body(u, w, a, qd, kd, gl, s_in, do, du, dw, da, dqd, dkd, dgl, dS):
        @pl.when(pl.program_id(0) == 0)
        def _():
            dS[...] = jnp.zeros_like(dS)

        hd = range(H)
        _, vj = jax.vjp(dn_step, [u[h] for h in hd], [w[h] for h in hd], [a[h] for h in hd], [qd[h] for h in hd],
                        [kd[h] for h in hd], [gl[h, 0] for h in hd], [s_in[0, h] for h in hd])
        res = vj(([do[h] for h in hd], [dS[h] for h in hd]))
        for h in hd:
            du[h], dw[h], da[h], dqd[h], dkd[h] = (res[j][h] for j in range(5))
            dgl[h, 0] = res[5][h]
            dS[h] = res[6][h]

    big = sds((H, T, HEAD_DIM))
    return pl.pallas_call(
        body, name=name, grid=(N,), in_specs=[hs, hs, at, hs, hs, one, st, hs], out_specs=[hs, hs, at, hs, hs, one],
        out_shape=[big, big, sds((H, T, C)), big, big, sds((H, N, 1, 1))],
        scratch_shapes=[pltpu.VMEM((H, HEAD_DIM, HEAD_DIM), F32)], compiler_params=_params(("arbitrary",)),
    )(u, w, a, qd, kd, gl, s_in, do)


def dn_post_fwd(o, projA, g, name, tm=256):
    H, T, _ = o.shape
    W = H * HEAD_DIM
    tl = Tiles(T, _pick(T, tm))
    (y,) = rowcall(name, lambda i, o, z, g: (dn_post(o, z, g),), [o, projA, g], [tl.heads(H), tl.row(W, 3), full((1, HEAD_DIM))],
                   [sds((T, W), BF16)], [tl.row(W)], [False], tl.n)
    return y


def dn_post_bwd(o, projA, g, dy, name, tm=256):
    H, T, _ = o.shape
    W = H * HEAD_DIM
    tl = Tiles(T, _pick(T, tm))

    def fn(i, o, z, g, dy):
        _, vj = jax.vjp(dn_post, o, z, g)
        return vj(dy.astype(F32))

    return rowcall(name, fn, [o, projA, g, dy], [tl.heads(H), tl.row(W, 3), full((1, HEAD_DIM)), tl.row(W)],
                   [sds((H, T, HEAD_DIM)), sds((T, W), BF16), sds((1, HEAD_DIM))],
                   [tl.heads(H), tl.row(W), full((1, HEAD_DIM))], [False, False, True], tl.n)


def gmlp_fwd(projB, ln_g, ln_b, sgw, sgbT, name, tm=512):
    T = projB.shape[0]
    G = sgw.shape[0]
    W = G * HEAD_DIM
    tl = Tiles(T, _pick(T, tm))
    (y,) = rowcall(name, lambda i, *a: (gmlp(*a),), [projB, projB, ln_g, ln_b, sgw, sgbT],
                   [tl.row(W, 0), tl.row(W, 1), full((1, W)), full((1, W)), full(sgw.shape), full(sgbT.shape)],
                   [sds((T, W), BF16)], [tl.row(W)], [False], tl.n)
    return y


def gmlp_bwd(projB, ln_g, ln_b, sgw, sgbT, dy, name, tm=512):
    T = projB.shape[0]
    G = sgw.shape[0]
    W = G * HEAD_DIM
    tl = Tiles(T, _pick(T, tm))

    def fn(i, u_raw, v_raw, ln_g, ln_b, sgw, sgbT, dy):
        _, vj = jax.vjp(gmlp, u_raw, v_raw, ln_g, ln_b, sgw, sgbT)
        return vj(dy.astype(F32))

    return rowcall(name, fn, [projB, projB, ln_g, ln_b, sgw, sgbT, dy],
                   [tl.row(W, 0), tl.row(W, 1), full((1, W)), full((1, W)), full(sgw.shape), full(sgbT.shape), tl.row(W)],
                   [sds((T, W), BF16), sds((T, W), BF16), sds((1, W)), sds((1, W)), sds(sgw.shape), sds(sgbT.shape)],
                   [tl.row(W), tl.row(W), full((1, W)), full((1, W)), full(sgw.shape), full(sgbT.shape)],
                   [False, False, True, True, True, True], tl.n)


def merge_fwd(projB, ap, bp, cb_a, name, tm=256):
    T, D = ap.shape
    tl = Tiles(T, _pick(T, tm))
    (m,) = rowcall(name, lambda i, *a: (merge(*a),), [projB, projB, ap, bp],
                   [tl.row(D, cb_a), tl.row(D, cb_a + 1), tl.row(D), tl.row(D)], [sds((T, D), BF16)], [tl.row(D)], [False], tl.n)
    return m


def merge_bwd(projB, ap, bp, dm, cb_a, name, tm=256):
    T, D = ap.shape
    tl = Tiles(T, _pick(T, tm))

    def fn(i, ga, gb, ap, bp, dm):
        _, vj = jax.vjp(merge, ga, gb, ap, bp)
        return vj(dm.astype(F32))

    return rowcall(name, fn, [projB, projB, ap, bp, dm], [tl.row(D, cb_a), tl.row(D, cb_a + 1), tl.row(D), tl.row(D), tl.row(D)],
                   [sds((T, D), BF16)] * 4, [tl.row(D)] * 4, [False] * 4, tl.n)


def ffn_act_fwd(gp, up, fcw8, fcb, name, tm=256, tc=512):
    T, F = gp.shape
    tl = Tiles(T, _pick(T, tm))
    tc = _pick(F, tc)
    R = tl.tm

    def fn(i, gprev, g, up, cw, cb):
        xwin = jnp.concatenate([jnp.where(i > 0, gprev, 0.0), g], axis=0)
        return (silu(causal_conv(xwin, cw, 3, R) + cb) * up,)

    (act,) = rowcall(name, fn, [gp, gp, up, fcw8, fcb], [tl.prevj(tc), tl.rowj(tc), tl.rowj(tc), constj(SUBLANES, tc), constj(1, tc)],
                     [sds((T, F), BF16)], [tl.rowj(tc)], [False], tl.n, F // tc)
    return act


def ffn_act_bwd(gp, up, fcw8, fcb, dact, name, tm=256, tc=512):
    T, F = gp.shape
    tl = Tiles(T, _pick(T, tm))
    tc = _pick(F, tc)
    R = tl.tm
    RE = R + SUBLANES

    def fn(i, gprev, g, gnext, up, upn, da, dan, cw, cb):
        last = i == tl.n - 1
        xwin = jnp.concatenate([jnp.where(i > 0, gprev, 0.0), g, jnp.where(last, 0.0, gnext)], axis=0)
        gate = causal_conv(xwin, cw, 3, RE) + cb
        upe = jnp.concatenate([up, upn], axis=0)
        dae = jnp.concatenate([da, jnp.where(last, 0.0, dan)], axis=0)
        s = jax.nn.sigmoid(gate)
        dgate = dae * upe * (s * (1.0 + gate * (1.0 - s)))
        dup = da * (gate[0:R, :] * s[0:R, :])
        dgp = cw[0:1, :] * dgate[2:2 + R, :] + cw[1:2, :] * dgate[1:1 + R, :] + cw[2:3, :] * dgate[0:R, :]
        dcw = rows_to8([jnp.sum(dgate[0:R, :] * xwin[6 + j:6 + j + R, :], axis=0, keepdims=True) for j in range(3)], tc)
        dcb = jnp.sum(dgate[0:R, :], axis=0, keepdims=True)
        return dgp, dup, dcw, dcb

    return rowcall(name, fn, [gp, gp, gp, up, up, dact, dact, fcw8, fcb],
                   [tl.prevj(tc), tl.rowj(tc), tl.nxtj(tc), tl.rowj(tc), tl.nxtj(tc), tl.rowj(tc), tl.nxtj(tc),
                    constj(SUBLANES, tc), constj(1, tc)],
                   [sds((T, F), BF16), sds((T, F), BF16), sds((SUBLANES, F)), sds((1, F))],
                   [tl.rowj(tc), tl.rowj(tc), constj(SUBLANES, tc), constj(1, tc)], [False, False, True, True], tl.n, F // tc)


def _me():
    return lax.axis_index("x"), lax.axis_index("y"), lax.axis_index("c")


def all_gather(shards, name):
    nt = len(shards)

    def body(*refs):
        xs, outs = refs[:nt], refs[nt:2 * nt]
        send_sems, recv_sems, local_sems = refs[2 * nt:]
        x, y, c = _me()
        me, sibling = (x, y, c), (x, y, 1 - c)
        chips = [(1 - x, y), (x, 1 - y), (1 - x, 1 - y)]

        def slot(t, p):
            return outs[t].at[4 * p[0] + 2 * p[1] + p[2]]

        def copy(t, k, block, to, src=None):
            return pltpu.make_async_remote_copy(
                src_ref=slot(t, block) if src is None else src, dst_ref=slot(t, block),
                send_sem=send_sems.at[t, k], recv_sem=recv_sems.at[t, k], device_id=to, device_id_type=MESH)

        mine = [pltpu.make_async_copy(xs[t], slot(t, me), local_sems.at[t]) for t in range(nt)]
        first = []
        for t in range(nt):
            mine[t].start()
            first.append(copy(t, 0, me, sibling, src=xs[t]))
            first += [copy(t, 1 + j, me, (*chip, c), src=xs[t]) for j, chip in enumerate(chips)]
        for cp in first:
            cp.start()
        passed = []
        for j, chip in enumerate(chips):
            for t in range(nt):
                copy(t, 1 + j, (*chip, c), me).wait_recv()
                cp = copy(t, 4 + j, (*chip, c), sibling)
                cp.start()
                passed.append(cp)
        for t in range(nt):
            copy(t, 0, sibling, me).wait_recv()
            for j, chip in enumerate(chips):
                copy(t, 4 + j, (*chip, 1 - c), me).wait_recv()
        for cp in first + passed:
            cp.wait_send()
        for t in range(nt):
            mine[t].wait()

    any_spec = pl.BlockSpec(memory_space=pl.ANY)
    return pl.pallas_call(
        body, name=name, in_specs=[any_spec] * nt, out_specs=[any_spec] * nt,
        out_shape=[jax.ShapeDtypeStruct((N_DEV,) + s.shape, s.dtype) for s in shards],
        scratch_shapes=[pltpu.SemaphoreType.DMA((nt, 7)), pltpu.SemaphoreType.DMA((nt, 7)), pltpu.SemaphoreType.DMA((nt,))],
    )(*shards)


_HBM = pl.BlockSpec(memory_space=pltpu.HBM)
_SEM = pl.BlockSpec(memory_space=pltpu.SEMAPHORE)
_ANY = pl.BlockSpec(memory_space=pl.ANY)
_DATAFLOW = pltpu.SideEffectType.DATAFLOW_SIDE_EFFECTING


def _peers():
    x, y, c = _me()
    out = []
    for k in range(1, N_DEV):
        p = (x ^ (k >> 2), y ^ ((k >> 1) & 1), c ^ (k & 1))
        out.append((k, p, 4 * p[0] + 2 * p[1] + p[2]))
    return out


def _split_copy(src, land, send_sems, recv_sems, t, k, peer, slot, my, scatter, receiving):
    return pltpu.make_async_remote_copy(
        src_ref=src.at[slot] if scatter else src, dst_ref=land.at[slot if receiving else my],
        send_sem=send_sems.at[t * (N_DEV - 1) + k - 1], recv_sem=recv_sems.at[t * (N_DEV - 1) + k - 1],
        device_id=peer, device_id_type=MESH)


def comm_start(groups, scatter, name, after=None):
    flat = [a for g in groups for a in g]
    nt = len(flat)
    lands = [lax.empty(a.shape if scatter else (N_DEV,) + a.shape, a.dtype) for a in flat]
    ng = len(groups)
    n_after = 0 if after is None else 1

    def body(*refs):
        src, land = refs[:nt], refs[nt:2 * nt]
        sems = refs[2 * nt + n_after:2 * nt + n_after + 2 * ng]
        token = refs[-1]
        x, y, c = _me()
        my = 4 * x + 2 * y + c
        t0 = 0
        for gi, g in enumerate(groups):
            for k, peer, slot in _peers():
                for t in range(len(g)):
                    _split_copy(src[t0 + t], land[t0 + t], sems[2 * gi], sems[2 * gi + 1], t, k, peer, slot, my, scatter,
                                False).start()
            t0 += len(g)
        token[...] = jnp.zeros_like(token)

    sem_shapes = []
    for g in groups:
        sem_shapes += [pltpu.SemaphoreType.DMA((len(g) * (N_DEV - 1),))] * 2
    res = pl.pallas_call(
        body, name=name, in_specs=[_HBM] * (2 * nt) + [_HBM] * n_after,
        out_specs=[_SEM] * (2 * ng) + [_HBM] * (2 * nt) + [pl.BlockSpec(memory_space=pltpu.VMEM)],
        out_shape=sem_shapes + [pltpu.HBM(a.shape, a.dtype) for a in flat + lands] + [sds((SUBLANES, LANES))],
        input_output_aliases={i: 2 * ng + i for i in range(2 * nt)},
        compiler_params=pltpu.CompilerParams(has_side_effects=_DATAFLOW),
    )(*[pltpu.with_memory_space_constraint(a, pltpu.HBM) for a in flat + lands + ([] if after is None else [after])])
    handles = []
    t0 = 0
    for gi, g in enumerate(groups):
        n = len(g)
        handles.append(dict(sems=(res[2 * gi], res[2 * gi + 1]), src=res[2 * ng + t0:2 * ng + t0 + n],
                            land=res[2 * ng + nt + t0:2 * ng + nt + t0 + n], scatter=scatter))
        t0 += n
    return handles, res[-1]


def comm_wait(handle, after, name):
    src, land, scatter = handle["src"], handle["land"], handle["scatter"]
    nt = len(src)

    def body(*refs):
        src_r, land_r = refs[:nt], refs[nt:2 * nt]
        send_sems, recv_sems = refs[2 * nt], refs[2 * nt + 1]
        x, y, c = _me()
        my = 4 * x + 2 * y + c
        for k, peer, slot in _peers():
            for t in range(nt):
                _split_copy(src_r[t], land_r[t], send_sems, recv_sems, t, k, peer, slot, my, scatter, False).wait_send()
                _split_copy(src_r[t], land_r[t], send_sems, recv_sems, t, k, peer, slot, my, scatter, True).wait_recv()

    after = list(after) if isinstance(after, (list, tuple)) else [after]
    res = pl.pallas_call(
        body, name=name, in_specs=[_HBM] * (2 * nt) + [_SEM, _SEM] + [_HBM] * len(after), out_specs=[_HBM] * (2 * nt),
        out_shape=[pltpu.HBM(a.shape, a.dtype) for a in list(src) + list(land)],
        input_output_aliases={i: i for i in range(2 * nt)},
        compiler_params=pltpu.CompilerParams(has_side_effects=_DATAFLOW),
    )(*src, *land, *handle["sems"], *[pltpu.with_memory_space_constraint(a, pltpu.HBM) for a in after])
    return res[:nt], res[nt:]


def sum_adamw_shard(own_src, land, me, w, m, v, l, prev, name, tr=256):
    L, R, C = w.shape
    by_rows = R % SUBLANES == 0 or C % LANES != 0
    tr, tc = (_pick(R, tr, SUBLANES), C) if by_rows else (R, _pick(C, 256))
    steps = R // tr if by_rows else C // tc
    c1 = 1.0 - ADAM_B1 ** ADAM_STEP
    c2 = 1.0 - ADAM_B2 ** ADAM_STEP
    n_prev = 0 if prev is None else 4

    def at(lead, i):
        return (lead, i, 0) if by_rows else (lead, 0, i)

    def body(me_ref, *refs):
        parts = refs[:N_DEV]
        w_r, m_r, v_r = refs[N_DEV:N_DEV + 3]
        g_o, d_o, m_o, v_o = refs[N_DEV + 3 + n_prev:]
        g = parts[0][0].astype(F32)
        for k in range(1, N_DEV):
            g = g + parts[k][0].astype(F32)
        mn = ADAM_B1 * m_r[0] + (1.0 - ADAM_B1) * g
        vn = ADAM_B2 * v_r[0] + (1.0 - ADAM_B2) * (g * g)
        g_o[0] = g
        d_o[0] = -ADAM_LR * ((mn / c1) / (jnp.sqrt(vn / c2) + ADAM_EPS) + ADAM_WD * w_r[0])
        m_o[0] = mn
        v_o[0] = vn

    part_specs = [pl.BlockSpec((1, tr, tc), lambda i, me, k=k: at(me[0] ^ k, i)) for k in range(N_DEV)]
    lay = pl.BlockSpec((1, tr, tc), lambda i, me: at(l, i))
    grid_spec = pltpu.PrefetchScalarGridSpec(
        num_scalar_prefetch=1, grid=(steps,), in_specs=part_specs + [lay] * 3 + [_ANY] * n_prev, out_specs=[lay] * 4)
    return pl.pallas_call(
        body, name=name, grid_spec=grid_spec, out_shape=[sds((L, R, C))] * 4,
        input_output_aliases={1 + N_DEV + 3 + j: j for j in range(n_prev)}, compiler_params=_params(("parallel",)),
    )(me, own_src, *[land] * (N_DEV - 1), w, m, v, *([] if prev is None else prev))


def sum_adamw(parts, w, m, v, name, tr=256):
    _, R, C = parts.shape
    tr = _pick(R, tr, SUBLANES)
    c1 = 1.0 - ADAM_B1 ** ADAM_STEP
    c2 = 1.0 - ADAM_B2 ** ADAM_STEP

    def body(p, w, m, v, g_o, d_o, m_o, v_o):
        g = p[0].astype(F32)
        for d in range(1, N_DEV):
            g = g + p[d].astype(F32)
        mn = ADAM_B1 * m[...] + (1.0 - ADAM_B1) * g
        vn = ADAM_B2 * v[...] + (1.0 - ADAM_B2) * (g * g)
        m_hat = mn / c1
        v_hat = vn / c2
        g_o[...] = g
        d_o[...] = -ADAM_LR * (m_hat / (jnp.sqrt(v_hat) + ADAM_EPS) + ADAM_WD * w[...])
        m_o[...] = mn
        v_o[...] = vn

    blk = pl.BlockSpec((tr, C), lambda i: (i, 0))
    return pl.pallas_call(
        body, name=name, grid=(R // tr,), in_specs=[pl.BlockSpec((N_DEV, tr, C), lambda i: (0, i, 0)), blk, blk, blk],
        out_specs=[blk] * 4, out_shape=[sds((R, C))] * 4, compiler_params=_params(("parallel",)),
    )(parts, w, m, v)


SHARDED = ("w_in", "dn_conv_w", "w_branch_a", "w_branch_b", "w_out", "ffn_w_gate", "ffn_w_up", "ffn_conv_w", "ffn_w_down")
TRANSPOSED = ("w_in", "ffn_w_gate", "ffn_w_up")
COL_SHARDED = ("dn_conv_w", "w_branch_a", "w_branch_b", "ffn_conv_w")
CONV_WEIGHTS = ("dn_conv_w", "ffn_conv_w")
REPLICATED = ("norm1_g", "dn_a_log", "dn_dt_bias", "dn_onorm_g", "sg_ln_g", "sg_ln_b", "sg_w", "sg_b", "norm2_g",
              "ffn_conv_b", "final_norm_g")
WEIGHTS = ("norm1_g", "w_in", "dn_conv_w", "dn_a_log", "dn_dt_bias", "dn_onorm_g", "sg_ln_g", "sg_ln_b", "sg_w", "sg_b",
           "w_branch_a", "w_branch_b", "w_out", "norm2_g", "ffn_w_gate", "ffn_w_up", "ffn_conv_w", "ffn_conv_b",
           "ffn_w_down", "final_norm_g")


def _columns(pieces, lo, hi):
    out = []
    for a, start, width in pieces:
        s, e = max(lo, start), min(hi, start + width)
        if s < e:
            out.append(a[:, s - start:e - start])
    return out[0] if len(out) == 1 else jnp.concatenate(out, axis=1)


def _assemble(name, g):
    if name in COL_SHARDED:
        return jnp.concatenate([g[d] for d in range(N_DEV)], axis=1)
    return g.reshape(N_DEV * g.shape[1], g.shape[2])


def _split(name, pieces, dtype):
    total = sum(w for _, _, w in pieces)
    if name in COL_SHARDED:
        cs = total // N_DEV
        return jnp.stack([_columns(pieces, d * cs, (d + 1) * cs).astype(dtype) for d in range(N_DEV)])
    a = pieces[0][0] if len(pieces) == 1 else jnp.concatenate([p[:w] for p, _, w in pieces], axis=0)
    return a.reshape(N_DEV, a.shape[0] // N_DEV, a.shape[1]).astype(dtype)


def _pad_lanes(a, lo, width=LANES):
    return jnp.pad(a, ((0, 0), (lo, width - lo - a.shape[1])))


def _pad_rows(a, rows=SUBLANES):
    return jnp.pad(a, ((0, rows - a.shape[0]), (0, 0)))


def kernel(x, norm1_g, w_in, dn_conv_w, dn_a_log, dn_dt_bias, dn_onorm_g, sg_ln_g, sg_ln_b, sg_w, sg_b, w_branch_a, w_branch_b, w_out, norm2_g, ffn_w_gate, ffn_w_up, ffn_conv_w, ffn_conv_b, ffn_w_down, final_norm_g, loss_target, m_norm1_g, m_w_in, m_dn_conv_w, m_dn_a_log, m_dn_dt_bias, m_dn_onorm_g, m_sg_ln_g, m_sg_ln_b, m_sg_w, m_sg_b, m_w_branch_a, m_w_branch_b, m_w_out, m_norm2_g, m_ffn_w_gate, m_ffn_w_up, m_ffn_conv_w, m_ffn_conv_b, m_ffn_w_down, m_final_norm_g, v_norm1_g, v_w_in, v_dn_conv_w, v_dn_a_log, v_dn_dt_bias, v_dn_onorm_g, v_sg_ln_g, v_sg_ln_b, v_sg_w, v_sg_b, v_w_branch_a, v_w_branch_b, v_w_out, v_norm2_g, v_ffn_w_gate, v_ffn_w_up, v_ffn_conv_w, v_ffn_conv_b, v_ffn_w_down, v_final_norm_g):
    W = dict(norm1_g=norm1_g, w_in=w_in, dn_conv_w=dn_conv_w, dn_a_log=dn_a_log, dn_dt_bias=dn_dt_bias, dn_onorm_g=dn_onorm_g,
             sg_ln_g=sg_ln_g, sg_ln_b=sg_ln_b, sg_w=sg_w, sg_b=sg_b, w_branch_a=w_branch_a, w_branch_b=w_branch_b, w_out=w_out,
             norm2_g=norm2_g, ffn_w_gate=ffn_w_gate, ffn_w_up=ffn_w_up, ffn_conv_w=ffn_conv_w, ffn_conv_b=ffn_conv_b,
             ffn_w_down=ffn_w_down, final_norm_g=final_norm_g)
    Mo = dict(norm1_g=m_norm1_g, w_in=m_w_in, dn_conv_w=m_dn_conv_w, dn_a_log=m_dn_a_log, dn_dt_bias=m_dn_dt_bias,
              dn_onorm_g=m_dn_onorm_g, sg_ln_g=m_sg_ln_g, sg_ln_b=m_sg_ln_b, sg_w=m_sg_w, sg_b=m_sg_b, w_branch_a=m_w_branch_a,
              w_branch_b=m_w_branch_b, w_out=m_w_out, norm2_g=m_norm2_g, ffn_w_gate=m_ffn_w_gate, ffn_w_up=m_ffn_w_up,
              ffn_conv_w=m_ffn_conv_w, ffn_conv_b=m_ffn_conv_b, ffn_w_down=m_ffn_w_down, final_norm_g=m_final_norm_g)
    Vo = dict(norm1_g=v_norm1_g, w_in=v_w_in, dn_conv_w=v_dn_conv_w, dn_a_log=v_dn_a_log, dn_dt_bias=v_dn_dt_bias,
              dn_onorm_g=v_dn_onorm_g, sg_ln_g=v_sg_ln_g, sg_ln_b=v_sg_ln_b, sg_w=v_sg_w, sg_b=v_sg_b, w_branch_a=v_w_branch_a,
              w_branch_b=v_w_branch_b, w_out=v_w_out, norm2_g=v_norm2_g, ffn_w_gate=v_ffn_w_gate, ffn_w_up=v_ffn_w_up,
              ffn_conv_w=v_ffn_conv_w, ffn_conv_b=v_ffn_conv_b, ffn_w_down=v_ffn_w_down, final_norm_g=v_final_norm_g)

    xs = x[0]
    tgt = loss_target[0]
    T, D = xs.shape
    depth = norm1_g.shape[0]
    H = dn_a_log.shape[1]
    G = sg_w.shape[1]
    WA = H * HEAD_DIM
    WB = G * HEAD_DIM
    N = T // DN_CHUNK
    colA = 4 * WA
    colB0 = colA + 2 * H
    cb_a = (2 * WB) // D

    my = 4 * lax.axis_index("x") + 2 * lax.axis_index("y") + lax.axis_index("c")
    me_arr = my.astype(jnp.int32).reshape(1)

    def view(d):
        return {n: (jnp.transpose(d[n], (0, 2, 1)) if n in TRANSPOSED else d[n]) for n in SHARDED}

    Wv, Mv, Vv = view(W), view(Mo), view(Vo)

    def shard(n, l):
        return Wv[n][l] if n in CONV_WEIGHTS else Wv[n][l].astype(BF16)

    first = [("w_in", 0), ("dn_conv_w", 0)]
    first_blocks = all_gather([shard(n, l) for n, l in first], "gather_first")
    gathered = dict(zip(first, first_blocks))
    gather_names = [[("w_branch_a", 0), ("w_branch_b", 0), ("w_out", 0)]]
    for l in range(depth):
        if l > 0:
            gather_names.append([("w_in", l), ("dn_conv_w", l), ("w_branch_a", l), ("w_branch_b", l), ("w_out", l)])
        gather_names.append([("ffn_w_gate", l), ("ffn_w_up", l), ("ffn_conv_w", l), ("ffn_w_down", l)])
    gather_handles, gather_tok = comm_start([[shard(n, l) for n, l in g] for g in gather_names], False, "gather_start",
                                            after=first_blocks[0])

    def need(n, l, after):
        if (n, l) not in gathered:
            gi = [i for i, g in enumerate(gather_names) if (n, l) in g][0]
            src, land = comm_wait(gather_handles[gi], after, f"gather_wait{gi}")
            for key, s, ld in zip(gather_names[gi], src, land):
                gathered[key] = lax.dynamic_update_index_in_dim(ld, s, my, 0)
        return gathered[(n, l)]

    def full(n, l, after):
        return _assemble(n, need(n, l, after))

    def layer_weights(l):
        return dict(
            g1=norm1_g[l][None], g2=norm2_g[l][None], alog=_pad_lanes(dn_a_log[l][None], H), dtb=_pad_lanes(dn_dt_bias[l][None], H),
            og=dn_onorm_g[l][None], lng=sg_ln_g[l][None], lnb=sg_ln_b[l][None], sgw=sg_w[l], sgbT=sg_b[l].T, fcb=ffn_conv_b[l][None])

    def mixer_in_weights(p, l, after):
        wt = full("w_in", l, after)
        p.update(wA=wt[:colA], wba=_pad_rows(wt[colA:colB0], LANES), wB=wt[colB0:], cw8=_pad_rows(full("dn_conv_w", l, after)))

    def mixer_out_weights(p, l, after):
        p.update(wa=full("w_branch_a", l, after), wb=full("w_branch_b", l, after), wo=full("w_out", l, after))

    def ffn_weights(p, l, after):
        p.update(wg=full("ffn_w_gate", l, after), wu=full("ffn_w_up", l, after), fcw8=_pad_rows(full("ffn_conv_w", l, after)),
                 wd=full("ffn_w_down", l, after))

    saved = []
    cur = xs
    for l in range(depth):
        p = layer_weights(l)
        t = f"l{l}_"
        h = norm_fwd(cur, p["g1"] + gather_tok[0, 0] if l == 0 else p["g1"], t + "norm1")
        mixer_in_weights(p, l, h)
        projA = matmul(h, p["wA"], "nt", t + "projA")
        pba = matmul(h, p["wba"], "nt", t + "proj_ba")
        projB = matmul(h, p["wB"], "nt", t + "projB")
        q, k, v, bg = dn_prep_fwd(projA, pba, p["cw8"], p["alog"], p["dtb"], H, t + "dn_prep")
        u, w, a, qd, kd, gl, tinv = dn_chunk_fwd(q, k, v, bg, t + "dn_chunk")
        o, s_in = dn_scan_fwd(u, w, a, qd, kd, gl, t + "dn_scan")
        y_a = dn_post_fwd(o, projA, p["og"], t + "dn_post")
        y_b = gmlp_fwd(projB, p["lng"], p["lnb"], p["sgw"], p["sgbT"], t + "gmlp")
        mixer_out_weights(p, l, y_b)
        ap = matmul(y_a, p["wa"], "nn", t + "branch_a")
        bp = matmul(y_b, p["wb"], "nn", t + "branch_b")
        merged = merge_fwd(projB, ap, bp, cb_a, t + "merge")
        x1 = matmul(merged, p["wo"], "nn", t + "out_proj", c=cur)
        h2 = norm_fwd(x1, p["g2"], t + "norm2")
        ffn_weights(p, l, h2)
        gp = matmul(h2, p["wg"], "nt", t + "ffn_gate")
        up = matmul(h2, p["wu"], "nt", t + "ffn_up")
        act = ffn_act_fwd(gp, up, p["fcw8"], p["fcb"], t + "ffn_act")
        x2 = matmul(act, p["wd"], "nn", t + "ffn_down", c=x1)
        saved.append(dict(p=p, x0=cur, h=h, projA=projA, pba=pba, projB=projB, q=q, k=k, v=v, bg=bg, tinv=tinv,
                          scan=(u, w, a, qd, kd, gl), s_in=s_in, o=o, y_a=y_a, y_b=y_b, ap=ap, bp=bp, merged=merged, x1=x1,
                          h2=h2, gp=gp, up=up, act=act))
        cur = x2

    loss_part, dx, dx_bf, d_final = head_fwd_bwd(cur, final_norm_g[None], tgt, "loss_head")
    loss = lax.psum(loss_part[0, 0], ("x", "y", "c"))

    grads_sh = {n: [None] * depth for n in SHARDED}
    grads_rep = {n: [None] * depth for n in REPLICATED if n != "final_norm_g"}
    exchanges = []

    def exchange(names, l, name, after=None):
        srcs = [_split(n, grads_sh[n][l], F32 if n in CONV_WEIGHTS else BF16) for n in names]
        (handle,), tok = comm_start([srcs], True, name, after=after)
        exchanges.append((names, l, handle))
        return tok

    def whole(a):
        return [(a, 0, a.shape[1])]

    sizes = [math.prod(W[n].shape) for n in REPLICATED]
    tile = SUBLANES * LANES
    nrows = [-(-sz // tile) * SUBLANES for sz in sizes]

    def pack(d):
        parts = [jnp.pad(d[n].reshape(-1).astype(F32), (0, r * LANES - sz)).reshape(r, LANES)
                 for n, sz, r in zip(REPLICATED, sizes, nrows)]
        return jnp.concatenate(parts, axis=0)

    mixer_tok = None
    for l in reversed(range(depth)):
        s = saved[l]
        p = s["p"]
        t = f"l{l}_b_"
        dact = matmul(dx_bf, p["wd"], "nt", t + "d_act")
        grads_sh["ffn_w_down"][l] = whole(matmul(s["act"], dx_bf, "tn", t + "dw_down", out_dtype=BF16, tn=2048))
        fcb = p["fcb"] if mixer_tok is None else p["fcb"] + mixer_tok[0, 0]
        dgp, dup, dfcw, dfcb = ffn_act_bwd(s["gp"], s["up"], p["fcw8"], fcb, dact, t + "ffn_act")
        dh2 = matmul([dgp, dup], [p["wg"], p["wu"]], "nn", t + "dh2")
        grads_sh["ffn_w_gate"][l] = whole(matmul(dgp, s["h2"], "tn", t + "dw_gate", out_dtype=BF16, tn=2048))
        grads_sh["ffn_w_up"][l] = whole(matmul(dup, s["h2"], "tn", t + "dw_up", out_dtype=BF16, tn=2048))
        grads_sh["ffn_conv_w"][l] = whole(dfcw[:3])
        grads_rep["ffn_conv_b"][l] = dfcb[0]
        tok = exchange(("ffn_w_down", "ffn_w_gate", "ffn_w_up", "ffn_conv_w"), l, t + "ffn_grads_start")
        dx1, dx1_bf, dg2 = norm_bwd(s["x1"], p["g2"] + tok[0, 0], dh2, dx, t + "norm2")
        grads_rep["norm2_g"][l] = dg2[0]
        dmerged = matmul(dx1_bf, p["wo"], "nt", t + "d_merged")
        grads_sh["w_out"][l] = whole(matmul(s["merged"], dx1_bf, "tn", t + "dw_out", out_dtype=BF16, tn=2048))
        dga, dgb, dap, dbp = merge_bwd(s["projB"], s["ap"], s["bp"], dmerged, cb_a, t + "merge")
        dya = matmul(dap, p["wa"], "nt", t + "d_ya")
        dyb = matmul(dbp, p["wb"], "nt", t + "d_yb")
        grads_sh["w_branch_a"][l] = whole(matmul(s["y_a"], dap, "tn", t + "dw_a", out_dtype=BF16))
        grads_sh["w_branch_b"][l] = whole(matmul(s["y_b"], dbp, "tn", t + "dw_b", out_dtype=BF16))
        du_raw, dv_raw, dlng, dlnb, dsgw, dsgbT = gmlp_bwd(s["projB"], p["lng"], p["lnb"], p["sgw"], p["sgbT"], dyb, t + "gmlp")
        grads_rep["sg_ln_g"][l], grads_rep["sg_ln_b"][l] = dlng[0], dlnb[0]
        grads_rep["sg_w"][l], grads_rep["sg_b"][l] = dsgw, dsgbT.T
        do, dz, dog = dn_post_bwd(s["o"], s["projA"], p["og"], dya, t + "dn_post")
        grads_rep["dn_onorm_g"][l] = dog[0]
        du, dw, da, dqd, dkd, dgl = dn_scan_bwd(*s["scan"], s["s_in"], do, t + "dn_scan")
        dq, dk, dv, dbg = dn_chunk_bwd(s["q"], s["k"], s["v"], s["bg"], s["tinv"], du, dw, da, dqd, dkd, dgl, t + "dn_chunk")
        dqkv, dba, dcw, dalog, ddtb = dn_prep_bwd(s["projA"], s["pba"], p["cw8"], p["alog"], p["dtb"], dq, dk, dv, dbg, H,
                                                  t + "dn_prep")
        grads_sh["dn_conv_w"][l] = whole(dcw[:4])
        grads_rep["dn_a_log"][l], grads_rep["dn_dt_bias"][l] = dalog[0, H:2 * H], ddtb[0, H:2 * H]
        tok = exchange(("w_out", "w_branch_a", "w_branch_b", "dn_conv_w"), l, t + "mixer_grads_start")
        dba = dba + tok[0, 0].astype(BF16)
        dprojA = jnp.concatenate([dqkv, dz], axis=1)
        dprojB = jnp.concatenate([du_raw, dv_raw, dga, dgb], axis=1)
        dwA = matmul(dprojA, s["h"], "tn", t + "dw_A", out_dtype=BF16, tn=2048)
        dwba = matmul(dba, s["h"], "tn", t + "dw_ba", out_dtype=BF16, tn=2048)
        dwB = matmul(dprojB, s["h"], "tn", t + "dw_B", out_dtype=BF16, tn=2048)
        grads_sh["w_in"][l] = [(dwA, 0, colA), (dwba, colA, 2 * H), (dwB, colB0, dwB.shape[0])]
        mixer_tok = exchange(("w_in",), l, t + "w_in_grads_start")
        dh = matmul([dba, dprojA, dprojB], [p["wba"] + mixer_tok[0, 0].astype(BF16), p["wA"], p["wB"]], "nn", t + "dh")
        dx, dx_bf, dg1 = norm_bwd(s["x0"], p["g1"], dh, dx1, t + "norm1")
        grads_rep["norm1_g"][l] = dg1[0]
        if l == 0:
            rep_full = {n: (jnp.stack(grads_rep[n]) if n != "final_norm_g" else d_final[0]) for n in REPLICATED}
            (small_handle,), small_tok = comm_start([[pack(rep_full)]], False, "small_grads_start")

    out = {}
    after = [dx, small_tok]

    def update_group(gi, after):
        names, l, handle = exchanges[gi]
        src, land = comm_wait(handle, after, f"grads_wait{gi}")
        done = []
        for n, s_, ld in zip(names, src, land):
            res = sum_adamw_shard(s_, ld, me_arr, Wv[n], Mv[n], Vv[n], l, out.get(n), f"adamw_{n}_{l}")
            out[n] = list(res)
            done.append(res[0])
        return done

    for gi in range(len(exchanges) - 1):
        after = update_group(gi, after)

    (small_src,), (small_land,) = comm_wait(small_handle, after, "small_grads_wait")
    rep_parts = lax.dynamic_update_index_in_dim(small_land, small_src, my, 0)
    res = sum_adamw(rep_parts, pack(W), pack(Mo), pack(Vo), "adamw_small")
    update_group(len(exchanges) - 1, after + [res[0]])
    for n in TRANSPOSED:
        out[n] = [jnp.transpose(r, (0, 2, 1)) for r in out[n]]
    row0 = 0
    for n, sz, nr in zip(REPLICATED, sizes, nrows):
        out[n] = [r[row0:row0 + nr].reshape(-1)[:sz].reshape(W[n].shape) for r in res]
        row0 += nr

    return (loss, dx[None], *[out[n][0] for n in WEIGHTS], *[out[n][1] for n in WEIGHTS],
            *[out[n][2] for n in WEIGHTS], *[out[n][3] for n in WEIGHTS])
```

```python
import functools
import math

import jax
import jax.numpy as jnp
from jax import lax
from jax.experimental import pallas as pl
from jax.experimental.pallas import tpu as pltpu

F32 = jnp.float32
BF16 = jnp.bfloat16
EPS = 1e-6
N_DEV = 8
LANES = 128
SUBLANES = 8
HEAD_DIM = 128
DN_CHUNK = 64
SG_CHUNK = 128
VMEM_LIMIT = 56 * 1024 * 1024
MESH = pl.DeviceIdType.MESH
HIGHEST = lax.Precision.HIGHEST

ADAM_LR = 0.001
ADAM_B1 = 0.9
ADAM_B2 = 0.999
ADAM_EPS = 1e-08
ADAM_WD = 0.01
ADAM_STEP = 10


def _pick(n, target, mult=LANES):
    best = None
    d = mult
    while d <= min(n, target):
        if n % d == 0:
            best = d
        d += mult
    return n if best is None else best


def _params(sem):
    return pltpu.CompilerParams(dimension_semantics=sem, vmem_limit_bytes=VMEM_LIMIT)


_NN = (((1,), (0,)), ((), ()))
_NT = (((1,), (1,)), ((), ()))
_TN = (((0,), (0,)), ((), ()))


def _dg(a, b, dims, hi):
    if hi == 2:
        return lax.dot_general(a.astype(F32), b.astype(F32), dims, precision=HIGHEST, preferred_element_type=F32)
    if hi == 1:
        a_hi, b_hi = a.astype(BF16), b.astype(BF16)
        a_lo, b_lo = (a - a_hi.astype(F32)).astype(BF16), (b - b_hi.astype(F32)).astype(BF16)
        ax, bx = dims[0][0][0], dims[0][1][0]
        a = jnp.concatenate([a_hi, a_hi, a_lo], axis=ax)
        b = jnp.concatenate([b_hi, b_lo, b_hi], axis=bx)
        return lax.dot_general(a, b, dims, preferred_element_type=F32)
    return lax.dot_general(a.astype(BF16), b.astype(BF16), dims, preferred_element_type=F32)


@functools.partial(jax.custom_vjp, nondiff_argnums=(2,))
def mm_nn(a, b, hi=False):
    return _dg(a, b, _NN, hi)


def _mm_nn_f(a, b, hi):
    return _dg(a, b, _NN, hi), (a, b)


def _mm_nn_b(hi, res, g):
    a, b = res
    return mm_nt(g, b, hi), mm_tn(a, g, hi)


@functools.partial(jax.custom_vjp, nondiff_argnums=(2,))
def mm_nt(a, b, hi=False):
    return _dg(a, b, _NT, hi)


def _mm_nt_f(a, b, hi):
    return _dg(a, b, _NT, hi), (a, b)


def _mm_nt_b(hi, res, g):
    a, b = res
    return mm_nn(g, b, hi), mm_tn(g, a, hi)


@functools.partial(jax.custom_vjp, nondiff_argnums=(2,))
def mm_tn(a, b, hi=False):
    return _dg(a, b, _TN, hi)


def _mm_tn_f(a, b, hi):
    return _dg(a, b, _TN, hi), (a, b)


def _mm_tn_b(hi, res, g):
    a, b = res
    return mm_nt(b, g, hi), mm_nn(a, g, hi)


mm_nn.defvjp(_mm_nn_f, _mm_nn_b)
mm_nt.defvjp(_mm_nt_f, _mm_nt_b)
mm_tn.defvjp(_mm_tn_f, _mm_tn_b)


def matmul(a, b, mode, name, c=None, out_dtype=F32, tm=1024, tn=1024, tk=2048):
    a_list = list(a) if isinstance(a, (list, tuple)) else [a]
    b_list = list(b) if isinstance(b, (list, tuple)) else [b]
    nterm = len(a_list)

    def dims_of(a, b):
        if mode == "nn":
            return a.shape[0], a.shape[1], b.shape[1]
        if mode == "nt":
            return a.shape[0], a.shape[1], b.shape[0]
        return a.shape[1], a.shape[0], b.shape[1]

    M, _, N = dims_of(a_list[0], b_list[0])
    tm, tn = _pick(M, tm), _pick(N, tn)
    tks = [_pick(dims_of(x, y)[1], tk) for x, y in zip(a_list, b_list)]
    nks = [dims_of(x, y)[1] // t for x, y, t in zip(a_list, b_list, tks)]
    offs = [sum(nks[:t]) for t in range(nterm)]
    nk = sum(nks)
    dims = {"nn": _NN, "nt": _NT, "tn": _TN}[mode]

    def specs_of(t):
        kk = lambda k: jnp.clip(k - offs[t], 0, nks[t] - 1)
        a_spec = (pl.BlockSpec((tks[t], tm), lambda i, j, k: (kk(k), i)) if mode == "tn"
                  else pl.BlockSpec((tm, tks[t]), lambda i, j, k: (i, kk(k))))
        b_spec = (pl.BlockSpec((tn, tks[t]), lambda i, j, k: (j, kk(k))) if mode == "nt"
                  else pl.BlockSpec((tks[t], tn), lambda i, j, k: (kk(k), j)))
        return [a_spec, b_spec]

    o_spec = pl.BlockSpec((tm, tn), lambda i, j, k: (i, j))
    has_c = c is not None
    own_acc = nk > 1 and out_dtype != F32

    def body(*refs):
        ab = refs[:2 * nterm]
        c_ref = refs[2 * nterm] if has_c else None
        o_ref = refs[2 * nterm + (1 if has_c else 0)]
        acc_ref = refs[-1] if own_acc else o_ref

        def dot(t):
            return lax.dot_general(ab[2 * t][...].astype(BF16), ab[2 * t + 1][...].astype(BF16), dims,
                                   preferred_element_type=F32)

        if nk == 1:
            o_ref[...] = (dot(0) + c_ref[...] if has_c else dot(0)).astype(o_ref.dtype)
        else:
            k = pl.program_id(2)

            @pl.when(k == 0)
            def _():
                acc_ref[...] = c_ref[...] if has_c else jnp.zeros_like(acc_ref)

            for t in range(nterm):
                if nterm == 1:
                    acc_ref[...] += dot(t)
                else:
                    @pl.when((k >= offs[t]) & (k < offs[t] + nks[t]))
                    def _(t=t):
                        acc_ref[...] += dot(t)

            if own_acc:
                @pl.when(k == nk - 1)
                def _():
                    o_ref[...] = acc_ref[...].astype(o_ref.dtype)

    ins, specs = [], []
    for t in range(nterm):
        ins += [a_list[t], b_list[t]]
        specs += specs_of(t)
    if has_c:
        ins.append(c)
        specs.append(o_spec)
    return pl.pallas_call(
        body, name=name, grid=(M // tm, N // tn, nk), in_specs=specs, out_specs=o_spec,
        out_shape=jax.ShapeDtypeStruct((M, N), out_dtype), scratch_shapes=[pltpu.VMEM((tm, tn), F32)] if own_acc else [],
        compiler_params=_params(("parallel", "parallel", "arbitrary")),
    )(*ins)


def rowcall(name, fn, ins, in_specs, outs, out_specs, acc, nrow, ncol=1, scratch=()):
    n_in, n_out = len(ins), len(outs)

    def body(*refs):
        i = pl.program_id(1)
        res = fn(i, *[r[...] for r in refs[:n_in]], *refs[n_in + n_out:])
        for r, v, is_acc in zip(refs[n_in:n_in + n_out], res, acc):
            if is_acc:
                @pl.when(i == 0)
                def _(r=r, v=v):
                    r[...] = v.astype(r.dtype)

                @pl.when(i > 0)
                def _(r=r, v=v):
                    r[...] += v.astype(r.dtype)
            else:
                r[...] = v.astype(r.dtype)

    return pl.pallas_call(
        body, name=name, grid=(ncol, nrow), in_specs=list(in_specs), out_specs=list(out_specs), out_shape=list(outs),
        scratch_shapes=list(scratch), compiler_params=_params(("parallel", "arbitrary")),
    )(*ins)


class Tiles:
    def __init__(self, T, tm):
        self.T, self.tm, self.n = T, tm, T // tm
        self.r8 = tm // SUBLANES

    def row(self, w, cb=0):
        return pl.BlockSpec((self.tm, w), lambda j, i: (i, cb))

    def rowj(self, tc):
        return pl.BlockSpec((self.tm, tc), lambda j, i: (i, j))

    def prev(self, w, cb=0):
        return pl.BlockSpec((SUBLANES, w), lambda j, i: (jnp.maximum(i * self.r8 - 1, 0), cb))

    def prevj(self, tc):
        return pl.BlockSpec((SUBLANES, tc), lambda j, i: (jnp.maximum(i * self.r8 - 1, 0), j))

    def nxt(self, w, cb=0):
        last = self.T // SUBLANES - 1
        return pl.BlockSpec((SUBLANES, w), lambda j, i: (jnp.minimum((i + 1) * self.r8, last), cb))

    def nxtj(self, tc):
        last = self.T // SUBLANES - 1
        return pl.BlockSpec((SUBLANES, tc), lambda j, i: (jnp.minimum((i + 1) * self.r8, last), j))

    def heads(self, H):
        return pl.BlockSpec((H, self.tm, HEAD_DIM), lambda j, i: (0, i, 0))

    def heads_nxt(self, H):
        last = self.T // SUBLANES - 1
        return pl.BlockSpec((H, SUBLANES, HEAD_DIM), lambda j, i: (0, jnp.minimum((i + 1) * self.r8, last), 0))


def full(shape):
    return pl.BlockSpec(tuple(shape), lambda j, i: (0,) * len(shape))


def constj(r, tc):
    return pl.BlockSpec((r, tc), lambda j, i: (0, j))


def sds(shape, dtype=F32):
    return jax.ShapeDtypeStruct(tuple(shape), dtype)


def rms(x, g):
    return x * lax.rsqrt(jnp.mean(x * x, axis=-1, keepdims=True) + EPS) * g


def sigmoid(x):
    return jax.nn.sigmoid(x)


def silu(x):
    return x * sigmoid(x)


def gelu(x):
    return 0.5 * x * (1.0 + lax.erf(x * (2.0 ** -0.5)))


def fill_window(win, i, last, prev, x, nxt=None):
    R = x.shape[0]
    win[0:SUBLANES, :] = jnp.where(i > 0, prev, 0.0)
    win[SUBLANES:SUBLANES + R, :] = x
    if nxt is not None:
        win[SUBLANES + R:2 * SUBLANES + R, :] = jnp.where(last, 0.0, nxt)


def conv_taps(win, K, R):
    base = SUBLANES - (K - 1)
    return [win[pl.ds(base + j, R), :] for j in range(K)]


def value_taps(xwin, K, R):
    base = SUBLANES - (K - 1)
    return [xwin[base + j:base + j + R, :] for j in range(K)]


def causal_conv(taps, w):
    out = w[0:1, :] * taps[0]
    for j in range(1, len(taps)):
        out = out + w[j:j + 1, :] * taps[j]
    return out


def rows_to8(rows, C):
    rid = lax.broadcasted_iota(jnp.int32, (SUBLANES, C), 0)
    out = jnp.zeros((SUBLANES, C), F32)
    for k, r in enumerate(rows):
        out = out + jnp.where(rid == k, jnp.broadcast_to(r, (SUBLANES, C)), 0.0)
    return out


def dn_qkv(pre, H):
    a = silu(pre)
    W = H * HEAD_DIM

    def l2(t):
        return t * lax.rsqrt(jnp.sum(t * t, axis=-1, keepdims=True) + EPS)

    q = [l2(a[:, h * HEAD_DIM:(h + 1) * HEAD_DIM]) for h in range(H)]
    k = [l2(a[:, W + h * HEAD_DIM:W + (h + 1) * HEAD_DIM]) for h in range(H)]
    v = [a[:, 2 * W + h * HEAD_DIM:2 * W + (h + 1) * HEAD_DIM] for h in range(H)]
    return q, k, v


def dn_gates(ba, alog, dtb, H, R):
    lane = lax.broadcasted_iota(jnp.int32, (R, LANES), 1)
    beta = sigmoid(ba)
    g = -jnp.exp(alog) * jax.nn.softplus(ba + dtb)
    g = jnp.where((lane >= H) & (lane < 2 * H), g, 0.0)
    ri = lax.broadcasted_iota(jnp.int32, (R, R), 0)
    ci = lax.broadcasted_iota(jnp.int32, (R, R), 1)
    cum = jnp.where((ri // DN_CHUNK == ci // DN_CHUNK) & (ci <= ri), 1.0, 0.0).astype(F32)
    gc = mm_nn(cum, g, 2)
    return jnp.where(lane < H, beta, gc)


def neumann_inverse(Ls):
    C = Ls[0].shape[0]
    ri = lax.broadcasted_iota(jnp.int32, (C, C), 0)
    ci = lax.broadcasted_iota(jnp.int32, (C, C), 1)
    eye = jnp.where(ri == ci, 1.0, 0.0).astype(F32)
    P = [-L for L in Ls]
    R = [eye + p for p in P]
    for _ in range(int(math.log2(C)) - 1):
        P = [mm_nn(p, p, 1) for p in P]
        R = [r + mm_nn(r, p, 1) for r, p in zip(R, P)]
    return R


@jax.custom_vjp
def saved_inverse(L, T):
    return T


def _saved_inverse_f(L, T):
    return T, T


def _saved_inverse_b(T, g):
    return -mm_tn(T, mm_nt(g, T, 1), 1), jnp.zeros_like(T)


saved_inverse.defvjp(_saved_inverse_f, _saved_inverse_b)


def gate_columns(bg, H):
    bgT = bg.T
    return ([bg[:, h:h + 1] for h in range(H)], [bg[:, H + h:H + h + 1] for h in range(H)],
            [bgT[H + h:H + h + 1, :] for h in range(H)])


def dn_chunk(q, k, v, beta, gc, gr, tinv=None, with_inverse=False):
    n = len(q)
    C = q[0].shape[0]
    ri = lax.broadcasted_iota(jnp.int32, (C, C), 0)
    ci = lax.broadcasted_iota(jnp.int32, (C, C), 1)
    qs = [q[h] * (HEAD_DIM ** -0.5) for h in range(n)]
    kb = [k[h] * beta[h] for h in range(n)]
    vb = [v[h] * beta[h] for h in range(n)]
    decay = [jnp.exp(jnp.where(ri >= ci, gc[h] - gr[h], -jnp.inf)) for h in range(n)]
    L = [jnp.where(ri > ci, mm_nt(kb[h], k[h]) * decay[h], 0.0) for h in range(n)]
    attn = [jnp.where(ri >= ci, mm_nt(qs[h], k[h]) * decay[h], 0.0) for h in range(n)]
    Tinv = neumann_inverse(L) if tinv is None else [saved_inverse(L[h], tinv[h]) for h in range(n)]
    eg = [jnp.exp(gc[h]) for h in range(n)]
    u = [mm_nn(Tinv[h], vb[h]) for h in range(n)]
    w = [mm_nn(Tinv[h], kb[h] * eg[h]) for h in range(n)]
    qd = [qs[h] * eg[h] for h in range(n)]
    gl = [gc[h][C - 1:C, :] for h in range(n)]
    kd = [k[h] * jnp.exp(gl[h] - gc[h]) for h in range(n)]
    return (u, w, attn, qd, kd, gl, Tinv) if with_inverse else (u, w, attn, qd, kd, gl)


def dn_step(u, w, a, qd, kd, gl, S):
    n = len(u)
    v_new = [u[h] - mm_nn(w[h], S[h]) for h in range(n)]
    o = [mm_nn(qd[h], S[h]) + mm_nn(a[h], v_new[h]) for h in range(n)]
    S_new = [S[h] * jnp.exp(gl[h]) + mm_tn(kd[h], v_new[h]) for h in range(n)]
    return o, S_new


def dn_post(o, z, g):
    H = o.shape[0]
    return jnp.concatenate([rms(o[h], g) * silu(z[:, h * HEAD_DIM:(h + 1) * HEAD_DIM]) for h in range(H)], axis=1)


def gmlp(u_raw, v_raw, ln_g, ln_b, sgw, sgbT):
    R = u_raw.shape[0]
    G = sgw.shape[0]
    nc = R // SG_CHUNK
    u = gelu(u_raw)
    vv = gelu(v_raw)
    xc = vv - jnp.mean(vv, axis=-1, keepdims=True)
    vg = xc * lax.rsqrt(jnp.mean(xc * xc, axis=-1, keepdims=True) + EPS) * ln_g + ln_b
    ri = lax.broadcasted_iota(jnp.int32, (SG_CHUNK, SG_CHUNK), 0)
    ci = lax.broadcasted_iota(jnp.int32, (SG_CHUNK, SG_CHUNK), 1)
    cols = []
    for g in range(G):
        ws = jnp.where(ri >= ci, sgw[g], 0.0)
        rhs = jnp.concatenate([vg[c * SG_CHUNK:(c + 1) * SG_CHUNK, g * HEAD_DIM:(g + 1) * HEAD_DIM] for c in range(nc)], axis=1)
        mixed = mm_nn(ws, rhs) + sgbT[:, g:g + 1]
        cols.append(jnp.concatenate([mixed[:, c * HEAD_DIM:(c + 1) * HEAD_DIM] for c in range(nc)], axis=0))
    return u * jnp.concatenate(cols, axis=1)


def merge(ga, gb, ap, bp):
    return sigmoid(ga) * ap + sigmoid(gb) * bp


def norm_fwd(x, g, name, tm=256):
    T, D = x.shape
    tl = Tiles(T, _pick(T, tm))
    (h,) = rowcall(name, lambda i, x, g: (rms(x, g),), [x, g], [tl.row(D), full((1, D))],
                   [sds((T, D), BF16)], [tl.row(D)], [False], tl.n)
    return h


def norm_bwd(x, g, dh, dres, name, tm=256):
    T, D = x.shape
    tl = Tiles(T, _pick(T, tm))

    def fn(i, x, g, dh, dres):
        _, vj = jax.vjp(rms, x, g)
        dx, dg = vj(dh.astype(F32))
        dx = dx + dres
        return dx, dx, dg

    return rowcall(name, fn, [x, g, dh, dres], [tl.row(D), full((1, D)), tl.row(D), tl.row(D)],
                   [sds((T, D)), sds((T, D), BF16), sds((1, D))], [tl.row(D), tl.row(D), full((1, D))],
                   [False, False, True], tl.n)


def head_fwd_bwd(x, g, tgt, name, tm=256):
    T, D = x.shape
    tl = Tiles(T, _pick(T, tm))

    def fn(i, x, g, tgt):
        y, vj = jax.vjp(rms, x, g)
        e = y - tgt
        loss = 0.5 * jnp.sum(jnp.mean(e * e, axis=-1, keepdims=True), axis=0, keepdims=True)
        dx, dg = vj(e * (1.0 / D))
        return loss, dx, dx, dg

    return rowcall(name, fn, [x, g, tgt], [tl.row(D), full((1, D)), tl.row(D)],
                   [sds((1, 1)), sds((T, D)), sds((T, D), BF16), sds((1, D))],
                   [full((1, 1)), tl.row(D), tl.row(D), full((1, D))], [True, False, False, True], tl.n)


def dn_prep_fwd(projA, pba, cw8, alog, dtb, H, name, tm=256):
    T = projA.shape[0]
    W3 = 3 * H * HEAD_DIM
    tl = Tiles(T, _pick(T, tm, DN_CHUNK))
    R = tl.tm

    def fn(i, xp, x, ba, cw, alog, dtb, win):
        fill_window(win, i, None, xp, x)
        q, k, v = dn_qkv(causal_conv(conv_taps(win, 4, R), cw), H)
        return jnp.stack(q), jnp.stack(k), jnp.stack(v), dn_gates(ba, alog, dtb, H, R)

    hs = sds((H, T, HEAD_DIM))
    return rowcall(name, fn, [projA, projA, pba, cw8, alog, dtb],
                   [tl.prev(W3), tl.row(W3), tl.row(LANES), full((SUBLANES, W3)), full((1, LANES)), full((1, LANES))],
                   [hs, hs, hs, sds((T, LANES))], [tl.heads(H)] * 3 + [tl.row(LANES)], [False] * 4, tl.n,
                   scratch=[pltpu.VMEM((SUBLANES + R, W3), F32)])


def dn_prep_bwd(projA, pba, cw8, alog, dtb, dq, dk, dv, dbg, H, name, tm=256):
    T = projA.shape[0]
    W3 = 3 * H * HEAD_DIM
    tl = Tiles(T, _pick(T, tm, DN_CHUNK))
    R = tl.tm
    RE = R + SUBLANES

    def fn(i, xp, x, xn, ba, cw, alog, dtb, dq, dk, dv, dqn, dkn, dvn, dbg, win, dp):
        last = i == tl.n - 1
        fill_window(win, i, last, xp, x, xn)
        taps = conv_taps(win, 4, RE)
        pre = causal_conv(taps, cw)
        ext = lambda d, dn: [jnp.concatenate([d[h], jnp.where(last, 0.0, dn[h])], axis=0) for h in range(H)]
        _, vj = jax.vjp(lambda p: dn_qkv(p, H), pre)
        (dpre,) = vj((ext(dq, dqn), ext(dk, dkn), ext(dv, dvn)))
        dp[...] = dpre
        dx = cw[3:4, :] * dpre[0:R, :]
        for j in range(3):
            dx = dx + cw[j:j + 1, :] * dp[pl.ds(3 - j, R), :]
        dcw = rows_to8([jnp.sum(dpre[0:R, :] * taps[j][0:R, :], axis=0, keepdims=True) for j in range(4)], W3)
        _, vjg = jax.vjp(lambda ba, alog, dtb: dn_gates(ba, alog, dtb, H, R), ba, alog, dtb)
        dba, dalog, ddtb = vjg(dbg)
        return dx, dba, dcw, dalog, ddtb

    return rowcall(name, fn, [projA, projA, projA, pba, cw8, alog, dtb, dq, dk, dv, dq, dk, dv, dbg],
                   [tl.prev(W3), tl.row(W3), tl.nxt(W3), tl.row(LANES), full((SUBLANES, W3)), full((1, LANES)), full((1, LANES))]
                   + [tl.heads(H)] * 3 + [tl.heads_nxt(H)] * 3 + [tl.row(LANES)],
                   [sds((T, W3), BF16), sds((T, LANES), BF16), sds((SUBLANES, W3)), sds((1, LANES)), sds((1, LANES))],
                   [tl.row(W3), tl.row(LANES), full((SUBLANES, W3)), full((1, LANES)), full((1, LANES))],
                   [False, False, True, True, True], tl.n,
                   scratch=[pltpu.VMEM((2 * SUBLANES + R, W3), F32), pltpu.VMEM((RE, W3), F32)])


def _chunk_specs(H, C):
    hs = pl.BlockSpec((H, C, HEAD_DIM), lambda n: (0, n, 0))
    col = pl.BlockSpec((H, 1, C, 1), lambda n: (0, n, 0, 0))
    rw = pl.BlockSpec((H, 1, 1, C), lambda n: (0, n, 0, 0))
    at = pl.BlockSpec((H, C, C), lambda n: (0, n, 0))
    one = pl.BlockSpec((H, 1, 1, 1), lambda n: (0, n, 0, 0))
    return hs, col, rw, at, one


def dn_chunk_fwd(q, k, v, bg, name):
    H, T, _ = q.shape
    C = DN_CHUNK
    N = T // C
    hs, _, _, at, one = _chunk_specs(H, C)
    gate = pl.BlockSpec((C, LANES), lambda n: (n, 0))

    def body(q, k, v, bg, u, w, a, qd, kd, gl, ti):
        hd = range(H)
        res = dn_chunk([q[h] for h in hd], [k[h] for h in hd], [v[h] for h in hd], *gate_columns(bg[...], H),
                       with_inverse=True)
        for h in hd:
            for ref, val in zip((u, w, a, qd, kd), res[:5]):
                ref[h] = val[h]
            gl[h, 0] = res[5][h]
            ti[h] = res[6][h]

    big = sds((H, T, HEAD_DIM))
    return pl.pallas_call(
        body, name=name, grid=(N,), in_specs=[hs, hs, hs, gate], out_specs=[hs, hs, at, hs, hs, one, at],
        out_shape=[big, big, sds((H, T, C)), big, big, sds((H, N, 1, 1)), sds((H, T, C))],
        compiler_params=_params(("parallel",)),
    )(q, k, v, bg)


def dn_chunk_bwd(q, k, v, bg, tinv, du, dw, da, dqd, dkd, dgl, name):
    H, T, _ = q.shape
    C = DN_CHUNK
    N = T // C
    hs, _, _, at, one = _chunk_specs(H, C)
    gate = pl.BlockSpec((C, LANES), lambda n: (n, 0))

    def body(q, k, v, bg, ti, du, dw, da, dqd, dkd, dgl, dq, dk, dv, dbg):
        hd = range(H)
        f = lambda q, k, v, b, gc, gr: dn_chunk(q, k, v, b, gc, gr, tinv=[ti[h] for h in hd])
        _, vj = jax.vjp(f, [q[h] for h in hd], [k[h] for h in hd], [v[h] for h in hd], *gate_columns(bg[...], H))
        res = vj(([du[h] for h in hd], [dw[h] for h in hd], [da[h] for h in hd], [dqd[h] for h in hd],
                  [dkd[h] for h in hd], [dgl[h, 0] for h in hd]))
        lane = lax.broadcasted_iota(jnp.int32, (C, LANES), 1)
        row = lax.broadcasted_iota(jnp.int32, (LANES, C), 0)
        cols = jnp.zeros((C, LANES), F32)
        rows = jnp.zeros((LANES, C), F32)
        for h in hd:
            dq[h], dk[h], dv[h] = res[0][h], res[1][h], res[2][h]
            cols = cols + jnp.where(lane == h, res[3][h], 0.0) + jnp.where(lane == H + h, res[4][h], 0.0)
            rows = rows + jnp.where(row == H + h, res[5][h], 0.0)
        dbg[...] = cols + rows.T

    big = sds((H, T, HEAD_DIM))
    return pl.pallas_call(
        body, name=name, grid=(N,), in_specs=[hs, hs, hs, gate, at, hs, hs, at, hs, hs, one],
        out_specs=[hs, hs, hs, gate], out_shape=[big, big, big, sds((T, LANES))], compiler_params=_params(("parallel",)),
    )(q, k, v, bg, tinv, du, dw, da, dqd, dkd, dgl)


def dn_scan_fwd(u, w, a, qd, kd, gl, name):
    H, T, _ = u.shape
    C = DN_CHUNK
    N = T // C
    hs, _, _, at, one = _chunk_specs(H, C)
    st = pl.BlockSpec((1, H, HEAD_DIM, HEAD_DIM), lambda n: (n, 0, 0, 0))

    def body(u, w, a, qd, kd, gl, o, s_in, S):
        @pl.when(pl.program_id(0) == 0)
        def _():
            S[...] = jnp.zeros_like(S)

        hd = range(H)
        s = [S[h] for h in hd]
        o_new, s_new = dn_step([u[h] for h in hd], [w[h] for h in hd], [a[h] for h in hd], [qd[h] for h in hd],
                               [kd[h] for h in hd], [gl[h, 0] for h in hd], s)
        for h in hd:
            s_in[0, h] = s[h]
            o[h] = o_new[h]
            S[h] = s_new[h]

    return pl.pallas_call(
        body, name=name, grid=(N,), in_specs=[hs, hs, at, hs, hs, one], out_specs=[hs, st],
        out_shape=[sds((H, T, HEAD_DIM)), sds((N, H, HEAD_DIM, HEAD_DIM))],
        scratch_shapes=[pltpu.VMEM((H, HEAD_DIM, HEAD_DIM), F32)], compiler_params=_params(("arbitrary",)),
    )(u, w, a, qd, kd, gl)


def dn_scan_bwd(u, w, a, qd, kd, gl, s_in, do, name):
    H, T, _ = u.shape
    C = DN_CHUNK
    N = T // C
    rev = lambda spec_shape, f: pl.BlockSpec(spec_shape, f)
    hs = rev((H, C, HEAD_DIM), lambda n: (0, N - 1 - n, 0))
    at = rev((H, C, C), lambda n: (0, N - 1 - n, 0))
    one = rev((H, 1, 1, 1), lambda n: (0, N - 1 - n, 0, 0))
    st = rev((1, H, HEAD_DIM, HEAD_DIM), lambda n: (N - 1 - n, 0, 0, 0))

    def body(u, w, a, qd, kd, gl, s_in, do, du, dw, da, dqd, dkd, dgl, dS):
        @pl.when(pl.program_id(0) == 0)
        def _():
            dS[...] = jnp.zeros_like(dS)

        hd = range(H)
        _, vj = jax.vjp(dn_step, [u[h] for h in hd], [w[h] for h in hd], [a[h] for h in hd], [qd[h] for h in hd],
                        [kd[h] for h in hd], [gl[h, 0] for h in hd], [s_in[0, h] for h in hd])
        res = vj(([do[h] for h in hd], [dS[h] for h in hd]))
        for h in hd:
            du[h], dw[h], da[h], dqd[h], dkd[h] = (res[j][h] for j in range(5))
            dgl[h, 0] = res[5][h]
            dS[h] = res[6][h]

    big = sds((H, T, HEAD_DIM))
    return pl.pallas_call(
        body, name=name, grid=(N,), in_specs=[hs, hs, at, hs, hs, one, st, hs], out_specs=[hs, hs, at, hs, hs, one],
        out_shape=[big, big, sds((H, T, C)), big, big, sds((H, N, 1, 1))],
        scratch_shapes=[pltpu.VMEM((H, HEAD_DIM, HEAD_DIM), F32)], compiler_params=_params(("arbitrary",)),
    )(u, w, a, qd, kd, gl, s_in, do)


def dn_post_fwd(o, projA, g, name, tm=256):
    H, T, _ = o.shape
    W = H * HEAD_DIM
    tl = Tiles(T, _pick(T, tm))
    (y,) = rowcall(name, lambda i, o, z, g: (dn_post(o, z, g),), [o, projA, g], [tl.heads(H), tl.row(W, 3), full((1, HEAD_DIM))],
                   [sds((T, W), BF16)], [tl.row(W)], [False], tl.n)
    return y


def dn_post_bwd(o, projA, g, dy, name, tm=256):
    H, T, _ = o.shape
    W = H * HEAD_DIM
    tl = Tiles(T, _pick(T, tm))

    def fn(i, o, z, g, dy):
        _, vj = jax.vjp(dn_post, o, z, g)
        return vj(dy.astype(F32))

    return rowcall(name, fn, [o, projA, g, dy], [tl.heads(H), tl.row(W, 3), full((1, HEAD_DIM)), tl.row(W)],
                   [sds((H, T, HEAD_DIM)), sds((T, W), BF16), sds((1, HEAD_DIM))],
                   [tl.heads(H), tl.row(W), full((1, HEAD_DIM))], [False, False, True], tl.n)


def gmlp_fwd(projB, ln_g, ln_b, sgw, sgbT, name, tm=512):
    T = projB.shape[0]
    G = sgw.shape[0]
    W = G * HEAD_DIM
    tl = Tiles(T, _pick(T, tm))
    (y,) = rowcall(name, lambda i, *a: (gmlp(*a),), [projB, projB, ln_g, ln_b, sgw, sgbT],
                   [tl.row(W, 0), tl.row(W, 1), full((1, W)), full((1, W)), full(sgw.shape), full(sgbT.shape)],
                   [sds((T, W), BF16)], [tl.row(W)], [False], tl.n)
    return y


def gmlp_bwd(projB, ln_g, ln_b, sgw, sgbT, dy, name, tm=512):
    T = projB.shape[0]
    G = sgw.shape[0]
    W = G * HEAD_DIM
    tl = Tiles(T, _pick(T, tm))

    def fn(i, u_raw, v_raw, ln_g, ln_b, sgw, sgbT, dy):
        _, vj = jax.vjp(gmlp, u_raw, v_raw, ln_g, ln_b, sgw, sgbT)
        return vj(dy.astype(F32))

    return rowcall(name, fn, [projB, projB, ln_g, ln_b, sgw, sgbT, dy],
                   [tl.row(W, 0), tl.row(W, 1), full((1, W)), full((1, W)), full(sgw.shape), full(sgbT.shape), tl.row(W)],
                   [sds((T, W), BF16), sds((T, W), BF16), sds((1, W)), sds((1, W)), sds(sgw.shape), sds(sgbT.shape)],
                   [tl.row(W), tl.row(W), full((1, W)), full((1, W)), full(sgw.shape), full(sgbT.shape)],
                   [False, False, True, True, True, True], tl.n)


def merge_fwd(projB, ap, bp, cb_a, name, tm=256):
    T, D = ap.shape
    tl = Tiles(T, _pick(T, tm))
    (m,) = rowcall(name, lambda i, *a: (merge(*a),), [projB, projB, ap, bp],
                   [tl.row(D, cb_a), tl.row(D, cb_a + 1), tl.row(D), tl.row(D)], [sds((T, D), BF16)], [tl.row(D)], [False], tl.n)
    return m


def merge_bwd(projB, ap, bp, dm, cb_a, name, tm=256):
    T, D = ap.shape
    tl = Tiles(T, _pick(T, tm))

    def fn(i, ga, gb, ap, bp, dm):
        _, vj = jax.vjp(merge, ga, gb, ap, bp)
        return vj(dm.astype(F32))

    return rowcall(name, fn, [projB, projB, ap, bp, dm], [tl.row(D, cb_a), tl.row(D, cb_a + 1), tl.row(D), tl.row(D), tl.row(D)],
                   [sds((T, D), BF16)] * 4, [tl.row(D)] * 4, [False] * 4, tl.n)


def ffn_act_fwd(gp, up, fcw8, fcb, name, tm=256, tc=512):
    T, F = gp.shape
    tl = Tiles(T, _pick(T, tm))
    tc = _pick(F, tc)
    R = tl.tm

    def fn(i, gprev, g, up, cw, cb):
        xwin = jnp.concatenate([jnp.where(i > 0, gprev, 0.0), g], axis=0)
        return (silu(causal_conv(value_taps(xwin, 3, R), cw) + cb) * up,)

    (act,) = rowcall(name, fn, [gp, gp, up, fcw8, fcb], [tl.prevj(tc), tl.rowj(tc), tl.rowj(tc), constj(SUBLANES, tc), constj(1, tc)],
                     [sds((T, F), BF16)], [tl.rowj(tc)], [False], tl.n, F // tc)
    return act


def ffn_act_bwd(gp, up, fcw8, fcb, dact, name, tm=256, tc=512):
    T, F = gp.shape
    tl = Tiles(T, _pick(T, tm))
    tc = _pick(F, tc)
    R = tl.tm
    RE = R + SUBLANES

    def fn(i, gprev, g, gnext, up, upn, da, dan, cw, cb):
        last = i == tl.n - 1
        xwin = jnp.concatenate([jnp.where(i > 0, gprev, 0.0), g, jnp.where(last, 0.0, gnext)], axis=0)
        taps = value_taps(xwin, 3, RE)
        gate = causal_conv(taps, cw) + cb
        upe = jnp.concatenate([up, upn], axis=0)
        dae = jnp.concatenate([da, jnp.where(last, 0.0, dan)], axis=0)
        s = sigmoid(gate)
        dgate = dae * upe * (s * (1.0 + gate * (1.0 - s)))
        dup = da * (gate[0:R, :] * s[0:R, :])
        dgp = cw[0:1, :] * dgate[2:2 + R, :] + cw[1:2, :] * dgate[1:1 + R, :] + cw[2:3, :] * dgate[0:R, :]
        dcw = rows_to8([jnp.sum(dgate[0:R, :] * taps[j][0:R, :], axis=0, keepdims=True) for j in range(3)], tc)
        dcb = jnp.sum(dgate[0:R, :], axis=0, keepdims=True)
        return dgp, dup, dcw, dcb

    return rowcall(name, fn, [gp, gp, gp, up, up, dact, dact, fcw8, fcb],
                   [tl.prevj(tc), tl.rowj(tc), tl.nxtj(tc), tl.rowj(tc), tl.nxtj(tc), tl.rowj(tc), tl.nxtj(tc),
                    constj(SUBLANES, tc), constj(1, tc)],
                   [sds((T, F), BF16), sds((T, F), BF16), sds((SUBLANES, F)), sds((1, F))],
                   [tl.rowj(tc), tl.rowj(tc), constj(SUBLANES, tc), constj(1, tc)], [False, False, True, True], tl.n, F // tc)


def _me():
    return lax.axis_index("x"), lax.axis_index("y"), lax.axis_index("c")


def all_gather(shards, name):
    nt = len(shards)

    def body(*refs):
        xs, outs = refs[:nt], refs[nt:2 * nt]
        send_sems, recv_sems, local_sems = refs[2 * nt:]
        x, y, c = _me()
        me, sibling = (x, y, c), (x, y, 1 - c)
        chips = [(1 - x, y), (x, 1 - y), (1 - x, 1 - y)]

        def slot(t, p):
            return outs[t].at[4 * p[0] + 2 * p[1] + p[2]]

        def copy(t, k, block, to, src=None):
            return pltpu.make_async_remote_copy(
                src_ref=slot(t, block) if src is None else src, dst_ref=slot(t, block),
                send_sem=send_sems.at[t, k], recv_sem=recv_sems.at[t, k], device_id=to, device_id_type=MESH)

        mine = [pltpu.make_async_copy(xs[t], slot(t, me), local_sems.at[t]) for t in range(nt)]
        first = []
        for t in range(nt):
            mine[t].start()
            first.append(copy(t, 0, me, sibling, src=xs[t]))
            first += [copy(t, 1 + j, me, (*chip, c), src=xs[t]) for j, chip in enumerate(chips)]
        for cp in first:
            cp.start()
        passed = []
        for j, chip in enumerate(chips):
            for t in range(nt):
                copy(t, 1 + j, (*chip, c), me).wait_recv()
                cp = copy(t, 4 + j, (*chip, c), sibling)
                cp.start()
                passed.append(cp)
        for t in range(nt):
            copy(t, 0, sibling, me).wait_recv()
            for j, chip in enumerate(chips):
                copy(t, 4 + j, (*chip, 1 - c), me).wait_recv()
        for cp in first + passed:
            cp.wait_send()
        for t in range(nt):
            mine[t].wait()

    any_spec = pl.BlockSpec(memory_space=pl.ANY)
    return pl.pallas_call(
        body, name=name, in_specs=[any_spec] * nt, out_specs=[any_spec] * nt,
        out_shape=[jax.ShapeDtypeStruct((N_DEV,) + s.shape, s.dtype) for s in shards],
        scratch_shapes=[pltpu.SemaphoreType.DMA((nt, 7)), pltpu.SemaphoreType.DMA((nt, 7)), pltpu.SemaphoreType.DMA((nt,))],
    )(*shards)


_HBM = pl.BlockSpec(memory_space=pltpu.HBM)
_SEM = pl.BlockSpec(memory_space=pltpu.SEMAPHORE)
_ANY = pl.BlockSpec(memory_space=pl.ANY)
_DATAFLOW = pltpu.SideEffectType.DATAFLOW_SIDE_EFFECTING


def _peers():
    x, y, c = _me()
    out = []
    for k in range(1, N_DEV):
        p = (x ^ (k >> 2), y ^ ((k >> 1) & 1), c ^ (k & 1))
        out.append((k, p, 4 * p[0] + 2 * p[1] + p[2]))
    return out


def _split_copy(src, land, send_sems, recv_sems, t, k, peer, slot, my, scatter, receiving):
    return pltpu.make_async_remote_copy(
        src_ref=src.at[slot] if scatter else land.at[my], dst_ref=land.at[slot if receiving else my],
        send_sem=send_sems.at[t * (N_DEV - 1) + k - 1], recv_sem=recv_sems.at[t * (N_DEV - 1) + k - 1],
        device_id=peer, device_id_type=MESH)


def comm_start(groups, scatter, name, after=None):
    flat = [a for g in groups for a in g]
    nt = len(flat)
    if scatter:
        lands = [lax.empty(a.shape, a.dtype) for a in flat]
    else:
        me = 4 * lax.axis_index("x") + 2 * lax.axis_index("y") + lax.axis_index("c")
        lands = [lax.dynamic_update_index_in_dim(lax.empty((N_DEV,) + a.shape, a.dtype), a, me, 0) for a in flat]
    ng = len(groups)
    n_after = 0 if after is None else 1

    def body(*refs):
        src, land = refs[:nt], refs[nt:2 * nt]
        sems = refs[2 * nt + n_after:2 * nt + n_after + 2 * ng]
        token = refs[-1]
        x, y, c = _me()
        my = 4 * x + 2 * y + c
        t0 = 0
        for gi, g in enumerate(groups):
            for k, peer, slot in _peers():
                for t in range(len(g)):
                    _split_copy(src[t0 + t], land[t0 + t], sems[2 * gi], sems[2 * gi + 1], t, k, peer, slot, my, scatter,
                                False).start()
            t0 += len(g)
        token[...] = jnp.zeros_like(token)

    sem_shapes = []
    for g in groups:
        sem_shapes += [pltpu.SemaphoreType.DMA((len(g) * (N_DEV - 1),))] * 2
    res = pl.pallas_call(
        body, name=name, in_specs=[_HBM] * (2 * nt) + [_HBM] * n_after,
        out_specs=[_SEM] * (2 * ng) + [_HBM] * (2 * nt) + [pl.BlockSpec(memory_space=pltpu.VMEM)],
        out_shape=sem_shapes + [pltpu.HBM(a.shape, a.dtype) for a in flat + lands] + [sds((SUBLANES, LANES))],
        input_output_aliases={i: 2 * ng + i for i in range(2 * nt)},
        compiler_params=pltpu.CompilerParams(has_side_effects=_DATAFLOW),
    )(*[pltpu.with_memory_space_constraint(a, pltpu.HBM) for a in flat + lands + ([] if after is None else [after])])
    handles = []
    t0 = 0
    for gi, g in enumerate(groups):
        n = len(g)
        handles.append(dict(sems=(res[2 * gi], res[2 * gi + 1]), src=res[2 * ng + t0:2 * ng + t0 + n],
                            land=res[2 * ng + nt + t0:2 * ng + nt + t0 + n], scatter=scatter))
        t0 += n
    return handles, res[-1]


def comm_wait(handle, after, name):
    src, land, scatter = handle["src"], handle["land"], handle["scatter"]
    nt = len(src)

    def body(*refs):
        src_r, land_r = refs[:nt], refs[nt:2 * nt]
        send_sems, recv_sems = refs[2 * nt], refs[2 * nt + 1]
        x, y, c = _me()
        my = 4 * x + 2 * y + c
        for k, peer, slot in _peers():
            for t in range(nt):
                _split_copy(src_r[t], land_r[t], send_sems, recv_sems, t, k, peer, slot, my, scatter, False).wait_send()
                _split_copy(src_r[t], land_r[t], send_sems, recv_sems, t, k, peer, slot, my, scatter, True).wait_recv()

    after = list(after) if isinstance(after, (list, tuple)) else [after]
    res = pl.pallas_call(
        body, name=name, in_specs=[_HBM] * (2 * nt) + [_SEM, _SEM] + [_HBM] * len(after), out_specs=[_HBM] * (2 * nt),
        out_shape=[pltpu.HBM(a.shape, a.dtype) for a in list(src) + list(land)],
        input_output_aliases={i: i for i in range(2 * nt)},
        compiler_params=pltpu.CompilerParams(has_side_effects=_DATAFLOW),
    )(*src, *land, *handle["sems"], *[pltpu.with_memory_space_constraint(a, pltpu.HBM) for a in after])
    return res[:nt], res[nt:]


def sum_adamw_shard(own_src, land, me, w, m, v, l, prev, name, tr=256):
    L, R, C = w.shape
    by_rows = R % SUBLANES == 0 or C % LANES != 0
    tr, tc = (_pick(R, tr, SUBLANES), C) if by_rows else (R, _pick(C, 256))
    steps = R // tr if by_rows else C // tc
    c1 = 1.0 - ADAM_B1 ** ADAM_STEP
    c2 = 1.0 - ADAM_B2 ** ADAM_STEP
    n_prev = 0 if prev is None else 4

    def at(lead, i):
        return (lead, i, 0) if by_rows else (lead, 0, i)

    def body(me_ref, *refs):
        parts = refs[:N_DEV]
        w_r, m_r, v_r = refs[N_DEV:N_DEV + 3]
        g_o, d_o, m_o, v_o = refs[N_DEV + 3 + n_prev:]
        g = parts[0][0].astype(F32)
        for k in range(1, N_DEV):
            g = g + parts[k][0].astype(F32)
        mn = ADAM_B1 * m_r[0] + (1.0 - ADAM_B1) * g
        vn = ADAM_B2 * v_r[0] + (1.0 - ADAM_B2) * (g * g)
        g_o[0] = g
        d_o[0] = -ADAM_LR * ((mn / c1) / (jnp.sqrt(vn / c2) + ADAM_EPS) + ADAM_WD * w_r[0])
        m_o[0] = mn
        v_o[0] = vn

    part_specs = [pl.BlockSpec((1, tr, tc), lambda i, me, k=k: at(me[0] ^ k, i)) for k in range(N_DEV)]
    lay = pl.BlockSpec((1, tr, tc), lambda i, me: at(l, i))
    grid_spec = pltpu.PrefetchScalarGridSpec(
        num_scalar_prefetch=1, grid=(steps,), in_specs=part_specs + [lay] * 3 + [_ANY] * n_prev, out_specs=[lay] * 4)
    return pl.pallas_call(
        body, name=name, grid_spec=grid_spec, out_shape=[sds((L, R, C))] * 4,
        input_output_aliases={1 + N_DEV + 3 + j: j for j in range(n_prev)}, compiler_params=_params(("parallel",)),
    )(me, own_src, *[land] * (N_DEV - 1), w, m, v, *([] if prev is None else prev))


def sum_adamw(parts, w, m, v, name, tr=256):
    _, R, C = parts.shape
    tr = _pick(R, tr, SUBLANES)
    c1 = 1.0 - ADAM_B1 ** ADAM_STEP
    c2 = 1.0 - ADAM_B2 ** ADAM_STEP

    def body(p, w, m, v, g_o, d_o, m_o, v_o):
        g = p[0].astype(F32)
        for d in range(1, N_DEV):
            g = g + p[d].astype(F32)
        mn = ADAM_B1 * m[...] + (1.0 - ADAM_B1) * g
        vn = ADAM_B2 * v[...] + (1.0 - ADAM_B2) * (g * g)
        m_hat = mn / c1
        v_hat = vn / c2
        g_o[...] = g
        d_o[...] = -ADAM_LR * (m_hat / (jnp.sqrt(v_hat) + ADAM_EPS) + ADAM_WD * w[...])
        m_o[...] = mn
        v_o[...] = vn

    blk = pl.BlockSpec((tr, C), lambda i: (i, 0))
    return pl.pallas_call(
        body, name=name, grid=(R // tr,), in_specs=[pl.BlockSpec((N_DEV, tr, C), lambda i: (0, i, 0)), blk, blk, blk],
        out_specs=[blk] * 4, out_shape=[sds((R, C))] * 4, compiler_params=_params(("parallel",)),
    )(parts, w, m, v)


SHARDED = ("w_in", "dn_conv_w", "w_branch_a", "w_branch_b", "w_out", "ffn_w_gate", "ffn_w_up", "ffn_conv_w", "ffn_w_down")
TRANSPOSED = ("w_in", "ffn_w_gate", "ffn_w_up")
COL_SHARDED = ("dn_conv_w", "w_branch_a", "w_branch_b", "ffn_conv_w")
CONV_WEIGHTS = ("dn_conv_w", "ffn_conv_w")
REPLICATED = ("norm1_g", "dn_a_log", "dn_dt_bias", "dn_onorm_g", "sg_ln_g", "sg_ln_b", "sg_w", "sg_b", "norm2_g",
              "ffn_conv_b", "final_norm_g")
WEIGHTS = ("norm1_g", "w_in", "dn_conv_w", "dn_a_log", "dn_dt_bias", "dn_onorm_g", "sg_ln_g", "sg_ln_b", "sg_w", "sg_b",
           "w_branch_a", "w_branch_b", "w_out", "norm2_g", "ffn_w_gate", "ffn_w_up", "ffn_conv_w", "ffn_conv_b",
           "ffn_w_down", "final_norm_g")


def _columns(pieces, lo, hi):
    out = []
    for a, start, width in pieces:
        s, e = max(lo, start), min(hi, start + width)
        if s < e:
            out.append(a[:, s - start:e - start])
    return out[0] if len(out) == 1 else jnp.concatenate(out, axis=1)


def _assemble(name, g):
    if name in COL_SHARDED:
        return jnp.concatenate([g[d] for d in range(N_DEV)], axis=1)
    return g.reshape(N_DEV * g.shape[1], g.shape[2])


def _split(name, pieces, dtype):
    total = sum(w for _, _, w in pieces)
    if name in COL_SHARDED:
        cs = total // N_DEV
        return jnp.stack([_columns(pieces, d * cs, (d + 1) * cs).astype(dtype) for d in range(N_DEV)])
    a = pieces[0][0] if len(pieces) == 1 else jnp.concatenate([p[:w] for p, _, w in pieces], axis=0)
    return a.reshape(N_DEV, a.shape[0] // N_DEV, a.shape[1]).astype(dtype)


def _pad_lanes(a, lo, width=LANES):
    return jnp.pad(a, ((0, 0), (lo, width - lo - a.shape[1])))


def _pad_rows(a, rows=SUBLANES):
    return jnp.pad(a, ((0, rows - a.shape[0]), (0, 0)))


def kernel(x, norm1_g, w_in, dn_conv_w, dn_a_log, dn_dt_bias, dn_onorm_g, sg_ln_g, sg_ln_b, sg_w, sg_b, w_branch_a, w_branch_b, w_out, norm2_g, ffn_w_gate, ffn_w_up, ffn_conv_w, ffn_conv_b, ffn_w_down, final_norm_g, loss_target, m_norm1_g, m_w_in, m_dn_conv_w, m_dn_a_log, m_dn_dt_bias, m_dn_onorm_g, m_sg_ln_g, m_sg_ln_b, m_sg_w, m_sg_b, m_w_branch_a, m_w_branch_b, m_w_out, m_norm2_g, m_ffn_w_gate, m_ffn_w_up, m_ffn_conv_w, m_ffn_conv_b, m_ffn_w_down, m_final_norm_g, v_norm1_g, v_w_in, v_dn_conv_w, v_dn_a_log, v_dn_dt_bias, v_dn_onorm_g, v_sg_ln_g, v_sg_ln_b, v_sg_w, v_sg_b, v_w_branch_a, v_w_branch_b, v_w_out, v_norm2_g, v_ffn_w_gate, v_ffn_w_up, v_ffn_conv_w, v_ffn_conv_b, v_ffn_w_down, v_final_norm_g):
    W = dict(norm1_g=norm1_g, w_in=w_in, dn_conv_w=dn_conv_w, dn_a_log=dn_a_log, dn_dt_bias=dn_dt_bias, dn_onorm_g=dn_onorm_g,
             sg_ln_g=sg_ln_g, sg_ln_b=sg_ln_b, sg_w=sg_w, sg_b=sg_b, w_branch_a=w_branch_a, w_branch_b=w_branch_b, w_out=w_out,
             norm2_g=norm2_g, ffn_w_gate=ffn_w_gate, ffn_w_up=ffn_w_up, ffn_conv_w=ffn_conv_w, ffn_conv_b=ffn_conv_b,
             ffn_w_down=ffn_w_down, final_norm_g=final_norm_g)
    Mo = dict(norm1_g=m_norm1_g, w_in=m_w_in, dn_conv_w=m_dn_conv_w, dn_a_log=m_dn_a_log, dn_dt_bias=m_dn_dt_bias,
              dn_onorm_g=m_dn_onorm_g, sg_ln_g=m_sg_ln_g, sg_ln_b=m_sg_ln_b, sg_w=m_sg_w, sg_b=m_sg_b, w_branch_a=m_w_branch_a,
              w_branch_b=m_w_branch_b, w_out=m_w_out, norm2_g=m_norm2_g, ffn_w_gate=m_ffn_w_gate, ffn_w_up=m_ffn_w_up,
              ffn_conv_w=m_ffn_conv_w, ffn_conv_b=m_ffn_conv_b, ffn_w_down=m_ffn_w_down, final_norm_g=m_final_norm_g)
    Vo = dict(norm1_g=v_norm1_g, w_in=v_w_in, dn_conv_w=v_dn_conv_w, dn_a_log=v_dn_a_log, dn_dt_bias=v_dn_dt_bias,
              dn_onorm_g=v_dn_onorm_g, sg_ln_g=v_sg_ln_g, sg_ln_b=v_sg_ln_b, sg_w=v_sg_w, sg_b=v_sg_b, w_branch_a=v_w_branch_a,
              w_branch_b=v_w_branch_b, w_out=v_w_out, norm2_g=v_norm2_g, ffn_w_gate=v_ffn_w_gate, ffn_w_up=v_ffn_w_up,
              ffn_conv_w=v_ffn_conv_w, ffn_conv_b=v_ffn_conv_b, ffn_w_down=v_ffn_w_down, final_norm_g=v_final_norm_g)

    xs = x[0]
    tgt = loss_target[0]
    T, D = xs.shape
    depth = norm1_g.shape[0]
    H = dn_a_log.shape[1]
    G = sg_w.shape[1]
    WA = H * HEAD_DIM
    WB = G * HEAD_DIM
    N = T // DN_CHUNK
    colA = 4 * WA
    colB0 = colA + 2 * H
    cb_a = (2 * WB) // D

    my = 4 * lax.axis_index("x") + 2 * lax.axis_index("y") + lax.axis_index("c")
    me_arr = my.astype(jnp.int32).reshape(1)

    def view(d):
        return {n: (jnp.transpose(d[n], (0, 2, 1)) if n in TRANSPOSED else d[n]) for n in SHARDED}

    Wv, Mv, Vv = view(W), view(Mo), view(Vo)

    def shard(n, l):
        return Wv[n][l] if n in CONV_WEIGHTS else Wv[n][l].astype(BF16)

    first = [("w_in", 0), ("dn_conv_w", 0)]
    first_blocks = all_gather([shard(n, l) for n, l in first], "gather_first")
    gathered = dict(zip(first, first_blocks))
    gather_names = [[("w_branch_a", 0), ("w_branch_b", 0), ("w_out", 0)]]
    for l in range(depth):
        if l > 0:
            gather_names.append([("w_in", l), ("dn_conv_w", l), ("w_branch_a", l), ("w_branch_b", l), ("w_out", l)])
        gather_names.append([("ffn_w_gate", l), ("ffn_w_up", l), ("ffn_conv_w", l), ("ffn_w_down", l)])
    gather_handles, gather_tok = comm_start([[shard(n, l) for n, l in g] for g in gather_names], False, "gather_start",
                                            after=first_blocks[0])

    def need(n, l, after):
        if (n, l) not in gathered:
            gi = [i for i, g in enumerate(gather_names) if (n, l) in g][0]
            src, land = comm_wait(gather_handles[gi], after, f"gather_wait{gi}")
            for key, s, ld in zip(gather_names[gi], src, land):
                gathered[key] = ld
        return gathered[(n, l)]

    def full(n, l, after):
        return _assemble(n, need(n, l, after))

    def layer_weights(l):
        return dict(
            g1=norm1_g[l][None], g2=norm2_g[l][None], alog=_pad_lanes(dn_a_log[l][None], H), dtb=_pad_lanes(dn_dt_bias[l][None], H),
            og=dn_onorm_g[l][None], lng=sg_ln_g[l][None], lnb=sg_ln_b[l][None], sgw=sg_w[l], sgbT=sg_b[l].T, fcb=ffn_conv_b[l][None])

    def mixer_in_weights(p, l, after):
        wt = full("w_in", l, after)
        p.update(wA=wt[:colA], wba=_pad_rows(wt[colA:colB0], LANES), wB=wt[colB0:], cw8=_pad_rows(full("dn_conv_w", l, after)))

    def mixer_out_weights(p, l, after):
        p.update(wa=full("w_branch_a", l, after), wb=full("w_branch_b", l, after), wo=full("w_out", l, after))

    def ffn_weights(p, l, after):
        p.update(wg=full("ffn_w_gate", l, after), wu=full("ffn_w_up", l, after), fcw8=_pad_rows(full("ffn_conv_w", l, after)),
                 wd=full("ffn_w_down", l, after))

    saved = []
    cur = xs
    for l in range(depth):
        p = layer_weights(l)
        t = f"l{l}_"
        h = norm_fwd(cur, p["g1"] + gather_tok[0, 0] if l == 0 else p["g1"], t + "norm1")
        mixer_in_weights(p, l, h)
        projA = matmul(h, p["wA"], "nt", t + "projA")
        pba = matmul(h, p["wba"], "nt", t + "proj_ba")
        projB = matmul(h, p["wB"], "nt", t + "projB")
        q, k, v, bg = dn_prep_fwd(projA, pba, p["cw8"], p["alog"], p["dtb"], H, t + "dn_prep")
        u, w, a, qd, kd, gl, tinv = dn_chunk_fwd(q, k, v, bg, t + "dn_chunk")
        o, s_in = dn_scan_fwd(u, w, a, qd, kd, gl, t + "dn_scan")
        y_a = dn_post_fwd(o, projA, p["og"], t + "dn_post")
        y_b = gmlp_fwd(projB, p["lng"], p["lnb"], p["sgw"], p["sgbT"], t + "gmlp")
        mixer_out_weights(p, l, y_b)
        ap = matmul(y_a, p["wa"], "nn", t + "branch_a")
        bp = matmul(y_b, p["wb"], "nn", t + "branch_b")
        merged = merge_fwd(projB, ap, bp, cb_a, t + "merge")
        x1 = matmul(merged, p["wo"], "nn", t + "out_proj", c=cur)
        h2 = norm_fwd(x1, p["g2"], t + "norm2")
        ffn_weights(p, l, h2)
        gp = matmul(h2, p["wg"], "nt", t + "ffn_gate")
        up = matmul(h2, p["wu"], "nt", t + "ffn_up")
        act = ffn_act_fwd(gp, up, p["fcw8"], p["fcb"], t + "ffn_act")
        x2 = matmul(act, p["wd"], "nn", t + "ffn_down", c=x1)
        saved.append(dict(p=p, x0=cur, h=h, projA=projA, pba=pba, projB=projB, q=q, k=k, v=v, bg=bg, tinv=tinv,
                          scan=(u, w, a, qd, kd, gl), s_in=s_in, o=o, y_a=y_a, y_b=y_b, ap=ap, bp=bp, merged=merged, x1=x1,
                          h2=h2, gp=gp, up=up, act=act))
        cur = x2

    loss_part, dx, dx_bf, d_final = head_fwd_bwd(cur, final_norm_g[None], tgt, "loss_head")
    loss = lax.psum(loss_part[0, 0], ("x", "y", "c"))

    grads_sh = {n: [None] * depth for n in SHARDED}
    grads_rep = {n: [None] * depth for n in REPLICATED if n != "final_norm_g"}
    exchanges = []

    def exchange(names, l, name, after=None):
        srcs = [_split(n, grads_sh[n][l], F32 if n in CONV_WEIGHTS else BF16) for n in names]
        (handle,), tok = comm_start([srcs], True, name, after=after)
        exchanges.append((names, l, handle))
        return tok

    def whole(a):
        return [(a, 0, a.shape[1])]

    sizes = [math.prod(W[n].shape) for n in REPLICATED]
    tile = SUBLANES * LANES
    nrows = [-(-sz // tile) * SUBLANES for sz in sizes]

    def pack(d):
        parts = [jnp.pad(d[n].reshape(-1).astype(F32), (0, r * LANES - sz)).reshape(r, LANES)
                 for n, sz, r in zip(REPLICATED, sizes, nrows)]
        return jnp.concatenate(parts, axis=0)

    mixer_tok = None
    for l in reversed(range(depth)):
        s = saved[l]
        p = s["p"]
        t = f"l{l}_b_"
        dact = matmul(dx_bf, p["wd"], "nt", t + "d_act")
        grads_sh["ffn_w_down"][l] = whole(matmul(s["act"], dx_bf, "tn", t + "dw_down", out_dtype=BF16, tn=2048))
        fcb = p["fcb"] if mixer_tok is None else p["fcb"] + mixer_tok[0, 0]
        dgp, dup, dfcw, dfcb = ffn_act_bwd(s["gp"], s["up"], p["fcw8"], fcb, dact, t + "ffn_act")
        dh2 = matmul([dgp, dup], [p["wg"], p["wu"]], "nn", t + "dh2")
        grads_sh["ffn_w_gate"][l] = whole(matmul(dgp, s["h2"], "tn", t + "dw_gate", out_dtype=BF16, tn=2048))
        grads_sh["ffn_w_up"][l] = whole(matmul(dup, s["h2"], "tn", t + "dw_up", out_dtype=BF16, tn=2048))
        grads_sh["ffn_conv_w"][l] = whole(dfcw[:3])
        grads_rep["ffn_conv_b"][l] = dfcb[0]
        tok = exchange(("ffn_w_down", "ffn_w_gate", "ffn_w_up", "ffn_conv_w"), l, t + "ffn_grads_start")
        dx1, dx1_bf, dg2 = norm_bwd(s["x1"], p["g2"] + tok[0, 0], dh2, dx, t + "norm2")
        grads_rep["norm2_g"][l] = dg2[0]
        dmerged = matmul(dx1_bf, p["wo"], "nt", t + "d_merged")
        grads_sh["w_out"][l] = whole(matmul(s["merged"], dx1_bf, "tn", t + "dw_out", out_dtype=BF16, tn=2048))
        dga, dgb, dap, dbp = merge_bwd(s["projB"], s["ap"], s["bp"], dmerged, cb_a, t + "merge")
        dya = matmul(dap, p["wa"], "nt", t + "d_ya")
        dyb = matmul(dbp, p["wb"], "nt", t + "d_yb")
        grads_sh["w_branch_a"][l] = whole(matmul(s["y_a"], dap, "tn", t + "dw_a", out_dtype=BF16))
        grads_sh["w_branch_b"][l] = whole(matmul(s["y_b"], dbp, "tn", t + "dw_b", out_dtype=BF16))
        du_raw, dv_raw, dlng, dlnb, dsgw, dsgbT = gmlp_bwd(s["projB"], p["lng"], p["lnb"], p["sgw"], p["sgbT"], dyb, t + "gmlp")
        grads_rep["sg_ln_g"][l], grads_rep["sg_ln_b"][l] = dlng[0], dlnb[0]
        grads_rep["sg_w"][l], grads_rep["sg_b"][l] = dsgw, dsgbT.T
        do, dz, dog = dn_post_bwd(s["o"], s["projA"], p["og"], dya, t + "dn_post")
        grads_rep["dn_onorm_g"][l] = dog[0]
        du, dw, da, dqd, dkd, dgl = dn_scan_bwd(*s["scan"], s["s_in"], do, t + "dn_scan")
        dq, dk, dv, dbg = dn_chunk_bwd(s["q"], s["k"], s["v"], s["bg"], s["tinv"], du, dw, da, dqd, dkd, dgl, t + "dn_chunk")
        dqkv, dba, dcw, dalog, ddtb = dn_prep_bwd(s["projA"], s["pba"], p["cw8"], p["alog"], p["dtb"], dq, dk, dv, dbg, H,
                                                  t + "dn_prep")
        grads_sh["dn_conv_w"][l] = whole(dcw[:4])
        grads_rep["dn_a_log"][l], grads_rep["dn_dt_bias"][l] = dalog[0, H:2 * H], ddtb[0, H:2 * H]
        tok = exchange(("w_out", "w_branch_a", "w_branch_b", "dn_conv_w"), l, t + "mixer_grads_start")
        dba = dba + tok[0, 0].astype(BF16)
        dprojA = jnp.concatenate([dqkv, dz], axis=1)
        dprojB = jnp.concatenate([du_raw, dv_raw, dga, dgb], axis=1)
        dwA = matmul(dprojA, s["h"], "tn", t + "dw_A", out_dtype=BF16, tn=2048)
        dwba = matmul(dba, s["h"], "tn", t + "dw_ba", out_dtype=BF16, tn=2048)
        dwB = matmul(dprojB, s["h"], "tn", t + "dw_B", out_dtype=BF16, tn=2048)
        grads_sh["w_in"][l] = [(dwA, 0, colA), (dwba, colA, 2 * H), (dwB, colB0, dwB.shape[0])]
        mixer_tok = exchange(("w_in",), l, t + "w_in_grads_start")
        dh = matmul([dba, dprojA, dprojB], [p["wba"] + mixer_tok[0, 0].astype(BF16), p["wA"], p["wB"]], "nn", t + "dh")
        dx, dx_bf, dg1 = norm_bwd(s["x0"], p["g1"], dh, dx1, t + "norm1")
        grads_rep["norm1_g"][l] = dg1[0]
        if l == 0:
            rep_full = {n: (jnp.stack(grads_rep[n]) if n != "final_norm_g" else d_final[0]) for n in REPLICATED}
            (small_handle,), small_tok = comm_start([[pack(rep_full)]], False, "small_grads_start")

    out = {}
    after = [dx, small_tok]

    def update_group(gi, after):
        names, l, handle = exchanges[gi]
        src, land = comm_wait(handle, after, f"grads_wait{gi}")
        done = []
        for n, s_, ld in zip(names, src, land):
            res = sum_adamw_shard(s_, ld, me_arr, Wv[n], Mv[n], Vv[n], l, out.get(n), f"adamw_{n}_{l}")
            out[n] = list(res)
            done.append(res[0])
        return done

    for gi in range(len(exchanges) - 1):
        after = update_group(gi, after)

    (small_src,), (small_land,) = comm_wait(small_handle, after, "small_grads_wait")
    res = sum_adamw(small_land, pack(W), pack(Mo), pack(Vo), "adamw_small")
    update_group(len(exchanges) - 1, after + [res[0]])
    for n in TRANSPOSED:
        out[n] = [jnp.transpose(r, (0, 2, 1)) for r in out[n]]
    row0 = 0
    for n, sz, nr in zip(REPLICATED, sizes, nrows):
        out[n] = [r[row0:row0 + nr].reshape(-1)[:sz].reshape(W[n].shape) for r in res]
        row0 += nr

    return (loss, dx[None], *[out[n][0] for n in WEIGHTS], *[out[n][1] for n in WEIGHTS],
            *[out[n][2] for n in WEIGHTS], *[out[n][3] for n in WEIGHTS])
```

```python
import functools
import math

import jax
import jax.numpy as jnp
from jax import lax
from jax.experimental import pallas as pl
from jax.experimental.pallas import tpu as pltpu

F32 = jnp.float32
BF16 = jnp.bfloat16
EPS = 1e-6
N_DEV = 8
LANES = 128
SUBLANES = 8
HEAD_DIM = 128
DN_CHUNK = 64
SG_CHUNK = 128
VMEM_LIMIT = 56 * 1024 * 1024
MESH = pl.DeviceIdType.MESH
HIGHEST = lax.Precision.HIGHEST

ADAM_LR = 0.001
ADAM_B1 = 0.9
ADAM_B2 = 0.999
ADAM_EPS = 1e-08
ADAM_WD = 0.01
ADAM_STEP = 10


def _pick(n, target, mult=LANES):
    best = None
    d = mult
    while d <= min(n, target):
        if n % d == 0:
            best = d
        d += mult
    return n if best is None else best


def _params(sem):
    return pltpu.CompilerParams(dimension_semantics=sem, vmem_limit_bytes=VMEM_LIMIT)


_NN = (((1,), (0,)), ((), ()))
_NT = (((1,), (1,)), ((), ()))
_TN = (((0,), (0,)), ((), ()))


def _dg(a, b, dims, hi):
    if hi == 2:
        return lax.dot_general(a.astype(F32), b.astype(F32), dims, precision=HIGHEST, preferred_element_type=F32)
    if hi == 1:
        a_hi, b_hi = a.astype(BF16), b.astype(BF16)
        a_lo, b_lo = (a - a_hi.astype(F32)).astype(BF16), (b - b_hi.astype(F32)).astype(BF16)
        ax, bx = dims[0][0][0], dims[0][1][0]
        a = jnp.concatenate([a_hi, a_hi, a_lo], axis=ax)
        b = jnp.concatenate([b_hi, b_lo, b_hi], axis=bx)
        return lax.dot_general(a, b, dims, preferred_element_type=F32)
    return lax.dot_general(a.astype(BF16), b.astype(BF16), dims, preferred_element_type=F32)


@functools.partial(jax.custom_vjp, nondiff_argnums=(2,))
def mm_nn(a, b, hi=False):
    return _dg(a, b, _NN, hi)


def _mm_nn_f(a, b, hi):
    return _dg(a, b, _NN, hi), (a, b)


def _mm_nn_b(hi, res, g):
    a, b = res
    return mm_nt(g, b, hi), mm_tn(a, g, hi)


@functools.partial(jax.custom_vjp, nondiff_argnums=(2,))
def mm_nt(a, b, hi=False):
    return _dg(a, b, _NT, hi)


def _mm_nt_f(a, b, hi):
    return _dg(a, b, _NT, hi), (a, b)


def _mm_nt_b(hi, res, g):
    a, b = res
    return mm_nn(g, b, hi), mm_tn(g, a, hi)


@functools.partial(jax.custom_vjp, nondiff_argnums=(2,))
def mm_tn(a, b, hi=False):
    return _dg(a, b, _TN, hi)


def _mm_tn_f(a, b, hi):
    return _dg(a, b, _TN, hi), (a, b)


def _mm_tn_b(hi, res, g):
    a, b = res
    return mm_nt(b, g, hi), mm_nn(a, g, hi)


mm_nn.defvjp(_mm_nn_f, _mm_nn_b)
mm_nt.defvjp(_mm_nt_f, _mm_nt_b)
mm_tn.defvjp(_mm_tn_f, _mm_tn_b)


def matmul(a, b, mode, name, c=None, out_dtype=F32, tm=1024, tn=1024, tk=2048):
    a_list = list(a) if isinstance(a, (list, tuple)) else [a]
    b_list = list(b) if isinstance(b, (list, tuple)) else [b]
    nterm = len(a_list)

    def dims_of(a, b):
        if mode == "nn":
            return a.shape[0], a.shape[1], b.shape[1]
        if mode == "nt":
            return a.shape[0], a.shape[1], b.shape[0]
        return a.shape[1], a.shape[0], b.shape[1]

    M, _, N = dims_of(a_list[0], b_list[0])
    tm, tn = _pick(M, tm), _pick(N, tn)
    tks = [_pick(dims_of(x, y)[1], tk) for x, y in zip(a_list, b_list)]
    nks = [dims_of(x, y)[1] // t for x, y, t in zip(a_list, b_list, tks)]
    offs = [sum(nks[:t]) for t in range(nterm)]
    nk = sum(nks)
    dims = {"nn": _NN, "nt": _NT, "tn": _TN}[mode]

    def specs_of(t):
        kk = lambda k: jnp.clip(k - offs[t], 0, nks[t] - 1)
        a_spec = (pl.BlockSpec((tks[t], tm), lambda i, j, k: (kk(k), i)) if mode == "tn"
                  else pl.BlockSpec((tm, tks[t]), lambda i, j, k: (i, kk(k))))
        b_spec = (pl.BlockSpec((tn, tks[t]), lambda i, j, k: (j, kk(k))) if mode == "nt"
                  else pl.BlockSpec((tks[t], tn), lambda i, j, k: (kk(k), j)))
        return [a_spec, b_spec]

    o_spec = pl.BlockSpec((tm, tn), lambda i, j, k: (i, j))
    has_c = c is not None
    own_acc = nk > 1 and out_dtype != F32

    def body(*refs):
        ab = refs[:2 * nterm]
        c_ref = refs[2 * nterm] if has_c else None
        o_ref = refs[2 * nterm + (1 if has_c else 0)]
        acc_ref = refs[-1] if own_acc else o_ref

        def dot(t):
            return lax.dot_general(ab[2 * t][...].astype(BF16), ab[2 * t + 1][...].astype(BF16), dims,
                                   preferred_element_type=F32)

        if nk == 1:
            o_ref[...] = (dot(0) + c_ref[...] if has_c else dot(0)).astype(o_ref.dtype)
        else:
            k = pl.program_id(2)

            @pl.when(k == 0)
            def _():
                acc_ref[...] = c_ref[...] if has_c else jnp.zeros_like(acc_ref)

            for t in range(nterm):
                if nterm == 1:
                    acc_ref[...] += dot(t)
                else:
                    @pl.when((k >= offs[t]) & (k < offs[t] + nks[t]))
                    def _(t=t):
                        acc_ref[...] += dot(t)

            if own_acc:
                @pl.when(k == nk - 1)
                def _():
                    o_ref[...] = acc_ref[...].astype(o_ref.dtype)

    ins, specs = [], []
    for t in range(nterm):
        ins += [a_list[t], b_list[t]]
        specs += specs_of(t)
    if has_c:
        ins.append(c)
        specs.append(o_spec)
    return pl.pallas_call(
        body, name=name, grid=(M // tm, N // tn, nk), in_specs=specs, out_specs=o_spec,
        out_shape=jax.ShapeDtypeStruct((M, N), out_dtype), scratch_shapes=[pltpu.VMEM((tm, tn), F32)] if own_acc else [],
        compiler_params=_params(("parallel", "parallel", "arbitrary")),
    )(*ins)


def rowcall(name, fn, ins, in_specs, outs, out_specs, acc, nrow, ncol=1, scratch=()):
    n_in, n_out = len(ins), len(outs)

    def body(*refs):
        i = pl.program_id(1)
        res = fn(i, *[r[...] for r in refs[:n_in]], *refs[n_in + n_out:])
        for r, v, is_acc in zip(refs[n_in:n_in + n_out], res, acc):
            if is_acc:
                @pl.when(i == 0)
                def _(r=r, v=v):
                    r[...] = v.astype(r.dtype)

                @pl.when(i > 0)
                def _(r=r, v=v):
                    r[...] += v.astype(r.dtype)
            else:
                r[...] = v.astype(r.dtype)

    return pl.pallas_call(
        body, name=name, grid=(ncol, nrow), in_specs=list(in_specs), out_specs=list(out_specs), out_shape=list(outs),
        scratch_shapes=list(scratch), compiler_params=_params(("parallel", "arbitrary")),
    )(*ins)


class Tiles:
    def __init__(self, T, tm):
        self.T, self.tm, self.n = T, tm, T // tm
        self.r8 = tm // SUBLANES

    def row(self, w, cb=0):
        return pl.BlockSpec((self.tm, w), lambda j, i: (i, cb))

    def rowj(self, tc):
        return pl.BlockSpec((self.tm, tc), lambda j, i: (i, j))

    def prev(self, w, cb=0):
        return pl.BlockSpec((SUBLANES, w), lambda j, i: (jnp.maximum(i * self.r8 - 1, 0), cb))

    def prevj(self, tc):
        return pl.BlockSpec((SUBLANES, tc), lambda j, i: (jnp.maximum(i * self.r8 - 1, 0), j))

    def nxt(self, w, cb=0):
        last = self.T // SUBLANES - 1
        return pl.BlockSpec((SUBLANES, w), lambda j, i: (jnp.minimum((i + 1) * self.r8, last), cb))

    def nxtj(self, tc):
        last = self.T // SUBLANES - 1
        return pl.BlockSpec((SUBLANES, tc), lambda j, i: (jnp.minimum((i + 1) * self.r8, last), j))

    def heads(self, H):
        return pl.BlockSpec((H, self.tm, HEAD_DIM), lambda j, i: (0, i, 0))

    def heads_nxt(self, H):
        last = self.T // SUBLANES - 1
        return pl.BlockSpec((H, SUBLANES, HEAD_DIM), lambda j, i: (0, jnp.minimum((i + 1) * self.r8, last), 0))


def full(shape):
    return pl.BlockSpec(tuple(shape), lambda j, i: (0,) * len(shape))


def constj(r, tc):
    return pl.BlockSpec((r, tc), lambda j, i: (0, j))


def sds(shape, dtype=F32):
    return jax.ShapeDtypeStruct(tuple(shape), dtype)


def rms(x, g):
    return x * lax.rsqrt(jnp.mean(x * x, axis=-1, keepdims=True) + EPS) * g


def sigmoid(x):
    return jax.nn.sigmoid(x)


def silu(x):
    return x * sigmoid(x)


def gelu(x):
    return 0.5 * x * (1.0 + lax.erf(x * (2.0 ** -0.5)))


def fill_window(win, i, last, prev, x, nxt=None):
    R = x.shape[0]
    win[0:SUBLANES, :] = jnp.where(i > 0, prev, 0.0)
    win[SUBLANES:SUBLANES + R, :] = x
    if nxt is not None:
        win[SUBLANES + R:2 * SUBLANES + R, :] = jnp.where(last, 0.0, nxt)


def conv_taps(win, K, R):
    base = SUBLANES - (K - 1)
    return [win[pl.ds(base + j, R), :] for j in range(K)]


def value_taps(xwin, K, R):
    base = SUBLANES - (K - 1)
    return [xwin[base + j:base + j + R, :] for j in range(K)]


def causal_conv(taps, w):
    out = w[0:1, :] * taps[0]
    for j in range(1, len(taps)):
        out = out + w[j:j + 1, :] * taps[j]
    return out


def rows_to8(rows, C):
    rid = lax.broadcasted_iota(jnp.int32, (SUBLANES, C), 0)
    out = jnp.zeros((SUBLANES, C), F32)
    for k, r in enumerate(rows):
        out = out + jnp.where(rid == k, jnp.broadcast_to(r, (SUBLANES, C)), 0.0)
    return out


def dn_qkv(pre, H):
    a = silu(pre)
    W = H * HEAD_DIM

    def l2(t):
        return t * lax.rsqrt(jnp.sum(t * t, axis=-1, keepdims=True) + EPS)

    q = [l2(a[:, h * HEAD_DIM:(h + 1) * HEAD_DIM]) for h in range(H)]
    k = [l2(a[:, W + h * HEAD_DIM:W + (h + 1) * HEAD_DIM]) for h in range(H)]
    v = [a[:, 2 * W + h * HEAD_DIM:2 * W + (h + 1) * HEAD_DIM] for h in range(H)]
    return q, k, v


def dn_gates(ba, alog, dtb, H, R):
    lane = lax.broadcasted_iota(jnp.int32, (R, LANES), 1)
    beta = sigmoid(ba)
    g = -jnp.exp(alog) * jax.nn.softplus(ba + dtb)
    g = jnp.where((lane >= H) & (lane < 2 * H), g, 0.0)
    ri = lax.broadcasted_iota(jnp.int32, (R, R), 0)
    ci = lax.broadcasted_iota(jnp.int32, (R, R), 1)
    cum = jnp.where((ri // DN_CHUNK == ci // DN_CHUNK) & (ci <= ri), 1.0, 0.0).astype(F32)
    gc = mm_nn(cum, g, 2)
    return jnp.where(lane < H, beta, gc)


def neumann_inverse(Ls):
    C = Ls[0].shape[0]
    ri = lax.broadcasted_iota(jnp.int32, (C, C), 0)
    ci = lax.broadcasted_iota(jnp.int32, (C, C), 1)
    eye = jnp.where(ri == ci, 1.0, 0.0).astype(F32)
    P = [-L for L in Ls]
    R = [eye + p for p in P]
    for _ in range(int(math.log2(C)) - 1):
        P = [mm_nn(p, p, 1) for p in P]
        R = [r + mm_nn(r, p, 1) for r, p in zip(R, P)]
    return R


@jax.custom_vjp
def saved_inverse(L, T):
    return T


def _saved_inverse_f(L, T):
    return T, T


def _saved_inverse_b(T, g):
    return -mm_tn(T, mm_nt(g, T, 1), 1), jnp.zeros_like(T)


saved_inverse.defvjp(_saved_inverse_f, _saved_inverse_b)


def gate_columns(bg, H):
    bgT = bg.T
    return ([bg[:, h:h + 1] for h in range(H)], [bg[:, H + h:H + h + 1] for h in range(H)],
            [bgT[H + h:H + h + 1, :] for h in range(H)])


def dn_chunk(q, k, v, beta, gc, gr, tinv=None, with_inverse=False):
    n = len(q)
    C = q[0].shape[0]
    ri = lax.broadcasted_iota(jnp.int32, (C, C), 0)
    ci = lax.broadcasted_iota(jnp.int32, (C, C), 1)
    qs = [q[h] * (HEAD_DIM ** -0.5) for h in range(n)]
    kb = [k[h] * beta[h] for h in range(n)]
    vb = [v[h] * beta[h] for h in range(n)]
    decay = [jnp.exp(jnp.where(ri >= ci, gc[h] - gr[h], -jnp.inf)) for h in range(n)]
    L = [jnp.where(ri > ci, mm_nt(kb[h], k[h]) * decay[h], 0.0) for h in range(n)]
    attn = [jnp.where(ri >= ci, mm_nt(qs[h], k[h]) * decay[h], 0.0) for h in range(n)]
    Tinv = neumann_inverse(L) if tinv is None else [saved_inverse(L[h], tinv[h]) for h in range(n)]
    eg = [jnp.exp(gc[h]) for h in range(n)]
    u = [mm_nn(Tinv[h], vb[h]) for h in range(n)]
    w = [mm_nn(Tinv[h], kb[h] * eg[h]) for h in range(n)]
    qd = [qs[h] * eg[h] for h in range(n)]
    gl = [gc[h][C - 1:C, :] for h in range(n)]
    kd = [k[h] * jnp.exp(gl[h] - gc[h]) for h in range(n)]
    return (u, w, attn, qd, kd, gl, Tinv) if with_inverse else (u, w, attn, qd, kd, gl)


def dn_step(u, w, a, qd, kd, gl, S):
    n = len(u)
    v_new = [u[h] - mm_nn(w[h], S[h]) for h in range(n)]
    o = [mm_nn(qd[h], S[h]) + mm_nn(a[h], v_new[h]) for h in range(n)]
    S_new = [S[h] * jnp.exp(gl[h]) + mm_tn(kd[h], v_new[h]) for h in range(n)]
    return o, S_new


def dn_post(o, z, g):
    H = o.shape[0]
    return jnp.concatenate([rms(o[h], g) * silu(z[:, h * HEAD_DIM:(h + 1) * HEAD_DIM]) for h in range(H)], axis=1)


def gmlp(u_raw, v_raw, ln_g, ln_b, sgw, sgbT):
    R = u_raw.shape[0]
    G = sgw.shape[0]
    nc = R // SG_CHUNK
    u = gelu(u_raw)
    vv = gelu(v_raw)
    xc = vv - jnp.mean(vv, axis=-1, keepdims=True)
    vg = xc * lax.rsqrt(jnp.mean(xc * xc, axis=-1, keepdims=True) + EPS) * ln_g + ln_b
    ri = lax.broadcasted_iota(jnp.int32, (SG_CHUNK, SG_CHUNK), 0)
    ci = lax.broadcasted_iota(jnp.int32, (SG_CHUNK, SG_CHUNK), 1)
    cols = []
    for g in range(G):
        ws = jnp.where(ri >= ci, sgw[g], 0.0)
        rhs = jnp.concatenate([vg[c * SG_CHUNK:(c + 1) * SG_CHUNK, g * HEAD_DIM:(g + 1) * HEAD_DIM] for c in range(nc)], axis=1)
        mixed = mm_nn(ws, rhs) + sgbT[:, g:g + 1]
        cols.append(jnp.concatenate([mixed[:, c * HEAD_DIM:(c + 1) * HEAD_DIM] for c in range(nc)], axis=0))
    return u * jnp.concatenate(cols, axis=1)


def merge(ga, gb, ap, bp):
    return sigmoid(ga) * ap + sigmoid(gb) * bp


def norm_fwd(x, g, name, tm=256):
    T, D = x.shape
    tl = Tiles(T, _pick(T, tm))
    (h,) = rowcall(name, lambda i, x, g: (rms(x, g),), [x, g], [tl.row(D), full((1, D))],
                   [sds((T, D), BF16)], [tl.row(D)], [False], tl.n)
    return h


def norm_bwd(x, g, dh, dres, name, tm=256):
    T, D = x.shape
    tl = Tiles(T, _pick(T, tm))

    def fn(i, x, g, dh, dres):
        _, vj = jax.vjp(rms, x, g)
        dx, dg = vj(dh.astype(F32))
        dx = dx + dres
        return dx, dx, dg

    return rowcall(name, fn, [x, g, dh, dres], [tl.row(D), full((1, D)), tl.row(D), tl.row(D)],
                   [sds((T, D)), sds((T, D), BF16), sds((1, D))], [tl.row(D), tl.row(D), full((1, D))],
                   [False, False, True], tl.n)


def head_fwd_bwd(x, g, tgt, name, tm=256):
    T, D = x.shape
    tl = Tiles(T, _pick(T, tm))

    def fn(i, x, g, tgt):
        y, vj = jax.vjp(rms, x, g)
        e = y - tgt
        loss = 0.5 * jnp.sum(jnp.mean(e * e, axis=-1, keepdims=True), axis=0, keepdims=True)
        dx, dg = vj(e * (1.0 / D))
        return loss, dx, dx, dg

    return rowcall(name, fn, [x, g, tgt], [tl.row(D), full((1, D)), tl.row(D)],
                   [sds((1, 1)), sds((T, D)), sds((T, D), BF16), sds((1, D))],
                   [full((1, 1)), tl.row(D), tl.row(D), full((1, D))], [True, False, False, True], tl.n)


def dn_prep_fwd(projA, pba, cw8, alog, dtb, H, name, tm=256):
    T = projA.shape[0]
    W3 = 3 * H * HEAD_DIM
    tl = Tiles(T, _pick(T, tm, DN_CHUNK))
    R = tl.tm

    def fn(i, xp, x, ba, cw, alog, dtb, win):
        fill_window(win, i, None, xp, x)
        q, k, v = dn_qkv(causal_conv(conv_taps(win, 4, R), cw), H)
        return jnp.stack(q), jnp.stack(k), jnp.stack(v), dn_gates(ba, alog, dtb, H, R)

    hs = sds((H, T, HEAD_DIM))
    return rowcall(name, fn, [projA, projA, pba, cw8, alog, dtb],
                   [tl.prev(W3), tl.row(W3), tl.row(LANES), full((SUBLANES, W3)), full((1, LANES)), full((1, LANES))],
                   [hs, hs, hs, sds((T, LANES))], [tl.heads(H)] * 3 + [tl.row(LANES)], [False] * 4, tl.n,
                   scratch=[pltpu.VMEM((SUBLANES + R, W3), F32)])


def dn_prep_bwd(projA, pba, cw8, alog, dtb, dq, dk, dv, dbg, H, name, tm=256):
    T = projA.shape[0]
    W3 = 3 * H * HEAD_DIM
    tl = Tiles(T, _pick(T, tm, DN_CHUNK))
    R = tl.tm
    RE = R + SUBLANES

    def fn(i, xp, x, xn, ba, cw, alog, dtb, dq, dk, dv, dqn, dkn, dvn, dbg, win, dp):
        last = i == tl.n - 1
        fill_window(win, i, last, xp, x, xn)
        taps = conv_taps(win, 4, RE)
        pre = causal_conv(taps, cw)
        ext = lambda d, dn: [jnp.concatenate([d[h], jnp.where(last, 0.0, dn[h])], axis=0) for h in range(H)]
        _, vj = jax.vjp(lambda p: dn_qkv(p, H), pre)
        (dpre,) = vj((ext(dq, dqn), ext(dk, dkn), ext(dv, dvn)))
        dp[...] = dpre
        dx = cw[3:4, :] * dpre[0:R, :]
        for j in range(3):
            dx = dx + cw[j:j + 1, :] * dp[pl.ds(3 - j, R), :]
        dcw = rows_to8([jnp.sum(dpre[0:R, :] * taps[j][0:R, :], axis=0, keepdims=True) for j in range(4)], W3)
        _, vjg = jax.vjp(lambda ba, alog, dtb: dn_gates(ba, alog, dtb, H, R), ba, alog, dtb)
        dba, dalog, ddtb = vjg(dbg)
        return dx, dba, dcw, dalog, ddtb

    return rowcall(name, fn, [projA, projA, projA, pba, cw8, alog, dtb, dq, dk, dv, dq, dk, dv, dbg],
                   [tl.prev(W3), tl.row(W3), tl.nxt(W3), tl.row(LANES), full((SUBLANES, W3)), full((1, LANES)), full((1, LANES))]
                   + [tl.heads(H)] * 3 + [tl.heads_nxt(H)] * 3 + [tl.row(LANES)],
                   [sds((T, W3), BF16), sds((T, LANES), BF16), sds((SUBLANES, W3)), sds((1, LANES)), sds((1, LANES))],
                   [tl.row(W3), tl.row(LANES), full((SUBLANES, W3)), full((1, LANES)), full((1, LANES))],
                   [False, False, True, True, True], tl.n,
                   scratch=[pltpu.VMEM((2 * SUBLANES + R, W3), F32), pltpu.VMEM((RE, W3), F32)])


def _chunk_specs(H, C):
    hs = pl.BlockSpec((H, C, HEAD_DIM), lambda n: (0, n, 0))
    col = pl.BlockSpec((H, 1, C, 1), lambda n: (0, n, 0, 0))
    rw = pl.BlockSpec((H, 1, 1, C), lambda n: (0, n, 0, 0))
    at = pl.BlockSpec((H, C, C), lambda n: (0, n, 0))
    one = pl.BlockSpec((H, 1, 1, 1), lambda n: (0, n, 0, 0))
    return hs, col, rw, at, one


def dn_chunk_fwd(q, k, v, bg, name):
    H, T, _ = q.shape
    C = DN_CHUNK
    N = T // C
    hs, _, _, at, one = _chunk_specs(H, C)
    gate = pl.BlockSpec((C, LANES), lambda n: (n, 0))

    def body(q, k, v, bg, u, w, a, qd, kd, gl, ti):
        hd = range(H)
        res = dn_chunk([q[h] for h in hd], [k[h] for h in hd], [v[h] for h in hd], *gate_columns(bg[...], H),
                       with_inverse=True)
        for h in hd:
            for ref, val in zip((u, w, a, qd, kd), res[:5]):
                ref[h] = val[h]
            gl[h, 0] = res[5][h]
            ti[h] = res[6][h]

    big = sds((H, T, HEAD_DIM))
    return pl.pallas_call(
        body, name=name, grid=(N,), in_specs=[hs, hs, hs, gate], out_specs=[hs, hs, at, hs, hs, one, at],
        out_shape=[big, big, sds((H, T, C)), big, big, sds((H, N, 1, 1)), sds((H, T, C))],
        compiler_params=_params(("parallel",)),
    )(q, k, v, bg)


def dn_chunk_bwd(q, k, v, bg, tinv, du, dw, da, dqd, dkd, dgl, name):
    H, T, _ = q.shape
    C = DN_CHUNK
    N = T // C
    hs, _, _, at, one = _chunk_specs(H, C)
    gate = pl.BlockSpec((C, LANES), lambda n: (n, 0))

    def body(q, k, v, bg, ti, du, dw, da, dqd, dkd, dgl, dq, dk, dv, dbg):
        hd = range(H)
        f = lambda q, k, v, b, gc, gr: dn_chunk(q, k, v, b, gc, gr, tinv=[ti[h] for h in hd])
        _, vj = jax.vjp(f, [q[h] for h in hd], [k[h] for h in hd], [v[h] for h in hd], *gate_columns(bg[...], H))
        res = vj(([du[h] for h in hd], [dw[h] for h in hd], [da[h] for h in hd], [dqd[h] for h in hd],
                  [dkd[h] for h in hd], [dgl[h, 0] for h in hd]))
        lane = lax.broadcasted_iota(jnp.int32, (C, LANES), 1)
        row = lax.broadcasted_iota(jnp.int32, (LANES, C), 0)
        cols = jnp.zeros((C, LANES), F32)
        rows = jnp.zeros((LANES, C), F32)
        for h in hd:
            dq[h], dk[h], dv[h] = res[0][h], res[1][h], res[2][h]
            cols = cols + jnp.where(lane == h, res[3][h], 0.0) + jnp.where(lane == H + h, res[4][h], 0.0)
            rows = rows + jnp.where(row == H + h, res[5][h], 0.0)
        dbg[...] = cols + rows.T

    big = sds((H, T, HEAD_DIM))
    return pl.pallas_call(
        body, name=name, grid=(N,), in_specs=[hs, hs, hs, gate, at, hs, hs, at, hs, hs, one],
        out_specs=[hs, hs, hs, gate], out_shape=[big, big, big, sds((T, LANES))], compiler_params=_params(("parallel",)),
    )(q, k, v, bg, tinv, du, dw, da, dqd, dkd, dgl)


def dn_scan_fwd(u, w, a, qd, kd, gl, name):
    H, T, _ = u.shape
    C = DN_CHUNK
    N = T // C
    hs, _, _, at, one = _chunk_specs(H, C)
    st = pl.BlockSpec((1, H, HEAD_DIM, HEAD_DIM), lambda n: (n, 0, 0, 0))

    def body(u, w, a, qd, kd, gl, o, s_in, S):
        @pl.when(pl.program_id(0) == 0)
        def _():
            S[...] = jnp.zeros_like(S)

        hd = range(H)
        s = [S[h] for h in hd]
        o_new, s_new = dn_step([u[h] for h in hd], [w[h] for h in hd], [a[h] for h in hd], [qd[h] for h in hd],
                               [kd[h] for h in hd], [gl[h, 0] for h in hd], s)
        for h in hd:
            s_in[0, h] = s[h]
            o[h] = o_new[h]
            S[h] = s_new[h]

    return pl.pallas_call(
        body, name=name, grid=(N,), in_specs=[hs, hs, at, hs, hs, one], out_specs=[hs, st],
        out_shape=[sds((H, T, HEAD_DIM)), sds((N, H, HEAD_DIM, HEAD_DIM))],
        scratch_shapes=[pltpu.VMEM((H, HEAD_DIM, HEAD_DIM), F32)], compiler_params=_params(("arbitrary",)),
    )(u, w, a, qd, kd, gl)


def dn_scan_bwd(u, w, a, qd, kd, gl, s_in, do, name):
    H, T, _ = u.shape
    C = DN_CHUNK
    N = T // C
    rev = lambda spec_shape, f: pl.BlockSpec(spec_shape, f)
    hs = rev((H, C, HEAD_DIM), lambda n: (0, N - 1 - n, 0))
    at = rev((H, C, C), lambda n: (0, N - 1 - n, 0))
    one = rev((H, 1, 1, 1), lambda n: (0, N - 1 - n, 0, 0))
    st = rev((1, H, HEAD_DIM, HEAD_DIM), lambda n: (N - 1 - n, 0, 0, 0))

    def body(u, w, a, qd, kd, gl, s_in, do, du, dw, da, dqd, dkd, dgl, dS):
        @pl.when(pl.program_id(0) == 0)
        def _():
            dS[...] = jnp.zeros_like(dS)

        hd = range(H)
        _, vj = jax.vjp(dn_step, [u[h] for h in hd], [w[h] for h in hd], [a[h] for h in hd], [qd[h] for h in hd],
                        [kd[h] for h in hd], [gl[h, 0] for h in hd], [s_in[0, h] for h in hd])
        res = vj(([do[h] for h in hd], [dS[h] for h in hd]))
        for h in hd:
            du[h], dw[h], da[h], dqd[h], dkd[h] = (res[j][h] for j in range(5))
            dgl[h, 0] = res[5][h]
            dS[h] = res[6][h]

    big = sds((H, T, HEAD_DIM))
    return pl.pallas_call(
        body, name=name, grid=(N,), in_specs=[hs, hs, at, hs, hs, one, st, hs], out_specs=[hs, hs, at, hs, hs, one],
        out_shape=[big, big, sds((H, T, C)), big, big, sds((H, N, 1, 1))],
        scratch_shapes=[pltpu.VMEM((H, HEAD_DIM, HEAD_DIM), F32)], compiler_params=_params(("arbitrary",)),
    )(u, w, a, qd, kd, gl, s_in, do)


def dn_post_fwd(o, projA, g, name, tm=256):
    H, T, _ = o.shape
    W = H * HEAD_DIM
    tl = Tiles(T, _pick(T, tm))
    (y,) = rowcall(name, lambda i, o, z, g: (dn_post(o, z, g),), [o, projA, g], [tl.heads(H), tl.row(W, 3), full((1, HEAD_DIM))],
                   [sds((T, W), BF16)], [tl.row(W)], [False], tl.n)
    return y


def dn_post_bwd(o, projA, g, dy, name, tm=256):
    H, T, _ = o.shape
    W = H * HEAD_DIM
    tl = Tiles(T, _pick(T, tm))

    def fn(i, o, z, g, dy):
        _, vj = jax.vjp(dn_post, o, z, g)
        return vj(dy.astype(F32))

    return rowcall(name, fn, [o, projA, g, dy], [tl.heads(H), tl.row(W, 3), full((1, HEAD_DIM)), tl.row(W)],
                   [sds((H, T, HEAD_DIM)), sds((T, W), BF16), sds((1, HEAD_DIM))],
                   [tl.heads(H), tl.row(W), full((1, HEAD_DIM))], [False, False, True], tl.n)


def gmlp_fwd(projB, ln_g, ln_b, sgw, sgbT, name, tm=512):
    T = projB.shape[0]
    G = sgw.shape[0]
    W = G * HEAD_DIM
    tl = Tiles(T, _pick(T, tm))
    (y,) = rowcall(name, lambda i, *a: (gmlp(*a),), [projB, projB, ln_g, ln_b, sgw, sgbT],
                   [tl.row(W, 0), tl.row(W, 1), full((1, W)), full((1, W)), full(sgw.shape), full(sgbT.shape)],
                   [sds((T, W), BF16)], [tl.row(W)], [False], tl.n)
    return y


def gmlp_bwd(projB, ln_g, ln_b, sgw, sgbT, dy, name, tm=512):
    T = projB.shape[0]
    G = sgw.shape[0]
    W = G * HEAD_DIM
    tl = Tiles(T, _pick(T, tm))

    def fn(i, u_raw, v_raw, ln_g, ln_b, sgw, sgbT, dy):
        _, vj = jax.vjp(gmlp, u_raw, v_raw, ln_g, ln_b, sgw, sgbT)
        return vj(dy.astype(F32))

    return rowcall(name, fn, [projB, projB, ln_g, ln_b, sgw, sgbT, dy],
                   [tl.row(W, 0), tl.row(W, 1), full((1, W)), full((1, W)), full(sgw.shape), full(sgbT.shape), tl.row(W)],
                   [sds((T, W), BF16), sds((T, W), BF16), sds((1, W)), sds((1, W)), sds(sgw.shape), sds(sgbT.shape)],
                   [tl.row(W), tl.row(W), full((1, W)), full((1, W)), full(sgw.shape), full(sgbT.shape)],
                   [False, False, True, True, True, True], tl.n)


def merge_fwd(projB, ap, bp, cb_a, name, tm=256):
    T, D = ap.shape
    tl = Tiles(T, _pick(T, tm))
    (m,) = rowcall(name, lambda i, *a: (merge(*a),), [projB, projB, ap, bp],
                   [tl.row(D, cb_a), tl.row(D, cb_a + 1), tl.row(D), tl.row(D)], [sds((T, D), BF16)], [tl.row(D)], [False], tl.n)
    return m


def merge_bwd(projB, ap, bp, dm, cb_a, name, tm=256):
    T, D = ap.shape
    tl = Tiles(T, _pick(T, tm))

    def fn(i, ga, gb, ap, bp, dm):
        _, vj = jax.vjp(merge, ga, gb, ap, bp)
        return vj(dm.astype(F32))

    return rowcall(name, fn, [projB, projB, ap, bp, dm], [tl.row(D, cb_a), tl.row(D, cb_a + 1), tl.row(D), tl.row(D), tl.row(D)],
                   [sds((T, D), BF16)] * 4, [tl.row(D)] * 4, [False] * 4, tl.n)


def ffn_act_fwd(gp, up, fcw8, fcb, name, tm=256, tc=512):
    T, F = gp.shape
    tl = Tiles(T, _pick(T, tm))
    tc = _pick(F, tc)
    R = tl.tm

    def fn(i, gprev, g, up, cw, cb):
        xwin = jnp.concatenate([jnp.where(i > 0, gprev, 0.0), g], axis=0)
        return (silu(causal_conv(value_taps(xwin, 3, R), cw) + cb) * up,)

    (act,) = rowcall(name, fn, [gp, gp, up, fcw8, fcb], [tl.prevj(tc), tl.rowj(tc), tl.rowj(tc), constj(SUBLANES, tc), constj(1, tc)],
                     [sds((T, F), BF16)], [tl.rowj(tc)], [False], tl.n, F // tc)
    return act


def ffn_act_bwd(gp, up, fcw8, fcb, dact, name, tm=256, tc=512):
    T, F = gp.shape
    tl = Tiles(T, _pick(T, tm))
    tc = _pick(F, tc)
    R = tl.tm
    RE = R + SUBLANES

    def fn(i, gprev, g, gnext, up, upn, da, dan, cw, cb):
        last = i == tl.n - 1
        xwin = jnp.concatenate([jnp.where(i > 0, gprev, 0.0), g, jnp.where(last, 0.0, gnext)], axis=0)
        taps = value_taps(xwin, 3, RE)
        gate = causal_conv(taps, cw) + cb
        upe = jnp.concatenate([up, upn], axis=0)
        dae = jnp.concatenate([da, jnp.where(last, 0.0, dan)], axis=0)
        s = sigmoid(gate)
        dgate = dae * upe * (s * (1.0 + gate * (1.0 - s)))
        dup = da * (gate[0:R, :] * s[0:R, :])
        dgp = cw[0:1, :] * dgate[2:2 + R, :] + cw[1:2, :] * dgate[1:1 + R, :] + cw[2:3, :] * dgate[0:R, :]
        dcw = rows_to8([jnp.sum(dgate[0:R, :] * taps[j][0:R, :], axis=0, keepdims=True) for j in range(3)], tc)
        dcb = jnp.sum(dgate[0:R, :], axis=0, keepdims=True)
        return dgp, dup, dcw, dcb

    return rowcall(name, fn, [gp, gp, gp, up, up, dact, dact, fcw8, fcb],
                   [tl.prevj(tc), tl.rowj(tc), tl.nxtj(tc), tl.rowj(tc), tl.nxtj(tc), tl.rowj(tc), tl.nxtj(tc),
                    constj(SUBLANES, tc), constj(1, tc)],
                   [sds((T, F), BF16), sds((T, F), BF16), sds((SUBLANES, F)), sds((1, F))],
                   [tl.rowj(tc), tl.rowj(tc), constj(SUBLANES, tc), constj(1, tc)], [False, False, True, True], tl.n, F // tc)


def _me():
    return lax.axis_index("x"), lax.axis_index("y"), lax.axis_index("c")


def all_gather(shards, name):
    nt = len(shards)

    def body(*refs):
        xs, outs = refs[:nt], refs[nt:2 * nt]
        send_sems, recv_sems, local_sems = refs[2 * nt:]
        x, y, c = _me()
        me, sibling = (x, y, c), (x, y, 1 - c)
        chips = [(1 - x, y), (x, 1 - y), (1 - x, 1 - y)]

        def slot(t, p):
            return outs[t].at[4 * p[0] + 2 * p[1] + p[2]]

        def copy(t, k, block, to, src=None):
            return pltpu.make_async_remote_copy(
                src_ref=slot(t, block) if src is None else src, dst_ref=slot(t, block),
                send_sem=send_sems.at[t, k], recv_sem=recv_sems.at[t, k], device_id=to, device_id_type=MESH)

        mine = [pltpu.make_async_copy(xs[t], slot(t, me), local_sems.at[t]) for t in range(nt)]
        first = []
        for t in range(nt):
            mine[t].start()
            first.append(copy(t, 0, me, sibling, src=xs[t]))
            first += [copy(t, 1 + j, me, (*chip, c), src=xs[t]) for j, chip in enumerate(chips)]
        for cp in first:
            cp.start()
        passed = []
        for j, chip in enumerate(chips):
            for t in range(nt):
                copy(t, 1 + j, (*chip, c), me).wait_recv()
                cp = copy(t, 4 + j, (*chip, c), sibling)
                cp.start()
                passed.append(cp)
        for t in range(nt):
            copy(t, 0, sibling, me).wait_recv()
            for j, chip in enumerate(chips):
                copy(t, 4 + j, (*chip, 1 - c), me).wait_recv()
        for cp in first + passed:
            cp.wait_send()
        for t in range(nt):
            mine[t].wait()

    any_spec = pl.BlockSpec(memory_space=pl.ANY)
    return pl.pallas_call(
        body, name=name, in_specs=[any_spec] * nt, out_specs=[any_spec] * nt,
        out_shape=[jax.ShapeDtypeStruct((N_DEV,) + s.shape, s.dtype) for s in shards],
        scratch_shapes=[pltpu.SemaphoreType.DMA((nt, 7)), pltpu.SemaphoreType.DMA((nt, 7)), pltpu.SemaphoreType.DMA((nt,))],
    )(*shards)


_HBM = pl.BlockSpec(memory_space=pltpu.HBM)
_SEM = pl.BlockSpec(memory_space=pltpu.SEMAPHORE)
_ANY = pl.BlockSpec(memory_space=pl.ANY)
_DATAFLOW = pltpu.SideEffectType.DATAFLOW_SIDE_EFFECTING


def _peers():
    x, y, c = _me()
    out = []
    for k in range(1, N_DEV):
        p = (x ^ (k >> 2), y ^ ((k >> 1) & 1), c ^ (k & 1))
        out.append((k, p, 4 * p[0] + 2 * p[1] + p[2]))
    return out


def _split_copy(src, land, send_sems, recv_sems, t, k, peer, slot, my, scatter, receiving):
    return pltpu.make_async_remote_copy(
        src_ref=src.at[slot] if scatter else land.at[my], dst_ref=land.at[slot if receiving else my],
        send_sem=send_sems.at[t * (N_DEV - 1) + k - 1], recv_sem=recv_sems.at[t * (N_DEV - 1) + k - 1],
        device_id=peer, device_id_type=MESH)


def comm_start(groups, scatter, name, after=None):
    flat = [a for g in groups for a in g]
    nt = len(flat)
    if scatter:
        lands = [lax.empty(a.shape, a.dtype) for a in flat]
    else:
        me = 4 * lax.axis_index("x") + 2 * lax.axis_index("y") + lax.axis_index("c")
        lands = [lax.dynamic_update_index_in_dim(lax.empty((N_DEV,) + a.shape, a.dtype), a, me, 0) for a in flat]
    ng = len(groups)
    n_after = 0 if after is None else 1

    def body(*refs):
        src, land = refs[:nt], refs[nt:2 * nt]
        sems = refs[2 * nt + n_after:2 * nt + n_after + 2 * ng]
        token = refs[-1]
        x, y, c = _me()
        my = 4 * x + 2 * y + c
        t0 = 0
        for gi, g in enumerate(groups):
            for k, peer, slot in _peers():
                for t in range(len(g)):
                    _split_copy(src[t0 + t], land[t0 + t], sems[2 * gi], sems[2 * gi + 1], t, k, peer, slot, my, scatter,
                                False).start()
            t0 += len(g)
        token[...] = jnp.zeros_like(token)

    sem_shapes = []
    for g in groups:
        sem_shapes += [pltpu.SemaphoreType.DMA((len(g) * (N_DEV - 1),))] * 2
    res = pl.pallas_call(
        body, name=name, in_specs=[_HBM] * (2 * nt) + [_HBM] * n_after,
        out_specs=[_SEM] * (2 * ng) + [_HBM] * (2 * nt) + [pl.BlockSpec(memory_space=pltpu.VMEM)],
        out_shape=sem_shapes + [pltpu.HBM(a.shape, a.dtype) for a in flat + lands] + [sds((SUBLANES, LANES))],
        input_output_aliases={i: 2 * ng + i for i in range(2 * nt)},
        compiler_params=pltpu.CompilerParams(has_side_effects=_DATAFLOW),
    )(*[pltpu.with_memory_space_constraint(a, pltpu.HBM) for a in flat + lands + ([] if after is None else [after])])
    handles = []
    t0 = 0
    for gi, g in enumerate(groups):
        n = len(g)
        handles.append(dict(sems=(res[2 * gi], res[2 * gi + 1]), src=res[2 * ng + t0:2 * ng + t0 + n],
                            land=res[2 * ng + nt + t0:2 * ng + nt + t0 + n], scatter=scatter))
        t0 += n
    return handles, res[-1]


def comm_wait(handle, after, name):
    src, land, scatter = handle["src"], handle["land"], handle["scatter"]
    nt = len(src)

    def body(*refs):
        src_r, land_r = refs[:nt], refs[nt:2 * nt]
        send_sems, recv_sems = refs[2 * nt], refs[2 * nt + 1]
        x, y, c = _me()
        my = 4 * x + 2 * y + c
        for k, peer, slot in _peers():
            for t in range(nt):
                _split_copy(src_r[t], land_r[t], send_sems, recv_sems, t, k, peer, slot, my, scatter, False).wait_send()
                _split_copy(src_r[t], land_r[t], send_sems, recv_sems, t, k, peer, slot, my, scatter, True).wait_recv()

    after = list(after) if isinstance(after, (list, tuple)) else [after]
    res = pl.pallas_call(
        body, name=name, in_specs=[_HBM] * (2 * nt) + [_SEM, _SEM] + [_HBM] * len(after), out_specs=[_HBM] * (2 * nt),
        out_shape=[pltpu.HBM(a.shape, a.dtype) for a in list(src) + list(land)],
        input_output_aliases={i: i for i in range(2 * nt)},
        compiler_params=pltpu.CompilerParams(has_side_effects=_DATAFLOW),
    )(*src, *land, *handle["sems"], *[pltpu.with_memory_space_constraint(a, pltpu.HBM) for a in after])
    return res[:nt], res[nt:]


def sum_adamw_shard(own_src, land, me, w, m, v, l, prev, name, tr=256):
    L, R, C = w.shape
    by_rows = R % SUBLANES == 0 or C % LANES != 0
    tr, tc = (_pick(R, tr, SUBLANES), C) if by_rows else (R, _pick(C, 256))
    steps = R // tr if by_rows else C // tc
    c1 = 1.0 - ADAM_B1 ** ADAM_STEP
    c2 = 1.0 - ADAM_B2 ** ADAM_STEP
    n_prev = 0 if prev is None else 4

    def at(lead, i):
        return (lead, i, 0) if by_rows else (lead, 0, i)

    def body(me_ref, *refs):
        parts = refs[:N_DEV]
        w_r, m_r, v_r = refs[N_DEV:N_DEV + 3]
        g_o, d_o, m_o, v_o = refs[N_DEV + 3 + n_prev:]
        g = parts[0][0].astype(F32)
        for k in range(1, N_DEV):
            g = g + parts[k][0].astype(F32)
        mn = ADAM_B1 * m_r[0] + (1.0 - ADAM_B1) * g
        vn = ADAM_B2 * v_r[0] + (1.0 - ADAM_B2) * (g * g)
        g_o[0] = g
        d_o[0] = -ADAM_LR * ((mn / c1) / (jnp.sqrt(vn / c2) + ADAM_EPS) + ADAM_WD * w_r[0])
        m_o[0] = mn
        v_o[0] = vn

    part_specs = [pl.BlockSpec((1, tr, tc), lambda i, me, k=k: at(me[0] ^ k, i)) for k in range(N_DEV)]
    lay = pl.BlockSpec((1, tr, tc), lambda i, me: at(l, i))
    grid_spec = pltpu.PrefetchScalarGridSpec(
        num_scalar_prefetch=1, grid=(steps,), in_specs=part_specs + [lay] * 3 + [_ANY] * n_prev, out_specs=[lay] * 4)
    return pl.pallas_call(
        body, name=name, grid_spec=grid_spec, out_shape=[sds((L, R, C))] * 4,
        input_output_aliases={1 + N_DEV + 3 + j: j for j in range(n_prev)}, compiler_params=_params(("parallel",)),
    )(me, own_src, *[land] * (N_DEV - 1), w, m, v, *([] if prev is None else prev))


def sum_adamw(parts, w, m, v, name, tr=256):
    _, R, C = parts.shape
    tr = _pick(R, tr, SUBLANES)
    c1 = 1.0 - ADAM_B1 ** ADAM_STEP
    c2 = 1.0 - ADAM_B2 ** ADAM_STEP

    def body(p, w, m, v, g_o, d_o, m_o, v_o):
        g = p[0].astype(F32)
        for d in range(1, N_DEV):
            g = g + p[d].astype(F32)
        mn = ADAM_B1 * m[...] + (1.0 - ADAM_B1) * g
        vn = ADAM_B2 * v[...] + (1.0 - ADAM_B2) * (g * g)
        m_hat = mn / c1
        v_hat = vn / c2
        g_o[...] = g
        d_o[...] = -ADAM_LR * (m_hat / (jnp.sqrt(v_hat) + ADAM_EPS) + ADAM_WD * w[...])
        m_o[...] = mn
        v_o[...] = vn

    blk = pl.BlockSpec((tr, C), lambda i: (i, 0))
    return pl.pallas_call(
        body, name=name, grid=(R // tr,), in_specs=[pl.BlockSpec((N_DEV, tr, C), lambda i: (0, i, 0)), blk, blk, blk],
        out_specs=[blk] * 4, out_shape=[sds((R, C))] * 4, compiler_params=_params(("parallel",)),
    )(parts, w, m, v)


SHARDED = ("w_in", "dn_conv_w", "w_branch_a", "w_branch_b", "w_out", "ffn_w_gate", "ffn_w_up", "ffn_conv_w", "ffn_w_down")
TRANSPOSED = ("w_in", "ffn_w_gate", "ffn_w_up")
COL_SHARDED = ("dn_conv_w", "w_branch_a", "w_branch_b", "ffn_conv_w")
CONV_WEIGHTS = ("dn_conv_w", "ffn_conv_w")
REPLICATED = ("norm1_g", "dn_a_log", "dn_dt_bias", "dn_onorm_g", "sg_ln_g", "sg_ln_b", "sg_w", "sg_b", "norm2_g",
              "ffn_conv_b", "final_norm_g")
WEIGHTS = ("norm1_g", "w_in", "dn_conv_w", "dn_a_log", "dn_dt_bias", "dn_onorm_g", "sg_ln_g", "sg_ln_b", "sg_w", "sg_b",
           "w_branch_a", "w_branch_b", "w_out", "norm2_g", "ffn_w_gate", "ffn_w_up", "ffn_conv_w", "ffn_conv_b",
           "ffn_w_down", "final_norm_g")


def _columns(pieces, lo, hi):
    out = []
    for a, start, width in pieces:
        s, e = max(lo, start), min(hi, start + width)
        if s < e:
            out.append(a[:, s - start:e - start])
    return out[0] if len(out) == 1 else jnp.concatenate(out, axis=1)


def _assemble(name, g):
    if name in COL_SHARDED:
        return jnp.concatenate([g[d] for d in range(N_DEV)], axis=1)
    return g.reshape(N_DEV * g.shape[1], g.shape[2])


def _split(name, pieces, dtype):
    total = sum(w for _, _, w in pieces)
    if name in COL_SHARDED:
        cs = total // N_DEV
        return jnp.stack([_columns(pieces, d * cs, (d + 1) * cs).astype(dtype) for d in range(N_DEV)])
    a = pieces[0][0] if len(pieces) == 1 else jnp.concatenate([p[:w] for p, _, w in pieces], axis=0)
    return a.reshape(N_DEV, a.shape[0] // N_DEV, a.shape[1]).astype(dtype)


def _pad_lanes(a, lo, width=LANES):
    return jnp.pad(a, ((0, 0), (lo, width - lo - a.shape[1])))


def _pad_rows(a, rows=SUBLANES):
    return jnp.pad(a, ((0, rows - a.shape[0]), (0, 0)))


def kernel(x, norm1_g, w_in, dn_conv_w, dn_a_log, dn_dt_bias, dn_onorm_g, sg_ln_g, sg_ln_b, sg_w, sg_b, w_branch_a, w_branch_b, w_out, norm2_g, ffn_w_gate, ffn_w_up, ffn_conv_w, ffn_conv_b, ffn_w_down, final_norm_g, loss_target, m_norm1_g, m_w_in, m_dn_conv_w, m_dn_a_log, m_dn_dt_bias, m_dn_onorm_g, m_sg_ln_g, m_sg_ln_b, m_sg_w, m_sg_b, m_w_branch_a, m_w_branch_b, m_w_out, m_norm2_g, m_ffn_w_gate, m_ffn_w_up, m_ffn_conv_w, m_ffn_conv_b, m_ffn_w_down, m_final_norm_g, v_norm1_g, v_w_in, v_dn_conv_w, v_dn_a_log, v_dn_dt_bias, v_dn_onorm_g, v_sg_ln_g, v_sg_ln_b, v_sg_w, v_sg_b, v_w_branch_a, v_w_branch_b, v_w_out, v_norm2_g, v_ffn_w_gate, v_ffn_w_up, v_ffn_conv_w, v_ffn_conv_b, v_ffn_w_down, v_final_norm_g):
    W = dict(norm1_g=norm1_g, w_in=w_in, dn_conv_w=dn_conv_w, dn_a_log=dn_a_log, dn_dt_bias=dn_dt_bias, dn_onorm_g=dn_onorm_g,
             sg_ln_g=sg_ln_g, sg_ln_b=sg_ln_b, sg_w=sg_w, sg_b=sg_b, w_branch_a=w_branch_a, w_branch_b=w_branch_b, w_out=w_out,
             norm2_g=norm2_g, ffn_w_gate=ffn_w_gate, ffn_w_up=ffn_w_up, ffn_conv_w=ffn_conv_w, ffn_conv_b=ffn_conv_b,
             ffn_w_down=ffn_w_down, final_norm_g=final_norm_g)
    Mo = dict(norm1_g=m_norm1_g, w_in=m_w_in, dn_conv_w=m_dn_conv_w, dn_a_log=m_dn_a_log, dn_dt_bias=m_dn_dt_bias,
              dn_onorm_g=m_dn_onorm_g, sg_ln_g=m_sg_ln_g, sg_ln_b=m_sg_ln_b, sg_w=m_sg_w, sg_b=m_sg_b, w_branch_a=m_w_branch_a,
              w_branch_b=m_w_branch_b, w_out=m_w_out, norm2_g=m_norm2_g, ffn_w_gate=m_ffn_w_gate, ffn_w_up=m_ffn_w_up,
              ffn_conv_w=m_ffn_conv_w, ffn_conv_b=m_ffn_conv_b, ffn_w_down=m_ffn_w_down, final_norm_g=m_final_norm_g)
    Vo = dict(norm1_g=v_norm1_g, w_in=v_w_in, dn_conv_w=v_dn_conv_w, dn_a_log=v_dn_a_log, dn_dt_bias=v_dn_dt_bias,
              dn_onorm_g=v_dn_onorm_g, sg_ln_g=v_sg_ln_g, sg_ln_b=v_sg_ln_b, sg_w=v_sg_w, sg_b=v_sg_b, w_branch_a=v_w_branch_a,
              w_branch_b=v_w_branch_b, w_out=v_w_out, norm2_g=v_norm2_g, ffn_w_gate=v_ffn_w_gate, ffn_w_up=v_ffn_w_up,
              ffn_conv_w=v_ffn_conv_w, ffn_conv_b=v_ffn_conv_b, ffn_w_down=v_ffn_w_down, final_norm_g=v_final_norm_g)

    xs = x[0]
    tgt = loss_target[0]
    T, D = xs.shape
    depth = norm1_g.shape[0]
    H = dn_a_log.shape[1]
    G = sg_w.shape[1]
    WA = H * HEAD_DIM
    WB = G * HEAD_DIM
    N = T // DN_CHUNK
    colA = 4 * WA
    colB0 = colA + 2 * H
    cb_a = (2 * WB) // D

    my = 4 * lax.axis_index("x") + 2 * lax.axis_index("y") + lax.axis_index("c")
    me_arr = my.astype(jnp.int32).reshape(1)

    def view(d):
        return {n: (jnp.transpose(d[n], (0, 2, 1)) if n in TRANSPOSED else d[n]) for n in SHARDED}

    Wv, Mv, Vv = view(W), view(Mo), view(Vo)

    def shard(n, l):
        return Wv[n][l] if n in CONV_WEIGHTS else Wv[n][l].astype(BF16)

    first = [("w_in", 0), ("dn_conv_w", 0)]
    first_blocks = all_gather([shard(n, l) for n, l in first], "gather_first")
    gathered = dict(zip(first, first_blocks))
    gather_names = []
    for l in range(depth):
        if l > 0:
            gather_names.append([("w_in", l), ("dn_conv_w", l)])
        gather_names.append([("w_branch_a", l), ("w_branch_b", l), ("w_out", l)])
        gather_names.append([("ffn_w_gate", l), ("ffn_w_up", l), ("ffn_conv_w", l)])
        gather_names.append([("ffn_w_down", l)])
    gather_handles, gather_tok = comm_start([[shard(n, l) for n, l in g] for g in gather_names], False, "gather_start",
                                            after=first_blocks[0])

    def need(n, l, after):
        if (n, l) not in gathered:
            gi = [i for i, g in enumerate(gather_names) if (n, l) in g][0]
            src, land = comm_wait(gather_handles[gi], after, f"gather_wait{gi}")
            for key, s, ld in zip(gather_names[gi], src, land):
                gathered[key] = ld
        return gathered[(n, l)]

    def full(n, l, after):
        return _assemble(n, need(n, l, after))

    def layer_weights(l):
        return dict(
            g1=norm1_g[l][None], g2=norm2_g[l][None], alog=_pad_lanes(dn_a_log[l][None], H), dtb=_pad_lanes(dn_dt_bias[l][None], H),
            og=dn_onorm_g[l][None], lng=sg_ln_g[l][None], lnb=sg_ln_b[l][None], sgw=sg_w[l], sgbT=sg_b[l].T, fcb=ffn_conv_b[l][None])

    def mixer_in_weights(p, l, after):
        wt = full("w_in", l, after)
        p.update(wA=wt[:colA], wba=_pad_rows(wt[colA:colB0], LANES), wB=wt[colB0:], cw8=_pad_rows(full("dn_conv_w", l, after)))

    def mixer_out_weights(p, l, after):
        p.update(wa=full("w_branch_a", l, after), wb=full("w_branch_b", l, after), wo=full("w_out", l, after))

    def ffn_weights(p, l, after):
        p.update(wg=full("ffn_w_gate", l, after), wu=full("ffn_w_up", l, after), fcw8=_pad_rows(full("ffn_conv_w", l, after)))

    saved = []
    cur = xs
    for l in range(depth):
        p = layer_weights(l)
        t = f"l{l}_"
        h = norm_fwd(cur, p["g1"] + gather_tok[0, 0] if l == 0 else p["g1"], t + "norm1")
        mixer_in_weights(p, l, h)
        projA = matmul(h, p["wA"], "nt", t + "projA")
        pba = matmul(h, p["wba"], "nt", t + "proj_ba")
        projB = matmul(h, p["wB"], "nt", t + "projB")
        q, k, v, bg = dn_prep_fwd(projA, pba, p["cw8"], p["alog"], p["dtb"], H, t + "dn_prep")
        u, w, a, qd, kd, gl, tinv = dn_chunk_fwd(q, k, v, bg, t + "dn_chunk")
        o, s_in = dn_scan_fwd(u, w, a, qd, kd, gl, t + "dn_scan")
        y_a = dn_post_fwd(o, projA, p["og"], t + "dn_post")
        y_b = gmlp_fwd(projB, p["lng"], p["lnb"], p["sgw"], p["sgbT"], t + "gmlp")
        mixer_out_weights(p, l, y_b)
        ap = matmul(y_a, p["wa"], "nn", t + "branch_a")
        bp = matmul(y_b, p["wb"], "nn", t + "branch_b")
        merged = merge_fwd(projB, ap, bp, cb_a, t + "merge")
        x1 = matmul(merged, p["wo"], "nn", t + "out_proj", c=cur)
        h2 = norm_fwd(x1, p["g2"], t + "norm2")
        ffn_weights(p, l, h2)
        gp = matmul(h2, p["wg"], "nt", t + "ffn_gate")
        up = matmul(h2, p["wu"], "nt", t + "ffn_up")
        act = ffn_act_fwd(gp, up, p["fcw8"], p["fcb"], t + "ffn_act")
        p.update(wd=full("ffn_w_down", l, act))
        x2 = matmul(act, p["wd"], "nn", t + "ffn_down", c=x1)
        saved.append(dict(p=p, x0=cur, h=h, projA=projA, pba=pba, projB=projB, q=q, k=k, v=v, bg=bg, tinv=tinv,
                          scan=(u, w, a, qd, kd, gl), s_in=s_in, o=o, y_a=y_a, y_b=y_b, ap=ap, bp=bp, merged=merged, x1=x1,
                          h2=h2, gp=gp, up=up, act=act))
        cur = x2

    loss_part, dx, dx_bf, d_final = head_fwd_bwd(cur, final_norm_g[None], tgt, "loss_head")
    loss = lax.psum(loss_part[0, 0], ("x", "y", "c"))

    grads_sh = {n: [None] * depth for n in SHARDED}
    grads_rep = {n: [None] * depth for n in REPLICATED if n != "final_norm_g"}
    exchanges = []

    def exchange(names, l, name, after=None):
        srcs = [_split(n, grads_sh[n][l], F32 if n in CONV_WEIGHTS else BF16) for n in names]
        (handle,), tok = comm_start([srcs], True, name, after=after)
        exchanges.append((names, l, handle))
        return tok

    def whole(a):
        return [(a, 0, a.shape[1])]

    sizes = [math.prod(W[n].shape) for n in REPLICATED]
    tile = SUBLANES * LANES
    nrows = [-(-sz // tile) * SUBLANES for sz in sizes]

    def pack(d):
        parts = [jnp.pad(d[n].reshape(-1).astype(F32), (0, r * LANES - sz)).reshape(r, LANES)
                 for n, sz, r in zip(REPLICATED, sizes, nrows)]
        return jnp.concatenate(parts, axis=0)

    mixer_tok = None
    for l in reversed(range(depth)):
        s = saved[l]
        p = s["p"]
        t = f"l{l}_b_"
        dact = matmul(dx_bf, p["wd"], "nt", t + "d_act")
        grads_sh["ffn_w_down"][l] = whole(matmul(s["act"], dx_bf, "tn", t + "dw_down", out_dtype=BF16, tn=2048))
        fcb = p["fcb"] if mixer_tok is None else p["fcb"] + mixer_tok[0, 0]
        dgp, dup, dfcw, dfcb = ffn_act_bwd(s["gp"], s["up"], p["fcw8"], fcb, dact, t + "ffn_act")
        dh2 = matmul([dgp, dup], [p["wg"], p["wu"]], "nn", t + "dh2")
        grads_sh["ffn_w_gate"][l] = whole(matmul(dgp, s["h2"], "tn", t + "dw_gate", out_dtype=BF16, tn=2048))
        grads_sh["ffn_w_up"][l] = whole(matmul(dup, s["h2"], "tn", t + "dw_up", out_dtype=BF16, tn=2048))
        grads_sh["ffn_conv_w"][l] = whole(dfcw[:3])
        grads_rep["ffn_conv_b"][l] = dfcb[0]
        tok = exchange(("ffn_w_down", "ffn_w_gate", "ffn_w_up", "ffn_conv_w"), l, t + "ffn_grads_start")
        dx1, dx1_bf, dg2 = norm_bwd(s["x1"], p["g2"] + tok[0, 0], dh2, dx, t + "norm2")
        grads_rep["norm2_g"][l] = dg2[0]
        dmerged = matmul(dx1_bf, p["wo"], "nt", t + "d_merged")
        grads_sh["w_out"][l] = whole(matmul(s["merged"], dx1_bf, "tn", t + "dw_out", out_dtype=BF16, tn=2048))
        dga, dgb, dap, dbp = merge_bwd(s["projB"], s["ap"], s["bp"], dmerged, cb_a, t + "merge")
        dya = matmul(dap, p["wa"], "nt", t + "d_ya")
        dyb = matmul(dbp, p["wb"], "nt", t + "d_yb")
        grads_sh["w_branch_a"][l] = whole(matmul(s["y_a"], dap, "tn", t + "dw_a", out_dtype=BF16))
        grads_sh["w_branch_b"][l] = whole(matmul(s["y_b"], dbp, "tn", t + "dw_b", out_dtype=BF16))
        du_raw, dv_raw, dlng, dlnb, dsgw, dsgbT = gmlp_bwd(s["projB"], p["lng"], p["lnb"], p["sgw"], p["sgbT"], dyb, t + "gmlp")
        grads_rep["sg_ln_g"][l], grads_rep["sg_ln_b"][l] = dlng[0], dlnb[0]
        grads_rep["sg_w"][l], grads_rep["sg_b"][l] = dsgw, dsgbT.T
        do, dz, dog = dn_post_bwd(s["o"], s["projA"], p["og"], dya, t + "dn_post")
        grads_rep["dn_onorm_g"][l] = dog[0]
        du, dw, da, dqd, dkd, dgl = dn_scan_bwd(*s["scan"], s["s_in"], do, t + "dn_scan")
        dq, dk, dv, dbg = dn_chunk_bwd(s["q"], s["k"], s["v"], s["bg"], s["tinv"], du, dw, da, dqd, dkd, dgl, t + "dn_chunk")
        dqkv, dba, dcw, dalog, ddtb = dn_prep_bwd(s["projA"], s["pba"], p["cw8"], p["alog"], p["dtb"], dq, dk, dv, dbg, H,
                                                  t + "dn_prep")
        grads_sh["dn_conv_w"][l] = whole(dcw[:4])
        grads_rep["dn_a_log"][l], grads_rep["dn_dt_bias"][l] = dalog[0, H:2 * H], ddtb[0, H:2 * H]
        tok = exchange(("w_out", "w_branch_a", "w_branch_b", "dn_conv_w"), l, t + "mixer_grads_start")
        dba = dba + tok[0, 0].astype(BF16)
        dprojA = jnp.concatenate([dqkv, dz], axis=1)
        dprojB = jnp.concatenate([du_raw, dv_raw, dga, dgb], axis=1)
        dwA = matmul(dprojA, s["h"], "tn", t + "dw_A", out_dtype=BF16, tn=2048)
        dwba = matmul(dba, s["h"], "tn", t + "dw_ba", out_dtype=BF16, tn=2048)
        dwB = matmul(dprojB, s["h"], "tn", t + "dw_B", out_dtype=BF16, tn=2048)
        grads_sh["w_in"][l] = [(dwA, 0, colA), (dwba, colA, 2 * H), (dwB, colB0, dwB.shape[0])]
        mixer_tok = exchange(("w_in",), l, t + "w_in_grads_start")
        dh = matmul([dba, dprojA, dprojB], [p["wba"] + mixer_tok[0, 0].astype(BF16), p["wA"], p["wB"]], "nn", t + "dh")
        dx, dx_bf, dg1 = norm_bwd(s["x0"], p["g1"], dh, dx1, t + "norm1")
        grads_rep["norm1_g"][l] = dg1[0]
        if l == 0:
            rep_full = {n: (jnp.stack(grads_rep[n]) if n != "final_norm_g" else d_final[0]) for n in REPLICATED}
            (small_handle,), small_tok = comm_start([[pack(rep_full)]], False, "small_grads_start")

    out = {}
    after = [dx, small_tok]

    def update_group(gi, after):
        names, l, handle = exchanges[gi]
        src, land = comm_wait(handle, after, f"grads_wait{gi}")
        done = []
        for n, s_, ld in zip(names, src, land):
            res = sum_adamw_shard(s_, ld, me_arr, Wv[n], Mv[n], Vv[n], l, out.get(n), f"adamw_{n}_{l}")
            out[n] = list(res)
            done.append(res[0])
        return done

    for gi in range(len(exchanges) - 1):
        after = update_group(gi, after)

    (small_src,), (small_land,) = comm_wait(small_handle, after, "small_grads_wait")
    res = sum_adamw(small_land, pack(W), pack(Mo), pack(Vo), "adamw_small")
    update_group(len(exchanges) - 1, after + [res[0]])
    for n in TRANSPOSED:
        out[n] = [jnp.transpose(r, (0, 2, 1)) for r in out[n]]
    row0 = 0
    for n, sz, nr in zip(REPLICATED, sizes, nrows):
        out[n] = [r[row0:row0 + nr].reshape(-1)[:sz].reshape(W[n].shape) for r in res]
        row0 += nr

    return (loss, dx[None], *[out[n][0] for n in WEIGHTS], *[out[n][1] for n in WEIGHTS],
            *[out[n][2] for n in WEIGHTS], *[out[n][3] for n in WEIGHTS])
```

```python
import functools
import math

import jax
import jax.numpy as jnp
from jax import lax
from jax.experimental import pallas as pl
from jax.experimental.pallas import tpu as pltpu

F32 = jnp.float32
BF16 = jnp.bfloat16
EPS = 1e-6
N_DEV = 8
LANES = 128
SUBLANES = 8
HEAD_DIM = 128
DN_CHUNK = 64
SG_CHUNK = 128
VMEM_LIMIT = 56 * 1024 * 1024
MESH = pl.DeviceIdType.MESH
HIGHEST = lax.Precision.HIGHEST

ADAM_LR = 0.001
ADAM_B1 = 0.9
ADAM_B2 = 0.999
ADAM_EPS = 1e-08
ADAM_WD = 0.01
ADAM_STEP = 10


def _pick(n, target, mult=LANES):
    best = None
    d = mult
    while d <= min(n, target):
        if n % d == 0:
            best = d
        d += mult
    return n if best is None else best


def _params(sem):
    return pltpu.CompilerParams(dimension_semantics=sem, vmem_limit_bytes=VMEM_LIMIT)


_NN = (((1,), (0,)), ((), ()))
_NT = (((1,), (1,)), ((), ()))
_TN = (((0,), (0,)), ((), ()))


def _dg(a, b, dims, hi):
    if hi == 2:
        return lax.dot_general(a.astype(F32), b.astype(F32), dims, precision=HIGHEST, preferred_element_type=F32)
    if hi == 1:
        a_hi, b_hi = a.astype(BF16), b.astype(BF16)
        a_lo, b_lo = (a - a_hi.astype(F32)).astype(BF16), (b - b_hi.astype(F32)).astype(BF16)
        ax, bx = dims[0][0][0], dims[0][1][0]
        a = jnp.concatenate([a_hi, a_hi, a_lo], axis=ax)
        b = jnp.concatenate([b_hi, b_lo, b_hi], axis=bx)
        return lax.dot_general(a, b, dims, preferred_element_type=F32)
    return lax.dot_general(a.astype(BF16), b.astype(BF16), dims, preferred_element_type=F32)


@functools.partial(jax.custom_vjp, nondiff_argnums=(2,))
def mm_nn(a, b, hi=False):
    return _dg(a, b, _NN, hi)


def _mm_nn_f(a, b, hi):
    return _dg(a, b, _NN, hi), (a, b)


def _mm_nn_b(hi, res, g):
    a, b = res
    return mm_nt(g, b, hi), mm_tn(a, g, hi)


@functools.partial(jax.custom_vjp, nondiff_argnums=(2,))
def mm_nt(a, b, hi=False):
    return _dg(a, b, _NT, hi)


def _mm_nt_f(a, b, hi):
    return _dg(a, b, _NT, hi), (a, b)


def _mm_nt_b(hi, res, g):
    a, b = res
    return mm_nn(g, b, hi), mm_tn(g, a, hi)


@functools.partial(jax.custom_vjp, nondiff_argnums=(2,))
def mm_tn(a, b, hi=False):
    return _dg(a, b, _TN, hi)


def _mm_tn_f(a, b, hi):
    return _dg(a, b, _TN, hi), (a, b)


def _mm_tn_b(hi, res, g):
    a, b = res
    return mm_nt(b, g, hi), mm_nn(a, g, hi)


mm_nn.defvjp(_mm_nn_f, _mm_nn_b)
mm_nt.defvjp(_mm_nt_f, _mm_nt_b)
mm_tn.defvjp(_mm_tn_f, _mm_tn_b)


def matmul(a, b, mode, name, c=None, out_dtype=F32, tm=1024, tn=1024, tk=2048):
    a_list = list(a) if isinstance(a, (list, tuple)) else [a]
    b_list = list(b) if isinstance(b, (list, tuple)) else [b]
    nterm = len(a_list)

    def dims_of(a, b):
        if mode == "nn":
            return a.shape[0], a.shape[1], b.shape[1]
        if mode == "nt":
            return a.shape[0], a.shape[1], b.shape[0]
        return a.shape[1], a.shape[0], b.shape[1]

    M, _, N = dims_of(a_list[0], b_list[0])
    tm, tn = _pick(M, tm), _pick(N, tn)
    tks = [_pick(dims_of(x, y)[1], tk) for x, y in zip(a_list, b_list)]
    nks = [dims_of(x, y)[1] // t for x, y, t in zip(a_list, b_list, tks)]
    offs = [sum(nks[:t]) for t in range(nterm)]
    nk = sum(nks)
    dims = {"nn": _NN, "nt": _NT, "tn": _TN}[mode]

    def specs_of(t):
        kk = lambda k: jnp.clip(k - offs[t], 0, nks[t] - 1)
        a_spec = (pl.BlockSpec((tks[t], tm), lambda i, j, k: (kk(k), i)) if mode == "tn"
                  else pl.BlockSpec((tm, tks[t]), lambda i, j, k: (i, kk(k))))
        b_spec = (pl.BlockSpec((tn, tks[t]), lambda i, j, k: (j, kk(k))) if mode == "nt"
                  else pl.BlockSpec((tks[t], tn), lambda i, j, k: (kk(k), j)))
        return [a_spec, b_spec]

    o_spec = pl.BlockSpec((tm, tn), lambda i, j, k: (i, j))
    has_c = c is not None
    own_acc = nk > 1 and out_dtype != F32

    def body(*refs):
        ab = refs[:2 * nterm]
        c_ref = refs[2 * nterm] if has_c else None
        o_ref = refs[2 * nterm + (1 if has_c else 0)]
        acc_ref = refs[-1] if own_acc else o_ref

        def dot(t):
            return lax.dot_general(ab[2 * t][...].astype(BF16), ab[2 * t + 1][...].astype(BF16), dims,
                                   preferred_element_type=F32)

        if nk == 1:
            o_ref[...] = (dot(0) + c_ref[...] if has_c else dot(0)).astype(o_ref.dtype)
        else:
            k = pl.program_id(2)

            @pl.when(k == 0)
            def _():
                acc_ref[...] = c_ref[...] if has_c else jnp.zeros_like(acc_ref)

            for t in range(nterm):
                if nterm == 1:
                    acc_ref[...] += dot(t)
                else:
                    @pl.when((k >= offs[t]) & (k < offs[t] + nks[t]))
                    def _(t=t):
                        acc_ref[...] += dot(t)

            if own_acc:
                @pl.when(k == nk - 1)
                def _():
                    o_ref[...] = acc_ref[...].astype(o_ref.dtype)

    ins, specs = [], []
    for t in range(nterm):
        ins += [a_list[t], b_list[t]]
        specs += specs_of(t)
    if has_c:
        ins.append(c)
        specs.append(o_spec)
    return pl.pallas_call(
        body, name=name, grid=(M // tm, N // tn, nk), in_specs=specs, out_specs=o_spec,
        out_shape=jax.ShapeDtypeStruct((M, N), out_dtype), scratch_shapes=[pltpu.VMEM((tm, tn), F32)] if own_acc else [],
        compiler_params=_params(("parallel", "parallel", "arbitrary")),
    )(*ins)


def rowcall(name, fn, ins, in_specs, outs, out_specs, acc, nrow, ncol=1, scratch=()):
    n_in, n_out = len(ins), len(outs)

    def body(*refs):
        i = pl.program_id(1)
        res = fn(i, *[r[...] for r in refs[:n_in]], *refs[n_in + n_out:])
        for r, v, is_acc in zip(refs[n_in:n_in + n_out], res, acc):
            if is_acc:
                @pl.when(i == 0)
                def _(r=r, v=v):
                    r[...] = v.astype(r.dtype)

                @pl.when(i > 0)
                def _(r=r, v=v):
                    r[...] += v.astype(r.dtype)
            else:
                r[...] = v.astype(r.dtype)

    return pl.pallas_call(
        body, name=name, grid=(ncol, nrow), in_specs=list(in_specs), out_specs=list(out_specs), out_shape=list(outs),
        scratch_shapes=list(scratch), compiler_params=_params(("parallel", "arbitrary")),
    )(*ins)


class Tiles:
    def __init__(self, T, tm):
        self.T, self.tm, self.n = T, tm, T // tm
        self.r8 = tm // SUBLANES

    def row(self, w, cb=0):
        return pl.BlockSpec((self.tm, w), lambda j, i: (i, cb))

    def rowj(self, tc):
        return pl.BlockSpec((self.tm, tc), lambda j, i: (i, j))

    def prev(self, w, cb=0):
        return pl.BlockSpec((SUBLANES, w), lambda j, i: (jnp.maximum(i * self.r8 - 1, 0), cb))

    def prevj(self, tc):
        return pl.BlockSpec((SUBLANES, tc), lambda j, i: (jnp.maximum(i * self.r8 - 1, 0), j))

    def nxt(self, w, cb=0):
        last = self.T // SUBLANES - 1
        return pl.BlockSpec((SUBLANES, w), lambda j, i: (jnp.minimum((i + 1) * self.r8, last), cb))

    def nxtj(self, tc):
        last = self.T // SUBLANES - 1
        return pl.BlockSpec((SUBLANES, tc), lambda j, i: (jnp.minimum((i + 1) * self.r8, last), j))

    def heads(self, H):
        return pl.BlockSpec((H, self.tm, HEAD_DIM), lambda j, i: (0, i, 0))

    def heads_nxt(self, H):
        last = self.T // SUBLANES - 1
        return pl.BlockSpec((H, SUBLANES, HEAD_DIM), lambda j, i: (0, jnp.minimum((i + 1) * self.r8, last), 0))


def full(shape):
    return pl.BlockSpec(tuple(shape), lambda j, i: (0,) * len(shape))


def constj(r, tc):
    return pl.BlockSpec((r, tc), lambda j, i: (0, j))


def sds(shape, dtype=F32):
    return jax.ShapeDtypeStruct(tuple(shape), dtype)


def rms(x, g):
    return x * lax.rsqrt(jnp.mean(x * x, axis=-1, keepdims=True) + EPS) * g


def sigmoid(x):
    return jax.nn.sigmoid(x)


def silu(x):
    return x * sigmoid(x)


def gelu(x):
    return 0.5 * x * (1.0 + lax.erf(x * (2.0 ** -0.5)))


def fill_window(win, i, last, prev, x, nxt=None):
    R = x.shape[0]
    win[0:SUBLANES, :] = jnp.where(i > 0, prev, 0.0)
    win[SUBLANES:SUBLANES + R, :] = x
    if nxt is not None:
        win[SUBLANES + R:2 * SUBLANES + R, :] = jnp.where(last, 0.0, nxt)


def conv_taps(win, K, R):
    base = SUBLANES - (K - 1)
    return [win[pl.ds(base + j, R), :] for j in range(K)]


def value_taps(xwin, K, R):
    base = SUBLANES - (K - 1)
    return [xwin[base + j:base + j + R, :] for j in range(K)]


def causal_conv(taps, w):
    out = w[0:1, :] * taps[0]
    for j in range(1, len(taps)):
        out = out + w[j:j + 1, :] * taps[j]
    return out


def rows_to8(rows, C):
    rid = lax.broadcasted_iota(jnp.int32, (SUBLANES, C), 0)
    out = jnp.zeros((SUBLANES, C), F32)
    for k, r in enumerate(rows):
        out = out + jnp.where(rid == k, jnp.broadcast_to(r, (SUBLANES, C)), 0.0)
    return out


def dn_qkv(pre, H):
    a = silu(pre)
    W = H * HEAD_DIM

    def l2(t):
        return t * lax.rsqrt(jnp.sum(t * t, axis=-1, keepdims=True) + EPS)

    q = [l2(a[:, h * HEAD_DIM:(h + 1) * HEAD_DIM]) for h in range(H)]
    k = [l2(a[:, W + h * HEAD_DIM:W + (h + 1) * HEAD_DIM]) for h in range(H)]
    v = [a[:, 2 * W + h * HEAD_DIM:2 * W + (h + 1) * HEAD_DIM] for h in range(H)]
    return q, k, v


def dn_gates(ba, alog, dtb, H, R):
    lane = lax.broadcasted_iota(jnp.int32, (R, LANES), 1)
    beta = sigmoid(ba)
    g = -jnp.exp(alog) * jax.nn.softplus(ba + dtb)
    g = jnp.where((lane >= H) & (lane < 2 * H), g, 0.0)
    ri = lax.broadcasted_iota(jnp.int32, (R, R), 0)
    ci = lax.broadcasted_iota(jnp.int32, (R, R), 1)
    cum = jnp.where((ri // DN_CHUNK == ci // DN_CHUNK) & (ci <= ri), 1.0, 0.0).astype(F32)
    gc = mm_nn(cum, g, 2)
    return jnp.where(lane < H, beta, gc)


def neumann_inverse(Ls):
    C = Ls[0].shape[0]
    ri = lax.broadcasted_iota(jnp.int32, (C, C), 0)
    ci = lax.broadcasted_iota(jnp.int32, (C, C), 1)
    eye = jnp.where(ri == ci, 1.0, 0.0).astype(F32)
    P = [-L for L in Ls]
    R = [eye + p for p in P]
    for _ in range(int(math.log2(C)) - 1):
        P = [mm_nn(p, p, 1) for p in P]
        R = [r + mm_nn(r, p, 1) for r, p in zip(R, P)]
    return R


@jax.custom_vjp
def saved_inverse(L, T):
    return T


def _saved_inverse_f(L, T):
    return T, T


def _saved_inverse_b(T, g):
    return -mm_tn(T, mm_nt(g, T, 1), 1), jnp.zeros_like(T)


saved_inverse.defvjp(_saved_inverse_f, _saved_inverse_b)


def gate_columns(bg, H):
    bgT = bg.T
    return ([bg[:, h:h + 1] for h in range(H)], [bg[:, H + h:H + h + 1] for h in range(H)],
            [bgT[H + h:H + h + 1, :] for h in range(H)])


def dn_chunk(q, k, v, beta, gc, gr, tinv=None, with_inverse=False):
    n = len(q)
    C = q[0].shape[0]
    ri = lax.broadcasted_iota(jnp.int32, (C, C), 0)
    ci = lax.broadcasted_iota(jnp.int32, (C, C), 1)
    qs = [q[h] * (HEAD_DIM ** -0.5) for h in range(n)]
    kb = [k[h] * beta[h] for h in range(n)]
    vb = [v[h] * beta[h] for h in range(n)]
    decay = [jnp.exp(jnp.where(ri >= ci, gc[h] - gr[h], -jnp.inf)) for h in range(n)]
    L = [jnp.where(ri > ci, mm_nt(kb[h], k[h]) * decay[h], 0.0) for h in range(n)]
    attn = [jnp.where(ri >= ci, mm_nt(qs[h], k[h]) * decay[h], 0.0) for h in range(n)]
    Tinv = neumann_inverse(L) if tinv is None else [saved_inverse(L[h], tinv[h]) for h in range(n)]
    eg = [jnp.exp(gc[h]) for h in range(n)]
    u = [mm_nn(Tinv[h], vb[h]) for h in range(n)]
    w = [mm_nn(Tinv[h], kb[h] * eg[h]) for h in range(n)]
    qd = [qs[h] * eg[h] for h in range(n)]
    gl = [gc[h][C - 1:C, :] for h in range(n)]
    kd = [k[h] * jnp.exp(gl[h] - gc[h]) for h in range(n)]
    return (u, w, attn, qd, kd, gl, Tinv) if with_inverse else (u, w, attn, qd, kd, gl)


def dn_step(u, w, a, qd, kd, gl, S):
    n = len(u)
    v_new = [u[h] - mm_nn(w[h], S[h]) for h in range(n)]
    o = [mm_nn(qd[h], S[h]) + mm_nn(a[h], v_new[h]) for h in range(n)]
    S_new = [S[h] * jnp.exp(gl[h]) + mm_tn(kd[h], v_new[h]) for h in range(n)]
    return o, S_new


def dn_post(o, z, g):
    H = o.shape[0]
    return jnp.concatenate([rms(o[h], g) * silu(z[:, h * HEAD_DIM:(h + 1) * HEAD_DIM]) for h in range(H)], axis=1)


def gmlp(u_raw, v_raw, ln_g, ln_b, sgw, sgbT):
    R = u_raw.shape[0]
    G = sgw.shape[0]
    nc = R // SG_CHUNK
    u = gelu(u_raw)
    vv = gelu(v_raw)
    xc = vv - jnp.mean(vv, axis=-1, keepdims=True)
    vg = xc * lax.rsqrt(jnp.mean(xc * xc, axis=-1, keepdims=True) + EPS) * ln_g + ln_b
    ri = lax.broadcasted_iota(jnp.int32, (SG_CHUNK, SG_CHUNK), 0)
    ci = lax.broadcasted_iota(jnp.int32, (SG_CHUNK, SG_CHUNK), 1)
    cols = []
    for g in range(G):
        ws = jnp.where(ri >= ci, sgw[g], 0.0)
        rhs = jnp.concatenate([vg[c * SG_CHUNK:(c + 1) * SG_CHUNK, g * HEAD_DIM:(g + 1) * HEAD_DIM] for c in range(nc)], axis=1)
        mixed = mm_nn(ws, rhs) + sgbT[:, g:g + 1]
        cols.append(jnp.concatenate([mixed[:, c * HEAD_DIM:(c + 1) * HEAD_DIM] for c in range(nc)], axis=0))
    return u * jnp.concatenate(cols, axis=1)


def merge(ga, gb, ap, bp):
    return sigmoid(ga) * ap + sigmoid(gb) * bp


def norm_fwd(x, g, name, tm=256):
    T, D = x.shape
    tl = Tiles(T, _pick(T, tm))
    (h,) = rowcall(name, lambda i, x, g: (rms(x, g),), [x, g], [tl.row(D), full((1, D))],
                   [sds((T, D), BF16)], [tl.row(D)], [False], tl.n)
    return h


def norm_bwd(x, g, dh, dres, name, tm=256):
    T, D = x.shape
    tl = Tiles(T, _pick(T, tm))

    def fn(i, x, g, dh, dres):
        _, vj = jax.vjp(rms, x, g)
        dx, dg = vj(dh.astype(F32))
        dx = dx + dres
        return dx, dx, dg

    return rowcall(name, fn, [x, g, dh, dres], [tl.row(D), full((1, D)), tl.row(D), tl.row(D)],
                   [sds((T, D)), sds((T, D), BF16), sds((1, D))], [tl.row(D), tl.row(D), full((1, D))],
                   [False, False, True], tl.n)


def head_fwd_bwd(x, g, tgt, name, tm=256):
    T, D = x.shape
    tl = Tiles(T, _pick(T, tm))

    def fn(i, x, g, tgt):
        y, vj = jax.vjp(rms, x, g)
        e = y - tgt
        loss = 0.5 * jnp.sum(jnp.mean(e * e, axis=-1, keepdims=True), axis=0, keepdims=True)
        dx, dg = vj(e * (1.0 / D))
        return loss, dx, dx, dg

    return rowcall(name, fn, [x, g, tgt], [tl.row(D), full((1, D)), tl.row(D)],
                   [sds((1, 1)), sds((T, D)), sds((T, D), BF16), sds((1, D))],
                   [full((1, 1)), tl.row(D), tl.row(D), full((1, D))], [True, False, False, True], tl.n)


def dn_prep_fwd(projA, pba, cw8, alog, dtb, H, name, tm=256):
    T = projA.shape[0]
    W3 = 3 * H * HEAD_DIM
    tl = Tiles(T, _pick(T, tm, DN_CHUNK))
    R = tl.tm

    def fn(i, xp, x, ba, cw, alog, dtb, win):
        fill_window(win, i, None, xp, x)
        q, k, v = dn_qkv(causal_conv(conv_taps(win, 4, R), cw), H)
        return jnp.stack(q), jnp.stack(k), jnp.stack(v), dn_gates(ba, alog, dtb, H, R)

    hs = sds((H, T, HEAD_DIM))
    return rowcall(name, fn, [projA, projA, pba, cw8, alog, dtb],
                   [tl.prev(W3), tl.row(W3), tl.row(LANES), full((SUBLANES, W3)), full((1, LANES)), full((1, LANES))],
                   [hs, hs, hs, sds((T, LANES))], [tl.heads(H)] * 3 + [tl.row(LANES)], [False] * 4, tl.n,
                   scratch=[pltpu.VMEM((SUBLANES + R, W3), F32)])


def dn_prep_bwd(projA, pba, cw8, alog, dtb, dq, dk, dv, dbg, H, name, tm=256):
    T = projA.shape[0]
    W3 = 3 * H * HEAD_DIM
    tl = Tiles(T, _pick(T, tm, DN_CHUNK))
    R = tl.tm
    RE = R + SUBLANES

    def fn(i, xp, x, xn, ba, cw, alog, dtb, dq, dk, dv, dqn, dkn, dvn, dbg, win, dp):
        last = i == tl.n - 1
        fill_window(win, i, last, xp, x, xn)
        taps = conv_taps(win, 4, RE)
        pre = causal_conv(taps, cw)
        ext = lambda d, dn: [jnp.concatenate([d[h], jnp.where(last, 0.0, dn[h])], axis=0) for h in range(H)]
        _, vj = jax.vjp(lambda p: dn_qkv(p, H), pre)
        (dpre,) = vj((ext(dq, dqn), ext(dk, dkn), ext(dv, dvn)))
        dp[...] = dpre
        dx = cw[3:4, :] * dpre[0:R, :]
        for j in range(3):
            dx = dx + cw[j:j + 1, :] * dp[pl.ds(3 - j, R), :]
        dcw = rows_to8([jnp.sum(dpre[0:R, :] * taps[j][0:R, :], axis=0, keepdims=True) for j in range(4)], W3)
        _, vjg = jax.vjp(lambda ba, alog, dtb: dn_gates(ba, alog, dtb, H, R), ba, alog, dtb)
        dba, dalog, ddtb = vjg(dbg)
        return dx, dba, dcw, dalog, ddtb

    return rowcall(name, fn, [projA, projA, projA, pba, cw8, alog, dtb, dq, dk, dv, dq, dk, dv, dbg],
                   [tl.prev(W3), tl.row(W3), tl.nxt(W3), tl.row(LANES), full((SUBLANES, W3)), full((1, LANES)), full((1, LANES))]
                   + [tl.heads(H)] * 3 + [tl.heads_nxt(H)] * 3 + [tl.row(LANES)],
                   [sds((T, W3), BF16), sds((T, LANES), BF16), sds((SUBLANES, W3)), sds((1, LANES)), sds((1, LANES))],
                   [tl.row(W3), tl.row(LANES), full((SUBLANES, W3)), full((1, LANES)), full((1, LANES))],
                   [False, False, True, True, True], tl.n,
                   scratch=[pltpu.VMEM((2 * SUBLANES + R, W3), F32), pltpu.VMEM((RE, W3), F32)])


def _scan_chunks_per_step(N, want=4):
    while N % want:
        want //= 2
    return want


def _multi_chunk_specs(H, C, P):
    hs = pl.BlockSpec((H, P * C, HEAD_DIM), lambda n: (0, n, 0))
    at = pl.BlockSpec((H, P * C, C), lambda n: (0, n, 0))
    one = pl.BlockSpec((H, P, 1, 1), lambda n: (0, n, 0, 0))
    gate = pl.BlockSpec((P * C, LANES), lambda n: (n, 0))
    return hs, at, one, gate


def dn_chunk_fwd(q, k, v, bg, name):
    H, T, _ = q.shape
    C = DN_CHUNK
    N = T // C
    P = _scan_chunks_per_step(N, 2)
    hs, at, one, gate = _multi_chunk_specs(H, C, P)

    def body(q, k, v, bg, u, w, a, qd, kd, gl, ti):
        pr = [(c, h) for c in range(P) for h in range(H)]
        rows = lambda c: slice(c * C, (c + 1) * C)
        cols = [gate_columns(bg[rows(c), :], H) for c in range(P)]
        res = dn_chunk([q[h, rows(c)] for c, h in pr], [k[h, rows(c)] for c, h in pr], [v[h, rows(c)] for c, h in pr],
                       [cols[c][0][h] for c, h in pr], [cols[c][1][h] for c, h in pr], [cols[c][2][h] for c, h in pr],
                       with_inverse=True)
        for i, (c, h) in enumerate(pr):
            for ref, val in zip((u, w, a, qd, kd), res[:5]):
                ref[h, rows(c)] = val[i]
            gl[h, c] = res[5][i]
            ti[h, rows(c)] = res[6][i]

    big = sds((H, T, HEAD_DIM))
    return pl.pallas_call(
        body, name=name, grid=(N // P,), in_specs=[hs, hs, hs, gate], out_specs=[hs, hs, at, hs, hs, one, at],
        out_shape=[big, big, sds((H, T, C)), big, big, sds((H, N, 1, 1)), sds((H, T, C))],
        compiler_params=_params(("parallel",)),
    )(q, k, v, bg)


def dn_chunk_bwd(q, k, v, bg, tinv, du, dw, da, dqd, dkd, dgl, name):
    H, T, _ = q.shape
    C = DN_CHUNK
    N = T // C
    P = _scan_chunks_per_step(N, 2)
    hs, at, one, gate = _multi_chunk_specs(H, C, P)

    def body(q, k, v, bg, ti, du, dw, da, dqd, dkd, dgl, dq, dk, dv, dbg):
        pr = [(c, h) for c in range(P) for h in range(H)]
        rws = lambda c: slice(c * C, (c + 1) * C)
        gcols = [gate_columns(bg[rws(c), :], H) for c in range(P)]
        f = lambda q, k, v, b, gc, gr: dn_chunk(q, k, v, b, gc, gr, tinv=[ti[h, rws(c)] for c, h in pr])
        _, vj = jax.vjp(f, [q[h, rws(c)] for c, h in pr], [k[h, rws(c)] for c, h in pr], [v[h, rws(c)] for c, h in pr],
                        [gcols[c][0][h] for c, h in pr], [gcols[c][1][h] for c, h in pr], [gcols[c][2][h] for c, h in pr])
        res = vj(([du[h, rws(c)] for c, h in pr], [dw[h, rws(c)] for c, h in pr], [da[h, rws(c)] for c, h in pr],
                  [dqd[h, rws(c)] for c, h in pr], [dkd[h, rws(c)] for c, h in pr], [dgl[h, c] for c, h in pr]))
        lane = lax.broadcasted_iota(jnp.int32, (C, LANES), 1)
        row = lax.broadcasted_iota(jnp.int32, (LANES, C), 0)
        for c in range(P):
            cols = jnp.zeros((C, LANES), F32)
            rows = jnp.zeros((LANES, C), F32)
            for h in range(H):
                i = c * H + h
                dq[h, rws(c)], dk[h, rws(c)], dv[h, rws(c)] = res[0][i], res[1][i], res[2][i]
                cols = cols + jnp.where(lane == h, res[3][i], 0.0) + jnp.where(lane == H + h, res[4][i], 0.0)
                rows = rows + jnp.where(row == H + h, res[5][i], 0.0)
            dbg[rws(c), :] = cols + rows.T

    big = sds((H, T, HEAD_DIM))
    return pl.pallas_call(
        body, name=name, grid=(N // P,), in_specs=[hs, hs, hs, gate, at, hs, hs, at, hs, hs, one],
        out_specs=[hs, hs, hs, gate], out_shape=[big, big, big, sds((T, LANES))], compiler_params=_params(("parallel",)),
    )(q, k, v, bg, tinv, du, dw, da, dqd, dkd, dgl)


def dn_scan_fwd(u, w, a, qd, kd, gl, name):
    H, T, _ = u.shape
    C = DN_CHUNK
    N = T // C
    P = _scan_chunks_per_step(N)
    hs = pl.BlockSpec((H, P * C, HEAD_DIM), lambda n: (0, n, 0))
    at = pl.BlockSpec((H, P * C, C), lambda n: (0, n, 0))
    one = pl.BlockSpec((H, P, 1, 1), lambda n: (0, n, 0, 0))
    st = pl.BlockSpec((P, H, HEAD_DIM, HEAD_DIM), lambda n: (n, 0, 0, 0))

    def body(u, w, a, qd, kd, gl, o, s_in, S):
        @pl.when(pl.program_id(0) == 0)
        def _():
            S[...] = jnp.zeros_like(S)

        hd = range(H)
        s = [S[h] for h in hd]
        for c in range(P):
            rows = slice(c * C, (c + 1) * C)
            o_new, s_new = dn_step([u[h, rows] for h in hd], [w[h, rows] for h in hd], [a[h, rows] for h in hd],
                                   [qd[h, rows] for h in hd], [kd[h, rows] for h in hd], [gl[h, c] for h in hd], s)
            for h in hd:
                s_in[c, h] = s[h]
                o[h, rows] = o_new[h]
            s = s_new
        for h in hd:
            S[h] = s[h]

    return pl.pallas_call(
        body, name=name, grid=(N // P,), in_specs=[hs, hs, at, hs, hs, one], out_specs=[hs, st],
        out_shape=[sds((H, T, HEAD_DIM)), sds((N, H, HEAD_DIM, HEAD_DIM))],
        scratch_shapes=[pltpu.VMEM((H, HEAD_DIM, HEAD_DIM), F32)], compiler_params=_params(("arbitrary",)),
    )(u, w, a, qd, kd, gl)


def dn_scan_bwd(u, w, a, qd, kd, gl, s_in, do, name):
    H, T, _ = u.shape
    C = DN_CHUNK
    N = T // C
    P = _scan_chunks_per_step(N)
    nb = N // P
    hs = pl.BlockSpec((H, P * C, HEAD_DIM), lambda n: (0, nb - 1 - n, 0))
    at = pl.BlockSpec((H, P * C, C), lambda n: (0, nb - 1 - n, 0))
    one = pl.BlockSpec((H, P, 1, 1), lambda n: (0, nb - 1 - n, 0, 0))
    st = pl.BlockSpec((P, H, HEAD_DIM, HEAD_DIM), lambda n: (nb - 1 - n, 0, 0, 0))

    def body(u, w, a, qd, kd, gl, s_in, do, du, dw, da, dqd, dkd, dgl, dS):
        @pl.when(pl.program_id(0) == 0)
        def _():
            dS[...] = jnp.zeros_like(dS)

        hd = range(H)
        ds = [dS[h] for h in hd]
        for c in reversed(range(P)):
            rows = slice(c * C, (c + 1) * C)
            _, vj = jax.vjp(dn_step, [u[h, rows] for h in hd], [w[h, rows] for h in hd], [a[h, rows] for h in hd],
                            [qd[h, rows] for h in hd], [kd[h, rows] for h in hd], [gl[h, c] for h in hd],
                            [s_in[c, h] for h in hd])
            res = vj(([do[h, rows] for h in hd], ds))
            for h in hd:
                du[h, rows], dw[h, rows], da[h, rows], dqd[h, rows], dkd[h, rows] = (res[j][h] for j in range(5))
                dgl[h, c] = res[5][h]
            ds = res[6]
        for h in hd:
            dS[h] = ds[h]

    big = sds((H, T, HEAD_DIM))
    return pl.pallas_call(
        body, name=name, grid=(nb,), in_specs=[hs, hs, at, hs, hs, one, st, hs], out_specs=[hs, hs, at, hs, hs, one],
        out_shape=[big, big, sds((H, T, C)), big, big, sds((H, N, 1, 1))],
        scratch_shapes=[pltpu.VMEM((H, HEAD_DIM, HEAD_DIM), F32)], compiler_params=_params(("arbitrary",)),
    )(u, w, a, qd, kd, gl, s_in, do)


def dn_post_fwd(o, projA, g, name, tm=256):
    H, T, _ = o.shape
    W = H * HEAD_DIM
    tl = Tiles(T, _pick(T, tm))
    (y,) = rowcall(name, lambda i, o, z, g: (dn_post(o, z, g),), [o, projA, g], [tl.heads(H), tl.row(W, 3), full((1, HEAD_DIM))],
                   [sds((T, W), BF16)], [tl.row(W)], [False], tl.n)
    return y


def dn_post_bwd(o, projA, g, dy, name, tm=256):
    H, T, _ = o.shape
    W = H * HEAD_DIM
    tl = Tiles(T, _pick(T, tm))

    def fn(i, o, z, g, dy):
        _, vj = jax.vjp(dn_post, o, z, g)
        return vj(dy.astype(F32))

    return rowcall(name, fn, [o, projA, g, dy], [tl.heads(H), tl.row(W, 3), full((1, HEAD_DIM)), tl.row(W)],
                   [sds((H, T, HEAD_DIM)), sds((T, W), BF16), sds((1, HEAD_DIM))],
                   [tl.heads(H), tl.row(W), full((1, HEAD_DIM))], [False, False, True], tl.n)


def gmlp_fwd(projB, ln_g, ln_b, sgw, sgbT, name, tm=512):
    T = projB.shape[0]
    G = sgw.shape[0]
    W = G * HEAD_DIM
    tl = Tiles(T, _pick(T, tm))
    (y,) = rowcall(name, lambda i, *a: (gmlp(*a),), [projB, projB, ln_g, ln_b, sgw, sgbT],
                   [tl.row(W, 0), tl.row(W, 1), full((1, W)), full((1, W)), full(sgw.shape), full(sgbT.shape)],
                   [sds((T, W), BF16)], [tl.row(W)], [False], tl.n)
    return y


def gmlp_bwd(projB, ln_g, ln_b, sgw, sgbT, dy, name, tm=512):
    T = projB.shape[0]
    G = sgw.shape[0]
    W = G * HEAD_DIM
    tl = Tiles(T, _pick(T, tm))

    def fn(i, u_raw, v_raw, ln_g, ln_b, sgw, sgbT, dy):
        _, vj = jax.vjp(gmlp, u_raw, v_raw, ln_g, ln_b, sgw, sgbT)
        return vj(dy.astype(F32))

    return rowcall(name, fn, [projB, projB, ln_g, ln_b, sgw, sgbT, dy],
                   [tl.row(W, 0), tl.row(W, 1), full((1, W)), full((1, W)), full(sgw.shape), full(sgbT.shape), tl.row(W)],
                   [sds((T, W), BF16), sds((T, W), BF16), sds((1, W)), sds((1, W)), sds(sgw.shape), sds(sgbT.shape)],
                   [tl.row(W), tl.row(W), full((1, W)), full((1, W)), full(sgw.shape), full(sgbT.shape)],
                   [False, False, True, True, True, True], tl.n)


def merge_fwd(projB, ap, bp, cb_a, name, tm=256):
    T, D = ap.shape
    tl = Tiles(T, _pick(T, tm))
    (m,) = rowcall(name, lambda i, *a: (merge(*a),), [projB, projB, ap, bp],
                   [tl.row(D, cb_a), tl.row(D, cb_a + 1), tl.row(D), tl.row(D)], [sds((T, D), BF16)], [tl.row(D)], [False], tl.n)
    return m


def merge_bwd(projB, ap, bp, dm, cb_a, name, tm=256):
    T, D = ap.shape
    tl = Tiles(T, _pick(T, tm))

    def fn(i, ga, gb, ap, bp, dm):
        _, vj = jax.vjp(merge, ga, gb, ap, bp)
        return vj(dm.astype(F32))

    return rowcall(name, fn, [projB, projB, ap, bp, dm], [tl.row(D, cb_a), tl.row(D, cb_a + 1), tl.row(D), tl.row(D), tl.row(D)],
                   [sds((T, D), BF16)] * 4, [tl.row(D)] * 4, [False] * 4, tl.n)


def ffn_act_fwd(gp, up, fcw8, fcb, name, tm=256, tc=512):
    T, F = gp.shape
    tl = Tiles(T, _pick(T, tm))
    tc = _pick(F, tc)
    R = tl.tm

    def fn(i, gprev, g, up, cw, cb):
        xwin = jnp.concatenate([jnp.where(i > 0, gprev, 0.0), g], axis=0)
        return (silu(causal_conv(value_taps(xwin, 3, R), cw) + cb) * up,)

    (act,) = rowcall(name, fn, [gp, gp, up, fcw8, fcb], [tl.prevj(tc), tl.rowj(tc), tl.rowj(tc), constj(SUBLANES, tc), constj(1, tc)],
                     [sds((T, F), BF16)], [tl.rowj(tc)], [False], tl.n, F // tc)
    return act


def ffn_act_bwd(gp, up, fcw8, fcb, dact, name, tm=256, tc=512):
    T, F = gp.shape
    tl = Tiles(T, _pick(T, tm))
    tc = _pick(F, tc)
    R = tl.tm
    RE = R + SUBLANES

    def fn(i, gprev, g, gnext, up, upn, da, dan, cw, cb):
        last = i == tl.n - 1
        xwin = jnp.concatenate([jnp.where(i > 0, gprev, 0.0), g, jnp.where(last, 0.0, gnext)], axis=0)
        taps = value_taps(xwin, 3, RE)
        gate = causal_conv(taps, cw) + cb
        upe = jnp.concatenate([up, upn], axis=0)
        dae = jnp.concatenate([da, jnp.where(last, 0.0, dan)], axis=0)
        s = sigmoid(gate)
        dgate = dae * upe * (s * (1.0 + gate * (1.0 - s)))
        dup = da * (gate[0:R, :] * s[0:R, :])
        dgp = cw[0:1, :] * dgate[2:2 + R, :] + cw[1:2, :] * dgate[1:1 + R, :] + cw[2:3, :] * dgate[0:R, :]
        dcw = rows_to8([jnp.sum(dgate[0:R, :] * taps[j][0:R, :], axis=0, keepdims=True) for j in range(3)], tc)
        dcb = jnp.sum(dgate[0:R, :], axis=0, keepdims=True)
        return dgp, dup, dcw, dcb

    return rowcall(name, fn, [gp, gp, gp, up, up, dact, dact, fcw8, fcb],
                   [tl.prevj(tc), tl.rowj(tc), tl.nxtj(tc), tl.rowj(tc), tl.nxtj(tc), tl.rowj(tc), tl.nxtj(tc),
                    constj(SUBLANES, tc), constj(1, tc)],
                   [sds((T, F), BF16), sds((T, F), BF16), sds((SUBLANES, F)), sds((1, F))],
                   [tl.rowj(tc), tl.rowj(tc), constj(SUBLANES, tc), constj(1, tc)], [False, False, True, True], tl.n, F // tc)


def _me():
    return lax.axis_index("x"), lax.axis_index("y"), lax.axis_index("c")


def all_gather(shards, name):
    nt = len(shards)

    def body(*refs):
        xs, outs = refs[:nt], refs[nt:2 * nt]
        send_sems, recv_sems, local_sems = refs[2 * nt:]
        x, y, c = _me()
        me, sibling = (x, y, c), (x, y, 1 - c)
        chips = [(1 - x, y), (x, 1 - y), (1 - x, 1 - y)]

        def slot(t, p):
            return outs[t].at[4 * p[0] + 2 * p[1] + p[2]]

        def copy(t, k, block, to, src=None):
            return pltpu.make_async_remote_copy(
                src_ref=slot(t, block) if src is None else src, dst_ref=slot(t, block),
                send_sem=send_sems.at[t, k], recv_sem=recv_sems.at[t, k], device_id=to, device_id_type=MESH)

        mine = [pltpu.make_async_copy(xs[t], slot(t, me), local_sems.at[t]) for t in range(nt)]
        first = []
        for t in range(nt):
            mine[t].start()
            first.append(copy(t, 0, me, sibling, src=xs[t]))
            first += [copy(t, 1 + j, me, (*chip, c), src=xs[t]) for j, chip in enumerate(chips)]
        for cp in first:
            cp.start()
        passed = []
        for j, chip in enumerate(chips):
            for t in range(nt):
                copy(t, 1 + j, (*chip, c), me).wait_recv()
                cp = copy(t, 4 + j, (*chip, c), sibling)
                cp.start()
                passed.append(cp)
        for t in range(nt):
            copy(t, 0, sibling, me).wait_recv()
            for j, chip in enumerate(chips):
                copy(t, 4 + j, (*chip, 1 - c), me).wait_recv()
        for cp in first + passed:
            cp.wait_send()
        for t in range(nt):
            mine[t].wait()

    any_spec = pl.BlockSpec(memory_space=pl.ANY)
    return pl.pallas_call(
        body, name=name, in_specs=[any_spec] * nt, out_specs=[any_spec] * nt,
        out_shape=[jax.ShapeDtypeStruct((N_DEV,) + s.shape, s.dtype) for s in shards],
        scratch_shapes=[pltpu.SemaphoreType.DMA((nt, 7)), pltpu.SemaphoreType.DMA((nt, 7)), pltpu.SemaphoreType.DMA((nt,))],
    )(*shards)


_HBM = pl.BlockSpec(memory_space=pltpu.HBM)
_SEM = pl.BlockSpec(memory_space=pltpu.SEMAPHORE)
_ANY = pl.BlockSpec(memory_space=pl.ANY)
_DATAFLOW = pltpu.SideEffectType.DATAFLOW_SIDE_EFFECTING


def _peers():
    x, y, c = _me()
    out = []
    for k in range(1, N_DEV):
        p = (x ^ (k >> 2), y ^ ((k >> 1) & 1), c ^ (k & 1))
        out.append((k, p, 4 * p[0] + 2 * p[1] + p[2]))
    return out


def _split_copy(src, land, send_sems, recv_sems, t, k, peer, slot, my, scatter, receiving):
    return pltpu.make_async_remote_copy(
        src_ref=src.at[slot] if scatter else land.at[my], dst_ref=land.at[slot if receiving else my],
        send_sem=send_sems.at[t * (N_DEV - 1) + k - 1], recv_sem=recv_sems.at[t * (N_DEV - 1) + k - 1],
        device_id=peer, device_id_type=MESH)


def comm_start(groups, scatter, name, after=None):
    flat = [a for g in groups for a in g]
    nt = len(flat)
    if scatter:
        lands = [lax.empty(a.shape, a.dtype) for a in flat]
    else:
        me = 4 * lax.axis_index("x") + 2 * lax.axis_index("y") + lax.axis_index("c")
        lands = [lax.dynamic_update_index_in_dim(lax.empty((N_DEV,) + a.shape, a.dtype), a, me, 0) for a in flat]
    ng = len(groups)
    n_after = 0 if after is None else 1

    def body(*refs):
        src, land = refs[:nt], refs[nt:2 * nt]
        sems = refs[2 * nt + n_after:2 * nt + n_after + 2 * ng]
        token = refs[-1]
        x, y, c = _me()
        my = 4 * x + 2 * y + c
        t0 = 0
        for gi, g in enumerate(groups):
            for k, peer, slot in _peers():
                for t in range(len(g)):
                    _split_copy(src[t0 + t], land[t0 + t], sems[2 * gi], sems[2 * gi + 1], t, k, peer, slot, my, scatter,
                                False).start()
            t0 += len(g)
        token[...] = jnp.zeros_like(token)

    sem_shapes = []
    for g in groups:
        sem_shapes += [pltpu.SemaphoreType.DMA((len(g) * (N_DEV - 1),))] * 2
    res = pl.pallas_call(
        body, name=name, in_specs=[_HBM] * (2 * nt) + [_HBM] * n_after,
        out_specs=[_SEM] * (2 * ng) + [_HBM] * (2 * nt) + [pl.BlockSpec(memory_space=pltpu.VMEM)],
        out_shape=sem_shapes + [pltpu.HBM(a.shape, a.dtype) for a in flat + lands] + [sds((SUBLANES, LANES))],
        input_output_aliases={i: 2 * ng + i for i in range(2 * nt)},
        compiler_params=pltpu.CompilerParams(has_side_effects=_DATAFLOW),
    )(*[pltpu.with_memory_space_constraint(a, pltpu.HBM) for a in flat + lands + ([] if after is None else [after])])
    handles = []
    t0 = 0
    for gi, g in enumerate(groups):
        n = len(g)
        handles.append(dict(sems=(res[2 * gi], res[2 * gi + 1]), src=res[2 * ng + t0:2 * ng + t0 + n],
                            land=res[2 * ng + nt + t0:2 * ng + nt + t0 + n], scatter=scatter))
        t0 += n
    return handles, res[-1]


def comm_wait(handle, after, name):
    src, land, scatter = handle["src"], handle["land"], handle["scatter"]
    nt = len(src)

    def body(*refs):
        src_r, land_r = refs[:nt], refs[nt:2 * nt]
        send_sems, recv_sems = refs[2 * nt], refs[2 * nt + 1]
        x, y, c = _me()
        my = 4 * x + 2 * y + c
        for k, peer, slot in _peers():
            for t in range(nt):
                _split_copy(src_r[t], land_r[t], send_sems, recv_sems, t, k, peer, slot, my, scatter, False).wait_send()
                _split_copy(src_r[t], land_r[t], send_sems, recv_sems, t, k, peer, slot, my, scatter, True).wait_recv()

    after = list(after) if isinstance(after, (list, tuple)) else [after]
    res = pl.pallas_call(
        body, name=name, in_specs=[_HBM] * (2 * nt) + [_SEM, _SEM] + [_HBM] * len(after), out_specs=[_HBM] * (2 * nt),
        out_shape=[pltpu.HBM(a.shape, a.dtype) for a in list(src) + list(land)],
        input_output_aliases={i: i for i in range(2 * nt)},
        compiler_params=pltpu.CompilerParams(has_side_effects=_DATAFLOW),
    )(*src, *land, *handle["sems"], *[pltpu.with_memory_space_constraint(a, pltpu.HBM) for a in after])
    return res[:nt], res[nt:]


def sum_adamw_shard(own_src, land, me, w, m, v, l, prev, name, tr=256):
    L, R, C = w.shape
    by_rows = R % SUBLANES == 0 or C % LANES != 0
    tr, tc = (_pick(R, tr, SUBLANES), C) if by_rows else (R, _pick(C, 256))
    steps = R // tr if by_rows else C // tc
    c1 = 1.0 - ADAM_B1 ** ADAM_STEP
    c2 = 1.0 - ADAM_B2 ** ADAM_STEP
    n_prev = 0 if prev is None else 4

    def at(lead, i):
        return (lead, i, 0) if by_rows else (lead, 0, i)

    def body(me_ref, *refs):
        parts = refs[:N_DEV]
        w_r, m_r, v_r = refs[N_DEV:N_DEV + 3]
        g_o, d_o, m_o, v_o = refs[N_DEV + 3 + n_prev:]
        g = parts[0][0].astype(F32)
        for k in range(1, N_DEV):
            g = g + parts[k][0].astype(F32)
        mn = ADAM_B1 * m_r[0] + (1.0 - ADAM_B1) * g
        vn = ADAM_B2 * v_r[0] + (1.0 - ADAM_B2) * (g * g)
        g_o[0] = g
        d_o[0] = -ADAM_LR * ((mn / c1) / (jnp.sqrt(vn / c2) + ADAM_EPS) + ADAM_WD * w_r[0])
        m_o[0] = mn
        v_o[0] = vn

    part_specs = [pl.BlockSpec((1, tr, tc), lambda i, me, k=k: at(me[0] ^ k, i)) for k in range(N_DEV)]
    lay = pl.BlockSpec((1, tr, tc), lambda i, me: at(l, i))
    grid_spec = pltpu.PrefetchScalarGridSpec(
        num_scalar_prefetch=1, grid=(steps,), in_specs=part_specs + [lay] * 3 + [_ANY] * n_prev, out_specs=[lay] * 4)
    return pl.pallas_call(
        body, name=name, grid_spec=grid_spec, out_shape=[sds((L, R, C))] * 4,
        input_output_aliases={1 + N_DEV + 3 + j: j for j in range(n_prev)}, compiler_params=_params(("parallel",)),
    )(me, own_src, *[land] * (N_DEV - 1), w, m, v, *([] if prev is None else prev))


def sum_adamw(parts, w, m, v, name, tr=256):
    _, R, C = parts.shape
    tr = _pick(R, tr, SUBLANES)
    c1 = 1.0 - ADAM_B1 ** ADAM_STEP
    c2 = 1.0 - ADAM_B2 ** ADAM_STEP

    def body(p, w, m, v, g_o, d_o, m_o, v_o):
        g = p[0].astype(F32)
        for d in range(1, N_DEV):
            g = g + p[d].astype(F32)
        mn = ADAM_B1 * m[...] + (1.0 - ADAM_B1) * g
        vn = ADAM_B2 * v[...] + (1.0 - ADAM_B2) * (g * g)
        m_hat = mn / c1
        v_hat = vn / c2
        g_o[...] = g
        d_o[...] = -ADAM_LR * (m_hat / (jnp.sqrt(v_hat) + ADAM_EPS) + ADAM_WD * w[...])
        m_o[...] = mn
        v_o[...] = vn

    blk = pl.BlockSpec((tr, C), lambda i: (i, 0))
    return pl.pallas_call(
        body, name=name, grid=(R // tr,), in_specs=[pl.BlockSpec((N_DEV, tr, C), lambda i: (0, i, 0)), blk, blk, blk],
        out_specs=[blk] * 4, out_shape=[sds((R, C))] * 4, compiler_params=_params(("parallel",)),
    )(parts, w, m, v)


SHARDED = ("w_in", "dn_conv_w", "w_branch_a", "w_branch_b", "w_out", "ffn_w_gate", "ffn_w_up", "ffn_conv_w", "ffn_w_down")
TRANSPOSED = ("w_in", "ffn_w_gate", "ffn_w_up")
COL_SHARDED = ("dn_conv_w", "w_branch_a", "w_branch_b", "ffn_conv_w")
CONV_WEIGHTS = ("dn_conv_w", "ffn_conv_w")
REPLICATED = ("norm1_g", "dn_a_log", "dn_dt_bias", "dn_onorm_g", "sg_ln_g", "sg_ln_b", "sg_w", "sg_b", "norm2_g",
              "ffn_conv_b", "final_norm_g")
WEIGHTS = ("norm1_g", "w_in", "dn_conv_w", "dn_a_log", "dn_dt_bias", "dn_onorm_g", "sg_ln_g", "sg_ln_b", "sg_w", "sg_b",
           "w_branch_a", "w_branch_b", "w_out", "norm2_g", "ffn_w_gate", "ffn_w_up", "ffn_conv_w", "ffn_conv_b",
           "ffn_w_down", "final_norm_g")


def _columns(pieces, lo, hi):
    out = []
    for a, start, width in pieces:
        s, e = max(lo, start), min(hi, start + width)
        if s < e:
            out.append(a[:, s - start:e - start])
    return out[0] if len(out) == 1 else jnp.concatenate(out, axis=1)


def _assemble(name, g):
    if name in COL_SHARDED:
        return jnp.concatenate([g[d] for d in range(N_DEV)], axis=1)
    return g.reshape(N_DEV * g.shape[1], g.shape[2])


def _split(name, pieces, dtype):
    total = sum(w for _, _, w in pieces)
    if name in COL_SHARDED:
        cs = total // N_DEV
        return jnp.stack([_columns(pieces, d * cs, (d + 1) * cs).astype(dtype) for d in range(N_DEV)])
    a = pieces[0][0] if len(pieces) == 1 else jnp.concatenate([p[:w] for p, _, w in pieces], axis=0)
    return a.reshape(N_DEV, a.shape[0] // N_DEV, a.shape[1]).astype(dtype)


def _pad_lanes(a, lo, width=LANES):
    return jnp.pad(a, ((0, 0), (lo, width - lo - a.shape[1])))


def _pad_rows(a, rows=SUBLANES):
    return jnp.pad(a, ((0, rows - a.shape[0]), (0, 0)))


def kernel(x, norm1_g, w_in, dn_conv_w, dn_a_log, dn_dt_bias, dn_onorm_g, sg_ln_g, sg_ln_b, sg_w, sg_b, w_branch_a, w_branch_b, w_out, norm2_g, ffn_w_gate, ffn_w_up, ffn_conv_w, ffn_conv_b, ffn_w_down, final_norm_g, loss_target, m_norm1_g, m_w_in, m_dn_conv_w, m_dn_a_log, m_dn_dt_bias, m_dn_onorm_g, m_sg_ln_g, m_sg_ln_b, m_sg_w, m_sg_b, m_w_branch_a, m_w_branch_b, m_w_out, m_norm2_g, m_ffn_w_gate, m_ffn_w_up, m_ffn_conv_w, m_ffn_conv_b, m_ffn_w_down, m_final_norm_g, v_norm1_g, v_w_in, v_dn_conv_w, v_dn_a_log, v_dn_dt_bias, v_dn_onorm_g, v_sg_ln_g, v_sg_ln_b, v_sg_w, v_sg_b, v_w_branch_a, v_w_branch_b, v_w_out, v_norm2_g, v_ffn_w_gate, v_ffn_w_up, v_ffn_conv_w, v_ffn_conv_b, v_ffn_w_down, v_final_norm_g):
    W = dict(norm1_g=norm1_g, w_in=w_in, dn_conv_w=dn_conv_w, dn_a_log=dn_a_log, dn_dt_bias=dn_dt_bias, dn_onorm_g=dn_onorm_g,
             sg_ln_g=sg_ln_g, sg_ln_b=sg_ln_b, sg_w=sg_w, sg_b=sg_b, w_branch_a=w_branch_a, w_branch_b=w_branch_b, w_out=w_out,
             norm2_g=norm2_g, ffn_w_gate=ffn_w_gate, ffn_w_up=ffn_w_up, ffn_conv_w=ffn_conv_w, ffn_conv_b=ffn_conv_b,
             ffn_w_down=ffn_w_down, final_norm_g=final_norm_g)
    Mo = dict(norm1_g=m_norm1_g, w_in=m_w_in, dn_conv_w=m_dn_conv_w, dn_a_log=m_dn_a_log, dn_dt_bias=m_dn_dt_bias,
              dn_onorm_g=m_dn_onorm_g, sg_ln_g=m_sg_ln_g, sg_ln_b=m_sg_ln_b, sg_w=m_sg_w, sg_b=m_sg_b, w_branch_a=m_w_branch_a,
              w_branch_b=m_w_branch_b, w_out=m_w_out, norm2_g=m_norm2_g, ffn_w_gate=m_ffn_w_gate, ffn_w_up=m_ffn_w_up,
              ffn_conv_w=m_ffn_conv_w, ffn_conv_b=m_ffn_conv_b, ffn_w_down=m_ffn_w_down, final_norm_g=m_final_norm_g)
    Vo = dict(norm1_g=v_norm1_g, w_in=v_w_in, dn_conv_w=v_dn_conv_w, dn_a_log=v_dn_a_log, dn_dt_bias=v_dn_dt_bias,
              dn_onorm_g=v_dn_onorm_g, sg_ln_g=v_sg_ln_g, sg_ln_b=v_sg_ln_b, sg_w=v_sg_w, sg_b=v_sg_b, w_branch_a=v_w_branch_a,
              w_branch_b=v_w_branch_b, w_out=v_w_out, norm2_g=v_norm2_g, ffn_w_gate=v_ffn_w_gate, ffn_w_up=v_ffn_w_up,
              ffn_conv_w=v_ffn_conv_w, ffn_conv_b=v_ffn_conv_b, ffn_w_down=v_ffn_w_down, final_norm_g=v_final_norm_g)

    xs = x[0]
    tgt = loss_target[0]
    T, D = xs.shape
    depth = norm1_g.shape[0]
    H = dn_a_log.shape[1]
    G = sg_w.shape[1]
    WA = H * HEAD_DIM
    WB = G * HEAD_DIM
    N = T // DN_CHUNK
    colA = 4 * WA
    colB0 = colA + 2 * H
    cb_a = (2 * WB) // D

    my = 4 * lax.axis_index("x") + 2 * lax.axis_index("y") + lax.axis_index("c")
    me_arr = my.astype(jnp.int32).reshape(1)

    def view(d):
        return {n: (jnp.transpose(d[n], (0, 2, 1)) if n in TRANSPOSED else d[n]) for n in SHARDED}

    Wv, Mv, Vv = view(W), view(Mo), view(Vo)

    def shard(n, l):
        return Wv[n][l] if n in CONV_WEIGHTS else Wv[n][l].astype(BF16)

    first = [("w_in", 0), ("dn_conv_w", 0)]
    first_blocks = all_gather([shard(n, l) for n, l in first], "gather_first")
    gathered = dict(zip(first, first_blocks))
    gather_names = []
    for l in range(depth):
        if l > 0:
            gather_names.append([("w_in", l), ("dn_conv_w", l)])
        gather_names.append([("w_branch_a", l), ("w_branch_b", l), ("w_out", l)])
        gather_names.append([("ffn_w_gate", l), ("ffn_w_up", l), ("ffn_conv_w", l)])
        gather_names.append([("ffn_w_down", l)])
    gather_handles, gather_tok = comm_start([[shard(n, l) for n, l in g] for g in gather_names], False, "gather_start",
                                            after=first_blocks[0])

    def need(n, l, after):
        if (n, l) not in gathered:
            gi = [i for i, g in enumerate(gather_names) if (n, l) in g][0]
            src, land = comm_wait(gather_handles[gi], after, f"gather_wait{gi}")
            for key, s, ld in zip(gather_names[gi], src, land):
                gathered[key] = ld
        return gathered[(n, l)]

    def full(n, l, after):
        return _assemble(n, need(n, l, after))

    def layer_weights(l):
        return dict(
            g1=norm1_g[l][None], g2=norm2_g[l][None], alog=_pad_lanes(dn_a_log[l][None], H), dtb=_pad_lanes(dn_dt_bias[l][None], H),
            og=dn_onorm_g[l][None], lng=sg_ln_g[l][None], lnb=sg_ln_b[l][None], sgw=sg_w[l], sgbT=sg_b[l].T, fcb=ffn_conv_b[l][None])

    def mixer_in_weights(p, l, after):
        wt = full("w_in", l, after)
        p.update(wA=wt[:colA], wba=_pad_rows(wt[colA:colB0], LANES), wB=wt[colB0:], cw8=_pad_rows(full("dn_conv_w", l, after)))

    def mixer_out_weights(p, l, after):
        p.update(wa=full("w_branch_a", l, after), wb=full("w_branch_b", l, after), wo=full("w_out", l, after))

    def ffn_weights(p, l, after):
        p.update(wg=full("ffn_w_gate", l, after), wu=full("ffn_w_up", l, after), fcw8=_pad_rows(full("ffn_conv_w", l, after)))

    saved = []
    cur = xs
    for l in range(depth):
        p = layer_weights(l)
        t = f"l{l}_"
        h = norm_fwd(cur, p["g1"] + gather_tok[0, 0] if l == 0 else p["g1"], t + "norm1")
        mixer_in_weights(p, l, h)
        projA = matmul(h, p["wA"], "nt", t + "projA")
        pba = matmul(h, p["wba"], "nt", t + "proj_ba")
        projB = matmul(h, p["wB"], "nt", t + "projB")
        q, k, v, bg = dn_prep_fwd(projA, pba, p["cw8"], p["alog"], p["dtb"], H, t + "dn_prep")
        u, w, a, qd, kd, gl, tinv = dn_chunk_fwd(q, k, v, bg, t + "dn_chunk")
        o, s_in = dn_scan_fwd(u, w, a, qd, kd, gl, t + "dn_scan")
        y_a = dn_post_fwd(o, projA, p["og"], t + "dn_post")
        y_b = gmlp_fwd(projB, p["lng"], p["lnb"], p["sgw"], p["sgbT"], t + "gmlp")
        mixer_out_weights(p, l, y_b)
        ap = matmul(y_a, p["wa"], "nn", t + "branch_a")
        bp = matmul(y_b, p["wb"], "nn", t + "branch_b")
        merged = merge_fwd(projB, ap, bp, cb_a, t + "merge")
        x1 = matmul(merged, p["wo"], "nn", t + "out_proj", c=cur)
        h2 = norm_fwd(x1, p["g2"], t + "norm2")
        ffn_weights(p, l, h2)
        gp = matmul(h2, p["wg"], "nt", t + "ffn_gate")
        up = matmul(h2, p["wu"], "nt", t + "ffn_up")
        act = ffn_act_fwd(gp, up, p["fcw8"], p["fcb"], t + "ffn_act")
        p.update(wd=full("ffn_w_down", l, act))
        x2 = matmul(act, p["wd"], "nn", t + "ffn_down", c=x1)
        saved.append(dict(p=p, x0=cur, h=h, projA=projA, pba=pba, projB=projB, q=q, k=k, v=v, bg=bg, tinv=tinv,
                          scan=(u, w, a, qd, kd, gl), s_in=s_in, o=o, y_a=y_a, y_b=y_b, ap=ap, bp=bp, merged=merged, x1=x1,
                          h2=h2, gp=gp, up=up, act=act))
        cur = x2

    loss_part, dx, dx_bf, d_final = head_fwd_bwd(cur, final_norm_g[None], tgt, "loss_head")
    loss = lax.psum(loss_part[0, 0], ("x", "y", "c"))

    grads_sh = {n: [None] * depth for n in SHARDED}
    grads_rep = {n: [None] * depth for n in REPLICATED if n != "final_norm_g"}
    exchanges = []

    def exchange(names, l, name, after=None):
        srcs = [_split(n, grads_sh[n][l], F32 if n in CONV_WEIGHTS else BF16) for n in names]
        (handle,), tok = comm_start([srcs], True, name, after=after)
        exchanges.append((names, l, handle))
        return tok

    def whole(a):
        return [(a, 0, a.shape[1])]

    sizes = [math.prod(W[n].shape) for n in REPLICATED]
    tile = SUBLANES * LANES
    nrows = [-(-sz // tile) * SUBLANES for sz in sizes]

    def pack(d):
        parts = [jnp.pad(d[n].reshape(-1).astype(F32), (0, r * LANES - sz)).reshape(r, LANES)
                 for n, sz, r in zip(REPLICATED, sizes, nrows)]
        return jnp.concatenate(parts, axis=0)

    mixer_tok = None
    for l in reversed(range(depth)):
        s = saved[l]
        p = s["p"]
        t = f"l{l}_b_"
        dact = matmul(dx_bf, p["wd"], "nt", t + "d_act")
        grads_sh["ffn_w_down"][l] = whole(matmul(s["act"], dx_bf, "tn", t + "dw_down", out_dtype=BF16, tn=2048))
        fcb = p["fcb"] if mixer_tok is None else p["fcb"] + mixer_tok[0, 0]
        dgp, dup, dfcw, dfcb = ffn_act_bwd(s["gp"], s["up"], p["fcw8"], fcb, dact, t + "ffn_act")
        dh2 = matmul([dgp, dup], [p["wg"], p["wu"]], "nn", t + "dh2")
        grads_sh["ffn_w_gate"][l] = whole(matmul(dgp, s["h2"], "tn", t + "dw_gate", out_dtype=BF16, tn=2048))
        grads_sh["ffn_w_up"][l] = whole(matmul(dup, s["h2"], "tn", t + "dw_up", out_dtype=BF16, tn=2048))
        grads_sh["ffn_conv_w"][l] = whole(dfcw[:3])
        grads_rep["ffn_conv_b"][l] = dfcb[0]
        tok = exchange(("ffn_w_down", "ffn_w_gate", "ffn_w_up", "ffn_conv_w"), l, t + "ffn_grads_start")
        dx1, dx1_bf, dg2 = norm_bwd(s["x1"], p["g2"] + tok[0, 0], dh2, dx, t + "norm2")
        grads_rep["norm2_g"][l] = dg2[0]
        dmerged = matmul(dx1_bf, p["wo"], "nt", t + "d_merged")
        grads_sh["w_out"][l] = whole(matmul(s["merged"], dx1_bf, "tn", t + "dw_out", out_dtype=BF16, tn=2048))
        dga, dgb, dap, dbp = merge_bwd(s["projB"], s["ap"], s["bp"], dmerged, cb_a, t + "merge")
        dya = matmul(dap, p["wa"], "nt", t + "d_ya")
        dyb = matmul(dbp, p["wb"], "nt", t + "d_yb")
        grads_sh["w_branch_a"][l] = whole(matmul(s["y_a"], dap, "tn", t + "dw_a", out_dtype=BF16))
        grads_sh["w_branch_b"][l] = whole(matmul(s["y_b"], dbp, "tn", t + "dw_b", out_dtype=BF16))
        du_raw, dv_raw, dlng, dlnb, dsgw, dsgbT = gmlp_bwd(s["projB"], p["lng"], p["lnb"], p["sgw"], p["sgbT"], dyb, t + "gmlp")
        grads_rep["sg_ln_g"][l], grads_rep["sg_ln_b"][l] = dlng[0], dlnb[0]
        grads_rep["sg_w"][l], grads_rep["sg_b"][l] = dsgw, dsgbT.T
        do, dz, dog = dn_post_bwd(s["o"], s["projA"], p["og"], dya, t + "dn_post")
        grads_rep["dn_onorm_g"][l] = dog[0]
        du, dw, da, dqd, dkd, dgl = dn_scan_bwd(*s["scan"], s["s_in"], do, t + "dn_scan")
        dq, dk, dv, dbg = dn_chunk_bwd(s["q"], s["k"], s["v"], s["bg"], s["tinv"], du, dw, da, dqd, dkd, dgl, t + "dn_chunk")
        dqkv, dba, dcw, dalog, ddtb = dn_prep_bwd(s["projA"], s["pba"], p["cw8"], p["alog"], p["dtb"], dq, dk, dv, dbg, H,
                                                  t + "dn_prep")
        grads_sh["dn_conv_w"][l] = whole(dcw[:4])
        grads_rep["dn_a_log"][l], grads_rep["dn_dt_bias"][l] = dalog[0, H:2 * H], ddtb[0, H:2 * H]
        tok = exchange(("w_out", "w_branch_a", "w_branch_b", "dn_conv_w"), l, t + "mixer_grads_start")
        dba = dba + tok[0, 0].astype(BF16)
        dprojA = jnp.concatenate([dqkv, dz], axis=1)
        dprojB = jnp.concatenate([du_raw, dv_raw, dga, dgb], axis=1)
        dwA = matmul(dprojA, s["h"], "tn", t + "dw_A", out_dtype=BF16, tn=2048)
        dwba = matmul(dba, s["h"], "tn", t + "dw_ba", out_dtype=BF16, tn=2048)
        dwB = matmul(dprojB, s["h"], "tn", t + "dw_B", out_dtype=BF16, tn=2048)
        grads_sh["w_in"][l] = [(dwA, 0, colA), (dwba, colA, 2 * H), (dwB, colB0, dwB.shape[0])]
        mixer_tok = exchange(("w_in",), l, t + "w_in_grads_start")
        dh = matmul([dba, dprojA, dprojB], [p["wba"] + mixer_tok[0, 0].astype(BF16), p["wA"], p["wB"]], "nn", t + "dh")
        dx, dx_bf, dg1 = norm_bwd(s["x0"], p["g1"], dh, dx1, t + "norm1")
        grads_rep["norm1_g"][l] = dg1[0]
        if l == 0:
            rep_full = {n: (jnp.stack(grads_rep[n]) if n != "final_norm_g" else d_final[0]) for n in REPLICATED}
            (small_handle,), small_tok = comm_start([[pack(rep_full)]], False, "small_grads_start")

    out = {}
    after = [dx, small_tok]

    def update_group(gi, after):
        names, l, handle = exchanges[gi]
        src, land = comm_wait(handle, after, f"grads_wait{gi}")
        done = []
        for n, s_, ld in zip(names, src, land):
            res = sum_adamw_shard(s_, ld, me_arr, Wv[n], Mv[n], Vv[n], l, out.get(n), f"adamw_{n}_{l}")
            out[n] = list(res)
            done.append(res[0])
        return done

    for gi in range(len(exchanges) - 1):
        after = update_group(gi, after)

    (small_src,), (small_land,) = comm_wait(small_handle, after, "small_grads_wait")
    res = sum_adamw(small_land, pack(W), pack(Mo), pack(Vo), "adamw_small")
    update_group(len(exchanges) - 1, after + [res[0]])
    for n in TRANSPOSED:
        out[n] = [jnp.transpose(r, (0, 2, 1)) for r in out[n]]
    row0 = 0
    for n, sz, nr in zip(REPLICATED, sizes, nrows):
        out[n] = [r[row0:row0 + nr].reshape(-1)[:sz].reshape(W[n].shape) for r in res]
        row0 += nr

    return (loss, dx[None], *[out[n][0] for n in WEIGHTS], *[out[n][1] for n in WEIGHTS],
            *[out[n][2] for n in WEIGHTS], *[out[n][3] for n in WEIGHTS])
```

```python
import functools
import math

import jax
import jax.numpy as jnp
from jax import lax
from jax.experimental import pallas as pl
from jax.experimental.pallas import tpu as pltpu

F32 = jnp.float32
BF16 = jnp.bfloat16
EPS = 1e-6
N_DEV = 8
LANES = 128
SUBLANES = 8
HEAD_DIM = 128
DN_CHUNK = 64
SG_CHUNK = 128
VMEM_LIMIT = 56 * 1024 * 1024
MESH = pl.DeviceIdType.MESH
HIGHEST = lax.Precision.HIGHEST

ADAM_LR = 0.001
ADAM_B1 = 0.9
ADAM_B2 = 0.999
ADAM_EPS = 1e-08
ADAM_WD = 0.01
ADAM_STEP = 10


def _pick(n, target, mult=LANES):
    best = None
    d = mult
    while d <= min(n, target):
        if n % d == 0:
            best = d
        d += mult
    return n if best is None else best


def _params(sem):
    return pltpu.CompilerParams(dimension_semantics=sem, vmem_limit_bytes=VMEM_LIMIT)


_NN = (((1,), (0,)), ((), ()))
_NT = (((1,), (1,)), ((), ()))
_TN = (((0,), (0,)), ((), ()))


def _dg(a, b, dims, hi):
    if hi == 2:
        return lax.dot_general(a.astype(F32), b.astype(F32), dims, precision=HIGHEST, preferred_element_type=F32)
    if hi == 1:
        a_hi, b_hi = a.astype(BF16), b.astype(BF16)
        a_lo, b_lo = (a - a_hi.astype(F32)).astype(BF16), (b - b_hi.astype(F32)).astype(BF16)
        ax, bx = dims[0][0][0], dims[0][1][0]
        a = jnp.concatenate([a_hi, a_hi, a_lo], axis=ax)
        b = jnp.concatenate([b_hi, b_lo, b_hi], axis=bx)
        return lax.dot_general(a, b, dims, preferred_element_type=F32)
    return lax.dot_general(a.astype(BF16), b.astype(BF16), dims, preferred_element_type=F32)


@functools.partial(jax.custom_vjp, nondiff_argnums=(2,))
def mm_nn(a, b, hi=False):
    return _dg(a, b, _NN, hi)


def _mm_nn_f(a, b, hi):
    return _dg(a, b, _NN, hi), (a, b)


def _mm_nn_b(hi, res, g):
    a, b = res
    return mm_nt(g, b, hi), mm_tn(a, g, hi)


@functools.partial(jax.custom_vjp, nondiff_argnums=(2,))
def mm_nt(a, b, hi=False):
    return _dg(a, b, _NT, hi)


def _mm_nt_f(a, b, hi):
    return _dg(a, b, _NT, hi), (a, b)


def _mm_nt_b(hi, res, g):
    a, b = res
    return mm_nn(g, b, hi), mm_tn(g, a, hi)


@functools.partial(jax.custom_vjp, nondiff_argnums=(2,))
def mm_tn(a, b, hi=False):
    return _dg(a, b, _TN, hi)


def _mm_tn_f(a, b, hi):
    return _dg(a, b, _TN, hi), (a, b)


def _mm_tn_b(hi, res, g):
    a, b = res
    return mm_nt(b, g, hi), mm_nn(a, g, hi)


mm_nn.defvjp(_mm_nn_f, _mm_nn_b)
mm_nt.defvjp(_mm_nt_f, _mm_nt_b)
mm_tn.defvjp(_mm_tn_f, _mm_tn_b)


def matmul(a, b, mode, name, c=None, out_dtype=F32, tm=1024, tn=1024, tk=2048):
    a_list = list(a) if isinstance(a, (list, tuple)) else [a]
    b_list = list(b) if isinstance(b, (list, tuple)) else [b]
    nterm = len(a_list)

    def dims_of(a, b):
        if mode == "nn":
            return a.shape[0], a.shape[1], b.shape[1]
        if mode == "nt":
            return a.shape[0], a.shape[1], b.shape[0]
        return a.shape[1], a.shape[0], b.shape[1]

    M, _, N = dims_of(a_list[0], b_list[0])
    tm, tn = _pick(M, tm), _pick(N, tn)
    tks = [_pick(dims_of(x, y)[1], tk) for x, y in zip(a_list, b_list)]
    nks = [dims_of(x, y)[1] // t for x, y, t in zip(a_list, b_list, tks)]
    offs = [sum(nks[:t]) for t in range(nterm)]
    nk = sum(nks)
    dims = {"nn": _NN, "nt": _NT, "tn": _TN}[mode]

    def specs_of(t):
        kk = lambda k: jnp.clip(k - offs[t], 0, nks[t] - 1)
        a_spec = (pl.BlockSpec((tks[t], tm), lambda i, j, k: (kk(k), i)) if mode == "tn"
                  else pl.BlockSpec((tm, tks[t]), lambda i, j, k: (i, kk(k))))
        b_spec = (pl.BlockSpec((tn, tks[t]), lambda i, j, k: (j, kk(k))) if mode == "nt"
                  else pl.BlockSpec((tks[t], tn), lambda i, j, k: (kk(k), j)))
        return [a_spec, b_spec]

    o_spec = pl.BlockSpec((tm, tn), lambda i, j, k: (i, j))
    has_c = c is not None
    own_acc = nk > 1 and out_dtype != F32

    def body(*refs):
        ab = refs[:2 * nterm]
        c_ref = refs[2 * nterm] if has_c else None
        o_ref = refs[2 * nterm + (1 if has_c else 0)]
        acc_ref = refs[-1] if own_acc else o_ref

        def dot(t):
            return lax.dot_general(ab[2 * t][...].astype(BF16), ab[2 * t + 1][...].astype(BF16), dims,
                                   preferred_element_type=F32)

        if nk == 1:
            o_ref[...] = (dot(0) + c_ref[...] if has_c else dot(0)).astype(o_ref.dtype)
        else:
            k = pl.program_id(2)

            @pl.when(k == 0)
            def _():
                acc_ref[...] = c_ref[...] if has_c else jnp.zeros_like(acc_ref)

            for t in range(nterm):
                if nterm == 1:
                    acc_ref[...] += dot(t)
                else:
                    @pl.when((k >= offs[t]) & (k < offs[t] + nks[t]))
                    def _(t=t):
                        acc_ref[...] += dot(t)

            if own_acc:
                @pl.when(k == nk - 1)
                def _():
                    o_ref[...] = acc_ref[...].astype(o_ref.dtype)

    ins, specs = [], []
    for t in range(nterm):
        ins += [a_list[t], b_list[t]]
        specs += specs_of(t)
    if has_c:
        ins.append(c)
        specs.append(o_spec)
    return pl.pallas_call(
        body, name=name, grid=(M // tm, N // tn, nk), in_specs=specs, out_specs=o_spec,
        out_shape=jax.ShapeDtypeStruct((M, N), out_dtype), scratch_shapes=[pltpu.VMEM((tm, tn), F32)] if own_acc else [],
        compiler_params=_params(("parallel", "parallel", "arbitrary")),
    )(*ins)


def rowcall(name, fn, ins, in_specs, outs, out_specs, acc, nrow, ncol=1, scratch=()):
    n_in, n_out = len(ins), len(outs)

    def body(*refs):
        i = pl.program_id(1)
        res = fn(i, *[r[...] for r in refs[:n_in]], *refs[n_in + n_out:])
        for r, v, is_acc in zip(refs[n_in:n_in + n_out], res, acc):
            if is_acc:
                @pl.when(i == 0)
                def _(r=r, v=v):
                    r[...] = v.astype(r.dtype)

                @pl.when(i > 0)
                def _(r=r, v=v):
                    r[...] += v.astype(r.dtype)
            else:
                r[...] = v.astype(r.dtype)

    return pl.pallas_call(
        body, name=name, grid=(ncol, nrow), in_specs=list(in_specs), out_specs=list(out_specs), out_shape=list(outs),
        scratch_shapes=list(scratch), compiler_params=_params(("parallel", "arbitrary")),
    )(*ins)


class Tiles:
    def __init__(self, T, tm):
        self.T, self.tm, self.n = T, tm, T // tm
        self.r8 = tm // SUBLANES

    def row(self, w, cb=0):
        return pl.BlockSpec((self.tm, w), lambda j, i: (i, cb))

    def rowj(self, tc):
        return pl.BlockSpec((self.tm, tc), lambda j, i: (i, j))

    def prev(self, w, cb=0):
        return pl.BlockSpec((SUBLANES, w), lambda j, i: (jnp.maximum(i * self.r8 - 1, 0), cb))

    def prevj(self, tc):
        return pl.BlockSpec((SUBLANES, tc), lambda j, i: (jnp.maximum(i * self.r8 - 1, 0), j))

    def nxt(self, w, cb=0):
        last = self.T // SUBLANES - 1
        return pl.BlockSpec((SUBLANES, w), lambda j, i: (jnp.minimum((i + 1) * self.r8, last), cb))

    def nxtj(self, tc):
        last = self.T // SUBLANES - 1
        return pl.BlockSpec((SUBLANES, tc), lambda j, i: (jnp.minimum((i + 1) * self.r8, last), j))

    def heads(self, H):
        return pl.BlockSpec((H, self.tm, HEAD_DIM), lambda j, i: (0, i, 0))

    def heads_nxt(self, H):
        last = self.T // SUBLANES - 1
        return pl.BlockSpec((H, SUBLANES, HEAD_DIM), lambda j, i: (0, jnp.minimum((i + 1) * self.r8, last), 0))


def full(shape):
    return pl.BlockSpec(tuple(shape), lambda j, i: (0,) * len(shape))


def constj(r, tc):
    return pl.BlockSpec((r, tc), lambda j, i: (0, j))


def sds(shape, dtype=F32):
    return jax.ShapeDtypeStruct(tuple(shape), dtype)


def rms(x, g):
    return x * lax.rsqrt(jnp.mean(x * x, axis=-1, keepdims=True) + EPS) * g


def sigmoid(x):
    return jax.nn.sigmoid(x)


def silu(x):
    return x * sigmoid(x)


def gelu(x):
    return 0.5 * x * (1.0 + lax.erf(x * (2.0 ** -0.5)))


def fill_window(win, i, last, prev, x, nxt=None):
    R = x.shape[0]
    win[0:SUBLANES, :] = jnp.where(i > 0, prev, 0.0)
    win[SUBLANES:SUBLANES + R, :] = x
    if nxt is not None:
        win[SUBLANES + R:2 * SUBLANES + R, :] = jnp.where(last, 0.0, nxt)


def conv_taps(win, K, R):
    base = SUBLANES - (K - 1)
    return [win[pl.ds(base + j, R), :] for j in range(K)]


def value_taps(xwin, K, R):
    base = SUBLANES - (K - 1)
    return [xwin[base + j:base + j + R, :] for j in range(K)]


def causal_conv(taps, w):
    out = w[0:1, :] * taps[0]
    for j in range(1, len(taps)):
        out = out + w[j:j + 1, :] * taps[j]
    return out


def rows_to8(rows, C):
    rid = lax.broadcasted_iota(jnp.int32, (SUBLANES, C), 0)
    out = jnp.zeros((SUBLANES, C), F32)
    for k, r in enumerate(rows):
        out = out + jnp.where(rid == k, jnp.broadcast_to(r, (SUBLANES, C)), 0.0)
    return out


def dn_qkv(pre, H):
    a = silu(pre)
    W = H * HEAD_DIM

    def l2(t):
        return t * lax.rsqrt(jnp.sum(t * t, axis=-1, keepdims=True) + EPS)

    q = [l2(a[:, h * HEAD_DIM:(h + 1) * HEAD_DIM]) for h in range(H)]
    k = [l2(a[:, W + h * HEAD_DIM:W + (h + 1) * HEAD_DIM]) for h in range(H)]
    v = [a[:, 2 * W + h * HEAD_DIM:2 * W + (h + 1) * HEAD_DIM] for h in range(H)]
    return q, k, v


def dn_gates(ba, alog, dtb, H, R):
    lane = lax.broadcasted_iota(jnp.int32, (R, LANES), 1)
    beta = sigmoid(ba)
    g = -jnp.exp(alog) * jax.nn.softplus(ba + dtb)
    g = jnp.where((lane >= H) & (lane < 2 * H), g, 0.0)
    ri = lax.broadcasted_iota(jnp.int32, (R, R), 0)
    ci = lax.broadcasted_iota(jnp.int32, (R, R), 1)
    cum = jnp.where((ri // DN_CHUNK == ci // DN_CHUNK) & (ci <= ri), 1.0, 0.0).astype(F32)
    gc = mm_nn(cum, g, 2)
    return jnp.where(lane < H, beta, gc)


def neumann_inverse(Ls):
    C = Ls[0].shape[0]
    ri = lax.broadcasted_iota(jnp.int32, (C, C), 0)
    ci = lax.broadcasted_iota(jnp.int32, (C, C), 1)
    eye = jnp.where(ri == ci, 1.0, 0.0).astype(F32)
    P = [-L for L in Ls]
    R = [eye + p for p in P]
    for _ in range(int(math.log2(C)) - 1):
        P = [mm_nn(p, p, 1) for p in P]
        R = [r + mm_nn(r, p, 1) for r, p in zip(R, P)]
    return R


@jax.custom_vjp
def saved_inverse(L, T):
    return T


def _saved_inverse_f(L, T):
    return T, T


def _saved_inverse_b(T, g):
    return -mm_tn(T, mm_nt(g, T)), jnp.zeros_like(T)


saved_inverse.defvjp(_saved_inverse_f, _saved_inverse_b)


def gate_columns(bg, H):
    bgT = bg.T
    return ([bg[:, h:h + 1] for h in range(H)], [bg[:, H + h:H + h + 1] for h in range(H)],
            [bgT[H + h:H + h + 1, :] for h in range(H)])


def dn_chunk(q, k, v, beta, gc, gr, tinv=None, with_inverse=False):
    n = len(q)
    C = q[0].shape[0]
    ri = lax.broadcasted_iota(jnp.int32, (C, C), 0)
    ci = lax.broadcasted_iota(jnp.int32, (C, C), 1)
    qs = [q[h] * (HEAD_DIM ** -0.5) for h in range(n)]
    kb = [k[h] * beta[h] for h in range(n)]
    vb = [v[h] * beta[h] for h in range(n)]
    decay = [jnp.exp(jnp.where(ri >= ci, gc[h] - gr[h], -jnp.inf)) for h in range(n)]
    L = [jnp.where(ri > ci, mm_nt(kb[h], k[h]) * decay[h], 0.0) for h in range(n)]
    attn = [jnp.where(ri >= ci, mm_nt(qs[h], k[h]) * decay[h], 0.0) for h in range(n)]
    Tinv = neumann_inverse(L) if tinv is None else [saved_inverse(L[h], tinv[h]) for h in range(n)]
    eg = [jnp.exp(gc[h]) for h in range(n)]
    u = [mm_nn(Tinv[h], vb[h]) for h in range(n)]
    w = [mm_nn(Tinv[h], kb[h] * eg[h]) for h in range(n)]
    qd = [qs[h] * eg[h] for h in range(n)]
    gl = [gc[h][C - 1:C, :] for h in range(n)]
    kd = [k[h] * jnp.exp(gl[h] - gc[h]) for h in range(n)]
    return (u, w, attn, qd, kd, gl, Tinv) if with_inverse else (u, w, attn, qd, kd, gl)


def dn_step(u, w, a, qd, kd, gl, S):
    n = len(u)
    v_new = [u[h] - mm_nn(w[h], S[h]) for h in range(n)]
    o = [mm_nn(qd[h], S[h]) + mm_nn(a[h], v_new[h]) for h in range(n)]
    S_new = [S[h] * jnp.exp(gl[h]) + mm_tn(kd[h], v_new[h]) for h in range(n)]
    return o, S_new


def dn_post(o, z, g):
    H = o.shape[0]
    return jnp.concatenate([rms(o[h], g) * silu(z[:, h * HEAD_DIM:(h + 1) * HEAD_DIM]) for h in range(H)], axis=1)


def gmlp(u_raw, v_raw, ln_g, ln_b, sgw, sgbT):
    R = u_raw.shape[0]
    G = sgw.shape[0]
    nc = R // SG_CHUNK
    u = gelu(u_raw)
    vv = gelu(v_raw)
    xc = vv - jnp.mean(vv, axis=-1, keepdims=True)
    vg = xc * lax.rsqrt(jnp.mean(xc * xc, axis=-1, keepdims=True) + EPS) * ln_g + ln_b
    ri = lax.broadcasted_iota(jnp.int32, (SG_CHUNK, SG_CHUNK), 0)
    ci = lax.broadcasted_iota(jnp.int32, (SG_CHUNK, SG_CHUNK), 1)
    cols = []
    for g in range(G):
        ws = jnp.where(ri >= ci, sgw[g], 0.0)
        rhs = jnp.concatenate([vg[c * SG_CHUNK:(c + 1) * SG_CHUNK, g * HEAD_DIM:(g + 1) * HEAD_DIM] for c in range(nc)], axis=1)
        mixed = mm_nn(ws, rhs) + sgbT[:, g:g + 1]
        cols.append(jnp.concatenate([mixed[:, c * HEAD_DIM:(c + 1) * HEAD_DIM] for c in range(nc)], axis=0))
    return u * jnp.concatenate(cols, axis=1)


def merge(ga, gb, ap, bp):
    return sigmoid(ga) * ap + sigmoid(gb) * bp


def norm_fwd(x, g, name, tm=256):
    T, D = x.shape
    tl = Tiles(T, _pick(T, tm))
    (h,) = rowcall(name, lambda i, x, g: (rms(x, g),), [x, g], [tl.row(D), full((1, D))],
                   [sds((T, D), BF16)], [tl.row(D)], [False], tl.n)
    return h


def norm_bwd(x, g, dh, dres, name, tm=256):
    T, D = x.shape
    tl = Tiles(T, _pick(T, tm))

    def fn(i, x, g, dh, dres):
        _, vj = jax.vjp(rms, x, g)
        dx, dg = vj(dh.astype(F32))
        dx = dx + dres
        return dx, dx, dg

    return rowcall(name, fn, [x, g, dh, dres], [tl.row(D), full((1, D)), tl.row(D), tl.row(D)],
                   [sds((T, D)), sds((T, D), BF16), sds((1, D))], [tl.row(D), tl.row(D), full((1, D))],
                   [False, False, True], tl.n)


def head_fwd_bwd(x, g, tgt, name, tm=256):
    T, D = x.shape
    tl = Tiles(T, _pick(T, tm))

    def fn(i, x, g, tgt):
        y, vj = jax.vjp(rms, x, g)
        e = y - tgt
        loss = 0.5 * jnp.sum(jnp.mean(e * e, axis=-1, keepdims=True), axis=0, keepdims=True)
        dx, dg = vj(e * (1.0 / D))
        return loss, dx, dx, dg

    return rowcall(name, fn, [x, g, tgt], [tl.row(D), full((1, D)), tl.row(D)],
                   [sds((1, 1)), sds((T, D)), sds((T, D), BF16), sds((1, D))],
                   [full((1, 1)), tl.row(D), tl.row(D), full((1, D))], [True, False, False, True], tl.n)


def dn_prep_fwd(projA, pba, cw8, alog, dtb, H, name, tm=256):
    T = projA.shape[0]
    W3 = 3 * H * HEAD_DIM
    tl = Tiles(T, _pick(T, tm, DN_CHUNK))
    R = tl.tm

    def fn(i, xp, x, ba, cw, alog, dtb, win):
        fill_window(win, i, None, xp, x)
        q, k, v = dn_qkv(causal_conv(conv_taps(win, 4, R), cw), H)
        return jnp.stack(q), jnp.stack(k), jnp.stack(v), dn_gates(ba, alog, dtb, H, R)

    hs = sds((H, T, HEAD_DIM))
    return rowcall(name, fn, [projA, projA, pba, cw8, alog, dtb],
                   [tl.prev(W3), tl.row(W3), tl.row(LANES), full((SUBLANES, W3)), full((1, LANES)), full((1, LANES))],
                   [hs, hs, hs, sds((T, LANES))], [tl.heads(H)] * 3 + [tl.row(LANES)], [False] * 4, tl.n,
                   scratch=[pltpu.VMEM((SUBLANES + R, W3), F32)])


def dn_prep_bwd(projA, pba, cw8, alog, dtb, dq, dk, dv, dbg, H, name, tm=256):
    T = projA.shape[0]
    W3 = 3 * H * HEAD_DIM
    tl = Tiles(T, _pick(T, tm, DN_CHUNK))
    R = tl.tm
    RE = R + SUBLANES

    def fn(i, xp, x, xn, ba, cw, alog, dtb, dq, dk, dv, dqn, dkn, dvn, dbg, win, dp):
        last = i == tl.n - 1
        fill_window(win, i, last, xp, x, xn)
        taps = conv_taps(win, 4, RE)
        pre = causal_conv(taps, cw)
        ext = lambda d, dn: [jnp.concatenate([d[h], jnp.where(last, 0.0, dn[h])], axis=0) for h in range(H)]
        _, vj = jax.vjp(lambda p: dn_qkv(p, H), pre)
        (dpre,) = vj((ext(dq, dqn), ext(dk, dkn), ext(dv, dvn)))
        dp[...] = dpre
        dx = cw[3:4, :] * dpre[0:R, :]
        for j in range(3):
            dx = dx + cw[j:j + 1, :] * dp[pl.ds(3 - j, R), :]
        dcw = rows_to8([jnp.sum(dpre[0:R, :] * taps[j][0:R, :], axis=0, keepdims=True) for j in range(4)], W3)
        _, vjg = jax.vjp(lambda ba, alog, dtb: dn_gates(ba, alog, dtb, H, R), ba, alog, dtb)
        dba, dalog, ddtb = vjg(dbg)
        return dx, dba, dcw, dalog, ddtb

    return rowcall(name, fn, [projA, projA, projA, pba, cw8, alog, dtb, dq, dk, dv, dq, dk, dv, dbg],
                   [tl.prev(W3), tl.row(W3), tl.nxt(W3), tl.row(LANES), full((SUBLANES, W3)), full((1, LANES)), full((1, LANES))]
                   + [tl.heads(H)] * 3 + [tl.heads_nxt(H)] * 3 + [tl.row(LANES)],
                   [sds((T, W3), BF16), sds((T, LANES), BF16), sds((SUBLANES, W3)), sds((1, LANES)), sds((1, LANES))],
                   [tl.row(W3), tl.row(LANES), full((SUBLANES, W3)), full((1, LANES)), full((1, LANES))],
                   [False, False, True, True, True], tl.n,
                   scratch=[pltpu.VMEM((2 * SUBLANES + R, W3), F32), pltpu.VMEM((RE, W3), F32)])


def _scan_chunks_per_step(N, want=4):
    while N % want:
        want //= 2
    return want


def _multi_chunk_specs(H, C, P):
    hs = pl.BlockSpec((H, P * C, HEAD_DIM), lambda n: (0, n, 0))
    at = pl.BlockSpec((H, P * C, C), lambda n: (0, n, 0))
    one = pl.BlockSpec((H, P, 1, 1), lambda n: (0, n, 0, 0))
    gate = pl.BlockSpec((P * C, LANES), lambda n: (n, 0))
    return hs, at, one, gate


def dn_chunk_fwd(q, k, v, bg, name):
    H, T, _ = q.shape
    C = DN_CHUNK
    N = T // C
    P = _scan_chunks_per_step(N, 4)
    hs, at, one, gate = _multi_chunk_specs(H, C, P)

    def body(q, k, v, bg, u, w, a, qd, kd, gl, ti):
        pr = [(c, h) for c in range(P) for h in range(H)]
        rows = lambda c: slice(c * C, (c + 1) * C)
        cols = [gate_columns(bg[rows(c), :], H) for c in range(P)]
        res = dn_chunk([q[h, rows(c)] for c, h in pr], [k[h, rows(c)] for c, h in pr], [v[h, rows(c)] for c, h in pr],
                       [cols[c][0][h] for c, h in pr], [cols[c][1][h] for c, h in pr], [cols[c][2][h] for c, h in pr],
                       with_inverse=True)
        for i, (c, h) in enumerate(pr):
            for ref, val in zip((u, w, a, qd, kd), res[:5]):
                ref[h, rows(c)] = val[i]
            gl[h, c] = res[5][i]
            ti[h, rows(c)] = res[6][i]

    big = sds((H, T, HEAD_DIM))
    return pl.pallas_call(
        body, name=name, grid=(N // P,), in_specs=[hs, hs, hs, gate], out_specs=[hs, hs, at, hs, hs, one, at],
        out_shape=[big, big, sds((H, T, C)), big, big, sds((H, N, 1, 1)), sds((H, T, C))],
        compiler_params=_params(("parallel",)),
    )(q, k, v, bg)


def dn_chunk_bwd(q, k, v, bg, tinv, du, dw, da, dqd, dkd, dgl, name):
    H, T, _ = q.shape
    C = DN_CHUNK
    N = T // C
    P = _scan_chunks_per_step(N, 2)
    hs, at, one, gate = _multi_chunk_specs(H, C, P)

    def body(q, k, v, bg, ti, du, dw, da, dqd, dkd, dgl, dq, dk, dv, dbg):
        pr = [(c, h) for c in range(P) for h in range(H)]
        rws = lambda c: slice(c * C, (c + 1) * C)
        gcols = [gate_columns(bg[rws(c), :], H) for c in range(P)]
        f = lambda q, k, v, b, gc, gr: dn_chunk(q, k, v, b, gc, gr, tinv=[ti[h, rws(c)] for c, h in pr])
        _, vj = jax.vjp(f, [q[h, rws(c)] for c, h in pr], [k[h, rws(c)] for c, h in pr], [v[h, rws(c)] for c, h in pr],
                        [gcols[c][0][h] for c, h in pr], [gcols[c][1][h] for c, h in pr], [gcols[c][2][h] for c, h in pr])
        res = vj(([du[h, rws(c)] for c, h in pr], [dw[h, rws(c)] for c, h in pr], [da[h, rws(c)] for c, h in pr],
                  [dqd[h, rws(c)] for c, h in pr], [dkd[h, rws(c)] for c, h in pr], [dgl[h, c] for c, h in pr]))
        lane = lax.broadcasted_iota(jnp.int32, (C, LANES), 1)
        row = lax.broadcasted_iota(jnp.int32, (LANES, C), 0)
        for c in range(P):
            cols = jnp.zeros((C, LANES), F32)
            rows = jnp.zeros((LANES, C), F32)
            for h in range(H):
                i = c * H + h
                dq[h, rws(c)], dk[h, rws(c)], dv[h, rws(c)] = res[0][i], res[1][i], res[2][i]
                cols = cols + jnp.where(lane == h, res[3][i], 0.0) + jnp.where(lane == H + h, res[4][i], 0.0)
                rows = rows + jnp.where(row == H + h, res[5][i], 0.0)
            dbg[rws(c), :] = cols + rows.T

    big = sds((H, T, HEAD_DIM))
    return pl.pallas_call(
        body, name=name, grid=(N // P,), in_specs=[hs, hs, hs, gate, at, hs, hs, at, hs, hs, one],
        out_specs=[hs, hs, hs, gate], out_shape=[big, big, big, sds((T, LANES))], compiler_params=_params(("parallel",)),
    )(q, k, v, bg, tinv, du, dw, da, dqd, dkd, dgl)


def dn_scan_fwd(u, w, a, qd, kd, gl, name):
    H, T, _ = u.shape
    C = DN_CHUNK
    N = T // C
    P = _scan_chunks_per_step(N)
    hs = pl.BlockSpec((H, P * C, HEAD_DIM), lambda n: (0, n, 0))
    at = pl.BlockSpec((H, P * C, C), lambda n: (0, n, 0))
    one = pl.BlockSpec((H, P, 1, 1), lambda n: (0, n, 0, 0))
    st = pl.BlockSpec((P, H, HEAD_DIM, HEAD_DIM), lambda n: (n, 0, 0, 0))

    def body(u, w, a, qd, kd, gl, o, s_in, S):
        @pl.when(pl.program_id(0) == 0)
        def _():
            S[...] = jnp.zeros_like(S)

        hd = range(H)
        s = [S[h] for h in hd]
        for c in range(P):
            rows = slice(c * C, (c + 1) * C)
            o_new, s_new = dn_step([u[h, rows] for h in hd], [w[h, rows] for h in hd], [a[h, rows] for h in hd],
                                   [qd[h, rows] for h in hd], [kd[h, rows] for h in hd], [gl[h, c] for h in hd], s)
            for h in hd:
                s_in[c, h] = s[h]
                o[h, rows] = o_new[h]
            s = s_new
        for h in hd:
            S[h] = s[h]

    return pl.pallas_call(
        body, name=name, grid=(N // P,), in_specs=[hs, hs, at, hs, hs, one], out_specs=[hs, st],
        out_shape=[sds((H, T, HEAD_DIM)), sds((N, H, HEAD_DIM, HEAD_DIM))],
        scratch_shapes=[pltpu.VMEM((H, HEAD_DIM, HEAD_DIM), F32)], compiler_params=_params(("arbitrary",)),
    )(u, w, a, qd, kd, gl)


def dn_scan_bwd(u, w, a, qd, kd, gl, s_in, do, name):
    H, T, _ = u.shape
    C = DN_CHUNK
    N = T // C
    P = _scan_chunks_per_step(N)
    nb = N // P
    hs = pl.BlockSpec((H, P * C, HEAD_DIM), lambda n: (0, nb - 1 - n, 0))
    at = pl.BlockSpec((H, P * C, C), lambda n: (0, nb - 1 - n, 0))
    one = pl.BlockSpec((H, P, 1, 1), lambda n: (0, nb - 1 - n, 0, 0))
    st = pl.BlockSpec((P, H, HEAD_DIM, HEAD_DIM), lambda n: (nb - 1 - n, 0, 0, 0))

    def body(u, w, a, qd, kd, gl, s_in, do, du, dw, da, dqd, dkd, dgl, dS):
        @pl.when(pl.program_id(0) == 0)
        def _():
            dS[...] = jnp.zeros_like(dS)

        hd = range(H)
        ds = [dS[h] for h in hd]
        for c in reversed(range(P)):
            rows = slice(c * C, (c + 1) * C)
            _, vj = jax.vjp(dn_step, [u[h, rows] for h in hd], [w[h, rows] for h in hd], [a[h, rows] for h in hd],
                            [qd[h, rows] for h in hd], [kd[h, rows] for h in hd], [gl[h, c] for h in hd],
                            [s_in[c, h] for h in hd])
            res = vj(([do[h, rows] for h in hd], ds))
            for h in hd:
                du[h, rows], dw[h, rows], da[h, rows], dqd[h, rows], dkd[h, rows] = (res[j][h] for j in range(5))
                dgl[h, c] = res[5][h]
            ds = res[6]
        for h in hd:
            dS[h] = ds[h]

    big = sds((H, T, HEAD_DIM))
    return pl.pallas_call(
        body, name=name, grid=(nb,), in_specs=[hs, hs, at, hs, hs, one, st, hs], out_specs=[hs, hs, at, hs, hs, one],
        out_shape=[big, big, sds((H, T, C)), big, big, sds((H, N, 1, 1))],
        scratch_shapes=[pltpu.VMEM((H, HEAD_DIM, HEAD_DIM), F32)], compiler_params=_params(("arbitrary",)),
    )(u, w, a, qd, kd, gl, s_in, do)


def dn_post_fwd(o, projA, g, name, tm=256):
    H, T, _ = o.shape
    W = H * HEAD_DIM
    tl = Tiles(T, _pick(T, tm))
    (y,) = rowcall(name, lambda i, o, z, g: (dn_post(o, z, g),), [o, projA, g], [tl.heads(H), tl.row(W, 3), full((1, HEAD_DIM))],
                   [sds((T, W), BF16)], [tl.row(W)], [False], tl.n)
    return y


def dn_post_bwd(o, projA, g, dy, name, tm=256):
    H, T, _ = o.shape
    W = H * HEAD_DIM
    tl = Tiles(T, _pick(T, tm))

    def fn(i, o, z, g, dy):
        _, vj = jax.vjp(dn_post, o, z, g)
        return vj(dy.astype(F32))

    return rowcall(name, fn, [o, projA, g, dy], [tl.heads(H), tl.row(W, 3), full((1, HEAD_DIM)), tl.row(W)],
                   [sds((H, T, HEAD_DIM)), sds((T, W), BF16), sds((1, HEAD_DIM))],
                   [tl.heads(H), tl.row(W), full((1, HEAD_DIM))], [False, False, True], tl.n)


def gmlp_fwd(projB, ln_g, ln_b, sgw, sgbT, name, tm=512):
    T = projB.shape[0]
    G = sgw.shape[0]
    W = G * HEAD_DIM
    tl = Tiles(T, _pick(T, tm))
    (y,) = rowcall(name, lambda i, *a: (gmlp(*a),), [projB, projB, ln_g, ln_b, sgw, sgbT],
                   [tl.row(W, 0), tl.row(W, 1), full((1, W)), full((1, W)), full(sgw.shape), full(sgbT.shape)],
                   [sds((T, W), BF16)], [tl.row(W)], [False], tl.n)
    return y


def gmlp_bwd(projB, ln_g, ln_b, sgw, sgbT, dy, name, tm=512):
    T = projB.shape[0]
    G = sgw.shape[0]
    W = G * HEAD_DIM
    tl = Tiles(T, _pick(T, tm))

    def fn(i, u_raw, v_raw, ln_g, ln_b, sgw, sgbT, dy):
        _, vj = jax.vjp(gmlp, u_raw, v_raw, ln_g, ln_b, sgw, sgbT)
        return vj(dy.astype(F32))

    return rowcall(name, fn, [projB, projB, ln_g, ln_b, sgw, sgbT, dy],
                   [tl.row(W, 0), tl.row(W, 1), full((1, W)), full((1, W)), full(sgw.shape), full(sgbT.shape), tl.row(W)],
                   [sds((T, W), BF16), sds((T, W), BF16), sds((1, W)), sds((1, W)), sds(sgw.shape), sds(sgbT.shape)],
                   [tl.row(W), tl.row(W), full((1, W)), full((1, W)), full(sgw.shape), full(sgbT.shape)],
                   [False, False, True, True, True, True], tl.n)


def merge_fwd(projB, ap, bp, cb_a, name, tm=256):
    T, D = ap.shape
    tl = Tiles(T, _pick(T, tm))
    (m,) = rowcall(name, lambda i, *a: (merge(*a),), [projB, projB, ap, bp],
                   [tl.row(D, cb_a), tl.row(D, cb_a + 1), tl.row(D), tl.row(D)], [sds((T, D), BF16)], [tl.row(D)], [False], tl.n)
    return m


def merge_bwd(projB, ap, bp, dm, cb_a, name, tm=256):
    T, D = ap.shape
    tl = Tiles(T, _pick(T, tm))

    def fn(i, ga, gb, ap, bp, dm):
        _, vj = jax.vjp(merge, ga, gb, ap, bp)
        return vj(dm.astype(F32))

    return rowcall(name, fn, [projB, projB, ap, bp, dm], [tl.row(D, cb_a), tl.row(D, cb_a + 1), tl.row(D), tl.row(D), tl.row(D)],
                   [sds((T, D), BF16)] * 4, [tl.row(D)] * 4, [False] * 4, tl.n)


def ffn_act_fwd(gp, up, fcw8, fcb, name, tm=256, tc=512):
    T, F = gp.shape
    tl = Tiles(T, _pick(T, tm))
    tc = _pick(F, tc)
    R = tl.tm

    def fn(i, gprev, g, up, cw, cb):
        xwin = jnp.concatenate([jnp.where(i > 0, gprev, 0.0), g], axis=0)
        return (silu(causal_conv(value_taps(xwin, 3, R), cw) + cb) * up,)

    (act,) = rowcall(name, fn, [gp, gp, up, fcw8, fcb], [tl.prevj(tc), tl.rowj(tc), tl.rowj(tc), constj(SUBLANES, tc), constj(1, tc)],
                     [sds((T, F), BF16)], [tl.rowj(tc)], [False], tl.n, F // tc)
    return act


def ffn_act_bwd(gp, up, fcw8, fcb, dact, name, tm=256, tc=512):
    T, F = gp.shape
    tl = Tiles(T, _pick(T, tm))
    tc = _pick(F, tc)
    R = tl.tm
    RE = R + SUBLANES

    def fn(i, gprev, g, gnext, up, upn, da, dan, cw, cb):
        last = i == tl.n - 1
        xwin = jnp.concatenate([jnp.where(i > 0, gprev, 0.0), g, jnp.where(last, 0.0, gnext)], axis=0)
        taps = value_taps(xwin, 3, RE)
        gate = causal_conv(taps, cw) + cb
        upe = jnp.concatenate([up, upn], axis=0)
        dae = jnp.concatenate([da, jnp.where(last, 0.0, dan)], axis=0)
        s = sigmoid(gate)
        dgate = dae * upe * (s * (1.0 + gate * (1.0 - s)))
        dup = da * (gate[0:R, :] * s[0:R, :])
        dgp = cw[0:1, :] * dgate[2:2 + R, :] + cw[1:2, :] * dgate[1:1 + R, :] + cw[2:3, :] * dgate[0:R, :]
        dcw = rows_to8([jnp.sum(dgate[0:R, :] * taps[j][0:R, :], axis=0, keepdims=True) for j in range(3)], tc)
        dcb = jnp.sum(dgate[0:R, :], axis=0, keepdims=True)
        return dgp, dup, dcw, dcb

    return rowcall(name, fn, [gp, gp, gp, up, up, dact, dact, fcw8, fcb],
                   [tl.prevj(tc), tl.rowj(tc), tl.nxtj(tc), tl.rowj(tc), tl.nxtj(tc), tl.rowj(tc), tl.nxtj(tc),
                    constj(SUBLANES, tc), constj(1, tc)],
                   [sds((T, F), BF16), sds((T, F), BF16), sds((SUBLANES, F)), sds((1, F))],
                   [tl.rowj(tc), tl.rowj(tc), constj(SUBLANES, tc), constj(1, tc)], [False, False, True, True], tl.n, F // tc)


def _me():
    return lax.axis_index("x"), lax.axis_index("y"), lax.axis_index("c")


def all_gather(shards, name):
    nt = len(shards)

    def body(*refs):
        xs, outs = refs[:nt], refs[nt:2 * nt]
        send_sems, recv_sems, local_sems = refs[2 * nt:]
        x, y, c = _me()
        me, sibling = (x, y, c), (x, y, 1 - c)
        chips = [(1 - x, y), (x, 1 - y), (1 - x, 1 - y)]

        def slot(t, p):
            return outs[t].at[4 * p[0] + 2 * p[1] + p[2]]

        def copy(t, k, block, to, src=None):
            return pltpu.make_async_remote_copy(
                src_ref=slot(t, block) if src is None else src, dst_ref=slot(t, block),
                send_sem=send_sems.at[t, k], recv_sem=recv_sems.at[t, k], device_id=to, device_id_type=MESH)

        mine = [pltpu.make_async_copy(xs[t], slot(t, me), local_sems.at[t]) for t in range(nt)]
        first = []
        for t in range(nt):
            mine[t].start()
            first.append(copy(t, 0, me, sibling, src=xs[t]))
            first += [copy(t, 1 + j, me, (*chip, c), src=xs[t]) for j, chip in enumerate(chips)]
        for cp in first:
            cp.start()
        passed = []
        for j, chip in enumerate(chips):
            for t in range(nt):
                copy(t, 1 + j, (*chip, c), me).wait_recv()
                cp = copy(t, 4 + j, (*chip, c), sibling)
                cp.start()
                passed.append(cp)
        for t in range(nt):
            copy(t, 0, sibling, me).wait_recv()
            for j, chip in enumerate(chips):
                copy(t, 4 + j, (*chip, 1 - c), me).wait_recv()
        for cp in first + passed:
            cp.wait_send()
        for t in range(nt):
            mine[t].wait()

    any_spec = pl.BlockSpec(memory_space=pl.ANY)
    return pl.pallas_call(
        body, name=name, in_specs=[any_spec] * nt, out_specs=[any_spec] * nt,
        out_shape=[jax.ShapeDtypeStruct((N_DEV,) + s.shape, s.dtype) for s in shards],
        scratch_shapes=[pltpu.SemaphoreType.DMA((nt, 7)), pltpu.SemaphoreType.DMA((nt, 7)), pltpu.SemaphoreType.DMA((nt,))],
    )(*shards)


_HBM = pl.BlockSpec(memory_space=pltpu.HBM)
_SEM = pl.BlockSpec(memory_space=pltpu.SEMAPHORE)
_ANY = pl.BlockSpec(memory_space=pl.ANY)
_DATAFLOW = pltpu.SideEffectType.DATAFLOW_SIDE_EFFECTING


def _peers():
    x, y, c = _me()
    out = []
    for k in range(1, N_DEV):
        p = (x ^ (k >> 2), y ^ ((k >> 1) & 1), c ^ (k & 1))
        out.append((k, p, 4 * p[0] + 2 * p[1] + p[2]))
    return out


def _split_copy(src, land, send_sems, recv_sems, t, k, peer, slot, my, scatter, receiving):
    return pltpu.make_async_remote_copy(
        src_ref=src.at[slot] if scatter else land.at[my], dst_ref=land.at[slot if receiving else my],
        send_sem=send_sems.at[t * (N_DEV - 1) + k - 1], recv_sem=recv_sems.at[t * (N_DEV - 1) + k - 1],
        device_id=peer, device_id_type=MESH)


def comm_start(groups, scatter, name, after=None):
    flat = [a for g in groups for a in g]
    nt = len(flat)
    if scatter:
        lands = [lax.empty(a.shape, a.dtype) for a in flat]
    else:
        me = 4 * lax.axis_index("x") + 2 * lax.axis_index("y") + lax.axis_index("c")
        lands = [lax.dynamic_update_index_in_dim(lax.empty((N_DEV,) + a.shape, a.dtype), a, me, 0) for a in flat]
    ng = len(groups)
    n_after = 0 if after is None else 1

    def body(*refs):
        src, land = refs[:nt], refs[nt:2 * nt]
        sems = refs[2 * nt + n_after:2 * nt + n_after + 2 * ng]
        token = refs[-1]
        x, y, c = _me()
        my = 4 * x + 2 * y + c
        t0 = 0
        for gi, g in enumerate(groups):
            for k, peer, slot in _peers():
                for t in range(len(g)):
                    _split_copy(src[t0 + t], land[t0 + t], sems[2 * gi], sems[2 * gi + 1], t, k, peer, slot, my, scatter,
                                False).start()
            t0 += len(g)
        token[...] = jnp.zeros_like(token)

    sem_shapes = []
    for g in groups:
        sem_shapes += [pltpu.SemaphoreType.DMA((len(g) * (N_DEV - 1),))] * 2
    res = pl.pallas_call(
        body, name=name, in_specs=[_HBM] * (2 * nt) + [_HBM] * n_after,
        out_specs=[_SEM] * (2 * ng) + [_HBM] * (2 * nt) + [pl.BlockSpec(memory_space=pltpu.VMEM)],
        out_shape=sem_shapes + [pltpu.HBM(a.shape, a.dtype) for a in flat + lands] + [sds((SUBLANES, LANES))],
        input_output_aliases={i: 2 * ng + i for i in range(2 * nt)},
        compiler_params=pltpu.CompilerParams(has_side_effects=_DATAFLOW),
    )(*[pltpu.with_memory_space_constraint(a, pltpu.HBM) for a in flat + lands + ([] if after is None else [after])])
    handles = []
    t0 = 0
    for gi, g in enumerate(groups):
        n = len(g)
        handles.append(dict(sems=(res[2 * gi], res[2 * gi + 1]), src=res[2 * ng + t0:2 * ng + t0 + n],
                            land=res[2 * ng + nt + t0:2 * ng + nt + t0 + n], scatter=scatter))
        t0 += n
    return handles, res[-1]


def comm_wait(handle, after, name):
    src, land, scatter = handle["src"], handle["land"], handle["scatter"]
    nt = len(src)

    def body(*refs):
        src_r, land_r = refs[:nt], refs[nt:2 * nt]
        send_sems, recv_sems = refs[2 * nt], refs[2 * nt + 1]
        x, y, c = _me()
        my = 4 * x + 2 * y + c
        for k, peer, slot in _peers():
            for t in range(nt):
                _split_copy(src_r[t], land_r[t], send_sems, recv_sems, t, k, peer, slot, my, scatter, False).wait_send()
                _split_copy(src_r[t], land_r[t], send_sems, recv_sems, t, k, peer, slot, my, scatter, True).wait_recv()

    after = list(after) if isinstance(after, (list, tuple)) else [after]
    res = pl.pallas_call(
        body, name=name, in_specs=[_HBM] * (2 * nt) + [_SEM, _SEM] + [_HBM] * len(after), out_specs=[_HBM] * (2 * nt),
        out_shape=[pltpu.HBM(a.shape, a.dtype) for a in list(src) + list(land)],
        input_output_aliases={i: i for i in range(2 * nt)},
        compiler_params=pltpu.CompilerParams(has_side_effects=_DATAFLOW),
    )(*src, *land, *handle["sems"], *[pltpu.with_memory_space_constraint(a, pltpu.HBM) for a in after])
    return res[:nt], res[nt:]


def sum_adamw_shard(own_src, land, me, w, m, v, l, prev, name, tr=256):
    L, R, C = w.shape
    by_rows = R % SUBLANES == 0 or C % LANES != 0
    tr, tc = (_pick(R, tr, SUBLANES), C) if by_rows else (R, _pick(C, 256))
    steps = R // tr if by_rows else C // tc
    c1 = 1.0 - ADAM_B1 ** ADAM_STEP
    c2 = 1.0 - ADAM_B2 ** ADAM_STEP
    n_prev = 0 if prev is None else 4

    def at(lead, i):
        return (lead, i, 0) if by_rows else (lead, 0, i)

    def body(me_ref, *refs):
        parts = refs[:N_DEV]
        w_r, m_r, v_r = refs[N_DEV:N_DEV + 3]
        g_o, d_o, m_o, v_o = refs[N_DEV + 3 + n_prev:]
        g = parts[0][0].astype(F32)
        for k in range(1, N_DEV):
            g = g + parts[k][0].astype(F32)
        mn = ADAM_B1 * m_r[0] + (1.0 - ADAM_B1) * g
        vn = ADAM_B2 * v_r[0] + (1.0 - ADAM_B2) * (g * g)
        g_o[0] = g
        d_o[0] = -ADAM_LR * ((mn / c1) / (jnp.sqrt(vn / c2) + ADAM_EPS) + ADAM_WD * w_r[0])
        m_o[0] = mn
        v_o[0] = vn

    part_specs = [pl.BlockSpec((1, tr, tc), lambda i, me, k=k: at(me[0] ^ k, i)) for k in range(N_DEV)]
    lay = pl.BlockSpec((1, tr, tc), lambda i, me: at(l, i))
    grid_spec = pltpu.PrefetchScalarGridSpec(
        num_scalar_prefetch=1, grid=(steps,), in_specs=part_specs + [lay] * 3 + [_ANY] * n_prev, out_specs=[lay] * 4)
    return pl.pallas_call(
        body, name=name, grid_spec=grid_spec, out_shape=[sds((L, R, C))] * 4,
        input_output_aliases={1 + N_DEV + 3 + j: j for j in range(n_prev)}, compiler_params=_params(("parallel",)),
    )(me, own_src, *[land] * (N_DEV - 1), w, m, v, *([] if prev is None else prev))


def sum_adamw(parts, w, m, v, name, tr=256):
    _, R, C = parts.shape
    tr = _pick(R, tr, SUBLANES)
    c1 = 1.0 - ADAM_B1 ** ADAM_STEP
    c2 = 1.0 - ADAM_B2 ** ADAM_STEP

    def body(p, w, m, v, g_o, d_o, m_o, v_o):
        g = p[0].astype(F32)
        for d in range(1, N_DEV):
            g = g + p[d].astype(F32)
        mn = ADAM_B1 * m[...] + (1.0 - ADAM_B1) * g
        vn = ADAM_B2 * v[...] + (1.0 - ADAM_B2) * (g * g)
        m_hat = mn / c1
        v_hat = vn / c2
        g_o[...] = g
        d_o[...] = -ADAM_LR * (m_hat / (jnp.sqrt(v_hat) + ADAM_EPS) + ADAM_WD * w[...])
        m_o[...] = mn
        v_o[...] = vn

    blk = pl.BlockSpec((tr, C), lambda i: (i, 0))
    return pl.pallas_call(
        body, name=name, grid=(R // tr,), in_specs=[pl.BlockSpec((N_DEV, tr, C), lambda i: (0, i, 0)), blk, blk, blk],
        out_specs=[blk] * 4, out_shape=[sds((R, C))] * 4, compiler_params=_params(("parallel",)),
    )(parts, w, m, v)


SHARDED = ("w_in", "dn_conv_w", "w_branch_a", "w_branch_b", "w_out", "ffn_w_gate", "ffn_w_up", "ffn_conv_w", "ffn_w_down")
TRANSPOSED = ("w_in", "ffn_w_gate", "ffn_w_up")
COL_SHARDED = ("dn_conv_w", "w_branch_a", "w_branch_b", "ffn_conv_w")
CONV_WEIGHTS = ("dn_conv_w", "ffn_conv_w")
REPLICATED = ("norm1_g", "dn_a_log", "dn_dt_bias", "dn_onorm_g", "sg_ln_g", "sg_ln_b", "sg_w", "sg_b", "norm2_g",
              "ffn_conv_b", "final_norm_g")
WEIGHTS = ("norm1_g", "w_in", "dn_conv_w", "dn_a_log", "dn_dt_bias", "dn_onorm_g", "sg_ln_g", "sg_ln_b", "sg_w", "sg_b",
           "w_branch_a", "w_branch_b", "w_out", "norm2_g", "ffn_w_gate", "ffn_w_up", "ffn_conv_w", "ffn_conv_b",
           "ffn_w_down", "final_norm_g")


def _columns(pieces, lo, hi):
    out = []
    for a, start, width in pieces:
        s, e = max(lo, start), min(hi, start + width)
        if s < e:
            out.append(a[:, s - start:e - start])
    return out[0] if len(out) == 1 else jnp.concatenate(out, axis=1)


def _assemble(name, g):
    if name in COL_SHARDED:
        return jnp.concatenate([g[d] for d in range(N_DEV)], axis=1)
    return g.reshape(N_DEV * g.shape[1], g.shape[2])


def _split(name, pieces, dtype):
    total = sum(w for _, _, w in pieces)
    if name in COL_SHARDED:
        cs = total // N_DEV
        return jnp.stack([_columns(pieces, d * cs, (d + 1) * cs).astype(dtype) for d in range(N_DEV)])
    a = pieces[0][0] if len(pieces) == 1 else jnp.concatenate([p[:w] for p, _, w in pieces], axis=0)
    return a.reshape(N_DEV, a.shape[0] // N_DEV, a.shape[1]).astype(dtype)


def _pad_lanes(a, lo, width=LANES):
    return jnp.pad(a, ((0, 0), (lo, width - lo - a.shape[1])))


def _pad_rows(a, rows=SUBLANES):
    return jnp.pad(a, ((0, rows - a.shape[0]), (0, 0)))


def kernel(x, norm1_g, w_in, dn_conv_w, dn_a_log, dn_dt_bias, dn_onorm_g, sg_ln_g, sg_ln_b, sg_w, sg_b, w_branch_a, w_branch_b, w_out, norm2_g, ffn_w_gate, ffn_w_up, ffn_conv_w, ffn_conv_b, ffn_w_down, final_norm_g, loss_target, m_norm1_g, m_w_in, m_dn_conv_w, m_dn_a_log, m_dn_dt_bias, m_dn_onorm_g, m_sg_ln_g, m_sg_ln_b, m_sg_w, m_sg_b, m_w_branch_a, m_w_branch_b, m_w_out, m_norm2_g, m_ffn_w_gate, m_ffn_w_up, m_ffn_conv_w, m_ffn_conv_b, m_ffn_w_down, m_final_norm_g, v_norm1_g, v_w_in, v_dn_conv_w, v_dn_a_log, v_dn_dt_bias, v_dn_onorm_g, v_sg_ln_g, v_sg_ln_b, v_sg_w, v_sg_b, v_w_branch_a, v_w_branch_b, v_w_out, v_norm2_g, v_ffn_w_gate, v_ffn_w_up, v_ffn_conv_w, v_ffn_conv_b, v_ffn_w_down, v_final_norm_g):
    W = dict(norm1_g=norm1_g, w_in=w_in, dn_conv_w=dn_conv_w, dn_a_log=dn_a_log, dn_dt_bias=dn_dt_bias, dn_onorm_g=dn_onorm_g,
             sg_ln_g=sg_ln_g, sg_ln_b=sg_ln_b, sg_w=sg_w, sg_b=sg_b, w_branch_a=w_branch_a, w_branch_b=w_branch_b, w_out=w_out,
             norm2_g=norm2_g, ffn_w_gate=ffn_w_gate, ffn_w_up=ffn_w_up, ffn_conv_w=ffn_conv_w, ffn_conv_b=ffn_conv_b,
             ffn_w_down=ffn_w_down, final_norm_g=final_norm_g)
    Mo = dict(norm1_g=m_norm1_g, w_in=m_w_in, dn_conv_w=m_dn_conv_w, dn_a_log=m_dn_a_log, dn_dt_bias=m_dn_dt_bias,
              dn_onorm_g=m_dn_onorm_g, sg_ln_g=m_sg_ln_g, sg_ln_b=m_sg_ln_b, sg_w=m_sg_w, sg_b=m_sg_b, w_branch_a=m_w_branch_a,
              w_branch_b=m_w_branch_b, w_out=m_w_out, norm2_g=m_norm2_g, ffn_w_gate=m_ffn_w_gate, ffn_w_up=m_ffn_w_up,
              ffn_conv_w=m_ffn_conv_w, ffn_conv_b=m_ffn_conv_b, ffn_w_down=m_ffn_w_down, final_norm_g=m_final_norm_g)
    Vo = dict(norm1_g=v_norm1_g, w_in=v_w_in, dn_conv_w=v_dn_conv_w, dn_a_log=v_dn_a_log, dn_dt_bias=v_dn_dt_bias,
              dn_onorm_g=v_dn_onorm_g, sg_ln_g=v_sg_ln_g, sg_ln_b=v_sg_ln_b, sg_w=v_sg_w, sg_b=v_sg_b, w_branch_a=v_w_branch_a,
              w_branch_b=v_w_branch_b, w_out=v_w_out, norm2_g=v_norm2_g, ffn_w_gate=v_ffn_w_gate, ffn_w_up=v_ffn_w_up,
              ffn_conv_w=v_ffn_conv_w, ffn_conv_b=v_ffn_conv_b, ffn_w_down=v_ffn_w_down, final_norm_g=v_final_norm_g)

    xs = x[0]
    tgt = loss_target[0]
    T, D = xs.shape
    depth = norm1_g.shape[0]
    H = dn_a_log.shape[1]
    G = sg_w.shape[1]
    WA = H * HEAD_DIM
    WB = G * HEAD_DIM
    N = T // DN_CHUNK
    colA = 4 * WA
    colB0 = colA + 2 * H
    cb_a = (2 * WB) // D

    my = 4 * lax.axis_index("x") + 2 * lax.axis_index("y") + lax.axis_index("c")
    me_arr = my.astype(jnp.int32).reshape(1)

    def view(d):
        return {n: (jnp.transpose(d[n], (0, 2, 1)) if n in TRANSPOSED else d[n]) for n in SHARDED}

    Wv, Mv, Vv = view(W), view(Mo), view(Vo)

    def shard(n, l):
        return Wv[n][l] if n in CONV_WEIGHTS else Wv[n][l].astype(BF16)

    first = [("w_in", 0), ("dn_conv_w", 0)]
    first_blocks = all_gather([shard(n, l) for n, l in first], "gather_first")
    gathered = dict(zip(first, first_blocks))
    gather_names = []
    for l in range(depth):
        if l > 0:
            gather_names.append([("w_in", l), ("dn_conv_w", l)])
        gather_names.append([("w_branch_a", l), ("w_branch_b", l), ("w_out", l)])
        gather_names.append([("ffn_w_gate", l), ("ffn_w_up", l), ("ffn_conv_w", l)])
        gather_names.append([("ffn_w_down", l)])
    gather_handles, gather_tok = comm_start([[shard(n, l) for n, l in g] for g in gather_names], False, "gather_start",
                                            after=first_blocks[0])

    def need(n, l, after):
        if (n, l) not in gathered:
            gi = [i for i, g in enumerate(gather_names) if (n, l) in g][0]
            src, land = comm_wait(gather_handles[gi], after, f"gather_wait{gi}")
            for key, s, ld in zip(gather_names[gi], src, land):
                gathered[key] = ld
        return gathered[(n, l)]

    def full(n, l, after):
        return _assemble(n, need(n, l, after))

    def layer_weights(l):
        return dict(
            g1=norm1_g[l][None], g2=norm2_g[l][None], alog=_pad_lanes(dn_a_log[l][None], H), dtb=_pad_lanes(dn_dt_bias[l][None], H),
            og=dn_onorm_g[l][None], lng=sg_ln_g[l][None], lnb=sg_ln_b[l][None], sgw=sg_w[l], sgbT=sg_b[l].T, fcb=ffn_conv_b[l][None])

    def mixer_in_weights(p, l, after):
        wt = full("w_in", l, after)
        p.update(wA=wt[:colA], wba=_pad_rows(wt[colA:colB0], LANES), wB=wt[colB0:], cw8=_pad_rows(full("dn_conv_w", l, after)))

    def mixer_out_weights(p, l, after):
        p.update(wa=full("w_branch_a", l, after), wb=full("w_branch_b", l, after), wo=full("w_out", l, after))

    def ffn_weights(p, l, after):
        p.update(wg=full("ffn_w_gate", l, after), wu=full("ffn_w_up", l, after), fcw8=_pad_rows(full("ffn_conv_w", l, after)))

    saved = []
    cur = xs
    for l in range(depth):
        p = layer_weights(l)
        t = f"l{l}_"
        h = norm_fwd(cur, p["g1"] + gather_tok[0, 0] if l == 0 else p["g1"], t + "norm1")
        mixer_in_weights(p, l, h)
        projA = matmul(h, p["wA"], "nt", t + "projA")
        pba = matmul(h, p["wba"], "nt", t + "proj_ba")
        projB = matmul(h, p["wB"], "nt", t + "projB")
        q, k, v, bg = dn_prep_fwd(projA, pba, p["cw8"], p["alog"], p["dtb"], H, t + "dn_prep")
        u, w, a, qd, kd, gl, tinv = dn_chunk_fwd(q, k, v, bg, t + "dn_chunk")
        o, s_in = dn_scan_fwd(u, w, a, qd, kd, gl, t + "dn_scan")
        y_a = dn_post_fwd(o, projA, p["og"], t + "dn_post")
        y_b = gmlp_fwd(projB, p["lng"], p["lnb"], p["sgw"], p["sgbT"], t + "gmlp")
        mixer_out_weights(p, l, y_b)
        ap = matmul(y_a, p["wa"], "nn", t + "branch_a")
        bp = matmul(y_b, p["wb"], "nn", t + "branch_b")
        merged = merge_fwd(projB, ap, bp, cb_a, t + "merge")
        x1 = matmul(merged, p["wo"], "nn", t + "out_proj", c=cur)
        h2 = norm_fwd(x1, p["g2"], t + "norm2")
        ffn_weights(p, l, h2)
        gp = matmul(h2, p["wg"], "nt", t + "ffn_gate")
        up = matmul(h2, p["wu"], "nt", t + "ffn_up")
        act = ffn_act_fwd(gp, up, p["fcw8"], p["fcb"], t + "ffn_act")
        p.update(wd=full("ffn_w_down", l, act))
        x2 = matmul(act, p["wd"], "nn", t + "ffn_down", c=x1)
        saved.append(dict(p=p, x0=cur, h=h, projA=projA, pba=pba, projB=projB, q=q, k=k, v=v, bg=bg, tinv=tinv,
                          scan=(u, w, a, qd, kd, gl), s_in=s_in, o=o, y_a=y_a, y_b=y_b, ap=ap, bp=bp, merged=merged, x1=x1,
                          h2=h2, gp=gp, up=up, act=act))
        cur = x2

    loss_part, dx, dx_bf, d_final = head_fwd_bwd(cur, final_norm_g[None], tgt, "loss_head")
    loss = lax.psum(loss_part[0, 0], ("x", "y", "c"))

    grads_sh = {n: [None] * depth for n in SHARDED}
    grads_rep = {n: [None] * depth for n in REPLICATED if n != "final_norm_g"}
    exchanges = []

    def exchange(names, l, name, after=None):
        srcs = [_split(n, grads_sh[n][l], F32 if n in CONV_WEIGHTS else BF16) for n in names]
        (handle,), tok = comm_start([srcs], True, name, after=after)
        exchanges.append((names, l, handle))
        return tok

    def whole(a):
        return [(a, 0, a.shape[1])]

    sizes = [math.prod(W[n].shape) for n in REPLICATED]
    tile = SUBLANES * LANES
    nrows = [-(-sz // tile) * SUBLANES for sz in sizes]

    def pack(d):
        parts = [jnp.pad(d[n].reshape(-1).astype(F32), (0, r * LANES - sz)).reshape(r, LANES)
                 for n, sz, r in zip(REPLICATED, sizes, nrows)]
        return jnp.concatenate(parts, axis=0)

    mixer_tok = None
    for l in reversed(range(depth)):
        s = saved[l]
        p = s["p"]
        t = f"l{l}_b_"
        dact = matmul(dx_bf, p["wd"], "nt", t + "d_act")
        grads_sh["ffn_w_down"][l] = whole(matmul(s["act"], dx_bf, "tn", t + "dw_down", out_dtype=BF16, tn=2048))
        fcb = p["fcb"] if mixer_tok is None else p["fcb"] + mixer_tok[0, 0]
        dgp, dup, dfcw, dfcb = ffn_act_bwd(s["gp"], s["up"], p["fcw8"], fcb, dact, t + "ffn_act")
        dh2 = matmul([dgp, dup], [p["wg"], p["wu"]], "nn", t + "dh2")
        grads_sh["ffn_w_gate"][l] = whole(matmul(dgp, s["h2"], "tn", t + "dw_gate", out_dtype=BF16, tn=2048))
        grads_sh["ffn_w_up"][l] = whole(matmul(dup, s["h2"], "tn", t + "dw_up", out_dtype=BF16, tn=2048))
        grads_sh["ffn_conv_w"][l] = whole(dfcw[:3])
        grads_rep["ffn_conv_b"][l] = dfcb[0]
        tok = exchange(("ffn_w_down", "ffn_w_gate", "ffn_w_up", "ffn_conv_w"), l, t + "ffn_grads_start")
        dx1, dx1_bf, dg2 = norm_bwd(s["x1"], p["g2"] + tok[0, 0], dh2, dx, t + "norm2")
        grads_rep["norm2_g"][l] = dg2[0]
        dmerged = matmul(dx1_bf, p["wo"], "nt", t + "d_merged")
        grads_sh["w_out"][l] = whole(matmul(s["merged"], dx1_bf, "tn", t + "dw_out", out_dtype=BF16, tn=2048))
        dga, dgb, dap, dbp = merge_bwd(s["projB"], s["ap"], s["bp"], dmerged, cb_a, t + "merge")
        dya = matmul(dap, p["wa"], "nt", t + "d_ya")
        dyb = matmul(dbp, p["wb"], "nt", t + "d_yb")
        grads_sh["w_branch_a"][l] = whole(matmul(s["y_a"], dap, "tn", t + "dw_a", out_dtype=BF16))
        grads_sh["w_branch_b"][l] = whole(matmul(s["y_b"], dbp, "tn", t + "dw_b", out_dtype=BF16))
        du_raw, dv_raw, dlng, dlnb, dsgw, dsgbT = gmlp_bwd(s["projB"], p["lng"], p["lnb"], p["sgw"], p["sgbT"], dyb, t + "gmlp")
        grads_rep["sg_ln_g"][l], grads_rep["sg_ln_b"][l] = dlng[0], dlnb[0]
        grads_rep["sg_w"][l], grads_rep["sg_b"][l] = dsgw, dsgbT.T
        do, dz, dog = dn_post_bwd(s["o"], s["projA"], p["og"], dya, t + "dn_post")
        grads_rep["dn_onorm_g"][l] = dog[0]
        du, dw, da, dqd, dkd, dgl = dn_scan_bwd(*s["scan"], s["s_in"], do, t + "dn_scan")
        dq, dk, dv, dbg = dn_chunk_bwd(s["q"], s["k"], s["v"], s["bg"], s["tinv"], du, dw, da, dqd, dkd, dgl, t + "dn_chunk")
        dqkv, dba, dcw, dalog, ddtb = dn_prep_bwd(s["projA"], s["pba"], p["cw8"], p["alog"], p["dtb"], dq, dk, dv, dbg, H,
                                                  t + "dn_prep")
        grads_sh["dn_conv_w"][l] = whole(dcw[:4])
        grads_rep["dn_a_log"][l], grads_rep["dn_dt_bias"][l] = dalog[0, H:2 * H], ddtb[0, H:2 * H]
        tok = exchange(("w_out", "w_branch_a", "w_branch_b", "dn_conv_w"), l, t + "mixer_grads_start")
        dba = dba + tok[0, 0].astype(BF16)
        dprojA = jnp.concatenate([dqkv, dz], axis=1)
        dprojB = jnp.concatenate([du_raw, dv_raw, dga, dgb], axis=1)
        dwA = matmul(dprojA, s["h"], "tn", t + "dw_A", out_dtype=BF16, tn=2048)
        dwba = matmul(dba, s["h"], "tn", t + "dw_ba", out_dtype=BF16, tn=2048)
        dwB = matmul(dprojB, s["h"], "tn", t + "dw_B", out_dtype=BF16, tn=2048)
        grads_sh["w_in"][l] = [(dwA, 0, colA), (dwba, colA, 2 * H), (dwB, colB0, dwB.shape[0])]
        mixer_tok = exchange(("w_in",), l, t + "w_in_grads_start")
        dh = matmul([dba, dprojA, dprojB], [p["wba"] + mixer_tok[0, 0].astype(BF16), p["wA"], p["wB"]], "nn", t + "dh")
        dx, dx_bf, dg1 = norm_bwd(s["x0"], p["g1"], dh, dx1, t + "norm1")
        grads_rep["norm1_g"][l] = dg1[0]
        if l == 0:
            rep_full = {n: (jnp.stack(grads_rep[n]) if n != "final_norm_g" else d_final[0]) for n in REPLICATED}
            (small_handle,), small_tok = comm_start([[pack(rep_full)]], False, "small_grads_start")

    out = {}
    after = [dx, small_tok]

    def update_group(gi, after):
        names, l, handle = exchanges[gi]
        src, land = comm_wait(handle, after, f"grads_wait{gi}")
        done = []
        for n, s_, ld in zip(names, src, land):
            res = sum_adamw_shard(s_, ld, me_arr, Wv[n], Mv[n], Vv[n], l, out.get(n), f"adamw_{n}_{l}")
            out[n] = list(res)
            done.append(res[0])
        return done

    for gi in range(len(exchanges) - 1):
        after = update_group(gi, after)

    (small_src,), (small_land,) = comm_wait(small_handle, after, "small_grads_wait")
    res = sum_adamw(small_land, pack(W), pack(Mo), pack(Vo), "adamw_small")
    update_group(len(exchanges) - 1, after + [res[0]])
    for n in TRANSPOSED:
        out[n] = [jnp.transpose(r, (0, 2, 1)) for r in out[n]]
    row0 = 0
    for n, sz, nr in zip(REPLICATED, sizes, nrows):
        out[n] = [r[row0:row0 + nr].reshape(-1)[:sz].reshape(W[n].shape) for r in res]
        row0 += nr

    return (loss, dx[None], *[out[n][0] for n in WEIGHTS], *[out[n][1] for n in WEIGHTS],
            *[out[n][2] for n in WEIGHTS], *[out[n][3] for n in WEIGHTS])
```

```python
import functools
import math

import jax
import jax.numpy as jnp
from jax import lax
from jax.experimental import pallas as pl
from jax.experimental.pallas import tpu as pltpu

F32 = jnp.float32
BF16 = jnp.bfloat16
EPS = 1e-6
N_DEV = 8
LANES = 128
SUBLANES = 8
HEAD_DIM = 128
DN_CHUNK = 64
SG_CHUNK = 128
VMEM_LIMIT = 56 * 1024 * 1024
MESH = pl.DeviceIdType.MESH
HIGHEST = lax.Precision.HIGHEST

ADAM_LR = 0.001
ADAM_B1 = 0.9
ADAM_B2 = 0.999
ADAM_EPS = 1e-08
ADAM_WD = 0.01
ADAM_STEP = 10


def _pick(n, target, mult=LANES):
    best = None
    d = mult
    while d <= min(n, target):
        if n % d == 0:
            best = d
        d += mult
    return n if best is None else best


def _params(sem):
    return pltpu.CompilerParams(dimension_semantics=sem, vmem_limit_bytes=VMEM_LIMIT)


_NN = (((1,), (0,)), ((), ()))
_NT = (((1,), (1,)), ((), ()))
_TN = (((0,), (0,)), ((), ()))


def _dg(a, b, dims, hi):
    if hi == 2:
        return lax.dot_general(a.astype(F32), b.astype(F32), dims, precision=HIGHEST, preferred_element_type=F32)
    if hi == 1:
        a_hi, b_hi = a.astype(BF16), b.astype(BF16)
        a_lo, b_lo = (a - a_hi.astype(F32)).astype(BF16), (b - b_hi.astype(F32)).astype(BF16)
        ax, bx = dims[0][0][0], dims[0][1][0]
        a = jnp.concatenate([a_hi, a_hi, a_lo], axis=ax)
        b = jnp.concatenate([b_hi, b_lo, b_hi], axis=bx)
        return lax.dot_general(a, b, dims, preferred_element_type=F32)
    return lax.dot_general(a.astype(BF16), b.astype(BF16), dims, preferred_element_type=F32)


@functools.partial(jax.custom_vjp, nondiff_argnums=(2,))
def mm_nn(a, b, hi=False):
    return _dg(a, b, _NN, hi)


def _mm_nn_f(a, b, hi):
    return _dg(a, b, _NN, hi), (a, b)


def _mm_nn_b(hi, res, g):
    a, b = res
    return mm_nt(g, b, hi), mm_tn(a, g, hi)


@functools.partial(jax.custom_vjp, nondiff_argnums=(2,))
def mm_nt(a, b, hi=False):
    return _dg(a, b, _NT, hi)


def _mm_nt_f(a, b, hi):
    return _dg(a, b, _NT, hi), (a, b)


def _mm_nt_b(hi, res, g):
    a, b = res
    return mm_nn(g, b, hi), mm_tn(g, a, hi)


@functools.partial(jax.custom_vjp, nondiff_argnums=(2,))
def mm_tn(a, b, hi=False):
    return _dg(a, b, _TN, hi)


def _mm_tn_f(a, b, hi):
    return _dg(a, b, _TN, hi), (a, b)


def _mm_tn_b(hi, res, g):
    a, b = res
    return mm_nt(b, g, hi), mm_nn(a, g, hi)


mm_nn.defvjp(_mm_nn_f, _mm_nn_b)
mm_nt.defvjp(_mm_nt_f, _mm_nt_b)
mm_tn.defvjp(_mm_tn_f, _mm_tn_b)


def matmul(a, b, mode, name, c=None, out_dtype=F32, tm=1024, tn=1024, tk=2048):
    a_list = list(a) if isinstance(a, (list, tuple)) else [a]
    b_list = list(b) if isinstance(b, (list, tuple)) else [b]
    nterm = len(a_list)

    def dims_of(a, b):
        if mode == "nn":
            return a.shape[0], a.shape[1], b.shape[1]
        if mode == "nt":
            return a.shape[0], a.shape[1], b.shape[0]
        return a.shape[1], a.shape[0], b.shape[1]

    M, _, N = dims_of(a_list[0], b_list[0])
    tm, tn = _pick(M, tm), _pick(N, tn)
    tks = [_pick(dims_of(x, y)[1], tk) for x, y in zip(a_list, b_list)]
    nks = [dims_of(x, y)[1] // t for x, y, t in zip(a_list, b_list, tks)]
    offs = [sum(nks[:t]) for t in range(nterm)]
    nk = sum(nks)
    dims = {"nn": _NN, "nt": _NT, "tn": _TN}[mode]

    def specs_of(t):
        kk = lambda k: jnp.clip(k - offs[t], 0, nks[t] - 1)
        a_spec = (pl.BlockSpec((tks[t], tm), lambda i, j, k: (kk(k), i)) if mode == "tn"
                  else pl.BlockSpec((tm, tks[t]), lambda i, j, k: (i, kk(k))))
        b_spec = (pl.BlockSpec((tn, tks[t]), lambda i, j, k: (j, kk(k))) if mode == "nt"
                  else pl.BlockSpec((tks[t], tn), lambda i, j, k: (kk(k), j)))
        return [a_spec, b_spec]

    o_spec = pl.BlockSpec((tm, tn), lambda i, j, k: (i, j))
    has_c = c is not None
    own_acc = nk > 1 and out_dtype != F32

    def body(*refs):
        ab = refs[:2 * nterm]
        c_ref = refs[2 * nterm] if has_c else None
        o_ref = refs[2 * nterm + (1 if has_c else 0)]
        acc_ref = refs[-1] if own_acc else o_ref

        def dot(t):
            return lax.dot_general(ab[2 * t][...].astype(BF16), ab[2 * t + 1][...].astype(BF16), dims,
                                   preferred_element_type=F32)

        if nk == 1:
            o_ref[...] = (dot(0) + c_ref[...] if has_c else dot(0)).astype(o_ref.dtype)
        else:
            k = pl.program_id(2)

            @pl.when(k == 0)
            def _():
                acc_ref[...] = c_ref[...] if has_c else jnp.zeros_like(acc_ref)

            for t in range(nterm):
                if nterm == 1:
                    acc_ref[...] += dot(t)
                else:
                    @pl.when((k >= offs[t]) & (k < offs[t] + nks[t]))
                    def _(t=t):
                        acc_ref[...] += dot(t)

            if own_acc:
                @pl.when(k == nk - 1)
                def _():
                    o_ref[...] = acc_ref[...].astype(o_ref.dtype)

    ins, specs = [], []
    for t in range(nterm):
        ins += [a_list[t], b_list[t]]
        specs += specs_of(t)
    if has_c:
        ins.append(c)
        specs.append(o_spec)
    return pl.pallas_call(
        body, name=name, grid=(M // tm, N // tn, nk), in_specs=specs, out_specs=o_spec,
        out_shape=jax.ShapeDtypeStruct((M, N), out_dtype), scratch_shapes=[pltpu.VMEM((tm, tn), F32)] if own_acc else [],
        compiler_params=_params(("parallel", "parallel", "arbitrary")),
    )(*ins)


def rowcall(name, fn, ins, in_specs, outs, out_specs, acc, nrow, ncol=1, scratch=()):
    n_in, n_out = len(ins), len(outs)

    def body(*refs):
        i = pl.program_id(1)
        res = fn(i, *[r[...] for r in refs[:n_in]], *refs[n_in + n_out:])
        for r, v, is_acc in zip(refs[n_in:n_in + n_out], res, acc):
            if is_acc:
                @pl.when(i == 0)
                def _(r=r, v=v):
                    r[...] = v.astype(r.dtype)

                @pl.when(i > 0)
                def _(r=r, v=v):
                    r[...] += v.astype(r.dtype)
            else:
                r[...] = v.astype(r.dtype)

    return pl.pallas_call(
        body, name=name, grid=(ncol, nrow), in_specs=list(in_specs), out_specs=list(out_specs), out_shape=list(outs),
        scratch_shapes=list(scratch), compiler_params=_params(("parallel", "arbitrary")),
    )(*ins)


class Tiles:
    def __init__(self, T, tm):
        self.T, self.tm, self.n = T, tm, T // tm
        self.r8 = tm // SUBLANES

    def row(self, w, cb=0):
        return pl.BlockSpec((self.tm, w), lambda j, i: (i, cb))

    def rowj(self, tc):
        return pl.BlockSpec((self.tm, tc), lambda j, i: (i, j))

    def prev(self, w, cb=0):
        return pl.BlockSpec((SUBLANES, w), lambda j, i: (jnp.maximum(i * self.r8 - 1, 0), cb))

    def prevj(self, tc):
        return pl.BlockSpec((SUBLANES, tc), lambda j, i: (jnp.maximum(i * self.r8 - 1, 0), j))

    def nxt(self, w, cb=0):
        last = self.T // SUBLANES - 1
        return pl.BlockSpec((SUBLANES, w), lambda j, i: (jnp.minimum((i + 1) * self.r8, last), cb))

    def nxtj(self, tc):
        last = self.T // SUBLANES - 1
        return pl.BlockSpec((SUBLANES, tc), lambda j, i: (jnp.minimum((i + 1) * self.r8, last), j))

    def heads(self, H):
        return pl.BlockSpec((H, self.tm, HEAD_DIM), lambda j, i: (0, i, 0))

    def heads_nxt(self, H):
        last = self.T // SUBLANES - 1
        return pl.BlockSpec((H, SUBLANES, HEAD_DIM), lambda j, i: (0, jnp.minimum((i + 1) * self.r8, last), 0))


def full(shape):
    return pl.BlockSpec(tuple(shape), lambda j, i: (0,) * len(shape))


def constj(r, tc):
    return pl.BlockSpec((r, tc), lambda j, i: (0, j))


def sds(shape, dtype=F32):
    return jax.ShapeDtypeStruct(tuple(shape), dtype)


def rms(x, g):
    return x * lax.rsqrt(jnp.mean(x * x, axis=-1, keepdims=True) + EPS) * g


def sigmoid(x):
    return jax.nn.sigmoid(x)


def silu(x):
    return x * sigmoid(x)


def gelu(x):
    return 0.5 * x * (1.0 + lax.erf(x * (2.0 ** -0.5)))


def fill_window(win, i, last, prev, x, nxt=None):
    R = x.shape[0]
    win[0:SUBLANES, :] = jnp.where(i > 0, prev, 0.0)
    win[SUBLANES:SUBLANES + R, :] = x
    if nxt is not None:
        win[SUBLANES + R:2 * SUBLANES + R, :] = jnp.where(last, 0.0, nxt)


def conv_taps(win, K, R):
    base = SUBLANES - (K - 1)
    return [win[pl.ds(base + j, R), :] for j in range(K)]


def value_taps(xwin, K, R):
    base = SUBLANES - (K - 1)
    return [xwin[base + j:base + j + R, :] for j in range(K)]


def causal_conv(taps, w):
    out = w[0:1, :] * taps[0]
    for j in range(1, len(taps)):
        out = out + w[j:j + 1, :] * taps[j]
    return out


def rows_to8(rows, C):
    rid = lax.broadcasted_iota(jnp.int32, (SUBLANES, C), 0)
    out = jnp.zeros((SUBLANES, C), F32)
    for k, r in enumerate(rows):
        out = out + jnp.where(rid == k, jnp.broadcast_to(r, (SUBLANES, C)), 0.0)
    return out


def dn_qkv(pre, H):
    a = silu(pre)
    W = H * HEAD_DIM

    def l2(t):
        return t * lax.rsqrt(jnp.sum(t * t, axis=-1, keepdims=True) + EPS)

    q = [l2(a[:, h * HEAD_DIM:(h + 1) * HEAD_DIM]) for h in range(H)]
    k = [l2(a[:, W + h * HEAD_DIM:W + (h + 1) * HEAD_DIM]) for h in range(H)]
    v = [a[:, 2 * W + h * HEAD_DIM:2 * W + (h + 1) * HEAD_DIM] for h in range(H)]
    return q, k, v


def dn_gates(ba, alog, dtb, H, R):
    lane = lax.broadcasted_iota(jnp.int32, (R, LANES), 1)
    beta = sigmoid(ba)
    g = -jnp.exp(alog) * jax.nn.softplus(ba + dtb)
    g = jnp.where((lane >= H) & (lane < 2 * H), g, 0.0)
    ri = lax.broadcasted_iota(jnp.int32, (R, R), 0)
    ci = lax.broadcasted_iota(jnp.int32, (R, R), 1)
    cum = jnp.where((ri // DN_CHUNK == ci // DN_CHUNK) & (ci <= ri), 1.0, 0.0).astype(F32)
    gc = mm_nn(cum, g, 2)
    return jnp.where(lane < H, beta, gc)


def neumann_inverse(Ls):
    C = Ls[0].shape[0]
    ri = lax.broadcasted_iota(jnp.int32, (C, C), 0)
    ci = lax.broadcasted_iota(jnp.int32, (C, C), 1)
    eye = jnp.where(ri == ci, 1.0, 0.0).astype(F32)
    P = [-L for L in Ls]
    R = [eye + p for p in P]
    for _ in range(int(math.log2(C)) - 1):
        P = [mm_nn(p, p, 1) for p in P]
        R = [r + mm_nn(r, p, 1) for r, p in zip(R, P)]
    return R


@jax.custom_vjp
def saved_inverse(L, T):
    return T


def _saved_inverse_f(L, T):
    return T, T


def _saved_inverse_b(T, g):
    return -mm_tn(T, mm_nt(g, T)), jnp.zeros_like(T)


saved_inverse.defvjp(_saved_inverse_f, _saved_inverse_b)


def gate_columns(bg, H):
    bgT = bg.T
    return ([bg[:, h:h + 1] for h in range(H)], [bg[:, H + h:H + h + 1] for h in range(H)],
            [bgT[H + h:H + h + 1, :] for h in range(H)])


def dn_chunk(q, k, v, beta, gc, gr, tinv=None, with_inverse=False):
    n = len(q)
    C = q[0].shape[0]
    ri = lax.broadcasted_iota(jnp.int32, (C, C), 0)
    ci = lax.broadcasted_iota(jnp.int32, (C, C), 1)
    qs = [q[h] * (HEAD_DIM ** -0.5) for h in range(n)]
    kb = [k[h] * beta[h] for h in range(n)]
    vb = [v[h] * beta[h] for h in range(n)]
    decay = [jnp.exp(jnp.where(ri >= ci, gc[h] - gr[h], -jnp.inf)) for h in range(n)]
    L = [jnp.where(ri > ci, mm_nt(kb[h], k[h]) * decay[h], 0.0) for h in range(n)]
    attn = [jnp.where(ri >= ci, mm_nt(qs[h], k[h]) * decay[h], 0.0) for h in range(n)]
    Tinv = neumann_inverse(L) if tinv is None else [saved_inverse(L[h], tinv[h]) for h in range(n)]
    eg = [jnp.exp(gc[h]) for h in range(n)]
    u = [mm_nn(Tinv[h], vb[h]) for h in range(n)]
    w = [mm_nn(Tinv[h], kb[h] * eg[h]) for h in range(n)]
    qd = [qs[h] * eg[h] for h in range(n)]
    gl = [gc[h][C - 1:C, :] for h in range(n)]
    kd = [k[h] * jnp.exp(gl[h] - gc[h]) for h in range(n)]
    return (u, w, attn, qd, kd, gl, Tinv) if with_inverse else (u, w, attn, qd, kd, gl)


def dn_step(u, w, a, qd, kd, gl, S):
    n = len(u)
    v_new = [u[h] - mm_nn(w[h], S[h]) for h in range(n)]
    o = [mm_nn(qd[h], S[h]) + mm_nn(a[h], v_new[h]) for h in range(n)]
    S_new = [S[h] * jnp.exp(gl[h]) + mm_tn(kd[h], v_new[h]) for h in range(n)]
    return o, S_new


def dn_post(o, z, g):
    H = o.shape[0]
    return jnp.concatenate([rms(o[h], g) * silu(z[:, h * HEAD_DIM:(h + 1) * HEAD_DIM]) for h in range(H)], axis=1)


def gmlp(u_raw, v_raw, ln_g, ln_b, sgw, sgbT):
    R = u_raw.shape[0]
    G = sgw.shape[0]
    nc = R // SG_CHUNK
    u = gelu(u_raw)
    vv = gelu(v_raw)
    xc = vv - jnp.mean(vv, axis=-1, keepdims=True)
    vg = xc * lax.rsqrt(jnp.mean(xc * xc, axis=-1, keepdims=True) + EPS) * ln_g + ln_b
    ri = lax.broadcasted_iota(jnp.int32, (SG_CHUNK, SG_CHUNK), 0)
    ci = lax.broadcasted_iota(jnp.int32, (SG_CHUNK, SG_CHUNK), 1)
    cols = []
    for g in range(G):
        ws = jnp.where(ri >= ci, sgw[g], 0.0)
        rhs = jnp.concatenate([vg[c * SG_CHUNK:(c + 1) * SG_CHUNK, g * HEAD_DIM:(g + 1) * HEAD_DIM] for c in range(nc)], axis=1)
        mixed = mm_nn(ws, rhs) + sgbT[:, g:g + 1]
        cols.append(jnp.concatenate([mixed[:, c * HEAD_DIM:(c + 1) * HEAD_DIM] for c in range(nc)], axis=0))
    return u * jnp.concatenate(cols, axis=1)


def merge(ga, gb, ap, bp):
    return sigmoid(ga) * ap + sigmoid(gb) * bp


def norm_fwd(x, g, name, tm=256):
    T, D = x.shape
    tl = Tiles(T, _pick(T, tm))
    (h,) = rowcall(name, lambda i, x, g: (rms(x, g),), [x, g], [tl.row(D), full((1, D))],
                   [sds((T, D), BF16)], [tl.row(D)], [False], tl.n)
    return h


def norm_bwd(x, g, dh, dres, name, tm=256):
    T, D = x.shape
    tl = Tiles(T, _pick(T, tm))

    def fn(i, x, g, dh, dres):
        _, vj = jax.vjp(rms, x, g)
        dx, dg = vj(dh.astype(F32))
        dx = dx + dres
        return dx, dx, dg

    return rowcall(name, fn, [x, g, dh, dres], [tl.row(D), full((1, D)), tl.row(D), tl.row(D)],
                   [sds((T, D)), sds((T, D), BF16), sds((1, D))], [tl.row(D), tl.row(D), full((1, D))],
                   [False, False, True], tl.n)


def head_fwd_bwd(x, g, tgt, name, tm=256):
    T, D = x.shape
    tl = Tiles(T, _pick(T, tm))

    def fn(i, x, g, tgt):
        y, vj = jax.vjp(rms, x, g)
        e = y - tgt
        loss = 0.5 * jnp.sum(jnp.mean(e * e, axis=-1, keepdims=True), axis=0, keepdims=True)
        dx, dg = vj(e * (1.0 / D))
        return loss, dx, dx, dg

    return rowcall(name, fn, [x, g, tgt], [tl.row(D), full((1, D)), tl.row(D)],
                   [sds((1, 1)), sds((T, D)), sds((T, D), BF16), sds((1, D))],
                   [full((1, 1)), tl.row(D), tl.row(D), full((1, D))], [True, False, False, True], tl.n)


def dn_prep_fwd(projA, pba, cw8, alog, dtb, H, name, tm=256):
    T = projA.shape[0]
    W3 = 3 * H * HEAD_DIM
    tl = Tiles(T, _pick(T, tm, DN_CHUNK))
    R = tl.tm

    def fn(i, xp, x, ba, cw, alog, dtb, win):
        fill_window(win, i, None, xp, x)
        q, k, v = dn_qkv(causal_conv(conv_taps(win, 4, R), cw), H)
        return jnp.stack(q), jnp.stack(k), jnp.stack(v), dn_gates(ba, alog, dtb, H, R)

    hs = sds((H, T, HEAD_DIM))
    return rowcall(name, fn, [projA, projA, pba, cw8, alog, dtb],
                   [tl.prev(W3), tl.row(W3), tl.row(LANES), full((SUBLANES, W3)), full((1, LANES)), full((1, LANES))],
                   [hs, hs, hs, sds((T, LANES))], [tl.heads(H)] * 3 + [tl.row(LANES)], [False] * 4, tl.n,
                   scratch=[pltpu.VMEM((SUBLANES + R, W3), F32)])


def dn_prep_bwd(projA, pba, cw8, alog, dtb, dq, dk, dv, dbg, H, name, tm=256):
    T = projA.shape[0]
    W3 = 3 * H * HEAD_DIM
    tl = Tiles(T, _pick(T, tm, DN_CHUNK))
    R = tl.tm
    RE = R + SUBLANES

    def fn(i, xp, x, xn, ba, cw, alog, dtb, dq, dk, dv, dqn, dkn, dvn, dbg, win, dp):
        last = i == tl.n - 1
        fill_window(win, i, last, xp, x, xn)
        taps = conv_taps(win, 4, RE)
        pre = causal_conv(taps, cw)
        ext = lambda d, dn: [jnp.concatenate([d[h], jnp.where(last, 0.0, dn[h])], axis=0) for h in range(H)]
        _, vj = jax.vjp(lambda p: dn_qkv(p, H), pre)
        (dpre,) = vj((ext(dq, dqn), ext(dk, dkn), ext(dv, dvn)))
        dp[...] = dpre
        dx = cw[3:4, :] * dpre[0:R, :]
        for j in range(3):
            dx = dx + cw[j:j + 1, :] * dp[pl.ds(3 - j, R), :]
        dcw = rows_to8([jnp.sum(dpre[0:R, :] * taps[j][0:R, :], axis=0, keepdims=True) for j in range(4)], W3)
        _, vjg = jax.vjp(lambda ba, alog, dtb: dn_gates(ba, alog, dtb, H, R), ba, alog, dtb)
        dba, dalog, ddtb = vjg(dbg)
        return dx, dba, dcw, dalog, ddtb

    return rowcall(name, fn, [projA, projA, projA, pba, cw8, alog, dtb, dq, dk, dv, dq, dk, dv, dbg],
                   [tl.prev(W3), tl.row(W3), tl.nxt(W3), tl.row(LANES), full((SUBLANES, W3)), full((1, LANES)), full((1, LANES))]
                   + [tl.heads(H)] * 3 + [tl.heads_nxt(H)] * 3 + [tl.row(LANES)],
                   [sds((T, W3), BF16), sds((T, LANES), BF16), sds((SUBLANES, W3)), sds((1, LANES)), sds((1, LANES))],
                   [tl.row(W3), tl.row(LANES), full((SUBLANES, W3)), full((1, LANES)), full((1, LANES))],
                   [False, False, True, True, True], tl.n,
                   scratch=[pltpu.VMEM((2 * SUBLANES + R, W3), F32), pltpu.VMEM((RE, W3), F32)])


def _scan_chunks_per_step(N, want=4):
    while N % want:
        want //= 2
    return want


def _multi_chunk_specs(H, C, P):
    hs = pl.BlockSpec((H, P * C, HEAD_DIM), lambda n: (0, n, 0))
    at = pl.BlockSpec((H, P * C, C), lambda n: (0, n, 0))
    one = pl.BlockSpec((H, P, 1, 1), lambda n: (0, n, 0, 0))
    gate = pl.BlockSpec((P * C, LANES), lambda n: (n, 0))
    return hs, at, one, gate


def dn_chunk_fwd(q, k, v, bg, name):
    H, T, _ = q.shape
    C = DN_CHUNK
    N = T // C
    P = _scan_chunks_per_step(N, 4)
    hs, at, one, gate = _multi_chunk_specs(H, C, P)

    def body(q, k, v, bg, u, w, a, qd, kd, gl, ti):
        pr = [(c, h) for c in range(P) for h in range(H)]
        rows = lambda c: slice(c * C, (c + 1) * C)
        cols = [gate_columns(bg[rows(c), :], H) for c in range(P)]
        res = dn_chunk([q[h, rows(c)] for c, h in pr], [k[h, rows(c)] for c, h in pr], [v[h, rows(c)] for c, h in pr],
                       [cols[c][0][h] for c, h in pr], [cols[c][1][h] for c, h in pr], [cols[c][2][h] for c, h in pr],
                       with_inverse=True)
        for i, (c, h) in enumerate(pr):
            for ref, val in zip((u, w, a, qd, kd), res[:5]):
                ref[h, rows(c)] = val[i]
            gl[h, c] = res[5][i]
            ti[h, rows(c)] = res[6][i]

    big = sds((H, T, HEAD_DIM))
    return pl.pallas_call(
        body, name=name, grid=(N // P,), in_specs=[hs, hs, hs, gate], out_specs=[hs, hs, at, hs, hs, one, at],
        out_shape=[big, big, sds((H, T, C)), big, big, sds((H, N, 1, 1)), sds((H, T, C))],
        compiler_params=_params(("parallel",)),
    )(q, k, v, bg)


def dn_chunk_bwd(q, k, v, bg, tinv, du, dw, da, dqd, dkd, dgl, name):
    H, T, _ = q.shape
    C = DN_CHUNK
    N = T // C
    P = _scan_chunks_per_step(N, 2)
    hs, at, one, gate = _multi_chunk_specs(H, C, P)

    def body(q, k, v, bg, ti, du, dw, da, dqd, dkd, dgl, dq, dk, dv, dbg):
        pr = [(c, h) for c in range(P) for h in range(H)]
        rws = lambda c: slice(c * C, (c + 1) * C)
        gcols = [gate_columns(bg[rws(c), :], H) for c in range(P)]
        f = lambda q, k, v, b, gc, gr: dn_chunk(q, k, v, b, gc, gr, tinv=[ti[h, rws(c)] for c, h in pr])
        _, vj = jax.vjp(f, [q[h, rws(c)] for c, h in pr], [k[h, rws(c)] for c, h in pr], [v[h, rws(c)] for c, h in pr],
                        [gcols[c][0][h] for c, h in pr], [gcols[c][1][h] for c, h in pr], [gcols[c][2][h] for c, h in pr])
        res = vj(([du[h, rws(c)] for c, h in pr], [dw[h, rws(c)] for c, h in pr], [da[h, rws(c)] for c, h in pr],
                  [dqd[h, rws(c)] for c, h in pr], [dkd[h, rws(c)] for c, h in pr], [dgl[h, c] for c, h in pr]))
        lane = lax.broadcasted_iota(jnp.int32, (C, LANES), 1)
        row = lax.broadcasted_iota(jnp.int32, (LANES, C), 0)
        for c in range(P):
            cols = jnp.zeros((C, LANES), F32)
            rows = jnp.zeros((LANES, C), F32)
            for h in range(H):
                i = c * H + h
                dq[h, rws(c)], dk[h, rws(c)], dv[h, rws(c)] = res[0][i], res[1][i], res[2][i]
                cols = cols + jnp.where(lane == h, res[3][i], 0.0) + jnp.where(lane == H + h, res[4][i], 0.0)
                rows = rows + jnp.where(row == H + h, res[5][i], 0.0)
            dbg[rws(c), :] = cols + rows.T

    big = sds((H, T, HEAD_DIM))
    return pl.pallas_call(
        body, name=name, grid=(N // P,), in_specs=[hs, hs, hs, gate, at, hs, hs, at, hs, hs, one],
        out_specs=[hs, hs, hs, gate], out_shape=[big, big, big, sds((T, LANES))], compiler_params=_params(("parallel",)),
    )(q, k, v, bg, tinv, du, dw, da, dqd, dkd, dgl)


def dn_scan_fwd(u, w, a, qd, kd, gl, name):
    H, T, _ = u.shape
    C = DN_CHUNK
    N = T // C
    P = _scan_chunks_per_step(N)
    hs = pl.BlockSpec((H, P * C, HEAD_DIM), lambda n: (0, n, 0))
    at = pl.BlockSpec((H, P * C, C), lambda n: (0, n, 0))
    one = pl.BlockSpec((H, P, 1, 1), lambda n: (0, n, 0, 0))
    st = pl.BlockSpec((P, H, HEAD_DIM, HEAD_DIM), lambda n: (n, 0, 0, 0))

    def body(u, w, a, qd, kd, gl, o, s_in, S):
        @pl.when(pl.program_id(0) == 0)
        def _():
            S[...] = jnp.zeros_like(S)

        hd = range(H)
        s = [S[h] for h in hd]
        for c in range(P):
            rows = slice(c * C, (c + 1) * C)
            o_new, s_new = dn_step([u[h, rows] for h in hd], [w[h, rows] for h in hd], [a[h, rows] for h in hd],
                                   [qd[h, rows] for h in hd], [kd[h, rows] for h in hd], [gl[h, c] for h in hd], s)
            for h in hd:
                s_in[c, h] = s[h]
                o[h, rows] = o_new[h]
            s = s_new
        for h in hd:
            S[h] = s[h]

    return pl.pallas_call(
        body, name=name, grid=(N // P,), in_specs=[hs, hs, at, hs, hs, one], out_specs=[hs, st],
        out_shape=[sds((H, T, HEAD_DIM)), sds((N, H, HEAD_DIM, HEAD_DIM))],
        scratch_shapes=[pltpu.VMEM((H, HEAD_DIM, HEAD_DIM), F32)], compiler_params=_params(("arbitrary",)),
    )(u, w, a, qd, kd, gl)


def dn_scan_bwd(u, w, a, qd, kd, gl, s_in, do, name):
    H, T, _ = u.shape
    C = DN_CHUNK
    N = T // C
    P = _scan_chunks_per_step(N)
    nb = N // P
    hs = pl.BlockSpec((H, P * C, HEAD_DIM), lambda n: (0, nb - 1 - n, 0))
    at = pl.BlockSpec((H, P * C, C), lambda n: (0, nb - 1 - n, 0))
    one = pl.BlockSpec((H, P, 1, 1), lambda n: (0, nb - 1 - n, 0, 0))
    st = pl.BlockSpec((P, H, HEAD_DIM, HEAD_DIM), lambda n: (nb - 1 - n, 0, 0, 0))

    def body(u, w, a, qd, kd, gl, s_in, do, du, dw, da, dqd, dkd, dgl, dS):
        @pl.when(pl.program_id(0) == 0)
        def _():
            dS[...] = jnp.zeros_like(dS)

        hd = range(H)
        ds = [dS[h] for h in hd]
        for c in reversed(range(P)):
            rows = slice(c * C, (c + 1) * C)
            _, vj = jax.vjp(dn_step, [u[h, rows] for h in hd], [w[h, rows] for h in hd], [a[h, rows] for h in hd],
                            [qd[h, rows] for h in hd], [kd[h, rows] for h in hd], [gl[h, c] for h in hd],
                            [s_in[c, h] for h in hd])
            res = vj(([do[h, rows] for h in hd], ds))
            for h in hd:
                du[h, rows], dw[h, rows], da[h, rows], dqd[h, rows], dkd[h, rows] = (res[j][h] for j in range(5))
                dgl[h, c] = res[5][h]
            ds = res[6]
        for h in hd:
            dS[h] = ds[h]

    big = sds((H, T, HEAD_DIM))
    return pl.pallas_call(
        body, name=name, grid=(nb,), in_specs=[hs, hs, at, hs, hs, one, st, hs], out_specs=[hs, hs, at, hs, hs, one],
        out_shape=[big, big, sds((H, T, C)), big, big, sds((H, N, 1, 1))],
        scratch_shapes=[pltpu.VMEM((H, HEAD_DIM, HEAD_DIM), F32)], compiler_params=_params(("arbitrary",)),
    )(u, w, a, qd, kd, gl, s_in, do)


def dn_post_fwd(o, projA, g, name, tm=256):
    H, T, _ = o.shape
    W = H * HEAD_DIM
    tl = Tiles(T, _pick(T, tm))
    (y,) = rowcall(name, lambda i, o, z, g: (dn_post(o, z, g),), [o, projA, g], [tl.heads(H), tl.row(W, 3), full((1, HEAD_DIM))],
                   [sds((T, W), BF16)], [tl.row(W)], [False], tl.n)
    return y


def dn_post_bwd(o, projA, g, dy, name, tm=256):
    H, T, _ = o.shape
    W = H * HEAD_DIM
    tl = Tiles(T, _pick(T, tm))

    def fn(i, o, z, g, dy):
        _, vj = jax.vjp(dn_post, o, z, g)
        return vj(dy.astype(F32))

    return rowcall(name, fn, [o, projA, g, dy], [tl.heads(H), tl.row(W, 3), full((1, HEAD_DIM)), tl.row(W)],
                   [sds((H, T, HEAD_DIM)), sds((T, W), BF16), sds((1, HEAD_DIM))],
                   [tl.heads(H), tl.row(W), full((1, HEAD_DIM))], [False, False, True], tl.n)


def gmlp_fwd(projB, ln_g, ln_b, sgw, sgbT, name, tm=512):
    T = projB.shape[0]
    G = sgw.shape[0]
    W = G * HEAD_DIM
    tl = Tiles(T, _pick(T, tm))
    (y,) = rowcall(name, lambda i, *a: (gmlp(*a),), [projB, projB, ln_g, ln_b, sgw, sgbT],
                   [tl.row(W, 0), tl.row(W, 1), full((1, W)), full((1, W)), full(sgw.shape), full(sgbT.shape)],
                   [sds((T, W), BF16)], [tl.row(W)], [False], tl.n)
    return y


def gmlp_bwd(projB, ln_g, ln_b, sgw, sgbT, dy, name, tm=512):
    T = projB.shape[0]
    G = sgw.shape[0]
    W = G * HEAD_DIM
    tl = Tiles(T, _pick(T, tm))

    def fn(i, u_raw, v_raw, ln_g, ln_b, sgw, sgbT, dy):
        _, vj = jax.vjp(gmlp, u_raw, v_raw, ln_g, ln_b, sgw, sgbT)
        return vj(dy.astype(F32))

    return rowcall(name, fn, [projB, projB, ln_g, ln_b, sgw, sgbT, dy],
                   [tl.row(W, 0), tl.row(W, 1), full((1, W)), full((1, W)), full(sgw.shape), full(sgbT.shape), tl.row(W)],
                   [sds((T, W), BF16), sds((T, W), BF16), sds((1, W)), sds((1, W)), sds(sgw.shape), sds(sgbT.shape)],
                   [tl.row(W), tl.row(W), full((1, W)), full((1, W)), full(sgw.shape), full(sgbT.shape)],
                   [False, False, True, True, True, True], tl.n)


def merge_fwd(projB, ap, bp, cb_a, name, tm=256):
    T, D = ap.shape
    tl = Tiles(T, _pick(T, tm))
    (m,) = rowcall(name, lambda i, *a: (merge(*a),), [projB, projB, ap, bp],
                   [tl.row(D, cb_a), tl.row(D, cb_a + 1), tl.row(D), tl.row(D)], [sds((T, D), BF16)], [tl.row(D)], [False], tl.n)
    return m


def merge_bwd(projB, ap, bp, dm, cb_a, name, tm=256):
    T, D = ap.shape
    tl = Tiles(T, _pick(T, tm))

    def fn(i, ga, gb, ap, bp, dm):
        _, vj = jax.vjp(merge, ga, gb, ap, bp)
        return vj(dm.astype(F32))

    return rowcall(name, fn, [projB, projB, ap, bp, dm], [tl.row(D, cb_a), tl.row(D, cb_a + 1), tl.row(D), tl.row(D), tl.row(D)],
                   [sds((T, D), BF16)] * 4, [tl.row(D)] * 4, [False] * 4, tl.n)


def ffn_act_fwd(gp, up, fcw8, fcb, name, tm=256, tc=512):
    T, F = gp.shape
    tl = Tiles(T, _pick(T, tm))
    tc = _pick(F, tc)
    R = tl.tm

    def fn(i, gprev, g, up, cw, cb):
        xwin = jnp.concatenate([jnp.where(i > 0, gprev, 0.0), g], axis=0)
        return (silu(causal_conv(value_taps(xwin, 3, R), cw) + cb) * up,)

    (act,) = rowcall(name, fn, [gp, gp, up, fcw8, fcb], [tl.prevj(tc), tl.rowj(tc), tl.rowj(tc), constj(SUBLANES, tc), constj(1, tc)],
                     [sds((T, F), BF16)], [tl.rowj(tc)], [False], tl.n, F // tc)
    return act


def ffn_act_bwd(gp, up, fcw8, fcb, dact, name, tm=256, tc=512):
    T, F = gp.shape
    tl = Tiles(T, _pick(T, tm))
    tc = _pick(F, tc)
    R = tl.tm
    RE = R + SUBLANES

    def fn(i, gprev, g, gnext, up, upn, da, dan, cw, cb):
        last = i == tl.n - 1
        xwin = jnp.concatenate([jnp.where(i > 0, gprev, 0.0), g, jnp.where(last, 0.0, gnext)], axis=0)
        taps = value_taps(xwin, 3, RE)
        gate = causal_conv(taps, cw) + cb
        upe = jnp.concatenate([up, upn], axis=0)
        dae = jnp.concatenate([da, jnp.where(last, 0.0, dan)], axis=0)
        s = sigmoid(gate)
        dgate = dae * upe * (s * (1.0 + gate * (1.0 - s)))
        dup = da * (gate[0:R, :] * s[0:R, :])
        dgp = cw[0:1, :] * dgate[2:2 + R, :] + cw[1:2, :] * dgate[1:1 + R, :] + cw[2:3, :] * dgate[0:R, :]
        dcw = rows_to8([jnp.sum(dgate[0:R, :] * taps[j][0:R, :], axis=0, keepdims=True) for j in range(3)], tc)
        dcb = jnp.sum(dgate[0:R, :], axis=0, keepdims=True)
        return dgp, dup, dcw, dcb

    return rowcall(name, fn, [gp, gp, gp, up, up, dact, dact, fcw8, fcb],
                   [tl.prevj(tc), tl.rowj(tc), tl.nxtj(tc), tl.rowj(tc), tl.nxtj(tc), tl.rowj(tc), tl.nxtj(tc),
                    constj(SUBLANES, tc), constj(1, tc)],
                   [sds((T, F), BF16), sds((T, F), BF16), sds((SUBLANES, F)), sds((1, F))],
                   [tl.rowj(tc), tl.rowj(tc), constj(SUBLANES, tc), constj(1, tc)], [False, False, True, True], tl.n, F // tc)


def _me():
    return lax.axis_index("x"), lax.axis_index("y"), lax.axis_index("c")


def all_gather(shards, name):
    nt = len(shards)

    def body(*refs):
        xs, outs = refs[:nt], refs[nt:2 * nt]
        send_sems, recv_sems, local_sems = refs[2 * nt:]
        x, y, c = _me()
        me, sibling = (x, y, c), (x, y, 1 - c)
        chips = [(1 - x, y), (x, 1 - y), (1 - x, 1 - y)]

        def slot(t, p):
            return outs[t].at[4 * p[0] + 2 * p[1] + p[2]]

        def copy(t, k, block, to, src=None):
            return pltpu.make_async_remote_copy(
                src_ref=slot(t, block) if src is None else src, dst_ref=slot(t, block),
                send_sem=send_sems.at[t, k], recv_sem=recv_sems.at[t, k], device_id=to, device_id_type=MESH)

        mine = [pltpu.make_async_copy(xs[t], slot(t, me), local_sems.at[t]) for t in range(nt)]
        first = []
        for t in range(nt):
            mine[t].start()
            first.append(copy(t, 0, me, sibling, src=xs[t]))
            first += [copy(t, 1 + j, me, (*chip, c), src=xs[t]) for j, chip in enumerate(chips)]
        for cp in first:
            cp.start()
        passed = []
        for j, chip in enumerate(chips):
            for t in range(nt):
                copy(t, 1 + j, (*chip, c), me).wait_recv()
                cp = copy(t, 4 + j, (*chip, c), sibling)
                cp.start()
                passed.append(cp)
        for t in range(nt):
            copy(t, 0, sibling, me).wait_recv()
            for j, chip in enumerate(chips):
                copy(t, 4 + j, (*chip, 1 - c), me).wait_recv()
        for cp in first + passed:
            cp.wait_send()
        for t in range(nt):
            mine[t].wait()

    any_spec = pl.BlockSpec(memory_space=pl.ANY)
    return pl.pallas_call(
        body, name=name, in_specs=[any_spec] * nt, out_specs=[any_spec] * nt,
        out_shape=[jax.ShapeDtypeStruct((N_DEV,) + s.shape, s.dtype) for s in shards],
        scratch_shapes=[pltpu.SemaphoreType.DMA((nt, 7)), pltpu.SemaphoreType.DMA((nt, 7)), pltpu.SemaphoreType.DMA((nt,))],
    )(*shards)


_HBM = pl.BlockSpec(memory_space=pltpu.HBM)
_SEM = pl.BlockSpec(memory_space=pltpu.SEMAPHORE)
_ANY = pl.BlockSpec(memory_space=pl.ANY)
_DATAFLOW = pltpu.SideEffectType.DATAFLOW_SIDE_EFFECTING


def _peers():
    x, y, c = _me()
    out = []
    for k in range(1, N_DEV):
        p = (x ^ (k >> 2), y ^ ((k >> 1) & 1), c ^ (k & 1))
        out.append((k, p, 4 * p[0] + 2 * p[1] + p[2]))
    return out


def _split_copy(src, land, send_sems, recv_sems, t, k, peer, slot, my, scatter, receiving):
    return pltpu.make_async_remote_copy(
        src_ref=src.at[slot] if scatter else land.at[my], dst_ref=land.at[slot if receiving else my],
        send_sem=send_sems.at[t * (N_DEV - 1) + k - 1], recv_sem=recv_sems.at[t * (N_DEV - 1) + k - 1],
        device_id=peer, device_id_type=MESH)


def comm_start(groups, scatter, name, after=None):
    flat = [a for g in groups for a in g]
    nt = len(flat)
    if scatter:
        lands = [lax.empty(a.shape, a.dtype) for a in flat]
    else:
        me = 4 * lax.axis_index("x") + 2 * lax.axis_index("y") + lax.axis_index("c")
        lands = [lax.dynamic_update_index_in_dim(lax.empty((N_DEV,) + a.shape, a.dtype), a, me, 0) for a in flat]
    ng = len(groups)
    n_after = 0 if after is None else 1

    def body(*refs):
        src, land = refs[:nt], refs[nt:2 * nt]
        sems = refs[2 * nt + n_after:2 * nt + n_after + 2 * ng]
        token = refs[-1]
        x, y, c = _me()
        my = 4 * x + 2 * y + c
        t0 = 0
        for gi, g in enumerate(groups):
            for k, peer, slot in _peers():
                for t in range(len(g)):
                    _split_copy(src[t0 + t], land[t0 + t], sems[2 * gi], sems[2 * gi + 1], t, k, peer, slot, my, scatter,
                                False).start()
            t0 += len(g)
        token[...] = jnp.zeros_like(token)

    sem_shapes = []
    for g in groups:
        sem_shapes += [pltpu.SemaphoreType.DMA((len(g) * (N_DEV - 1),))] * 2
    res = pl.pallas_call(
        body, name=name, in_specs=[_HBM] * (2 * nt) + [_HBM] * n_after,
        out_specs=[_SEM] * (2 * ng) + [_HBM] * (2 * nt) + [pl.BlockSpec(memory_space=pltpu.VMEM)],
        out_shape=sem_shapes + [pltpu.HBM(a.shape, a.dtype) for a in flat + lands] + [sds((SUBLANES, LANES))],
        input_output_aliases={i: 2 * ng + i for i in range(2 * nt)},
        compiler_params=pltpu.CompilerParams(has_side_effects=_DATAFLOW),
    )(*[pltpu.with_memory_space_constraint(a, pltpu.HBM) for a in flat + lands + ([] if after is None else [after])])
    handles = []
    t0 = 0
    for gi, g in enumerate(groups):
        n = len(g)
        handles.append(dict(sems=(res[2 * gi], res[2 * gi + 1]), src=res[2 * ng + t0:2 * ng + t0 + n],
                            land=res[2 * ng + nt + t0:2 * ng + nt + t0 + n], scatter=scatter))
        t0 += n
    return handles, res[-1]


def comm_wait(handle, after, name):
    src, land, scatter = handle["src"], handle["land"], handle["scatter"]
    nt = len(src)

    def body(*refs):
        src_r, land_r = refs[:nt], refs[nt:2 * nt]
        send_sems, recv_sems = refs[2 * nt], refs[2 * nt + 1]
        x, y, c = _me()
        my = 4 * x + 2 * y + c
        for k, peer, slot in _peers():
            for t in range(nt):
                _split_copy(src_r[t], land_r[t], send_sems, recv_sems, t, k, peer, slot, my, scatter, False).wait_send()
                _split_copy(src_r[t], land_r[t], send_sems, recv_sems, t, k, peer, slot, my, scatter, True).wait_recv()

    after = list(after) if isinstance(after, (list, tuple)) else [after]
    res = pl.pallas_call(
        body, name=name, in_specs=[_HBM] * (2 * nt) + [_SEM, _SEM] + [_HBM] * len(after), out_specs=[_HBM] * (2 * nt),
        out_shape=[pltpu.HBM(a.shape, a.dtype) for a in list(src) + list(land)],
        input_output_aliases={i: i for i in range(2 * nt)},
        compiler_params=pltpu.CompilerParams(has_side_effects=_DATAFLOW),
    )(*src, *land, *handle["sems"], *[pltpu.with_memory_space_constraint(a, pltpu.HBM) for a in after])
    return res[:nt], res[nt:]


def sum_adamw_shard(own_src, land, me, w, m, v, l, prev, name, tr=256):
    L, R, C = w.shape
    by_rows = R % SUBLANES == 0 or C % LANES != 0
    tr, tc = (_pick(R, tr, SUBLANES), C) if by_rows else (R, _pick(C, 256))
    steps = R // tr if by_rows else C // tc
    c1 = 1.0 - ADAM_B1 ** ADAM_STEP
    c2 = 1.0 - ADAM_B2 ** ADAM_STEP
    n_prev = 0 if prev is None else 4

    def at(lead, i):
        return (lead, i, 0) if by_rows else (lead, 0, i)

    def body(me_ref, *refs):
        parts = refs[:N_DEV]
        w_r, m_r, v_r = refs[N_DEV:N_DEV + 3]
        g_o, d_o, m_o, v_o = refs[N_DEV + 3 + n_prev:]
        g = parts[0][0].astype(F32)
        for k in range(1, N_DEV):
            g = g + parts[k][0].astype(F32)
        mn = ADAM_B1 * m_r[0] + (1.0 - ADAM_B1) * g
        vn = ADAM_B2 * v_r[0] + (1.0 - ADAM_B2) * (g * g)
        g_o[0] = g
        d_o[0] = -ADAM_LR * ((mn / c1) / (jnp.sqrt(vn / c2) + ADAM_EPS) + ADAM_WD * w_r[0])
        m_o[0] = mn
        v_o[0] = vn

    part_specs = [pl.BlockSpec((1, tr, tc), lambda i, me, k=k: at(me[0] ^ k, i)) for k in range(N_DEV)]
    lay = pl.BlockSpec((1, tr, tc), lambda i, me: at(l, i))
    grid_spec = pltpu.PrefetchScalarGridSpec(
        num_scalar_prefetch=1, grid=(steps,), in_specs=part_specs + [lay] * 3 + [_ANY] * n_prev, out_specs=[lay] * 4)
    return pl.pallas_call(
        body, name=name, grid_spec=grid_spec, out_shape=[sds((L, R, C))] * 4,
        input_output_aliases={1 + N_DEV + 3 + j: j for j in range(n_prev)}, compiler_params=_params(("parallel",)),
    )(me, own_src, *[land] * (N_DEV - 1), w, m, v, *([] if prev is None else prev))


def sum_adamw(parts, w, m, v, name, tr=256):
    _, R, C = parts.shape
    tr = _pick(R, tr, SUBLANES)
    c1 = 1.0 - ADAM_B1 ** ADAM_STEP
    c2 = 1.0 - ADAM_B2 ** ADAM_STEP

    def body(p, w, m, v, g_o, d_o, m_o, v_o):
        g = p[0].astype(F32)
        for d in range(1, N_DEV):
            g = g + p[d].astype(F32)
        mn = ADAM_B1 * m[...] + (1.0 - ADAM_B1) * g
        vn = ADAM_B2 * v[...] + (1.0 - ADAM_B2) * (g * g)
        m_hat = mn / c1
        v_hat = vn / c2
        g_o[...] = g
        d_o[...] = -ADAM_LR * (m_hat / (jnp.sqrt(v_hat) + ADAM_EPS) + ADAM_WD * w[...])
        m_o[...] = mn
        v_o[...] = vn

    blk = pl.BlockSpec((tr, C), lambda i: (i, 0))
    return pl.pallas_call(
        body, name=name, grid=(R // tr,), in_specs=[pl.BlockSpec((N_DEV, tr, C), lambda i: (0, i, 0)), blk, blk, blk],
        out_specs=[blk] * 4, out_shape=[sds((R, C))] * 4, compiler_params=_params(("parallel",)),
    )(parts, w, m, v)


SHARDED = ("w_in", "dn_conv_w", "w_branch_a", "w_branch_b", "w_out", "ffn_w_gate", "ffn_w_up", "ffn_conv_w", "ffn_w_down")
TRANSPOSED = ("w_in", "ffn_w_gate", "ffn_w_up")
COL_SHARDED = ("dn_conv_w", "w_branch_a", "w_branch_b", "ffn_conv_w")
CONV_WEIGHTS = ("dn_conv_w", "ffn_conv_w")
REPLICATED = ("norm1_g", "dn_a_log", "dn_dt_bias", "dn_onorm_g", "sg_ln_g", "sg_ln_b", "sg_w", "sg_b", "norm2_g",
              "ffn_conv_b", "final_norm_g")
WEIGHTS = ("norm1_g", "w_in", "dn_conv_w", "dn_a_log", "dn_dt_bias", "dn_onorm_g", "sg_ln_g", "sg_ln_b", "sg_w", "sg_b",
           "w_branch_a", "w_branch_b", "w_out", "norm2_g", "ffn_w_gate", "ffn_w_up", "ffn_conv_w", "ffn_conv_b",
           "ffn_w_down", "final_norm_g")


def _columns(pieces, lo, hi):
    out = []
    for a, start, width in pieces:
        s, e = max(lo, start), min(hi, start + width)
        if s < e:
            out.append(a[:, s - start:e - start])
    return out[0] if len(out) == 1 else jnp.concatenate(out, axis=1)


def _assemble(name, g):
    if name in COL_SHARDED:
        return jnp.concatenate([g[d] for d in range(N_DEV)], axis=1)
    return g.reshape(N_DEV * g.shape[1], g.shape[2])


def _split(name, pieces, dtype):
    total = sum(w for _, _, w in pieces)
    if name in COL_SHARDED:
        cs = total // N_DEV
        return jnp.stack([_columns(pieces, d * cs, (d + 1) * cs).astype(dtype) for d in range(N_DEV)])
    a = pieces[0][0] if len(pieces) == 1 else jnp.concatenate([p[:w] for p, _, w in pieces], axis=0)
    return a.reshape(N_DEV, a.shape[0] // N_DEV, a.shape[1]).astype(dtype)


def _pad_lanes(a, lo, width=LANES):
    return jnp.pad(a, ((0, 0), (lo, width - lo - a.shape[1])))


def _pad_rows(a, rows=SUBLANES):
    return jnp.pad(a, ((0, rows - a.shape[0]), (0, 0)))


def kernel(x, norm1_g, w_in, dn_conv_w, dn_a_log, dn_dt_bias, dn_onorm_g, sg_ln_g, sg_ln_b, sg_w, sg_b, w_branch_a, w_branch_b, w_out, norm2_g, ffn_w_gate, ffn_w_up, ffn_conv_w, ffn_conv_b, ffn_w_down, final_norm_g, loss_target, m_norm1_g, m_w_in, m_dn_conv_w, m_dn_a_log, m_dn_dt_bias, m_dn_onorm_g, m_sg_ln_g, m_sg_ln_b, m_sg_w, m_sg_b, m_w_branch_a, m_w_branch_b, m_w_out, m_norm2_g, m_ffn_w_gate, m_ffn_w_up, m_ffn_conv_w, m_ffn_conv_b, m_ffn_w_down, m_final_norm_g, v_norm1_g, v_w_in, v_dn_conv_w, v_dn_a_log, v_dn_dt_bias, v_dn_onorm_g, v_sg_ln_g, v_sg_ln_b, v_sg_w, v_sg_b, v_w_branch_a, v_w_branch_b, v_w_out, v_norm2_g, v_ffn_w_gate, v_ffn_w_up, v_ffn_conv_w, v_ffn_conv_b, v_ffn_w_down, v_final_norm_g):
    W = dict(norm1_g=norm1_g, w_in=w_in, dn_conv_w=dn_conv_w, dn_a_log=dn_a_log, dn_dt_bias=dn_dt_bias, dn_onorm_g=dn_onorm_g,
             sg_ln_g=sg_ln_g, sg_ln_b=sg_ln_b, sg_w=sg_w, sg_b=sg_b, w_branch_a=w_branch_a, w_branch_b=w_branch_b, w_out=w_out,
             norm2_g=norm2_g, ffn_w_gate=ffn_w_gate, ffn_w_up=ffn_w_up, ffn_conv_w=ffn_conv_w, ffn_conv_b=ffn_conv_b,
             ffn_w_down=ffn_w_down, final_norm_g=final_norm_g)
    Mo = dict(norm1_g=m_norm1_g, w_in=m_w_in, dn_conv_w=m_dn_conv_w, dn_a_log=m_dn_a_log, dn_dt_bias=m_dn_dt_bias,
              dn_onorm_g=m_dn_onorm_g, sg_ln_g=m_sg_ln_g, sg_ln_b=m_sg_ln_b, sg_w=m_sg_w, sg_b=m_sg_b, w_branch_a=m_w_branch_a,
              w_branch_b=m_w_branch_b, w_out=m_w_out, norm2_g=m_norm2_g, ffn_w_gate=m_ffn_w_gate, ffn_w_up=m_ffn_w_up,
              ffn_conv_w=m_ffn_conv_w, ffn_conv_b=m_ffn_conv_b, ffn_w_down=m_ffn_w_down, final_norm_g=m_final_norm_g)
    Vo = dict(norm1_g=v_norm1_g, w_in=v_w_in, dn_conv_w=v_dn_conv_w, dn_a_log=v_dn_a_log, dn_dt_bias=v_dn_dt_bias,
              dn_onorm_g=v_dn_onorm_g, sg_ln_g=v_sg_ln_g, sg_ln_b=v_sg_ln_b, sg_w=v_sg_w, sg_b=v_sg_b, w_branch_a=v_w_branch_a,
              w_branch_b=v_w_branch_b, w_out=v_w_out, norm2_g=v_norm2_g, ffn_w_gate=v_ffn_w_gate, ffn_w_up=v_ffn_w_up,
              ffn_conv_w=v_ffn_conv_w, ffn_conv_b=v_ffn_conv_b, ffn_w_down=v_ffn_w_down, final_norm_g=v_final_norm_g)

    xs = x[0]
    tgt = loss_target[0]
    T, D = xs.shape
    depth = norm1_g.shape[0]
    H = dn_a_log.shape[1]
    G = sg_w.shape[1]
    WA = H * HEAD_DIM
    WB = G * HEAD_DIM
    N = T // DN_CHUNK
    colA = 4 * WA
    colB0 = colA + 2 * H
    cb_a = (2 * WB) // D

    my = 4 * lax.axis_index("x") + 2 * lax.axis_index("y") + lax.axis_index("c")
    me_arr = my.astype(jnp.int32).reshape(1)

    def view(d):
        return {n: (jnp.transpose(d[n], (0, 2, 1)) if n in TRANSPOSED else d[n]) for n in SHARDED}

    Wv, Mv, Vv = view(W), view(Mo), view(Vo)

    def shard(n, l):
        return Wv[n][l] if n in CONV_WEIGHTS else Wv[n][l].astype(BF16)

    gathered = {}
    gather_names = []
    for l in range(depth):
        gather_names.append([("w_in", l), ("dn_conv_w", l)])
        gather_names.append([("w_branch_a", l), ("w_branch_b", l), ("w_out", l)])
        gather_names.append([("ffn_w_gate", l), ("ffn_w_up", l), ("ffn_conv_w", l)])
        gather_names.append([("ffn_w_down", l)])
    first_handles, first_tok = comm_start([[shard(n, l) for n, l in gather_names[0]]], False, "gather_first_start")
    rest_handles, gather_tok = comm_start([[shard(n, l) for n, l in g] for g in gather_names[1:]], False, "gather_start",
                                          after=first_tok)
    gather_handles = first_handles + rest_handles
    local_first = [a for d in (Wv, Mv, Vv) for n, a in d.items() if n in TRANSPOSED]

    def need(n, l, after):
        if (n, l) not in gathered:
            gi = [i for i, g in enumerate(gather_names) if (n, l) in g][0]
            src, land = comm_wait(gather_handles[gi], after, f"gather_wait{gi}")
            for key, s, ld in zip(gather_names[gi], src, land):
                gathered[key] = ld
        return gathered[(n, l)]

    def full(n, l, after):
        return _assemble(n, need(n, l, after))

    def layer_weights(l):
        return dict(
            g1=norm1_g[l][None], g2=norm2_g[l][None], alog=_pad_lanes(dn_a_log[l][None], H), dtb=_pad_lanes(dn_dt_bias[l][None], H),
            og=dn_onorm_g[l][None], lng=sg_ln_g[l][None], lnb=sg_ln_b[l][None], sgw=sg_w[l], sgbT=sg_b[l].T, fcb=ffn_conv_b[l][None])

    def mixer_in_weights(p, l, after):
        wt = full("w_in", l, after)
        p.update(wA=wt[:colA], wba=_pad_rows(wt[colA:colB0], LANES), wB=wt[colB0:], cw8=_pad_rows(full("dn_conv_w", l, after)))

    def mixer_out_weights(p, l, after):
        p.update(wa=full("w_branch_a", l, after), wb=full("w_branch_b", l, after), wo=full("w_out", l, after))

    def ffn_weights(p, l, after):
        p.update(wg=full("ffn_w_gate", l, after), wu=full("ffn_w_up", l, after), fcw8=_pad_rows(full("ffn_conv_w", l, after)))

    saved = []
    cur = xs
    for l in range(depth):
        p = layer_weights(l)
        t = f"l{l}_"
        h = norm_fwd(cur, p["g1"] + gather_tok[0, 0] if l == 0 else p["g1"], t + "norm1")
        mixer_in_weights(p, l, [h] + local_first if l == 0 else h)
        projA = matmul(h, p["wA"], "nt", t + "projA")
        pba = matmul(h, p["wba"], "nt", t + "proj_ba")
        projB = matmul(h, p["wB"], "nt", t + "projB")
        q, k, v, bg = dn_prep_fwd(projA, pba, p["cw8"], p["alog"], p["dtb"], H, t + "dn_prep")
        u, w, a, qd, kd, gl, tinv = dn_chunk_fwd(q, k, v, bg, t + "dn_chunk")
        o, s_in = dn_scan_fwd(u, w, a, qd, kd, gl, t + "dn_scan")
        y_a = dn_post_fwd(o, projA, p["og"], t + "dn_post")
        y_b = gmlp_fwd(projB, p["lng"], p["lnb"], p["sgw"], p["sgbT"], t + "gmlp")
        mixer_out_weights(p, l, y_b)
        ap = matmul(y_a, p["wa"], "nn", t + "branch_a")
        bp = matmul(y_b, p["wb"], "nn", t + "branch_b")
        merged = merge_fwd(projB, ap, bp, cb_a, t + "merge")
        x1 = matmul(merged, p["wo"], "nn", t + "out_proj", c=cur)
        h2 = norm_fwd(x1, p["g2"], t + "norm2")
        ffn_weights(p, l, h2)
        gp = matmul(h2, p["wg"], "nt", t + "ffn_gate")
        up = matmul(h2, p["wu"], "nt", t + "ffn_up")
        act = ffn_act_fwd(gp, up, p["fcw8"], p["fcb"], t + "ffn_act")
        p.update(wd=full("ffn_w_down", l, act))
        x2 = matmul(act, p["wd"], "nn", t + "ffn_down", c=x1)
        saved.append(dict(p=p, x0=cur, h=h, projA=projA, pba=pba, projB=projB, q=q, k=k, v=v, bg=bg, tinv=tinv,
                          scan=(u, w, a, qd, kd, gl), s_in=s_in, o=o, y_a=y_a, y_b=y_b, ap=ap, bp=bp, merged=merged, x1=x1,
                          h2=h2, gp=gp, up=up, act=act))
        cur = x2

    loss_part, dx, dx_bf, d_final = head_fwd_bwd(cur, final_norm_g[None], tgt, "loss_head")
    loss = lax.psum(loss_part[0, 0], ("x", "y", "c"))

    grads_sh = {n: [None] * depth for n in SHARDED}
    grads_rep = {n: [None] * depth for n in REPLICATED if n != "final_norm_g"}
    exchanges = []

    def exchange(names, l, name, after=None):
        srcs = [_split(n, grads_sh[n][l], F32 if n in CONV_WEIGHTS else BF16) for n in names]
        (handle,), tok = comm_start([srcs], True, name, after=after)
        exchanges.append((names, l, handle))
        return tok

    def whole(a):
        return [(a, 0, a.shape[1])]

    sizes = [math.prod(W[n].shape) for n in REPLICATED]
    tile = SUBLANES * LANES
    nrows = [-(-sz // tile) * SUBLANES for sz in sizes]

    def pack(d):
        parts = [jnp.pad(d[n].reshape(-1).astype(F32), (0, r * LANES - sz)).reshape(r, LANES)
                 for n, sz, r in zip(REPLICATED, sizes, nrows)]
        return jnp.concatenate(parts, axis=0)

    mixer_tok = None
    for l in reversed(range(depth)):
        s = saved[l]
        p = s["p"]
        t = f"l{l}_b_"
        dact = matmul(dx_bf, p["wd"], "nt", t + "d_act")
        grads_sh["ffn_w_down"][l] = whole(matmul(s["act"], dx_bf, "tn", t + "dw_down", out_dtype=BF16, tn=2048))
        fcb = p["fcb"] if mixer_tok is None else p["fcb"] + mixer_tok[0, 0]
        dgp, dup, dfcw, dfcb = ffn_act_bwd(s["gp"], s["up"], p["fcw8"], fcb, dact, t + "ffn_act")
        dh2 = matmul([dgp, dup], [p["wg"], p["wu"]], "nn", t + "dh2")
        grads_sh["ffn_w_gate"][l] = whole(matmul(dgp, s["h2"], "tn", t + "dw_gate", out_dtype=BF16, tn=2048))
        grads_sh["ffn_w_up"][l] = whole(matmul(dup, s["h2"], "tn", t + "dw_up", out_dtype=BF16, tn=2048))
        grads_sh["ffn_conv_w"][l] = whole(dfcw[:3])
        grads_rep["ffn_conv_b"][l] = dfcb[0]
        tok = exchange(("ffn_w_down", "ffn_w_gate", "ffn_w_up", "ffn_conv_w"), l, t + "ffn_grads_start")
        dx1, dx1_bf, dg2 = norm_bwd(s["x1"], p["g2"] + tok[0, 0], dh2, dx, t + "norm2")
        grads_rep["norm2_g"][l] = dg2[0]
        dmerged = matmul(dx1_bf, p["wo"], "nt", t + "d_merged")
        grads_sh["w_out"][l] = whole(matmul(s["merged"], dx1_bf, "tn", t + "dw_out", out_dtype=BF16, tn=2048))
        dga, dgb, dap, dbp = merge_bwd(s["projB"], s["ap"], s["bp"], dmerged, cb_a, t + "merge")
        dya = matmul(dap, p["wa"], "nt", t + "d_ya")
        dyb = matmul(dbp, p["wb"], "nt", t + "d_yb")
        grads_sh["w_branch_a"][l] = whole(matmul(s["y_a"], dap, "tn", t + "dw_a", out_dtype=BF16))
        grads_sh["w_branch_b"][l] = whole(matmul(s["y_b"], dbp, "tn", t + "dw_b", out_dtype=BF16))
        du_raw, dv_raw, dlng, dlnb, dsgw, dsgbT = gmlp_bwd(s["projB"], p["lng"], p["lnb"], p["sgw"], p["sgbT"], dyb, t + "gmlp")
        grads_rep["sg_ln_g"][l], grads_rep["sg_ln_b"][l] = dlng[0], dlnb[0]
        grads_rep["sg_w"][l], grads_rep["sg_b"][l] = dsgw, dsgbT.T
        do, dz, dog = dn_post_bwd(s["o"], s["projA"], p["og"], dya, t + "dn_post")
        grads_rep["dn_onorm_g"][l] = dog[0]
        du, dw, da, dqd, dkd, dgl = dn_scan_bwd(*s["scan"], s["s_in"], do, t + "dn_scan")
        dq, dk, dv, dbg = dn_chunk_bwd(s["q"], s["k"], s["v"], s["bg"], s["tinv"], du, dw, da, dqd, dkd, dgl, t + "dn_chunk")
        dqkv, dba, dcw, dalog, ddtb = dn_prep_bwd(s["projA"], s["pba"], p["cw8"], p["alog"], p["dtb"], dq, dk, dv, dbg, H,
                                                  t + "dn_prep")
        grads_sh["dn_conv_w"][l] = whole(dcw[:4])
        grads_rep["dn_a_log"][l], grads_rep["dn_dt_bias"][l] = dalog[0, H:2 * H], ddtb[0, H:2 * H]
        tok = exchange(("w_out", "w_branch_a", "w_branch_b", "dn_conv_w"), l, t + "mixer_grads_start")
        dba = dba + tok[0, 0].astype(BF16)
        dprojA = jnp.concatenate([dqkv, dz], axis=1)
        dprojB = jnp.concatenate([du_raw, dv_raw, dga, dgb], axis=1)
        dwA = matmul(dprojA, s["h"], "tn", t + "dw_A", out_dtype=BF16, tn=2048)
        dwba = matmul(dba, s["h"], "tn", t + "dw_ba", out_dtype=BF16, tn=2048)
        dwB = matmul(dprojB, s["h"], "tn", t + "dw_B", out_dtype=BF16, tn=2048)
        grads_sh["w_in"][l] = [(dwA, 0, colA), (dwba, colA, 2 * H), (dwB, colB0, dwB.shape[0])]
        mixer_tok = exchange(("w_in",), l, t + "w_in_grads_start")
        dh = matmul([dba, dprojA, dprojB], [p["wba"] + mixer_tok[0, 0].astype(BF16), p["wA"], p["wB"]], "nn", t + "dh")
        dx, dx_bf, dg1 = norm_bwd(s["x0"], p["g1"], dh, dx1, t + "norm1")
        grads_rep["norm1_g"][l] = dg1[0]
        if l == 0:
            rep_full = {n: (jnp.stack(grads_rep[n]) if n != "final_norm_g" else d_final[0]) for n in REPLICATED}
            (small_handle,), small_tok = comm_start([[pack(rep_full)]], False, "small_grads_start")

    out = {}
    after = [dx, small_tok]

    def update_group(gi, after):
        names, l, handle = exchanges[gi]
        src, land = comm_wait(handle, after, f"grads_wait{gi}")
        done = []
        for n, s_, ld in zip(names, src, land):
            res = sum_adamw_shard(s_, ld, me_arr, Wv[n], Mv[n], Vv[n], l, out.get(n), f"adamw_{n}_{l}")
            out[n] = list(res)
            done.append(res[0])
        return done

    for gi in range(len(exchanges) - 1):
        after = update_group(gi, after)

    (small_src,), (small_land,) = comm_wait(small_handle, after, "small_grads_wait")
    res = sum_adamw(small_land, pack(W), pack(Mo), pack(Vo), "adamw_small")
    update_group(len(exchanges) - 1, after + [res[0]])
    for n in TRANSPOSED:
        out[n] = [jnp.transpose(r, (0, 2, 1)) for r in out[n]]
    row0 = 0
    for n, sz, nr in zip(REPLICATED, sizes, nrows):
        out[n] = [r[row0:row0 + nr].reshape(-1)[:sz].reshape(W[n].shape) for r in res]
        row0 += nr

    return (loss, dx[None], *[out[n][0] for n in WEIGHTS], *[out[n][1] for n in WEIGHTS],
            *[out[n][2] for n in WEIGHTS], *[out[n][3] for n in WEIGHTS])
```
